```python
import math, functools
import jax, jax.numpy as jnp
from jax import lax
import numpy as np

D_MODEL = 1024
BATCH = 2
SEQ = 8192
DEPTH = 2
DEC_BATCH = 8
DEC_SEQ = 16
PAST_LEN = 1024

CHUNK = 64
N_BRANCH = 4
BRANCH_W = 512
S5_GROUP = 16
S5_GROUPS = BRANCH_W // S5_GROUP
S5_STATE = 64
DSA_HEADS = 8
DSA_KV_HEADS = 2
DSA_HEAD_DIM = 64
IDX_HEADS = 4
IDX_DIM = 64
DSA_TOPK_MAX = 256
Q_BLOCK = 128
REL_BUCKETS = 32
REL_MAX_DIST = 128
GDN_HEADS = 4
GDN_DK = 128
GDN_DV = 128
GDN_QK = GDN_HEADS * GDN_DK
GDN_VW = GDN_HEADS * GDN_DV
CONV_W = 4
GDN_CONV_CH = 2 * GDN_QK + GDN_VW
GLA_HEADS = 4
GLA_DK = 64
GLA_DV = 128
GLA_GATE_RANK = 16
GLA_TAU = 16.0
GLA_BLOCK = 16
LN_EPS = 1e-5
RMS_EPS = 1e-6
DN_ALPHA = (2 * DEPTH) ** 0.25
DN_BETA = (8 * DEPTH) ** -0.25

IN_LAYOUT = (
    ('a_u', BRANCH_W), ('a_gate', BRANCH_W),
    ('b_q', DSA_HEADS * DSA_HEAD_DIM), ('b_k', DSA_KV_HEADS * DSA_HEAD_DIM),
    ('b_v', DSA_KV_HEADS * DSA_HEAD_DIM), ('b_qi', IDX_HEADS * IDX_DIM), ('b_ki', IDX_DIM),
    ('b_wi', IDX_HEADS), ('b_gate', BRANCH_W),
    ('c_qkv', GDN_CONV_CH), ('c_beta', GDN_HEADS), ('c_a', GDN_HEADS), ('c_gate', BRANCH_W),
    ('d_q', GLA_HEADS * GLA_DK), ('d_k', GLA_HEADS * GLA_DK), ('d_v', GLA_HEADS * GLA_DV),
    ('d_g', GLA_GATE_RANK), ('d_gate', BRANCH_W),
    ('merge', N_BRANCH * D_MODEL),
)
IN_WIDTH = sum(w for _, w in IN_LAYOUT)

kernel_name = 'hybrid_streaming_encoder_step'


def split_cols(h):
    out = {}
    off = 0
    for name, width in IN_LAYOUT:
        out[name] = h[..., off:off + width]
        off += width
    return out


def layer_norm(x, g, b):
    xf = x.astype(jnp.float32)
    mu = jnp.mean(xf, axis=-1, keepdims=True)
    var = jnp.mean(jnp.square(xf - mu), axis=-1, keepdims=True)
    y = (xf - mu) * lax.rsqrt(var + LN_EPS) * g.astype(jnp.float32) + b.astype(jnp.float32)
    return y.astype(x.dtype)


def rms_norm(x, g):
    xf = x.astype(jnp.float32)
    return xf * lax.rsqrt(jnp.mean(xf * xf, axis=-1, keepdims=True) + RMS_EPS) * g.astype(jnp.float32)


def l2_normalize(x):
    xf = x.astype(jnp.float32)
    return xf * lax.rsqrt(jnp.sum(xf * xf, axis=-1, keepdims=True) + RMS_EPS)


def pad_time(a, blk):
    pad = (-a.shape[1]) % blk
    if pad:
        a = jnp.pad(a, [(0, 0), (0, pad)] + [(0, 0)] * (a.ndim - 2))
    return a


def to_blocks(a, blk):
    n, t = a.shape[:2]
    a = a.reshape((n, t // blk, blk) + a.shape[2:])
    return jnp.swapaxes(jnp.moveaxis(a, 1, 0), 2, 3)


def from_blocks(a, t):
    a = jnp.moveaxis(jnp.swapaxes(a, 2, 3), 0, 1)
    n, nb, blk = a.shape[:3]
    return a.reshape((n, nb * blk) + a.shape[3:])[:, :t]


def s5_branch(u, p, st):
    n, t, _ = u.shape
    f32 = jnp.float32
    uf = u.astype(f32).reshape(n, t, S5_GROUPS, S5_GROUP)
    lam_r = jnp.minimum(p['s5_a_re'].astype(f32), -1e-4)
    lam_i = p['s5_a_im'].astype(f32)
    dt = jnp.exp(p['s5_log_dt'].astype(f32))[:, None]
    mag = jnp.exp(lam_r * dt)
    ar = mag * jnp.cos(lam_i * dt)
    ai = mag * jnp.sin(lam_i * dt)
    den = lam_r * lam_r + lam_i * lam_i
    nr = ar - 1.0
    cr = (nr * lam_r + ai * lam_i) / den
    ci = (ai * lam_r - nr * lam_i) / den
    b_re = p['s5_b_re'].astype(f32)
    b_im = p['s5_b_im'].astype(f32)
    bb_r = cr[..., None] * b_re - ci[..., None] * b_im
    bb_i = cr[..., None] * b_im + ci[..., None] * b_re
    xr = jnp.einsum('ntgh,gph->ntgp', uf, bb_r)
    xi = jnp.einsum('ntgh,gph->ntgp', uf, bb_i)
    if st is not None:
        h0r = st['s5_re'].astype(f32)
        h0i = st['s5_im'].astype(f32)
        xr = xr.at[:, 0].add(ar * h0r - ai * h0i)
        xi = xi.at[:, 0].add(ar * h0i + ai * h0r)
    a_r = jnp.broadcast_to(ar, xr.shape)
    a_i = jnp.broadcast_to(ai, xi.shape)

    def combine(e1, e2):
        a1r, a1i, b1r, b1i = e1
        a2r, a2i, b2r, b2i = e2
        return (a2r * a1r - a2i * a1i, a2r * a1i + a2i * a1r,
                a2r * b1r - a2i * b1i + b2r, a2r * b1i + a2i * b1r + b2i)

    _, _, hr, hi = lax.associative_scan(combine, (a_r, a_i, xr, xi), axis=1)
    y = (jnp.einsum('ntgp,ghp->ntgh', hr, p['s5_c_re'].astype(f32))
         - jnp.einsum('ntgp,ghp->ntgh', hi, p['s5_c_im'].astype(f32)))
    y = y + p['s5_d'].astype(f32) * uf
    y = jax.nn.gelu(y.reshape(n, t, BRANCH_W)).astype(u.dtype)
    g = y @ p['s5_w_glu']
    y = g[..., :BRANCH_W] * jax.nn.sigmoid(g[..., BRANCH_W:])
    return y, hr[:, -1], hi[:, -1]


def t5_bucket(rel):
    nb = REL_BUCKETS // 2
    max_exact = nb // 2
    ret = jnp.where(rel > 0, nb, 0)
    dist = jnp.abs(rel)
    distf = jnp.maximum(dist, 1).astype(jnp.float32)
    large = max_exact + (jnp.log(distf / max_exact) / math.log(REL_MAX_DIST / max_exact)
                         * (nb - max_exact)).astype(jnp.int32)
    large = jnp.minimum(large, nb - 1)
    return ret + jnp.where(dist < max_exact, dist, large)


def dsa_attend(q, qi, wi, qpos, k, v, ki, kpos, rel_bias, topk):
    f32 = jnp.float32
    n, nq, nh, dh = q.shape
    rep = nh // DSA_KV_HEADS
    sc = jnp.einsum('nqhd,nld->nqhl', qi.astype(f32), ki.astype(f32)) * IDX_DIM ** -0.5
    sc = jnp.einsum('nqhl,nqh->nql', jax.nn.relu(sc), wi.astype(f32) * IDX_HEADS ** -0.5)
    allowed = kpos[None, :] < ((qpos // CHUNK + 1) * CHUNK)[:, None]
    sc = jnp.where(allowed[None], sc, -jnp.inf)
    top_s, top_i = lax.top_k(sc, topk)
    valid = jnp.isfinite(top_s)
    kg = jax.vmap(lambda kb, ib: kb[ib])(k, top_i)
    vg = jax.vmap(lambda vb, ib: vb[ib])(v, top_i)
    qg = q.reshape(n, nq, DSA_KV_HEADS, rep, dh)
    logits = jnp.einsum('nqgrd,nqkgd->nqgrk', qg, kg).astype(f32) * dh ** -0.5
    rel = kpos[top_i] - qpos[None, :, None]
    bias = rel_bias[t5_bucket(rel)].astype(f32)
    bias = bias.reshape(n, nq, topk, DSA_KV_HEADS, rep).transpose(0, 1, 3, 4, 2)
    logits = jnp.where(valid[:, :, None, None, :], logits + bias, -jnp.inf)
    prob = jax.nn.softmax(logits, axis=-1).astype(v.dtype)
    o = jnp.einsum('nqgrk,nqkgd->nqgrd', prob, vg)
    return o.reshape(n, nq, nh * dh)


def dsa_branch(h, rel_bias, st):
    n, t, _ = h['b_q'].shape
    q = h['b_q'].reshape(n, t, DSA_HEADS, DSA_HEAD_DIM)
    k = h['b_k'].reshape(n, t, DSA_KV_HEADS, DSA_HEAD_DIM)
    v = h['b_v'].reshape(n, t, DSA_KV_HEADS, DSA_HEAD_DIM)
    qi = h['b_qi'].reshape(n, t, IDX_HEADS, IDX_DIM)
    ki = h['b_ki']
    wi = h['b_wi']
    if st is None:
        k_all, v_all, ki_all = k, v, ki
        q0 = 0
    else:
        k_all = jnp.concatenate([st['k'].astype(k.dtype), k], axis=1)
        v_all = jnp.concatenate([st['v'].astype(v.dtype), v], axis=1)
        ki_all = jnp.concatenate([st['kidx'].astype(ki.dtype), ki], axis=1)
        q0 = st['k'].shape[1]
    n_keys = k_all.shape[1]
    topk = min(DSA_TOPK_MAX, n_keys // 4)
    kpos = jnp.arange(n_keys, dtype=jnp.int32)
    qpos = q0 + jnp.arange(t, dtype=jnp.int32)
    attend = functools.partial(dsa_attend, k=k_all, v=v_all, ki=ki_all, kpos=kpos,
                               rel_bias=rel_bias, topk=topk)
    if t % Q_BLOCK == 0:
        nb = t // Q_BLOCK
        blk = lambda a: jnp.moveaxis(a.reshape((n, nb, Q_BLOCK) + a.shape[2:]), 1, 0)
        o = lax.map(lambda xs: attend(*xs), (blk(q), blk(qi), blk(wi), qpos.reshape(nb, Q_BLOCK)))
        o = jnp.moveaxis(o, 0, 1).reshape(n, t, DSA_HEADS * DSA_HEAD_DIM)
    else:
        o = attend(q, qi, wi, qpos)
    return o, k, v, ki


def causal_conv(x, w, prev):
    n, t, c = x.shape
    if prev is None:
        prev = jnp.zeros((n, CONV_W - 1, c), x.dtype)
    xp = jnp.concatenate([prev.astype(x.dtype), x], axis=1)
    y = xp[:, 0:t] * w[0]
    for i in range(1, CONV_W):
        y = y + xp[:, i:i + t] * w[i]
    return y, xp[:, t:]


def gdn_block_step(s, xs):
    q, k, v, g, beta = xs
    c = q.shape[2]
    gc = jnp.cumsum(g, axis=-1)
    idx = jnp.arange(c)
    strict = idx[:, None] > idx[None, :]
    incl = idx[:, None] >= idx[None, :]
    diff = gc[..., :, None] - gc[..., None, :]
    dec_strict = jnp.where(strict, jnp.exp(jnp.where(strict, diff, 0.0)), 0.0)
    dec_incl = jnp.where(incl, jnp.exp(jnp.where(incl, diff, 0.0)), 0.0)
    a = beta[..., :, None] * jnp.einsum('nhid,nhjd->nhij', k, k) * dec_strict
    m = a + jnp.eye(c, dtype=a.dtype)
    rhs = jnp.concatenate([(beta * jnp.exp(gc))[..., None] * k, beta[..., None] * v], axis=-1)
    sol = lax.linalg.triangular_solve(m, rhs, left_side=True, lower=True, unit_diagonal=True)
    w_blk, u_blk = sol[..., :GDN_DK], sol[..., GDN_DK:]
    v_new = u_blk - jnp.einsum('nhik,nhkv->nhiv', w_blk, s)
    o = (jnp.exp(gc)[..., None] * jnp.einsum('nhik,nhkv->nhiv', q, s)
         + jnp.einsum('nhij,nhjv->nhiv', jnp.einsum('nhik,nhjk->nhij', q, k) * dec_incl, v_new))
    g_last = gc[..., -1:]
    s = (jnp.exp(g_last)[..., None] * s
         + jnp.einsum('nhjk,nhjv->nhkv', k * jnp.exp(g_last - gc)[..., None], v_new))
    return s, o


def gdn_branch(h, p, st):
    raw = h['c_qkv']
    n, t, _ = raw.shape
    f32 = jnp.float32
    conv_out, conv_state = causal_conv(raw, p['gdn_conv'], None if st is None else st['gdn_conv'])
    qkv = jax.nn.silu(conv_out.astype(f32))
    q = l2_normalize(qkv[..., :GDN_QK].reshape(n, t, GDN_HEADS, GDN_DK)) * GDN_DK ** -0.5
    k = l2_normalize(qkv[..., GDN_QK:2 * GDN_QK].reshape(n, t, GDN_HEADS, GDN_DK))
    v = qkv[..., 2 * GDN_QK:].reshape(n, t, GDN_HEADS, GDN_DV)
    beta = jax.nn.sigmoid(h['c_beta'].astype(f32))
    g = -jnp.exp(p['gdn_a_log'].astype(f32)) * jax.nn.softplus(
        h['c_a'].astype(f32) + p['gdn_dt_bias'].astype(f32))
    s0 = (jnp.zeros((n, GDN_HEADS, GDN_DK, GDN_DV), f32) if st is None
          else st['gdn'].astype(f32))
    xs = tuple(to_blocks(pad_time(a, CHUNK), CHUNK) for a in (q, k, v, g, beta))
    s_new, o = lax.scan(gdn_block_step, s0, xs)
    o = rms_norm(from_blocks(o, t), p['gdn_norm'])
    return o.reshape(n, t, BRANCH_W).astype(raw.dtype), s_new, conv_state


def gla_block_step(s, xs):
    q, k, v, lg = xs
    c = q.shape[2]
    bc = jnp.cumsum(lg, axis=2)
    idx = jnp.arange(c)
    incl = (idx[:, None] >= idx[None, :])[:, :, None]
    diff = bc[:, :, :, None, :] - bc[:, :, None, :, :]
    dec = jnp.where(incl, jnp.exp(jnp.where(incl, diff, 0.0)), 0.0)
    a = jnp.einsum('nhik,nhjk,nhijk->nhij', q, k, dec)
    o = (jnp.einsum('nhij,nhjv->nhiv', a, v)
         + jnp.einsum('nhik,nhkv->nhiv', q * jnp.exp(bc), s))
    b_last = bc[:, :, -1:, :]
    s = (jnp.exp(b_last[:, :, 0, :])[..., None] * s
         + jnp.einsum('nhjk,nhjv->nhkv', k * jnp.exp(b_last - bc), v))
    return s, o


def gla_branch(h, p, st):
    n, t, _ = h['d_q'].shape
    f32 = jnp.float32
    q = h['d_q'].astype(f32).reshape(n, t, GLA_HEADS, GLA_DK) * GLA_DK ** -0.5
    k = h['d_k'].astype(f32).reshape(n, t, GLA_HEADS, GLA_DK)
    v = h['d_v'].astype(f32).reshape(n, t, GLA_HEADS, GLA_DV)
    logit = h['d_g'].astype(f32) @ p['gla_w_g2'].astype(f32) + p['gla_b_g'].astype(f32)
    lg = (jax.nn.log_sigmoid(logit) / GLA_TAU).reshape(n, t, GLA_HEADS, GLA_DK)
    s0 = (jnp.zeros((n, GLA_HEADS, GLA_DK, GLA_DV), f32) if st is None
          else st['gla'].astype(f32))
    xs = tuple(to_blocks(pad_time(a, GLA_BLOCK), GLA_BLOCK) for a in (q, k, v, lg))
    s_new, o = lax.scan(gla_block_step, s0, xs)
    o = rms_norm(from_blocks(o, t), p['gla_norm'])
    return o.reshape(n, t, BRANCH_W).astype(h['d_q'].dtype), s_new


def trunk_layer(x, p, rel_bias, st):
    n, t, _ = x.shape
    h = split_cols(x @ p['w_in'])
    ya, s5_re, s5_im = s5_branch(h['a_u'], p, st)
    yb, k_new, v_new, ki_new = dsa_branch(h, rel_bias, st)
    yc, gdn_s, gdn_conv = gdn_branch(h, p, st)
    yd, gla_s = gla_branch(h, p, st)
    branches = jnp.stack([ya * jax.nn.silu(h['a_gate']), yb * jax.nn.silu(h['b_gate']),
                          yc * jax.nn.silu(h['c_gate']), yd * jax.nn.silu(h['d_gate'])], axis=2)
    proj = jnp.einsum('ntbw,bwd->ntbd', branches, p['w_branch'])
    gates = jax.nn.sigmoid(h['merge'].reshape(n, t, N_BRANCH, D_MODEL))
    mixed = jnp.sum(gates * proj, axis=2) @ p['w_out']
    y = layer_norm(DN_ALPHA * x + mixed, p['ln_g'], p['ln_b'])
    new = {'k': k_new, 'v': v_new, 'kidx': ki_new, 's5_re': s5_re, 's5_im': s5_im,
           'gdn': gdn_s, 'gdn_conv': gdn_conv, 'gla': gla_s}
    return y, new


def stack_layers(states, name):
    return jnp.stack([s[name] for s in states], axis=0)


def setup_inputs(seed: int = 0) -> dict:
    key = jax.random.key(seed)
    ks = jax.random.split(key, 32)
    f32 = jnp.float32

    def nrm(i, shape, scale):
        return scale * jax.random.normal(ks[i], shape, f32)

    def unif(i, shape, lo, hi):
        return jax.random.uniform(ks[i], shape, f32, lo, hi)

    gdn_dt = jnp.exp(unif(25, (DEPTH, GDN_HEADS), math.log(1e-3), math.log(1e-1)))
    return {
        'x_prompt': nrm(0, (BATCH, SEQ, D_MODEL), 1.0),
        'x_sample': nrm(1, (DEC_BATCH, DEC_SEQ, D_MODEL), 1.0),
        'cache_k': nrm(2, (DEPTH, DEC_BATCH, PAST_LEN, DSA_KV_HEADS, DSA_HEAD_DIM), 1.0),
        'cache_v': nrm(3, (DEPTH, DEC_BATCH, PAST_LEN, DSA_KV_HEADS, DSA_HEAD_DIM), 1.0),
        'cache_kidx': nrm(4, (DEPTH, DEC_BATCH, PAST_LEN, IDX_DIM), 1.0),
        'state_s5_re': nrm(5, (DEPTH, DEC_BATCH, S5_GROUPS, S5_STATE), 0.1),
        'state_s5_im': nrm(6, (DEPTH, DEC_BATCH, S5_GROUPS, S5_STATE), 0.1),
        'state_gdn': nrm(7, (DEPTH, DEC_BATCH, GDN_HEADS, GDN_DK, GDN_DV), 0.1),
        'state_gdn_conv': nrm(8, (DEPTH, DEC_BATCH, CONV_W - 1, GDN_CONV_CH), 1.0),
        'state_gla': nrm(9, (DEPTH, DEC_BATCH, GLA_HEADS, GLA_DK, GLA_DV), 0.1),
        'w_in': nrm(10, (DEPTH, D_MODEL, IN_WIDTH), D_MODEL ** -0.5),
        'w_branch': nrm(11, (DEPTH, N_BRANCH, BRANCH_W, D_MODEL), BRANCH_W ** -0.5),
        'w_out': nrm(12, (DEPTH, D_MODEL, D_MODEL), DN_BETA * D_MODEL ** -0.5),
        'ln_g': 1.0 + nrm(13, (DEPTH, D_MODEL), 0.02),
        'ln_b': nrm(14, (DEPTH, D_MODEL), 0.02),
        'rel_bias': nrm(15, (REL_BUCKETS, DSA_HEADS), 0.2),
        's5_a_re': -0.5 + nrm(16, (DEPTH, S5_GROUPS, S5_STATE), 0.01),
        's5_a_im': math.pi * jnp.arange(S5_STATE, dtype=f32) + nrm(17, (DEPTH, S5_GROUPS, S5_STATE), 0.01),
        's5_log_dt': unif(18, (DEPTH, S5_GROUPS), math.log(1e-3), math.log(1e-1)),
        's5_b_re': nrm(19, (DEPTH, S5_GROUPS, S5_STATE, S5_GROUP), (2 * S5_GROUP) ** -0.5),
        's5_b_im': nrm(20, (DEPTH, S5_GROUPS, S5_STATE, S5_GROUP), (2 * S5_GROUP) ** -0.5),
        's5_c_re': nrm(21, (DEPTH, S5_GROUPS, S5_GROUP, S5_STATE), (2 * S5_STATE) ** -0.5),
        's5_c_im': nrm(22, (DEPTH, S5_GROUPS, S5_GROUP, S5_STATE), (2 * S5_STATE) ** -0.5),
        's5_d': nrm(23, (DEPTH, S5_GROUPS, S5_GROUP), 1.0),
        's5_w_glu': nrm(24, (DEPTH, BRANCH_W, 2 * BRANCH_W), BRANCH_W ** -0.5),
        'gdn_conv': nrm(26, (DEPTH, CONV_W, GDN_CONV_CH), CONV_W ** -0.5),
        'gdn_a_log': jnp.log(unif(27, (DEPTH, GDN_HEADS), 1.0, 16.0)),
        'gdn_dt_bias': gdn_dt + jnp.log(-jnp.expm1(-gdn_dt)),
        'gdn_norm': 1.0 + nrm(28, (DEPTH, GDN_DV), 0.02),
        'gla_w_g2': nrm(29, (DEPTH, GLA_GATE_RANK, GLA_HEADS * GLA_DK), GLA_GATE_RANK ** -0.5),
        'gla_b_g': nrm(30, (DEPTH, GLA_HEADS * GLA_DK), 0.1),
        'gla_norm': 1.0 + nrm(31, (DEPTH, GLA_DV), 0.02),
    }


def reference(x_prompt, x_sample, cache_k, cache_v, cache_kidx, state_s5_re, state_s5_im,
              state_gdn, state_gdn_conv, state_gla, w_in, w_branch, w_out, ln_g, ln_b, rel_bias,
              s5_a_re, s5_a_im, s5_log_dt, s5_b_re, s5_b_im, s5_c_re, s5_c_im, s5_d, s5_w_glu,
              gdn_conv, gdn_a_log, gdn_dt_bias, gdn_norm, gla_w_g2, gla_b_g, gla_norm):
    yp = x_prompt
    ys = x_sample
    new_p = []
    new_s = []
    for l in range(DEPTH):
        p = {'w_in': w_in[l], 'w_branch': w_branch[l], 'w_out': w_out[l],
             'ln_g': ln_g[l], 'ln_b': ln_b[l],
             's5_a_re': s5_a_re[l], 's5_a_im': s5_a_im[l], 's5_log_dt': s5_log_dt[l],
             's5_b_re': s5_b_re[l], 's5_b_im': s5_b_im[l], 's5_c_re': s5_c_re[l],
             's5_c_im': s5_c_im[l], 's5_d': s5_d[l], 's5_w_glu': s5_w_glu[l],
             'gdn_conv': gdn_conv[l], 'gdn_a_log': gdn_a_log[l], 'gdn_dt_bias': gdn_dt_bias[l],
             'gdn_norm': gdn_norm[l], 'gla_w_g2': gla_w_g2[l], 'gla_b_g': gla_b_g[l],
             'gla_norm': gla_norm[l]}
        st = {'k': cache_k[l], 'v': cache_v[l], 'kidx': cache_kidx[l],
              's5_re': state_s5_re[l], 's5_im': state_s5_im[l], 'gdn': state_gdn[l],
              'gdn_conv': state_gdn_conv[l], 'gla': state_gla[l]}
        yp, stp = trunk_layer(yp, p, rel_bias, None)
        ys, sts = trunk_layer(ys, p, rel_bias, st)
        new_p.append(stp)
        new_s.append(sts)
    return (yp, ys,
            stack_layers(new_p, 'k'), stack_layers(new_p, 'v'), stack_layers(new_p, 'kidx'),
            stack_layers(new_p, 's5_re'), stack_layers(new_p, 's5_im'), stack_layers(new_p, 'gdn'),
            stack_layers(new_p, 'gdn_conv'), stack_layers(new_p, 'gla'),
            stack_layers(new_s, 'k'), stack_layers(new_s, 'v'), stack_layers(new_s, 'kidx'),
            stack_layers(new_s, 's5_re'), stack_layers(new_s, 's5_im'), stack_layers(new_s, 'gdn'),
            stack_layers(new_s, 'gdn_conv'), stack_layers(new_s, 'gla'))
```

```python
import functools
import math

import jax
import jax.numpy as jnp
from jax import lax
from jax.experimental import pallas as pl
from jax.experimental.pallas import tpu as pltpu

F32 = jnp.float32
BF16 = jnp.bfloat16
HIGHEST = lax.Precision.HIGHEST

D_MODEL = 1024
DEPTH = 2
CHUNK = 64
N_BRANCH = 4
BRANCH_W = 512
S5_GROUP = 16
S5_GROUPS = BRANCH_W // S5_GROUP
S5_STATE = 64
S5_LANES = S5_GROUPS * S5_STATE
DSA_HEADS = 8
DSA_KV_HEADS = 2
DSA_HEAD_DIM = 64
IDX_HEADS = 4
IDX_DIM = 64
DSA_TOPK_MAX = 256
REL_BUCKETS = 32
REL_MAX_DIST = 128
GDN_HEADS = 4
GDN_DK = 128
GDN_DV = 128
GDN_QK = GDN_HEADS * GDN_DK
GDN_VW = GDN_HEADS * GDN_DV
CONV_W = 4
GDN_CONV_CH = 2 * GDN_QK + GDN_VW
GLA_HEADS = 4
GLA_DK = 64
GLA_DV = 128
GLA_GATE_RANK = 16
GLA_TAU = 16.0
GLA_BLOCK = 16
LN_EPS = 1e-5
RMS_EPS = 1e-6
DN_ALPHA = (2 * DEPTH) ** 0.25

LANE = 128
KEY_BLOCK = 128
Q_BLOCK = 128
VMEM_LIMIT = 56 * 1024 * 1024

IN_LAYOUT = (
    ('a_u', BRANCH_W), ('a_gate', BRANCH_W),
    ('b_q', DSA_HEADS * DSA_HEAD_DIM), ('b_k', DSA_KV_HEADS * DSA_HEAD_DIM),
    ('b_v', DSA_KV_HEADS * DSA_HEAD_DIM), ('b_qi', IDX_HEADS * IDX_DIM), ('b_ki', IDX_DIM),
    ('b_wi', IDX_HEADS), ('b_gate', BRANCH_W),
    ('c_qkv', GDN_CONV_CH), ('c_beta', GDN_HEADS), ('c_a', GDN_HEADS), ('c_gate', BRANCH_W),
    ('d_q', GLA_HEADS * GLA_DK), ('d_k', GLA_HEADS * GLA_DK), ('d_v', GLA_HEADS * GLA_DV),
    ('d_g', GLA_GATE_RANK), ('d_gate', BRANCH_W),
    ('merge', N_BRANCH * D_MODEL),
)

PROJ_ORDER = ('merge', 'a_u', 'a_gate', 'b_q', 'b_qi', 'b_k', 'b_v', 'c_qkv', 'b_gate', 'c_gate',
              'd_v', 'd_gate', 'd_q', 'd_k', 'b_ki', 'b_wi', 'c_beta', 'c_a', 'd_g')
MISC_WI = IDX_DIM
MISC_BETA = MISC_WI + IDX_HEADS
MISC_A = MISC_BETA + GDN_HEADS
MISC_G = MISC_A + GDN_HEADS
MISC_PAD = LANE - (MISC_G + GLA_GATE_RANK)


def _proj_offsets():
    widths = dict(IN_LAYOUT)
    off, out = 0, {}
    for name in PROJ_ORDER:
        out[name] = off
        off += widths[name]
    return out, off + MISC_PAD


PROJ_OFF, PROJ_W = _proj_offsets()
OFF_MISC = PROJ_OFF['b_ki']
HK_W = 5 * LANE
HK_V = 2 * LANE
HK_KI = 4 * LANE


def _dot(a, b, precision=None):
    return jnp.dot(a, b, preferred_element_type=F32, precision=precision)


def _dot_nt(a, b, precision=None):
    return lax.dot_general(a, b, (((1,), (1,)), ((), ())), preferred_element_type=F32, precision=precision)


def _dot_tn(a, b, precision=None):
    return lax.dot_general(a, b, (((0,), (0,)), ((), ())), preferred_element_type=F32, precision=precision)


def _params(*sem):
    return pltpu.CompilerParams(dimension_semantics=sem, vmem_limit_bytes=VMEM_LIMIT)


def _softplus(x):
    return jnp.maximum(x, 0.0) + jnp.log1p(jnp.exp(-jnp.abs(x)))


def _iota(shape, dim):
    return lax.broadcasted_iota(jnp.int32, shape, dim)


def _mm_kernel(x_ref, w_ref, o_ref):
    o_ref[...] = _dot(x_ref[...].astype(BF16), w_ref[...]).astype(o_ref.dtype)


def _matmul(x, w, out_dtype, tn):
    rows, kdim = x.shape
    width = w.shape[1]
    tm = min(rows, 1024)
    return pl.pallas_call(
        _mm_kernel,
        grid=(rows // tm, width // tn),
        in_specs=[pl.BlockSpec((tm, kdim), lambda i, j: (i, 0)),
                  pl.BlockSpec((kdim, tn), lambda i, j: (0, j))],
        out_specs=pl.BlockSpec((tm, tn), lambda i, j: (i, j)),
        out_shape=jax.ShapeDtypeStruct((rows, width), out_dtype),
        compiler_params=_params("parallel", "parallel"),
        name="proj_matmul",
    )(x, w)


def _s5_kernel(u_ref, h0r_ref, h0i_ref, ar_ref, ai_ref, bbr_ref, bbi_ref, ccr_ref, cci_ref, d_ref, wglu_ref,
               ya_ref, hr_out, hi_out, xr_s, xi_s, cr_s, ci_s, *, tc):
    @pl.when(pl.program_id(1) == 0)
    def _():
        cr_s[...] = h0r_ref[...]
        ci_s[...] = h0i_ref[...]

    u = u_ref[...]
    ub = u.astype(BF16)
    xr_s[...] = _dot(ub, bbr_ref[...])
    xi_s[...] = _dot(ub, bbi_ref[...])
    ar = ar_ref[...]
    ai = ai_ref[...]

    def step(t, carry):
        hr, hi = carry
        nhr = ar * hr - ai * hi + xr_s[pl.ds(t, 1), :]
        nhi = ar * hi + ai * hr + xi_s[pl.ds(t, 1), :]
        xr_s[pl.ds(t, 1), :] = nhr
        xi_s[pl.ds(t, 1), :] = nhi
        return nhr, nhi

    hr, hi = lax.fori_loop(0, tc, step, (cr_s[...], ci_s[...]), unroll=8)
    cr_s[...] = hr
    ci_s[...] = hi
    hr_out[...] = hr
    hi_out[...] = hi
    y = (_dot(xr_s[...].astype(BF16), ccr_ref[...]) - _dot(xi_s[...].astype(BF16), cci_ref[...])
         + d_ref[...] * u)
    y = jax.nn.gelu(y)
    g = _dot(y.astype(BF16), wglu_ref[...])
    ya_ref[...] = g[:, :BRANCH_W] * jax.nn.sigmoid(g[:, BRANCH_W:])


def _s5_call(h, n, t, h0r, h0i, sp):
    tc = min(t, 256)
    nc = t // tc
    const = lambda shape: pl.BlockSpec(shape, lambda i, c: (0,) * len(shape))
    state = pl.BlockSpec((None, 1, S5_LANES), lambda i, c: (i, 0, 0))
    return pl.pallas_call(
        functools.partial(_s5_kernel, tc=tc),
        grid=(n, nc),
        in_specs=[pl.BlockSpec((tc, BRANCH_W), lambda i, c: (i * nc + c, PROJ_OFF['a_u'] // BRANCH_W)),
                  state, state,
                  const((1, S5_LANES)), const((1, S5_LANES)),
                  const((BRANCH_W, S5_LANES)), const((BRANCH_W, S5_LANES)),
                  const((S5_LANES, BRANCH_W)), const((S5_LANES, BRANCH_W)),
                  const((1, BRANCH_W)), const((BRANCH_W, 2 * BRANCH_W))],
        out_specs=[pl.BlockSpec((tc, BRANCH_W), lambda i, c: (i * nc + c, 0)), state, state],
        out_shape=[jax.ShapeDtypeStruct((n * t, BRANCH_W), F32),
                   jax.ShapeDtypeStruct((n, 1, S5_LANES), F32),
                   jax.ShapeDtypeStruct((n, 1, S5_LANES), F32)],
        scratch_shapes=[pltpu.VMEM((tc, S5_LANES), F32), pltpu.VMEM((tc, S5_LANES), F32),
                        pltpu.VMEM((1, S5_LANES), F32), pltpu.VMEM((1, S5_LANES), F32)],
        compiler_params=_params("parallel", "arbitrary"),
        name="s5_branch",
    )(h, h0r, h0i, sp['ar'], sp['ai'], sp['bbr'], sp['bbi'], sp['ccr'], sp['cci'], sp['d'], sp['wglu'])


def _s5_prepare(a_re, a_im, log_dt, b_re, b_im, c_re, c_im, d, w_glu):
    lam_r = jnp.minimum(a_re, -1e-4)
    lam_i = a_im
    dt = jnp.exp(log_dt)[:, None]
    mag = jnp.exp(lam_r * dt)
    ar = mag * jnp.cos(lam_i * dt)
    ai = mag * jnp.sin(lam_i * dt)
    den = lam_r * lam_r + lam_i * lam_i
    nr = ar - 1.0
    cr = (nr * lam_r + ai * lam_i) / den
    ci = (ai * lam_r - nr * lam_i) / den
    bb_r = cr[..., None] * b_re - ci[..., None] * b_im
    bb_i = cr[..., None] * b_im + ci[..., None] * b_re
    eye = jnp.eye(S5_GROUPS, dtype=F32)

    def blockdiag_in(bb):
        m = eye[:, None, :, None] * jnp.swapaxes(bb, 1, 2)[:, :, None, :]
        return m.reshape(BRANCH_W, S5_LANES).astype(BF16)

    def blockdiag_out(c):
        m = eye[:, None, :, None] * jnp.swapaxes(c, 1, 2)[:, :, None, :]
        return m.reshape(S5_LANES, BRANCH_W).astype(BF16)

    return {'ar': ar.reshape(1, S5_LANES), 'ai': ai.reshape(1, S5_LANES),
            'bbr': blockdiag_in(bb_r), 'bbi': blockdiag_in(bb_i),
            'ccr': blockdiag_out(c_re), 'cci': blockdiag_out(c_im),
            'd': d.reshape(1, BRANCH_W), 'wglu': w_glu.astype(BF16)}


def _sortable(x):
    bits = lax.bitcast_convert_type(x, jnp.int32)
    key = bits ^ ((bits >> 31) & jnp.int32(0x7FFFFFFF))
    return jnp.where(key == -1, 0, key)


def _dsa_kernel(q_ref, qi_ref, misc_ref, hk_ref, near_ref, far_ref, triu_ref, o_ref,
                sc_s, acc_s, m_s, l_s, *, qb, b_off, n_keys, topk):
    b = pl.program_id(1) + b_off
    nkb = b + 1
    lane = _iota((qb, LANE), 1)
    low_half = lane < DSA_HEAD_DIM
    qpos = b * Q_BLOCK + _iota((qb, LANE), 0)
    limit = jnp.minimum(((qpos >> 6) + 1) * CHUNK, n_keys)

    def allowed_mask(j):
        return (j * KEY_BLOCK + lane) < limit

    misc = misc_ref[...]
    qi = qi_ref[...]
    qi_m, wi_b = [], []
    for hd in range(IDX_HEADS):
        pair = qi[:, (hd // 2) * LANE:(hd // 2 + 1) * LANE]
        qi_m.append(jnp.where(low_half == (hd % 2 == 0), pair, 0.0).astype(BF16))
        w = misc[:, MISC_WI + hd:MISC_WI + hd + 1] * (IDX_DIM ** -0.5 * IDX_HEADS ** -0.5)
        wi_b.append(jnp.broadcast_to(w, (qb, LANE)))

    def score_block(j, carry):
        kid = hk_ref[pl.ds(pl.multiple_of(j * KEY_BLOCK, KEY_BLOCK), KEY_BLOCK), HK_KI:HK_KI + LANE]
        sc = jnp.zeros((qb, LANE), F32)
        for hd in range(IDX_HEADS):
            sc = sc + jnp.maximum(_dot_nt(qi_m[hd], kid), 0.0) * wi_b[hd]
        sc = jnp.where(allowed_mask(j), sc, -jnp.inf)
        sc_s[j] = _sortable(sc)
        return carry

    lax.fori_loop(0, nkb, score_block, 0)

    def count_ge(cand):
        def body(j, acc):
            return acc + jnp.where(sc_s[j] >= cand, 1.0, 0.0)
        acc = lax.fori_loop(0, nkb, body, jnp.zeros((qb, LANE), F32))
        return jnp.broadcast_to(jnp.sum(acc, axis=1, keepdims=True), (qb, LANE))

    kf = float(topk)
    res = jnp.where(count_ge(jnp.zeros((qb, LANE), jnp.int32)) >= kf, 0, jnp.iinfo(jnp.int32).min)

    def bit_step(i, res):
        cand = res + (jnp.int32(1) << (30 - i))
        return jnp.where(count_ge(cand) >= kf, cand, res)

    thr = lax.fori_loop(0, 31, bit_step, res)
    need = kf - count_ge(thr + 1)

    q = q_ref[...] * (DSA_HEAD_DIM ** -0.5)
    q_m = []
    for hd in range(DSA_HEADS):
        pair = q[:, (hd // 2) * LANE:(hd // 2 + 1) * LANE]
        q_m.append(jnp.where(low_half == (hd % 2 == 0), pair, 0.0).astype(BF16))
    acc_s[...] = jnp.zeros_like(acc_s)
    m_s[...] = jnp.full_like(m_s, -1e30)
    l_s[...] = jnp.zeros_like(l_s)
    triu = triu_ref[...]

    def attend(j, ties, bias_of):
        keys = sc_s[j]
        eq = keys == thr
        prefix = _dot(jnp.where(eq, 1.0, 0.0).astype(BF16), triu) + ties
        sel = ((keys > thr) | (eq & (prefix <= need))) & allowed_mask(j)
        row0 = pl.multiple_of(j * KEY_BLOCK, KEY_BLOCK)
        for hd in range(DSA_HEADS):
            grp = hd // (DSA_HEADS // DSA_KV_HEADS)
            kd = hk_ref[pl.ds(row0, KEY_BLOCK), grp * LANE:(grp + 1) * LANE]
            vd = hk_ref[pl.ds(row0, KEY_BLOCK), HK_V + grp * LANE:HK_V + (grp + 1) * LANE]
            logits = _dot_nt(q_m[hd], kd) + bias_of(hd)
            m_old = m_s[hd]
            m_new = jnp.maximum(m_old, jnp.max(jnp.where(sel, logits, -1e30), axis=1, keepdims=True))
            p = jnp.where(sel, jnp.exp(logits - m_new), 0.0)
            alpha = jnp.exp(m_old - m_new)
            l_s[hd] = alpha * l_s[hd] + jnp.sum(p, axis=1, keepdims=True)
            acc_s[hd] = alpha * acc_s[hd] + _dot(p.astype(BF16), vd)
            m_s[hd] = m_new
        return jnp.broadcast_to(prefix[:, LANE - 1:LANE], (qb, LANE))

    far_bias = lambda hd: far_ref[hd:hd + 1, :]
    ties = lax.fori_loop(0, jnp.maximum(nkb - 2, 0), lambda j, t: attend(j, t, far_bias),
                         jnp.zeros((qb, LANE), F32))
    @pl.when(nkb >= 2)
    def _():
        t2 = attend(nkb - 2, ties, lambda hd: near_ref[hd, 1, 0:qb, :])
        attend(nkb - 1, t2, lambda hd: near_ref[hd, 0, 0:qb, :])

    @pl.when(nkb < 2)
    def _():
        attend(nkb - 1, ties, lambda hd: near_ref[hd, 0, 0:qb, :])

    for pr in range(DSA_HEADS // 2):
        o0 = acc_s[2 * pr] / l_s[2 * pr]
        o1 = acc_s[2 * pr + 1] / l_s[2 * pr + 1]
        o_ref[:, pr * LANE:(pr + 1) * LANE] = jnp.where(low_half, o0, o1)


def _dsa_call(h, hk, n, t, near, far, triu, *, b_off, n_keys, topk):
    qb = min(t, Q_BLOCK)
    nqb = t // qb
    lp = hk.shape[1]
    nkb_max = lp // KEY_BLOCK
    wq = DSA_HEADS * DSA_HEAD_DIM
    wqi = IDX_HEADS * IDX_DIM
    return pl.pallas_call(
        functools.partial(_dsa_kernel, qb=qb, b_off=b_off, n_keys=n_keys, topk=topk),
        grid=(n, nqb),
        in_specs=[pl.BlockSpec((qb, wq), lambda i, c: (i * nqb + c, PROJ_OFF['b_q'] // wq)),
                  pl.BlockSpec((qb, wqi), lambda i, c: (i * nqb + c, PROJ_OFF['b_qi'] // wqi)),
                  pl.BlockSpec((qb, LANE), lambda i, c: (i * nqb + c, OFF_MISC // LANE)),
                  pl.BlockSpec((None, lp, HK_W), lambda i, c: (i, 0, 0)),
                  pl.BlockSpec((DSA_HEADS, 2, Q_BLOCK, LANE), lambda i, c: (0, 0, 0, 0)),
                  pl.BlockSpec((DSA_HEADS, LANE), lambda i, c: (0, 0)),
                  pl.BlockSpec((KEY_BLOCK, LANE), lambda i, c: (0, 0))],
        out_specs=pl.BlockSpec((qb, wq), lambda i, c: (i * nqb + c, 0)),
        out_shape=jax.ShapeDtypeStruct((n * t, wq), F32),
        scratch_shapes=[pltpu.VMEM((nkb_max, qb, LANE), jnp.int32),
                        pltpu.VMEM((DSA_HEADS, qb, LANE), F32),
                        pltpu.VMEM((DSA_HEADS, qb, 1), F32),
                        pltpu.VMEM((DSA_HEADS, qb, 1), F32)],
        compiler_params=_params("parallel", "arbitrary"),
        name="dsa_branch",
    )(h, h, h, hk, near, far, triu)


def _t5_bucket(rel):
    nb = REL_BUCKETS // 2
    max_exact = nb // 2
    ret = jnp.where(rel > 0, nb, 0)
    dist = jnp.abs(rel)
    distf = jnp.maximum(dist, 1).astype(F32)
    large = max_exact + (jnp.log(distf / max_exact) / math.log(REL_MAX_DIST / max_exact)
                         * (nb - max_exact)).astype(jnp.int32)
    large = jnp.minimum(large, nb - 1)
    return ret + jnp.where(dist < max_exact, dist, large)


def _dsa_bias_tables(rel_bias):
    qr = jnp.arange(Q_BLOCK, dtype=jnp.int32)[:, None]
    lc = jnp.arange(KEY_BLOCK, dtype=jnp.int32)[None, :]
    rel = jnp.stack([lc - qr, lc - qr - KEY_BLOCK], axis=0)
    near = jnp.moveaxis(rel_bias[_t5_bucket(rel)].astype(F32), -1, 0)
    far = rel_bias[_t5_bucket(jnp.int32(-2 * KEY_BLOCK))].astype(F32)
    return near, jnp.broadcast_to(far[:, None], (DSA_HEADS, LANE))


def _gdn_kernel(raw_ref, misc_ref, s0_ref, conv0_ref, convw_ref, alog_ref, dtb_ref, norm_ref,
                o_ref, s_out, conv_out, xp_s, st_s, *, c):
    @pl.when(pl.program_id(1) == 0)
    def _():
        xp_s[0:8, :] = conv0_ref[...]
        st_s[...] = s0_ref[...]

    raw = raw_ref[...]
    xp_s[8:8 + c, :] = raw
    w = convw_ref[...]
    y = raw * w[CONV_W - 1:CONV_W, :]
    for i in range(CONV_W - 1):
        y = y + xp_s[5 + i:5 + i + c, :] * w[i:i + 1, :]
    tail = xp_s[c:c + 8, :]
    xp_s[0:8, :] = tail
    conv_out[...] = tail
    qkv = jax.nn.silu(y)

    misc = misc_ref[...]
    beta_all = jax.nn.sigmoid(misc)
    g_all = -jnp.exp(alog_ref[...]) * _softplus(misc + dtb_ref[...])
    row = _iota((c, c), 0)
    col = _iota((c, c), 1)
    tril = jnp.where(row >= col, 1.0, 0.0)
    eye = row == col
    strict = row > col
    gc_all = _dot(tril, g_all, precision=HIGHEST)
    steps = int(math.log2(c))
    for hd in range(GDN_HEADS):
        qh = qkv[:, hd * GDN_DK:(hd + 1) * GDN_DK]
        kh = qkv[:, GDN_QK + hd * GDN_DK:GDN_QK + (hd + 1) * GDN_DK]
        vh = qkv[:, 2 * GDN_QK + hd * GDN_DV:2 * GDN_QK + (hd + 1) * GDN_DV]
        qh = qh * lax.rsqrt(jnp.sum(qh * qh, axis=1, keepdims=True) + RMS_EPS) * (GDN_DK ** -0.5)
        kh = kh * lax.rsqrt(jnp.sum(kh * kh, axis=1, keepdims=True) + RMS_EPS)
        beta = beta_all[:, MISC_BETA + hd:MISC_BETA + hd + 1]
        gc = gc_all[:, MISC_A + hd:MISC_A + hd + 1]
        gc_b = jnp.broadcast_to(gc, (c, c))
        gc_row = jnp.sum(jnp.where(eye, gc_b, 0.0), axis=0, keepdims=True)
        diff = gc_b - gc_row
        dec_strict = jnp.where(strict, jnp.exp(jnp.where(strict, diff, 0.0)), 0.0)
        dec_incl = jnp.where(row >= col, jnp.exp(jnp.where(row >= col, diff, 0.0)), 0.0)
        kb = kh.astype(BF16)
        qb = qh.astype(BF16)
        a = beta * _dot_nt(kb, kb) * dec_strict
        pw = -a
        inv = jnp.where(eye, 1.0, 0.0) + pw
        for _ in range(steps - 1):
            pw = _dot(pw, pw, precision=HIGHEST)
            inv = inv + _dot(inv, pw, precision=HIGHEST)
        egc = jnp.exp(gc)
        rhs = jnp.concatenate([(beta * egc) * kh, beta * vh], axis=1)
        sol = _dot(inv, rhs, precision=HIGHEST)
        w_blk = sol[:, :GDN_DK]
        u_blk = sol[:, GDN_DK:]
        s = st_s[hd]
        sb = s.astype(BF16)
        v_new = u_blk - _dot(w_blk.astype(BF16), sb)
        vb = v_new.astype(BF16)
        o = egc * _dot(qb, sb) + _dot((_dot_nt(qb, kb) * dec_incl).astype(BF16), vb)
        g_last = gc[c - 1:c, :]
        st_s[hd] = jnp.exp(g_last) * s + _dot_tn((kh * jnp.exp(g_last - gc)).astype(BF16), vb)
        o = o * lax.rsqrt(jnp.mean(o * o, axis=1, keepdims=True) + RMS_EPS) * norm_ref[...]
        o_ref[:, hd * GDN_DV:(hd + 1) * GDN_DV] = o
    s_out[...] = st_s[...]


def _gdn_call(h, n, t, s0, conv0, gp):
    c = min(t, CHUNK)
    nc = t // c
    const = lambda shape: pl.BlockSpec(shape, lambda i, j: (0,) * len(shape))
    state = pl.BlockSpec((None, GDN_HEADS, GDN_DK, GDN_DV), lambda i, j: (i, 0, 0, 0))
    conv = pl.BlockSpec((None, 8, GDN_CONV_CH), lambda i, j: (i, 0, 0))
    return pl.pallas_call(
        functools.partial(_gdn_kernel, c=c),
        grid=(n, nc),
        in_specs=[pl.BlockSpec((c, GDN_CONV_CH), lambda i, j: (i * nc + j, PROJ_OFF['c_qkv'] // GDN_CONV_CH)),
                  pl.BlockSpec((c, LANE), lambda i, j: (i * nc + j, OFF_MISC // LANE)),
                  state, conv,
                  const((CONV_W, GDN_CONV_CH)), const((1, LANE)), const((1, LANE)), const((1, GDN_DV))],
        out_specs=[pl.BlockSpec((c, GDN_VW), lambda i, j: (i * nc + j, 0)), state, conv],
        out_shape=[jax.ShapeDtypeStruct((n * t, GDN_VW), F32),
                   jax.ShapeDtypeStruct((n, GDN_HEADS, GDN_DK, GDN_DV), F32),
                   jax.ShapeDtypeStruct((n, 8, GDN_CONV_CH), F32)],
        scratch_shapes=[pltpu.VMEM((c + 8, GDN_CONV_CH), F32),
                        pltpu.VMEM((GDN_HEADS, GDN_DK, GDN_DV), F32)],
        compiler_params=_params("parallel", "arbitrary"),
        name="gdn_branch",
    )(h, h, s0, conv0, gp['convw'], gp['alog'], gp['dtb'], gp['norm'])


def _lane_place(v, off):
    return jnp.zeros((1, LANE), F32).at[0, off:off + v.shape[0]].set(v.astype(F32))


def _gla_kernel(qk_ref, v_ref, misc_ref, s0_ref, wg_ref, wgt_ref, bg_ref, bgc_ref, norm_ref, hexp_ref, gsum_ref,
                o_ref, s_out, st_s, *, tc):
    blk = GLA_BLOCK
    wk = GLA_HEADS * GLA_DK

    @pl.when(pl.program_id(1) == 0)
    def _():
        st_s[...] = s0_ref[...]

    row = _iota((blk, blk), 0)
    col = _iota((blk, blk), 1)
    tril = jnp.where(row >= col, 1.0, 0.0)
    jrow = _iota((blk, wk), 0)
    lane_k = _iota((blk, wk), 1)
    head_mask = [(lane_k >= hd * GLA_DK) & (lane_k < (hd + 1) * GLA_DK) for hd in range(GLA_HEADS)]

    def block(sb, carry):
        r0 = pl.multiple_of(sb * blk, blk)
        qk = qk_ref[pl.ds(r0, blk), :]
        q = qk[:, :wk] * (GLA_DK ** -0.5)
        k = qk[:, wk:]
        v = v_ref[pl.ds(r0, blk), :]
        mb = misc_ref[pl.ds(r0, blk), :].astype(BF16)
        lg = jax.nn.log_sigmoid(_dot(mb, wg_ref[...]) + bg_ref[...]) * (1.0 / GLA_TAU)
        lg_t = jax.nn.log_sigmoid(_dot_nt(wgt_ref[...], mb) + bgc_ref[:, 0:blk]) * (1.0 / GLA_TAU)
        bc = _dot(tril, lg, precision=HIGHEST)
        slabs = []
        for i in range(blk):
            keep = jrow <= i
            dec = jnp.where(keep, jnp.exp(jnp.where(keep, bc[i:i + 1, :] - bc, 0.0)), 0.0)
            slabs.append(dec * q[i:i + 1, :] * k)
        dmat = jnp.concatenate(slabs, axis=0).astype(BF16)
        a_exp = _dot(dmat, hexp_ref[...])
        prod = a_exp * jnp.concatenate([v] * blk, axis=0)
        o = _dot(gsum_ref[...], prod, precision=HIGHEST)
        qe = q * jnp.exp(bc)
        b_last = bc[blk - 1:blk, :]
        kt = k * jnp.exp(b_last - bc)
        s = st_s[...]
        sbf = s.astype(BF16)
        upd = jnp.zeros_like(s)
        parts = []
        for hd in range(GLA_HEADS):
            parts.append(_dot(jnp.where(head_mask[hd], qe, 0.0).astype(BF16), sbf))
            vh = v[:, hd * GLA_DV:(hd + 1) * GLA_DV].astype(BF16)
            upd = upd + _dot_tn(jnp.where(head_mask[hd], kt, 0.0).astype(BF16), vh)
        o = o + jnp.concatenate(parts, axis=1)
        decay = jnp.exp(jnp.sum(lg_t, axis=1, keepdims=True))
        st_s[...] = decay * s + upd
        for hd in range(GLA_HEADS):
            oh = o[:, hd * GLA_DV:(hd + 1) * GLA_DV]
            oh = oh * lax.rsqrt(jnp.mean(oh * oh, axis=1, keepdims=True) + RMS_EPS) * norm_ref[...]
            o_ref[pl.ds(r0, blk), hd * GLA_DV:(hd + 1) * GLA_DV] = oh
        return carry

    lax.fori_loop(0, tc // blk, block, 0)
    s_out[...] = st_s[...]


def _gla_call(h, n, t, s0, gp):
    tc = min(t, 128)
    nc = t // tc
    wk = GLA_HEADS * GLA_DK
    wv = GLA_HEADS * GLA_DV
    const = lambda shape: pl.BlockSpec(shape, lambda i, j: (0,) * len(shape))
    state = pl.BlockSpec((None, wk, GLA_DV), lambda i, j: (i, 0, 0))
    return pl.pallas_call(
        functools.partial(_gla_kernel, tc=tc),
        grid=(n, nc),
        in_specs=[pl.BlockSpec((tc, 2 * wk), lambda i, j: (i * nc + j, PROJ_OFF['d_q'] // (2 * wk))),
                  pl.BlockSpec((tc, wv), lambda i, j: (i * nc + j, PROJ_OFF['d_v'] // wv)),
                  pl.BlockSpec((tc, LANE), lambda i, j: (i * nc + j, OFF_MISC // LANE)),
                  state,
                  const((LANE, wk)), const((wk, LANE)), const((1, wk)), const((wk, LANE)), const((1, GLA_DV)),
                  const((wk, wv)), const((GLA_BLOCK, GLA_BLOCK * GLA_BLOCK))],
        out_specs=[pl.BlockSpec((tc, wv), lambda i, j: (i * nc + j, 0)), state],
        out_shape=[jax.ShapeDtypeStruct((n * t, wv), F32),
                   jax.ShapeDtypeStruct((n, wk, GLA_DV), F32)],
        scratch_shapes=[pltpu.VMEM((wk, GLA_DV), F32)],
        compiler_params=_params("parallel", "arbitrary"),
        name="gla_branch",
    )(h, h, h, s0, gp['wg'], gp['wgt'], gp['bg'], gp['bgc'], gp['norm'], gp['hexp'], gp['gsum'])


def _gla_prepare(w_g2, b_g, norm):
    wk = GLA_HEADS * GLA_DK
    wg = jnp.zeros((LANE, wk), F32).at[MISC_G:MISC_G + GLA_GATE_RANK].set(w_g2).astype(BF16)
    ck = jnp.arange(wk)[:, None] // GLA_DK
    cv = jnp.arange(GLA_HEADS * GLA_DV)[None, :] // GLA_DV
    ij = jnp.arange(GLA_BLOCK * GLA_BLOCK)[None, :] // GLA_BLOCK
    return {'wg': wg, 'wgt': wg.T, 'bg': b_g.reshape(1, wk).astype(F32),
            'bgc': jnp.broadcast_to(b_g.astype(F32)[:, None], (wk, LANE)),
            'norm': norm.reshape(1, GLA_DV).astype(F32),
            'hexp': (ck == cv).astype(BF16),
            'gsum': (jnp.arange(GLA_BLOCK)[:, None] == ij).astype(F32)}


def _merge_kernel(x_ref, ya_ref, yb_ref, yc_ref, yd_ref, ga_ref, gb_ref, gc_ref, gd_ref, mg_ref,
                  wb_ref, wo_ref, lng_ref, lnb_ref, o_ref):
    mixed = None
    for br, (y_ref, g_ref) in enumerate(((ya_ref, ga_ref), (yb_ref, gb_ref), (yc_ref, gc_ref), (yd_ref, gd_ref))):
        act = (y_ref[...] * jax.nn.silu(g_ref[...])).astype(BF16)
        term = jax.nn.sigmoid(mg_ref[:, br * D_MODEL:(br + 1) * D_MODEL]) * _dot(act, wb_ref[br])
        mixed = term if mixed is None else mixed + term
    z = DN_ALPHA * x_ref[...] + _dot(mixed.astype(BF16), wo_ref[...])
    mu = jnp.mean(z, axis=1, keepdims=True)
    zc = z - mu
    var = jnp.mean(zc * zc, axis=1, keepdims=True)
    o_ref[...] = zc * lax.rsqrt(var + LN_EPS) * lng_ref[...] + lnb_ref[...]


def _merge_call(x, h, ya, yb, yc, yd, wb, wo, lng, lnb):
    rows = x.shape[0]
    tm = min(rows, 256)
    bw = BRANCH_W
    ybs = pl.BlockSpec((tm, bw), lambda i: (i, 0))
    gate = lambda name: pl.BlockSpec((tm, bw), lambda i: (i, PROJ_OFF[name] // bw))
    return pl.pallas_call(
        _merge_kernel,
        grid=(rows // tm,),
        in_specs=[pl.BlockSpec((tm, D_MODEL), lambda i: (i, 0)), ybs, ybs, ybs, ybs,
                  gate('a_gate'), gate('b_gate'), gate('c_gate'), gate('d_gate'),
                  pl.BlockSpec((tm, N_BRANCH * D_MODEL), lambda i: (i, 0)),
                  pl.BlockSpec((N_BRANCH, bw, D_MODEL), lambda i: (0, 0, 0)),
                  pl.BlockSpec((D_MODEL, D_MODEL), lambda i: (0, 0)),
                  pl.BlockSpec((1, D_MODEL), lambda i: (0, 0)),
                  pl.BlockSpec((1, D_MODEL), lambda i: (0, 0))],
        out_specs=pl.BlockSpec((tm, D_MODEL), lambda i: (i, 0)),
        out_shape=jax.ShapeDtypeStruct((rows, D_MODEL), F32),
        compiler_params=_params("parallel"),
        name="merge_norm",
    )(x, ya, yb, yc, yd, h, h, h, h, h, wb, wo, lng, lnb)


def _split_w_in(w_in):
    out, off = {}, 0
    for name, width in IN_LAYOUT:
        out[name] = w_in[:, off:off + width]
        off += width
    return out


def _layer_weights(w_in):
    cols = _split_w_in(w_in)
    w_proj = jnp.concatenate([cols[name] for name in PROJ_ORDER]
                             + [jnp.zeros((D_MODEL, MISC_PAD), w_in.dtype)], axis=1).astype(BF16)
    k, v, ki = cols['b_k'], cols['b_v'], cols['b_ki']
    dh = DSA_HEAD_DIM
    w_key = jnp.concatenate([k[:, :dh], k[:, :dh], k[:, dh:], k[:, dh:],
                             v[:, :dh], v[:, :dh], v[:, dh:], v[:, dh:], ki, ki], axis=1).astype(BF16)
    return w_proj, w_key


def _dup_keys(k, v, ki):
    return jnp.concatenate([k[:, :, 0], k[:, :, 0], k[:, :, 1], k[:, :, 1],
                            v[:, :, 0], v[:, :, 0], v[:, :, 1], v[:, :, 1], ki, ki], axis=-1).astype(BF16)


def _layer(x, lw, st):
    n, t, _ = x.shape
    x2 = x.reshape(n * t, D_MODEL)
    h = _matmul(x2, lw['w_proj'], F32, tn=PROJ_W // 9)
    hk_new = _matmul(x2, lw['w_key'], BF16, tn=HK_W).reshape(n, t, HK_W)
    if st is None:
        hk, n_keys, b_off = hk_new, t, 0
        s5r = jnp.zeros((n, 1, S5_LANES), F32)
        s5i = jnp.zeros((n, 1, S5_LANES), F32)
        gdn0 = jnp.zeros((n, GDN_HEADS, GDN_DK, GDN_DV), F32)
        conv0 = jnp.zeros((n, 8, GDN_CONV_CH), F32)
        gla0 = jnp.zeros((n, GLA_HEADS * GLA_DK, GLA_DV), F32)
    else:
        past = st['k'].shape[1]
        n_keys = past + t
        b_off = past // Q_BLOCK
        lp = -(-n_keys // KEY_BLOCK) * KEY_BLOCK
        hk = jnp.concatenate([_dup_keys(st['k'], st['v'], st['kidx']), hk_new,
                              jnp.zeros((n, lp - n_keys, HK_W), BF16)], axis=1)
        s5r = st['s5_re'].reshape(n, 1, S5_LANES)
        s5i = st['s5_im'].reshape(n, 1, S5_LANES)
        gdn0 = st['gdn']
        conv0 = jnp.concatenate([jnp.zeros((n, 8 - (CONV_W - 1), GDN_CONV_CH), F32), st['gdn_conv']], axis=1)
        gla0 = st['gla'].reshape(n, GLA_HEADS * GLA_DK, GLA_DV)
    topk = min(DSA_TOPK_MAX, n_keys // 4)

    ya, s5r_new, s5i_new = _s5_call(h, n, t, s5r, s5i, lw['s5'])
    yb = _dsa_call(h, hk, n, t, lw['near'], lw['far'], lw['triu'], b_off=b_off, n_keys=n_keys, topk=topk)
    yc, gdn_new, conv_new = _gdn_call(h, n, t, gdn0, conv0, lw['gdn'])
    yd, gla_new = _gla_call(h, n, t, gla0, lw['gla'])
    y = _merge_call(x2, h, ya, yb, yc, yd, lw['w_branch'], lw['w_out'], lw['ln_g'], lw['ln_b'])

    def cols(name, width):
        return h[:, PROJ_OFF[name]:PROJ_OFF[name] + width].reshape(n, t, width)

    new = {'k': cols('b_k', DSA_KV_HEADS * DSA_HEAD_DIM).reshape(n, t, DSA_KV_HEADS, DSA_HEAD_DIM),
           'v': cols('b_v', DSA_KV_HEADS * DSA_HEAD_DIM).reshape(n, t, DSA_KV_HEADS, DSA_HEAD_DIM),
           'kidx': cols('b_ki', IDX_DIM),
           's5_re': s5r_new.reshape(n, S5_GROUPS, S5_STATE), 's5_im': s5i_new.reshape(n, S5_GROUPS, S5_STATE),
           'gdn': gdn_new, 'gdn_conv': conv_new[:, 8 - (CONV_W - 1):],
           'gla': gla_new.reshape(n, GLA_HEADS, GLA_DK, GLA_DV)}
    return y.reshape(n, t, D_MODEL), new


def kernel(x_prompt, x_sample, cache_k, cache_v, cache_kidx, state_s5_re, state_s5_im, state_gdn, state_gdn_conv, state_gla, w_in, w_branch, w_out, ln_g, ln_b, rel_bias, s5_a_re, s5_a_im, s5_log_dt, s5_b_re, s5_b_im, s5_c_re, s5_c_im, s5_d, s5_w_glu, gdn_conv, gdn_a_log, gdn_dt_bias, gdn_norm, gla_w_g2, gla_b_g, gla_norm):
    near, far = _dsa_bias_tables(rel_bias)
    triu = (jnp.arange(KEY_BLOCK)[:, None] <= jnp.arange(LANE)[None, :]).astype(BF16)
    yp, ys = x_prompt, x_sample
    new_p, new_s = [], []
    for l in range(DEPTH):
        w_proj, w_key = _layer_weights(w_in[l])
        lw = {'w_proj': w_proj, 'w_key': w_key, 'near': near, 'far': far, 'triu': triu,
              'w_branch': w_branch[l].astype(BF16), 'w_out': w_out[l].astype(BF16),
              'ln_g': ln_g[l].reshape(1, D_MODEL), 'ln_b': ln_b[l].reshape(1, D_MODEL),
              's5': _s5_prepare(s5_a_re[l], s5_a_im[l], s5_log_dt[l], s5_b_re[l], s5_b_im[l],
                                s5_c_re[l], s5_c_im[l], s5_d[l], s5_w_glu[l]),
              'gdn': {'convw': gdn_conv[l], 'alog': _lane_place(gdn_a_log[l], MISC_A),
                      'dtb': _lane_place(gdn_dt_bias[l], MISC_A), 'norm': gdn_norm[l].reshape(1, GDN_DV)},
              'gla': _gla_prepare(gla_w_g2[l], gla_b_g[l], gla_norm[l])}
        st = {'k': cache_k[l], 'v': cache_v[l], 'kidx': cache_kidx[l], 's5_re': state_s5_re[l],
              's5_im': state_s5_im[l], 'gdn': state_gdn[l], 'gdn_conv': state_gdn_conv[l], 'gla': state_gla[l]}
        yp, stp = _layer(yp, lw, None)
        ys, sts = _layer(ys, lw, st)
        new_p.append(stp)
        new_s.append(sts)
    names = ('k', 'v', 'kidx', 's5_re', 's5_im', 'gdn', 'gdn_conv', 'gla')
    stack = lambda states, name: jnp.stack([s[name] for s in states], axis=0)
    return ((yp, ys) + tuple(stack(new_p, nm) for nm in names) + tuple(stack(new_s, nm) for nm in names))
```

```python
import functools
import math

import jax
import jax.numpy as jnp
from jax import lax
from jax.experimental import pallas as pl
from jax.experimental.pallas import tpu as pltpu

F32 = jnp.float32
BF16 = jnp.bfloat16
HIGHEST = lax.Precision.HIGHEST

D_MODEL = 1024
DEPTH = 2
CHUNK = 64
N_BRANCH = 4
BRANCH_W = 512
S5_GROUP = 16
S5_GROUPS = BRANCH_W // S5_GROUP
S5_STATE = 64
S5_LANES = S5_GROUPS * S5_STATE
DSA_HEADS = 8
DSA_KV_HEADS = 2
DSA_HEAD_DIM = 64
IDX_HEADS = 4
IDX_DIM = 64
DSA_TOPK_MAX = 256
REL_BUCKETS = 32
REL_MAX_DIST = 128
GDN_HEADS = 4
GDN_DK = 128
GDN_DV = 128
GDN_QK = GDN_HEADS * GDN_DK
GDN_VW = GDN_HEADS * GDN_DV
CONV_W = 4
GDN_CONV_CH = 2 * GDN_QK + GDN_VW
GLA_HEADS = 4
GLA_DK = 64
GLA_DV = 128
GLA_GATE_RANK = 16
GLA_TAU = 16.0
GLA_BLOCK = 16
LN_EPS = 1e-5
RMS_EPS = 1e-6
DN_ALPHA = (2 * DEPTH) ** 0.25

LANE = 128
KEY_BLOCK = 128
KEY_SUB = 4
KEY_TILE = KEY_SUB * KEY_BLOCK
FOLD_ROWS = 64
Q_BLOCK = 128
VMEM_LIMIT = 56 * 1024 * 1024

IN_LAYOUT = (
    ('a_u', BRANCH_W), ('a_gate', BRANCH_W),
    ('b_q', DSA_HEADS * DSA_HEAD_DIM), ('b_k', DSA_KV_HEADS * DSA_HEAD_DIM),
    ('b_v', DSA_KV_HEADS * DSA_HEAD_DIM), ('b_qi', IDX_HEADS * IDX_DIM), ('b_ki', IDX_DIM),
    ('b_wi', IDX_HEADS), ('b_gate', BRANCH_W),
    ('c_qkv', GDN_CONV_CH), ('c_beta', GDN_HEADS), ('c_a', GDN_HEADS), ('c_gate', BRANCH_W),
    ('d_q', GLA_HEADS * GLA_DK), ('d_k', GLA_HEADS * GLA_DK), ('d_v', GLA_HEADS * GLA_DV),
    ('d_g', GLA_GATE_RANK), ('d_gate', BRANCH_W),
    ('merge', N_BRANCH * D_MODEL),
)

PROJ_ORDER = ('merge', 'a_u', 'a_gate', 'b_q', 'b_qi', 'b_k', 'b_v', 'c_qkv', 'b_gate', 'c_gate',
              'd_v', 'd_gate', 'd_q', 'd_k', 'b_ki', 'b_wi', 'c_beta', 'c_a', 'd_g')
MISC_WI = IDX_DIM
MISC_BETA = MISC_WI + IDX_HEADS
MISC_A = MISC_BETA + GDN_HEADS
MISC_G = MISC_A + GDN_HEADS
MISC_PAD = LANE - (MISC_G + GLA_GATE_RANK)


def _proj_offsets():
    widths = dict(IN_LAYOUT)
    off, out = 0, {}
    for name in PROJ_ORDER:
        out[name] = off
        off += widths[name]
    return out, off + MISC_PAD


PROJ_OFF, PROJ_W = _proj_offsets()
OFF_MISC = PROJ_OFF['b_ki']
HK_W = 5 * LANE
HK_V = 2 * LANE
HK_KI = 4 * LANE


def _dot(a, b, precision=None):
    return jnp.dot(a, b, preferred_element_type=F32, precision=precision)


def _dot_nt(a, b, precision=None):
    return lax.dot_general(a, b, (((1,), (1,)), ((), ())), preferred_element_type=F32, precision=precision)


def _dot_tn(a, b, precision=None):
    return lax.dot_general(a, b, (((0,), (0,)), ((), ())), preferred_element_type=F32, precision=precision)


def _params(*sem):
    return pltpu.CompilerParams(dimension_semantics=sem, vmem_limit_bytes=VMEM_LIMIT)


def _softplus(x):
    return jnp.maximum(x, 0.0) + jnp.log1p(jnp.exp(-jnp.abs(x)))


def _iota(shape, dim):
    return lax.broadcasted_iota(jnp.int32, shape, dim)


def _mm_kernel(x_ref, w_ref, o_ref):
    o_ref[...] = _dot(x_ref[...].astype(BF16), w_ref[...]).astype(o_ref.dtype)


def _matmul(x, w, out_dtype, tn):
    rows, kdim = x.shape
    width = w.shape[1]
    tm = min(rows, 1024)
    return pl.pallas_call(
        _mm_kernel,
        grid=(rows // tm, width // tn),
        in_specs=[pl.BlockSpec((tm, kdim), lambda i, j: (i, 0)),
                  pl.BlockSpec((kdim, tn), lambda i, j: (0, j))],
        out_specs=pl.BlockSpec((tm, tn), lambda i, j: (i, j)),
        out_shape=jax.ShapeDtypeStruct((rows, width), out_dtype),
        compiler_params=_params("parallel", "parallel"),
        name="proj_matmul",
    )(x, w)


def _s5_kernel(u_ref, h0r_ref, h0i_ref, ar_ref, ai_ref, bbr_ref, bbi_ref, ccr_ref, cci_ref, d_ref, wglu_ref,
               ya_ref, hr_out, hi_out, xr_s, xi_s, cr_s, ci_s, *, tc):
    @pl.when(pl.program_id(1) == 0)
    def _():
        cr_s[...] = h0r_ref[...]
        ci_s[...] = h0i_ref[...]

    u = u_ref[...]
    ub = u.astype(BF16)
    xr_s[...] = _dot(ub, bbr_ref[...])
    xi_s[...] = _dot(ub, bbi_ref[...])
    ar = ar_ref[...]
    ai = ai_ref[...]

    def step(t, carry):
        hr, hi = carry
        nhr = ar * hr - ai * hi + xr_s[pl.ds(t, 1), :]
        nhi = ar * hi + ai * hr + xi_s[pl.ds(t, 1), :]
        xr_s[pl.ds(t, 1), :] = nhr
        xi_s[pl.ds(t, 1), :] = nhi
        return nhr, nhi

    hr, hi = lax.fori_loop(0, tc, step, (cr_s[...], ci_s[...]), unroll=8)
    cr_s[...] = hr
    ci_s[...] = hi
    hr_out[...] = hr
    hi_out[...] = hi
    y = (_dot(xr_s[...].astype(BF16), ccr_ref[...]) - _dot(xi_s[...].astype(BF16), cci_ref[...])
         + d_ref[...] * u)
    y = jax.nn.gelu(y)
    g = _dot(y.astype(BF16), wglu_ref[...])
    ya_ref[...] = g[:, :BRANCH_W] * jax.nn.sigmoid(g[:, BRANCH_W:])


def _s5_call(h, n, t, h0r, h0i, sp):
    tc = min(t, 256)
    nc = t // tc
    const = lambda shape: pl.BlockSpec(shape, lambda i, c: (0,) * len(shape))
    state = pl.BlockSpec((None, 1, S5_LANES), lambda i, c: (i, 0, 0))
    return pl.pallas_call(
        functools.partial(_s5_kernel, tc=tc),
        grid=(n, nc),
        in_specs=[pl.BlockSpec((tc, BRANCH_W), lambda i, c: (i * nc + c, PROJ_OFF['a_u'] // BRANCH_W)),
                  state, state,
                  const((1, S5_LANES)), const((1, S5_LANES)),
                  const((BRANCH_W, S5_LANES)), const((BRANCH_W, S5_LANES)),
                  const((S5_LANES, BRANCH_W)), const((S5_LANES, BRANCH_W)),
                  const((1, BRANCH_W)), const((BRANCH_W, 2 * BRANCH_W))],
        out_specs=[pl.BlockSpec((tc, BRANCH_W), lambda i, c: (i * nc + c, 0)), state, state],
        out_shape=[jax.ShapeDtypeStruct((n * t, BRANCH_W), F32),
                   jax.ShapeDtypeStruct((n, 1, S5_LANES), F32),
                   jax.ShapeDtypeStruct((n, 1, S5_LANES), F32)],
        scratch_shapes=[pltpu.VMEM((tc, S5_LANES), F32), pltpu.VMEM((tc, S5_LANES), F32),
                        pltpu.VMEM((1, S5_LANES), F32), pltpu.VMEM((1, S5_LANES), F32)],
        compiler_params=_params("parallel", "arbitrary"),
        name="s5_branch",
    )(h, h0r, h0i, sp['ar'], sp['ai'], sp['bbr'], sp['bbi'], sp['ccr'], sp['cci'], sp['d'], sp['wglu'])


def _s5_prepare(a_re, a_im, log_dt, b_re, b_im, c_re, c_im, d, w_glu):
    lam_r = jnp.minimum(a_re, -1e-4)
    lam_i = a_im
    dt = jnp.exp(log_dt)[:, None]
    mag = jnp.exp(lam_r * dt)
    ar = mag * jnp.cos(lam_i * dt)
    ai = mag * jnp.sin(lam_i * dt)
    den = lam_r * lam_r + lam_i * lam_i
    nr = ar - 1.0
    cr = (nr * lam_r + ai * lam_i) / den
    ci = (ai * lam_r - nr * lam_i) / den
    bb_r = cr[..., None] * b_re - ci[..., None] * b_im
    bb_i = cr[..., None] * b_im + ci[..., None] * b_re
    eye = jnp.eye(S5_GROUPS, dtype=F32)

    def blockdiag_in(bb):
        m = eye[:, None, :, None] * jnp.swapaxes(bb, 1, 2)[:, :, None, :]
        return m.reshape(BRANCH_W, S5_LANES).astype(BF16)

    def blockdiag_out(c):
        m = eye[:, None, :, None] * jnp.swapaxes(c, 1, 2)[:, :, None, :]
        return m.reshape(S5_LANES, BRANCH_W).astype(BF16)

    return {'ar': ar.reshape(1, S5_LANES), 'ai': ai.reshape(1, S5_LANES),
            'bbr': blockdiag_in(bb_r), 'bbi': blockdiag_in(bb_i),
            'ccr': blockdiag_out(c_re), 'cci': blockdiag_out(c_im),
            'd': d.reshape(1, BRANCH_W), 'wglu': w_glu.astype(BF16)}


def _sortable(x):
    bits = lax.bitcast_convert_type(x, jnp.int32)
    key = bits ^ ((bits >> 31) & jnp.int32(0x7FFFFFFF))
    return jnp.where(key == -1, 0, key)


def _dsa_kernel(q_ref, qi_ref, misc_ref, hk_ref, bias_ref, tril_ref, o_ref, sc_s, acc_s, *, qb, b_off, n_keys, topk):
    b = pl.program_id(1) + b_off
    nkt = (b + KEY_SUB) // KEY_SUB
    fold = KEY_TILE // FOLD_ROWS
    key_row = _iota((KEY_TILE, qb), 0)
    qpos = b * Q_BLOCK + _iota((1, qb), 1)
    limit = jnp.minimum(((qpos >> 6) + 1) * CHUNK, n_keys)
    low_half = _iota((qb, LANE), 1) < DSA_HEAD_DIM

    def key_rows(ref_cols, j):
        return hk_ref[pl.ds(pl.multiple_of(j * KEY_TILE, KEY_TILE), KEY_TILE), ref_cols:ref_cols + LANE]

    def half_masked(x, hd):
        pair = x[:, (hd // 2) * LANE:(hd // 2 + 1) * LANE]
        return jnp.where(low_half == (hd % 2 == 0), pair, 0.0).astype(BF16)

    misc = misc_ref[...]
    pick = jnp.where(_iota((8, LANE), 1) == MISC_WI + _iota((8, LANE), 0), 1.0, 0.0)
    wi_t = _dot_nt(pick, misc, precision=HIGHEST) * (IDX_DIM ** -0.5 * IDX_HEADS ** -0.5)
    qi = qi_ref[...]
    qi_m = [half_masked(qi, hd) for hd in range(IDX_HEADS)]

    def score_block(j, carry):
        kid = key_rows(HK_KI, j)
        sc = jnp.zeros((KEY_TILE, qb), F32)
        for hd in range(IDX_HEADS):
            sc = sc + jnp.maximum(_dot_nt(kid, qi_m[hd]), 0.0) * wi_t[hd:hd + 1, :]
        sc = jnp.where(j * KEY_TILE + key_row < limit, sc, -jnp.inf)
        sc_s[j] = _sortable(sc)
        return carry

    lax.fori_loop(0, nkt, score_block, 0)

    def count_ge(cand):
        def body(j, acc):
            return acc + jnp.where(sc_s[j] >= cand, 1.0, 0.0).reshape(fold, FOLD_ROWS, qb).sum(axis=0)
        acc = lax.fori_loop(0, nkt, body, jnp.zeros((FOLD_ROWS, qb), F32))
        return jnp.sum(acc, axis=0, keepdims=True)

    kf = float(topk)
    res = jnp.where(count_ge(jnp.zeros((1, qb), jnp.int32)) >= kf, 0, jnp.iinfo(jnp.int32).min)

    def bit_step(i, res):
        cand = res + (jnp.int32(1) << (30 - i))
        return jnp.where(count_ge(cand) >= kf, cand, res)

    thr = lax.fori_loop(0, 31, bit_step, res)
    need = kf - count_ge(thr + 1)

    tril = tril_ref[...]

    def select_block(j, ties):
        keys = sc_s[j]
        eq = keys == thr
        prefix = _dot(tril, jnp.where(eq, 1.0, 0.0).astype(BF16)) + ties
        sel = ((keys > thr) | (eq & (prefix <= need))) & (j * KEY_TILE + key_row < limit)
        sc_s[j] = lax.bitcast_convert_type(jnp.where(sel, 0.0, -1e30), jnp.int32)
        return prefix[KEY_TILE - 1:KEY_TILE, :]

    lax.fori_loop(0, nkt, select_block, jnp.zeros((1, qb), F32))

    q = q_ref[...] * (DSA_HEAD_DIM ** -0.5)
    ones = jnp.ones((KEY_TILE, LANE), BF16)
    rep = DSA_HEADS // DSA_KV_HEADS
    for grp in range(DSA_KV_HEADS):
        heads = range(grp * rep, (grp + 1) * rep)
        q_m = [half_masked(q, hd) for hd in heads]

        def logits_t(j, kd, mask, r):
            hd = grp * rep + r
            bias = [bias_ref[hd, jnp.clip(b - (KEY_SUB * j + s), 0, 2), :, 0:qb] for s in range(KEY_SUB)]
            return _dot_nt(kd, q_m[r]) + jnp.concatenate(bias, axis=0) + mask

        def max_block(j, ms):
            kd = key_rows(grp * LANE, j)
            mask = lax.bitcast_convert_type(sc_s[j], F32)
            return tuple(jnp.maximum(m, logits_t(j, kd, mask, r).reshape(fold, FOLD_ROWS, qb).max(axis=0))
                         for r, m in enumerate(ms))

        ms = lax.fori_loop(0, nkt, max_block, tuple(jnp.full((FOLD_ROWS, qb), -1e30, F32) for _ in heads))
        m_rows = [jnp.max(m, axis=0, keepdims=True) for m in ms]
        acc_s[...] = jnp.zeros_like(acc_s)

        def acc_block(j, carry):
            kd = key_rows(grp * LANE, j)
            mask = lax.bitcast_convert_type(sc_s[j], F32)
            vd = jnp.concatenate([key_rows(HK_V + grp * LANE, j), ones], axis=1)
            for r in range(rep):
                p = jnp.exp(logits_t(j, kd, mask, r) - m_rows[r]).astype(BF16)
                acc_s[r] += _dot_tn(p, vd)
            return carry

        lax.fori_loop(0, nkt, acc_block, 0)
        for pr in range(rep // 2):
            o0 = acc_s[2 * pr, :, :LANE] / acc_s[2 * pr, :, LANE:]
            o1 = acc_s[2 * pr + 1, :, :LANE] / acc_s[2 * pr + 1, :, LANE:]
            col = (grp * (rep // 2) + pr) * LANE
            o_ref[:, col:col + LANE] = jnp.where(low_half, o0, o1)


def _dsa_call(h, hk, n, t, bias, tril, *, b_off, n_keys, topk):
    qb = min(t, Q_BLOCK)
    nqb = t // qb
    lp = hk.shape[1]
    wq = DSA_HEADS * DSA_HEAD_DIM
    wqi = IDX_HEADS * IDX_DIM
    return pl.pallas_call(
        functools.partial(_dsa_kernel, qb=qb, b_off=b_off, n_keys=n_keys, topk=topk),
        grid=(n, nqb),
        in_specs=[pl.BlockSpec((qb, wq), lambda i, c: (i * nqb + c, PROJ_OFF['b_q'] // wq)),
                  pl.BlockSpec((qb, wqi), lambda i, c: (i * nqb + c, PROJ_OFF['b_qi'] // wqi)),
                  pl.BlockSpec((qb, LANE), lambda i, c: (i * nqb + c, OFF_MISC // LANE)),
                  pl.BlockSpec((None, lp, HK_W), lambda i, c: (i, 0, 0)),
                  pl.BlockSpec((DSA_HEADS, 3, KEY_BLOCK, Q_BLOCK), lambda i, c: (0, 0, 0, 0)),
                  pl.BlockSpec((KEY_TILE, KEY_TILE), lambda i, c: (0, 0))],
        out_specs=pl.BlockSpec((qb, wq), lambda i, c: (i * nqb + c, 0)),
        out_shape=jax.ShapeDtypeStruct((n * t, wq), F32),
        scratch_shapes=[pltpu.VMEM((lp // KEY_TILE, KEY_TILE, qb), jnp.int32),
                        pltpu.VMEM((DSA_HEADS // DSA_KV_HEADS, qb, 2 * LANE), F32)],
        compiler_params=_params("parallel", "arbitrary"),
        name="dsa_branch",
    )(h, h, h, hk, bias, tril)


def _t5_bucket(rel):
    nb = REL_BUCKETS // 2
    max_exact = nb // 2
    ret = jnp.where(rel > 0, nb, 0)
    dist = jnp.abs(rel)
    distf = jnp.maximum(dist, 1).astype(F32)
    large = max_exact + (jnp.log(distf / max_exact) / math.log(REL_MAX_DIST / max_exact)
                         * (nb - max_exact)).astype(jnp.int32)
    large = jnp.minimum(large, nb - 1)
    return ret + jnp.where(dist < max_exact, dist, large)


def _dsa_bias_tables(rel_bias):
    lc = jnp.arange(KEY_BLOCK, dtype=jnp.int32)[:, None]
    qr = jnp.arange(Q_BLOCK, dtype=jnp.int32)[None, :]
    rel = jnp.stack([lc - qr - back * KEY_BLOCK for back in range(3)], axis=0)
    return jnp.moveaxis(rel_bias[_t5_bucket(rel)].astype(F32), -1, 0)


def _gdn_kernel(raw_ref, misc_ref, s0_ref, conv0_ref, convw_ref, alog_ref, dtb_ref, norm_ref,
                o_ref, s_out, conv_out, xp_s, st_s, *, c):
    @pl.when(pl.program_id(1) == 0)
    def _():
        xp_s[0:8, :] = conv0_ref[...]
        st_s[...] = s0_ref[...]

    raw = raw_ref[...]
    xp_s[8:8 + c, :] = raw
    w = convw_ref[...]
    y = raw * w[CONV_W - 1:CONV_W, :]
    for i in range(CONV_W - 1):
        y = y + xp_s[5 + i:5 + i + c, :] * w[i:i + 1, :]
    tail = xp_s[c:c + 8, :]
    xp_s[0:8, :] = tail
    conv_out[...] = tail
    qkv = jax.nn.silu(y)

    misc = misc_ref[...]
    beta_all = jax.nn.sigmoid(misc)
    g_all = -jnp.exp(alog_ref[...]) * _softplus(misc + dtb_ref[...])
    row = _iota((c, c), 0)
    col = _iota((c, c), 1)
    tril = jnp.where(row >= col, 1.0, 0.0)
    eye = row == col
    strict = row > col
    gc_all = _dot(tril, g_all, precision=HIGHEST)
    steps = int(math.log2(c))
    for hd in range(GDN_HEADS):
        qh = qkv[:, hd * GDN_DK:(hd + 1) * GDN_DK]
        kh = qkv[:, GDN_QK + hd * GDN_DK:GDN_QK + (hd + 1) * GDN_DK]
        vh = qkv[:, 2 * GDN_QK + hd * GDN_DV:2 * GDN_QK + (hd + 1) * GDN_DV]
        qh = qh * lax.rsqrt(jnp.sum(qh * qh, axis=1, keepdims=True) + RMS_EPS) * (GDN_DK ** -0.5)
        kh = kh * lax.rsqrt(jnp.sum(kh * kh, axis=1, keepdims=True) + RMS_EPS)
        beta = beta_all[:, MISC_BETA + hd:MISC_BETA + hd + 1]
        gc = gc_all[:, MISC_A + hd:MISC_A + hd + 1]
        gc_b = jnp.broadcast_to(gc, (c, c))
        gc_row = jnp.sum(jnp.where(eye, gc_b, 0.0), axis=0, keepdims=True)
        diff = gc_b - gc_row
        dec_strict = jnp.where(strict, jnp.exp(jnp.where(strict, diff, 0.0)), 0.0)
        dec_incl = jnp.where(row >= col, jnp.exp(jnp.where(row >= col, diff, 0.0)), 0.0)
        kb = kh.astype(BF16)
        qb = qh.astype(BF16)
        a = beta * _dot_nt(kb, kb) * dec_strict
        pw = -a
        inv = jnp.where(eye, 1.0, 0.0) + pw
        for _ in range(steps - 1):
            pw = _dot(pw, pw, precision=HIGHEST)
            inv = inv + _dot(inv, pw, precision=HIGHEST)
        egc = jnp.exp(gc)
        rhs = jnp.concatenate([(beta * egc) * kh, beta * vh], axis=1)
        sol = _dot(inv, rhs, precision=HIGHEST)
        w_blk = sol[:, :GDN_DK]
        u_blk = sol[:, GDN_DK:]
        s = st_s[hd]
        sb = s.astype(BF16)
        v_new = u_blk - _dot(w_blk.astype(BF16), sb)
        vb = v_new.astype(BF16)
        o = egc * _dot(qb, sb) + _dot((_dot_nt(qb, kb) * dec_incl).astype(BF16), vb)
        g_last = gc[c - 1:c, :]
        st_s[hd] = jnp.exp(g_last) * s + _dot_tn((kh * jnp.exp(g_last - gc)).astype(BF16), vb)
        o = o * lax.rsqrt(jnp.mean(o * o, axis=1, keepdims=True) + RMS_EPS) * norm_ref[...]
        o_ref[:, hd * GDN_DV:(hd + 1) * GDN_DV] = o
    s_out[...] = st_s[...]


def _gdn_call(h, n, t, s0, conv0, gp):
    c = min(t, CHUNK)
    nc = t // c
    const = lambda shape: pl.BlockSpec(shape, lambda i, j: (0,) * len(shape))
    state = pl.BlockSpec((None, GDN_HEADS, GDN_DK, GDN_DV), lambda i, j: (i, 0, 0, 0))
    conv = pl.BlockSpec((None, 8, GDN_CONV_CH), lambda i, j: (i, 0, 0))
    return pl.pallas_call(
        functools.partial(_gdn_kernel, c=c),
        grid=(n, nc),
        in_specs=[pl.BlockSpec((c, GDN_CONV_CH), lambda i, j: (i * nc + j, PROJ_OFF['c_qkv'] // GDN_CONV_CH)),
                  pl.BlockSpec((c, LANE), lambda i, j: (i * nc + j, OFF_MISC // LANE)),
                  state, conv,
                  const((CONV_W, GDN_CONV_CH)), const((1, LANE)), const((1, LANE)), const((1, GDN_DV))],
        out_specs=[pl.BlockSpec((c, GDN_VW), lambda i, j: (i * nc + j, 0)), state, conv],
        out_shape=[jax.ShapeDtypeStruct((n * t, GDN_VW), F32),
                   jax.ShapeDtypeStruct((n, GDN_HEADS, GDN_DK, GDN_DV), F32),
                   jax.ShapeDtypeStruct((n, 8, GDN_CONV_CH), F32)],
        scratch_shapes=[pltpu.VMEM((c + 8, GDN_CONV_CH), F32),
                        pltpu.VMEM((GDN_HEADS, GDN_DK, GDN_DV), F32)],
        compiler_params=_params("parallel", "arbitrary"),
        name="gdn_branch",
    )(h, h, s0, conv0, gp['convw'], gp['alog'], gp['dtb'], gp['norm'])


def _lane_place(v, off):
    return jnp.zeros((1, LANE), F32).at[0, off:off + v.shape[0]].set(v.astype(F32))


def _gla_kernel(qk_ref, v_ref, misc_ref, s0_ref, wg_ref, wgt_ref, bg_ref, bgc_ref, norm_ref, hexp_ref, gsum_ref,
                o_ref, s_out, st_s, *, tc):
    blk = GLA_BLOCK
    wk = GLA_HEADS * GLA_DK

    @pl.when(pl.program_id(1) == 0)
    def _():
        st_s[...] = s0_ref[...]

    row = _iota((blk, blk), 0)
    col = _iota((blk, blk), 1)
    tril = jnp.where(row >= col, 1.0, 0.0)
    jrow = _iota((blk, wk), 0)
    lane_k = _iota((blk, wk), 1)
    head_mask = [(lane_k >= hd * GLA_DK) & (lane_k < (hd + 1) * GLA_DK) for hd in range(GLA_HEADS)]

    def block(sb, carry):
        r0 = pl.multiple_of(sb * blk, blk)
        qk = qk_ref[pl.ds(r0, blk), :]
        q = qk[:, :wk] * (GLA_DK ** -0.5)
        k = qk[:, wk:]
        v = v_ref[pl.ds(r0, blk), :]
        mb = misc_ref[pl.ds(r0, blk), :].astype(BF16)
        lg = jax.nn.log_sigmoid(_dot(mb, wg_ref[...]) + bg_ref[...]) * (1.0 / GLA_TAU)
        lg_t = jax.nn.log_sigmoid(_dot_nt(wgt_ref[...], mb) + bgc_ref[:, 0:blk]) * (1.0 / GLA_TAU)
        bc = _dot(tril, lg, precision=HIGHEST)
        slabs = []
        for i in range(blk):
            keep = jrow <= i
            dec = jnp.where(keep, jnp.exp(jnp.where(keep, bc[i:i + 1, :] - bc, 0.0)), 0.0)
            slabs.append(dec * q[i:i + 1, :] * k)
        dmat = jnp.concatenate(slabs, axis=0).astype(BF16)
        a_exp = _dot(dmat, hexp_ref[...])
        prod = a_exp * jnp.concatenate([v] * blk, axis=0)
        o = _dot(gsum_ref[...], prod, precision=HIGHEST)
        qe = q * jnp.exp(bc)
        b_last = bc[blk - 1:blk, :]
        kt = k * jnp.exp(b_last - bc)
        s = st_s[...]
        sbf = s.astype(BF16)
        upd = jnp.zeros_like(s)
        parts = []
        for hd in range(GLA_HEADS):
            parts.append(_dot(jnp.where(head_mask[hd], qe, 0.0).astype(BF16), sbf))
            vh = v[:, hd * GLA_DV:(hd + 1) * GLA_DV].astype(BF16)
            upd = upd + _dot_tn(jnp.where(head_mask[hd], kt, 0.0).astype(BF16), vh)
        o = o + jnp.concatenate(parts, axis=1)
        decay = jnp.exp(jnp.sum(lg_t, axis=1, keepdims=True))
        st_s[...] = decay * s + upd
        for hd in range(GLA_HEADS):
            oh = o[:, hd * GLA_DV:(hd + 1) * GLA_DV]
            oh = oh * lax.rsqrt(jnp.mean(oh * oh, axis=1, keepdims=True) + RMS_EPS) * norm_ref[...]
            o_ref[pl.ds(r0, blk), hd * GLA_DV:(hd + 1) * GLA_DV] = oh
        return carry

    lax.fori_loop(0, tc // blk, block, 0)
    s_out[...] = st_s[...]


def _gla_call(h, n, t, s0, gp):
    tc = min(t, 128)
    nc = t // tc
    wk = GLA_HEADS * GLA_DK
    wv = GLA_HEADS * GLA_DV
    const = lambda shape: pl.BlockSpec(shape, lambda i, j: (0,) * len(shape))
    state = pl.BlockSpec((None, wk, GLA_DV), lambda i, j: (i, 0, 0))
    return pl.pallas_call(
        functools.partial(_gla_kernel, tc=tc),
        grid=(n, nc),
        in_specs=[pl.BlockSpec((tc, 2 * wk), lambda i, j: (i * nc + j, PROJ_OFF['d_q'] // (2 * wk))),
                  pl.BlockSpec((tc, wv), lambda i, j: (i * nc + j, PROJ_OFF['d_v'] // wv)),
                  pl.BlockSpec((tc, LANE), lambda i, j: (i * nc + j, OFF_MISC // LANE)),
                  state,
                  const((LANE, wk)), const((wk, LANE)), const((1, wk)), const((wk, LANE)), const((1, GLA_DV)),
                  const((wk, wv)), const((GLA_BLOCK, GLA_BLOCK * GLA_BLOCK))],
        out_specs=[pl.BlockSpec((tc, wv), lambda i, j: (i * nc + j, 0)), state],
        out_shape=[jax.ShapeDtypeStruct((n * t, wv), F32),
                   jax.ShapeDtypeStruct((n, wk, GLA_DV), F32)],
        scratch_shapes=[pltpu.VMEM((wk, GLA_DV), F32)],
        compiler_params=_params("parallel", "arbitrary"),
        name="gla_branch",
    )(h, h, h, s0, gp['wg'], gp['wgt'], gp['bg'], gp['bgc'], gp['norm'], gp['hexp'], gp['gsum'])


def _gla_prepare(w_g2, b_g, norm):
    wk = GLA_HEADS * GLA_DK
    wg = jnp.zeros((LANE, wk), F32).at[MISC_G:MISC_G + GLA_GATE_RANK].set(w_g2).astype(BF16)
    ck = jnp.arange(wk)[:, None] // GLA_DK
    cv = jnp.arange(GLA_HEADS * GLA_DV)[None, :] // GLA_DV
    ij = jnp.arange(GLA_BLOCK * GLA_BLOCK)[None, :] // GLA_BLOCK
    return {'wg': wg, 'wgt': wg.T, 'bg': b_g.reshape(1, wk).astype(F32),
            'bgc': jnp.broadcast_to(b_g.astype(F32)[:, None], (wk, LANE)),
            'norm': norm.reshape(1, GLA_DV).astype(F32),
            'hexp': (ck == cv).astype(BF16),
            'gsum': (jnp.arange(GLA_BLOCK)[:, None] == ij).astype(F32)}


def _merge_kernel(x_ref, ya_ref, yb_ref, yc_ref, yd_ref, ga_ref, gb_ref, gc_ref, gd_ref, mg_ref,
                  wb_ref, wo_ref, lng_ref, lnb_ref, o_ref):
    mixed = None
    for br, (y_ref, g_ref) in enumerate(((ya_ref, ga_ref), (yb_ref, gb_ref), (yc_ref, gc_ref), (yd_ref, gd_ref))):
        act = (y_ref[...] * jax.nn.silu(g_ref[...])).astype(BF16)
        term = jax.nn.sigmoid(mg_ref[:, br * D_MODEL:(br + 1) * D_MODEL]) * _dot(act, wb_ref[br])
        mixed = term if mixed is None else mixed + term
    z = DN_ALPHA * x_ref[...] + _dot(mixed.astype(BF16), wo_ref[...])
    mu = jnp.mean(z, axis=1, keepdims=True)
    zc = z - mu
    var = jnp.mean(zc * zc, axis=1, keepdims=True)
    o_ref[...] = zc * lax.rsqrt(var + LN_EPS) * lng_ref[...] + lnb_ref[...]


def _merge_call(x, h, ya, yb, yc, yd, wb, wo, lng, lnb):
    rows = x.shape[0]
    tm = min(rows, 256)
    bw = BRANCH_W
    ybs = pl.BlockSpec((tm, bw), lambda i: (i, 0))
    gate = lambda name: pl.BlockSpec((tm, bw), lambda i: (i, PROJ_OFF[name] // bw))
    return pl.pallas_call(
        _merge_kernel,
        grid=(rows // tm,),
        in_specs=[pl.BlockSpec((tm, D_MODEL), lambda i: (i, 0)), ybs, ybs, ybs, ybs,
                  gate('a_gate'), gate('b_gate'), gate('c_gate'), gate('d_gate'),
                  pl.BlockSpec((tm, N_BRANCH * D_MODEL), lambda i: (i, 0)),
                  pl.BlockSpec((N_BRANCH, bw, D_MODEL), lambda i: (0, 0, 0)),
                  pl.BlockSpec((D_MODEL, D_MODEL), lambda i: (0, 0)),
                  pl.BlockSpec((1, D_MODEL), lambda i: (0, 0)),
                  pl.BlockSpec((1, D_MODEL), lambda i: (0, 0))],
        out_specs=pl.BlockSpec((tm, D_MODEL), lambda i: (i, 0)),
        out_shape=jax.ShapeDtypeStruct((rows, D_MODEL), F32),
        compiler_params=_params("parallel"),
        name="merge_norm",
    )(x, ya, yb, yc, yd, h, h, h, h, h, wb, wo, lng, lnb)


def _split_w_in(w_in):
    out, off = {}, 0
    for name, width in IN_LAYOUT:
        out[name] = w_in[:, off:off + width]
        off += width
    return out


def _layer_weights(w_in):
    cols = _split_w_in(w_in)
    w_proj = jnp.concatenate([cols[name] for name in PROJ_ORDER]
                             + [jnp.zeros((D_MODEL, MISC_PAD), w_in.dtype)], axis=1).astype(BF16)
    k, v, ki = cols['b_k'], cols['b_v'], cols['b_ki']
    dh = DSA_HEAD_DIM
    w_key = jnp.concatenate([k[:, :dh], k[:, :dh], k[:, dh:], k[:, dh:],
                             v[:, :dh], v[:, :dh], v[:, dh:], v[:, dh:], ki, ki], axis=1).astype(BF16)
    return w_proj, w_key


def _dup_keys(k, v, ki):
    return jnp.concatenate([k[:, :, 0], k[:, :, 0], k[:, :, 1], k[:, :, 1],
                            v[:, :, 0], v[:, :, 0], v[:, :, 1], v[:, :, 1], ki, ki], axis=-1).astype(BF16)


def _layer(x, lw, st):
    n, t, _ = x.shape
    x2 = x.reshape(n * t, D_MODEL)
    h = _matmul(x2, lw['w_proj'], F32, tn=PROJ_W // 9)
    hk_new = _matmul(x2, lw['w_key'], BF16, tn=HK_W).reshape(n, t, HK_W)
    if st is None:
        hk, n_keys, b_off = hk_new, t, 0
        s5r = jnp.zeros((n, 1, S5_LANES), F32)
        s5i = jnp.zeros((n, 1, S5_LANES), F32)
        gdn0 = jnp.zeros((n, GDN_HEADS, GDN_DK, GDN_DV), F32)
        conv0 = jnp.zeros((n, 8, GDN_CONV_CH), F32)
        gla0 = jnp.zeros((n, GLA_HEADS * GLA_DK, GLA_DV), F32)
    else:
        past = st['k'].shape[1]
        n_keys = past + t
        b_off = past // Q_BLOCK
        lp = -(-n_keys // KEY_TILE) * KEY_TILE
        hk = jnp.concatenate([_dup_keys(st['k'], st['v'], st['kidx']), hk_new,
                              jnp.zeros((n, lp - n_keys, HK_W), BF16)], axis=1)
        s5r = st['s5_re'].reshape(n, 1, S5_LANES)
        s5i = st['s5_im'].reshape(n, 1, S5_LANES)
        gdn0 = st['gdn']
        conv0 = jnp.concatenate([jnp.zeros((n, 8 - (CONV_W - 1), GDN_CONV_CH), F32), st['gdn_conv']], axis=1)
        gla0 = st['gla'].reshape(n, GLA_HEADS * GLA_DK, GLA_DV)
    topk = min(DSA_TOPK_MAX, n_keys // 4)

    ya, s5r_new, s5i_new = _s5_call(h, n, t, s5r, s5i, lw['s5'])
    yb = _dsa_call(h, hk, n, t, lw['bias'], lw['tril'], b_off=b_off, n_keys=n_keys, topk=topk)
    yc, gdn_new, conv_new = _gdn_call(h, n, t, gdn0, conv0, lw['gdn'])
    yd, gla_new = _gla_call(h, n, t, gla0, lw['gla'])
    y = _merge_call(x2, h, ya, yb, yc, yd, lw['w_branch'], lw['w_out'], lw['ln_g'], lw['ln_b'])

    def cols(name, width):
        return h[:, PROJ_OFF[name]:PROJ_OFF[name] + width].reshape(n, t, width)

    new = {'k': cols('b_k', DSA_KV_HEADS * DSA_HEAD_DIM).reshape(n, t, DSA_KV_HEADS, DSA_HEAD_DIM),
           'v': cols('b_v', DSA_KV_HEADS * DSA_HEAD_DIM).reshape(n, t, DSA_KV_HEADS, DSA_HEAD_DIM),
           'kidx': cols('b_ki', IDX_DIM),
           's5_re': s5r_new.reshape(n, S5_GROUPS, S5_STATE), 's5_im': s5i_new.reshape(n, S5_GROUPS, S5_STATE),
           'gdn': gdn_new, 'gdn_conv': conv_new[:, 8 - (CONV_W - 1):],
           'gla': gla_new.reshape(n, GLA_HEADS, GLA_DK, GLA_DV)}
    return y.reshape(n, t, D_MODEL), new


def kernel(x_prompt, x_sample, cache_k, cache_v, cache_kidx, state_s5_re, state_s5_im, state_gdn, state_gdn_conv, state_gla, w_in, w_branch, w_out, ln_g, ln_b, rel_bias, s5_a_re, s5_a_im, s5_log_dt, s5_b_re, s5_b_im, s5_c_re, s5_c_im, s5_d, s5_w_glu, gdn_conv, gdn_a_log, gdn_dt_bias, gdn_norm, gla_w_g2, gla_b_g, gla_norm):
    bias = _dsa_bias_tables(rel_bias)
    tril = (jnp.arange(KEY_TILE)[:, None] >= jnp.arange(KEY_TILE)[None, :]).astype(BF16)
    yp, ys = x_prompt, x_sample
    new_p, new_s = [], []
    for l in range(DEPTH):
        w_proj, w_key = _layer_weights(w_in[l])
        lw = {'w_proj': w_proj, 'w_key': w_key, 'bias': bias, 'tril': tril,
              'w_branch': w_branch[l].astype(BF16), 'w_out': w_out[l].astype(BF16),
              'ln_g': ln_g[l].reshape(1, D_MODEL), 'ln_b': ln_b[l].reshape(1, D_MODEL),
              's5': _s5_prepare(s5_a_re[l], s5_a_im[l], s5_log_dt[l], s5_b_re[l], s5_b_im[l],
                                s5_c_re[l], s5_c_im[l], s5_d[l], s5_w_glu[l]),
              'gdn': {'convw': gdn_conv[l], 'alog': _lane_place(gdn_a_log[l], MISC_A),
                      'dtb': _lane_place(gdn_dt_bias[l], MISC_A), 'norm': gdn_norm[l].reshape(1, GDN_DV)},
              'gla': _gla_prepare(gla_w_g2[l], gla_b_g[l], gla_norm[l])}
        st = {'k': cache_k[l], 'v': cache_v[l], 'kidx': cache_kidx[l], 's5_re': state_s5_re[l],
              's5_im': state_s5_im[l], 'gdn': state_gdn[l], 'gdn_conv': state_gdn_conv[l], 'gla': state_gla[l]}
        yp, stp = _layer(yp, lw, None)
        ys, sts = _layer(ys, lw, st)
        new_p.append(stp)
        new_s.append(sts)
    names = ('k', 'v', 'kidx', 's5_re', 's5_im', 'gdn', 'gdn_conv', 'gla')
    stack = lambda states, name: jnp.stack([s[name] for s in states], axis=0)
    return ((yp, ys) + tuple(stack(new_p, nm) for nm in names) + tuple(stack(new_s, nm) for nm in names))
```

```python
import functools
import math

import jax
import jax.numpy as jnp
from jax import lax
from jax.experimental import pallas as pl
from jax.experimental.pallas import tpu as pltpu

F32 = jnp.float32
BF16 = jnp.bfloat16
HIGHEST = lax.Precision.HIGHEST

D_MODEL = 1024
DEPTH = 2
CHUNK = 64
N_BRANCH = 4
BRANCH_W = 512
S5_GROUP = 16
S5_GROUPS = BRANCH_W // S5_GROUP
S5_STATE = 64
S5_LANES = S5_GROUPS * S5_STATE
DSA_HEADS = 8
DSA_KV_HEADS = 2
DSA_HEAD_DIM = 64
IDX_HEADS = 4
IDX_DIM = 64
DSA_TOPK_MAX = 256
REL_BUCKETS = 32
REL_MAX_DIST = 128
GDN_HEADS = 4
GDN_DK = 128
GDN_DV = 128
GDN_QK = GDN_HEADS * GDN_DK
GDN_VW = GDN_HEADS * GDN_DV
CONV_W = 4
GDN_CONV_CH = 2 * GDN_QK + GDN_VW
GLA_HEADS = 4
GLA_DK = 64
GLA_DV = 128
GLA_GATE_RANK = 16
GLA_TAU = 16.0
GLA_BLOCK = 16
LN_EPS = 1e-5
RMS_EPS = 1e-6
DN_ALPHA = (2 * DEPTH) ** 0.25
LOG2E = math.log2(math.e)

LANE = 128
KEY_BLOCK = 128
KEY_SUB = 4
KEY_TILE = KEY_SUB * KEY_BLOCK
FOLD_ROWS = 64
Q_BLOCK = 128
VMEM_LIMIT = 56 * 1024 * 1024

IN_LAYOUT = (
    ('a_u', BRANCH_W), ('a_gate', BRANCH_W),
    ('b_q', DSA_HEADS * DSA_HEAD_DIM), ('b_k', DSA_KV_HEADS * DSA_HEAD_DIM),
    ('b_v', DSA_KV_HEADS * DSA_HEAD_DIM), ('b_qi', IDX_HEADS * IDX_DIM), ('b_ki', IDX_DIM),
    ('b_wi', IDX_HEADS), ('b_gate', BRANCH_W),
    ('c_qkv', GDN_CONV_CH), ('c_beta', GDN_HEADS), ('c_a', GDN_HEADS), ('c_gate', BRANCH_W),
    ('d_q', GLA_HEADS * GLA_DK), ('d_k', GLA_HEADS * GLA_DK), ('d_v', GLA_HEADS * GLA_DV),
    ('d_g', GLA_GATE_RANK), ('d_gate', BRANCH_W),
    ('merge', N_BRANCH * D_MODEL),
)

PROJ_ORDER = ('merge', 'a_u', 'a_gate', 'b_q', 'b_qi', 'b_k', 'b_v', 'c_qkv', 'b_gate', 'c_gate',
              'd_v', 'd_gate', 'd_q', 'd_k', 'b_ki', 'b_wi', 'c_beta', 'c_a', 'd_g')
MISC_WI = IDX_DIM
MISC_BETA = MISC_WI + IDX_HEADS
MISC_A = MISC_BETA + GDN_HEADS
MISC_G = MISC_A + GDN_HEADS
MISC_PAD = LANE - (MISC_G + GLA_GATE_RANK)


def _proj_offsets():
    widths = dict(IN_LAYOUT)
    off, out = 0, {}
    for name in PROJ_ORDER:
        out[name] = off
        off += widths[name]
    return out, off + MISC_PAD


PROJ_OFF, PROJ_W = _proj_offsets()
OFF_MISC = PROJ_OFF['b_ki']
HK_W = 5 * LANE
HK_V = 2 * LANE
HK_KI = 4 * LANE


def _dot(a, b, precision=None):
    return jnp.dot(a, b, preferred_element_type=F32, precision=precision)


def _dot_nt(a, b, precision=None):
    return lax.dot_general(a, b, (((1,), (1,)), ((), ())), preferred_element_type=F32, precision=precision)


def _dot_tn(a, b, precision=None):
    return lax.dot_general(a, b, (((0,), (0,)), ((), ())), preferred_element_type=F32, precision=precision)


def _split_bf16(x):
    hi = x.astype(BF16)
    return hi, (x - hi.astype(F32)).astype(BF16)


def _dot3(a, b):
    a_hi, a_lo = _split_bf16(a)
    b_hi, b_lo = _split_bf16(b)
    return _dot(a_hi, b_hi) + (_dot(a_hi, b_lo) + _dot(a_lo, b_hi))


def _params(*sem):
    return pltpu.CompilerParams(dimension_semantics=sem, vmem_limit_bytes=VMEM_LIMIT)


def _softplus(x):
    return jnp.maximum(x, 0.0) + jnp.log1p(jnp.exp(-jnp.abs(x)))


def _iota(shape, dim):
    return lax.broadcasted_iota(jnp.int32, shape, dim)


def _mm_kernel(x_ref, w_ref, o_ref):
    o_ref[...] = _dot(x_ref[...].astype(BF16), w_ref[...]).astype(o_ref.dtype)


def _matmul(x, w, out_dtype, tn):
    rows, kdim = x.shape
    width = w.shape[1]
    tm = min(rows, 1024)
    return pl.pallas_call(
        _mm_kernel,
        grid=(rows // tm, width // tn),
        in_specs=[pl.BlockSpec((tm, kdim), lambda i, j: (i, 0)),
                  pl.BlockSpec((kdim, tn), lambda i, j: (0, j))],
        out_specs=pl.BlockSpec((tm, tn), lambda i, j: (i, j)),
        out_shape=jax.ShapeDtypeStruct((rows, width), out_dtype),
        compiler_params=_params("parallel", "parallel"),
        name="proj_matmul",
    )(x, w)


def _s5_kernel(u_ref, h0r_ref, h0i_ref, ar_ref, ai_ref, bbr_ref, bbi_ref, ccr_ref, cci_ref, d_ref, wglu_ref,
               ya_ref, hr_out, hi_out, xr_s, xi_s, cr_s, ci_s, *, tc):
    @pl.when(pl.program_id(1) == 0)
    def _():
        cr_s[...] = h0r_ref[...]
        ci_s[...] = h0i_ref[...]

    u = u_ref[...]
    ub = u.astype(BF16)
    xr_s[...] = _dot(ub, bbr_ref[...])
    xi_s[...] = _dot(ub, bbi_ref[...])
    ar = ar_ref[...]
    ai = ai_ref[...]

    def step(t, carry):
        hr, hi = carry
        nhr = ar * hr - ai * hi + xr_s[pl.ds(t, 1), :]
        nhi = ar * hi + ai * hr + xi_s[pl.ds(t, 1), :]
        xr_s[pl.ds(t, 1), :] = nhr
        xi_s[pl.ds(t, 1), :] = nhi
        return nhr, nhi

    hr, hi = lax.fori_loop(0, tc, step, (cr_s[...], ci_s[...]), unroll=8)
    cr_s[...] = hr
    ci_s[...] = hi
    hr_out[...] = hr
    hi_out[...] = hi
    y = (_dot(xr_s[...].astype(BF16), ccr_ref[...]) - _dot(xi_s[...].astype(BF16), cci_ref[...])
         + d_ref[...] * u)
    y = jax.nn.gelu(y)
    g = _dot(y.astype(BF16), wglu_ref[...])
    ya_ref[...] = g[:, :BRANCH_W] * jax.nn.sigmoid(g[:, BRANCH_W:])


def _s5_call(h, n, t, h0r, h0i, sp):
    tc = min(t, 256)
    nc = t // tc
    const = lambda shape: pl.BlockSpec(shape, lambda i, c: (0,) * len(shape))
    state = pl.BlockSpec((None, 1, S5_LANES), lambda i, c: (i, 0, 0))
    return pl.pallas_call(
        functools.partial(_s5_kernel, tc=tc),
        grid=(n, nc),
        in_specs=[pl.BlockSpec((tc, BRANCH_W), lambda i, c: (i * nc + c, PROJ_OFF['a_u'] // BRANCH_W)),
                  state, state,
                  const((1, S5_LANES)), const((1, S5_LANES)),
                  const((BRANCH_W, S5_LANES)), const((BRANCH_W, S5_LANES)),
                  const((S5_LANES, BRANCH_W)), const((S5_LANES, BRANCH_W)),
                  const((1, BRANCH_W)), const((BRANCH_W, 2 * BRANCH_W))],
        out_specs=[pl.BlockSpec((tc, BRANCH_W), lambda i, c: (i * nc + c, 0)), state, state],
        out_shape=[jax.ShapeDtypeStruct((n * t, BRANCH_W), F32),
                   jax.ShapeDtypeStruct((n, 1, S5_LANES), F32),
                   jax.ShapeDtypeStruct((n, 1, S5_LANES), F32)],
        scratch_shapes=[pltpu.VMEM((tc, S5_LANES), F32), pltpu.VMEM((tc, S5_LANES), F32),
                        pltpu.VMEM((1, S5_LANES), F32), pltpu.VMEM((1, S5_LANES), F32)],
        compiler_params=_params("parallel", "arbitrary"),
        name="s5_branch",
    )(h, h0r, h0i, sp['ar'], sp['ai'], sp['bbr'], sp['bbi'], sp['ccr'], sp['cci'], sp['d'], sp['wglu'])


def _s5_prepare(a_re, a_im, log_dt, b_re, b_im, c_re, c_im, d, w_glu):
    lam_r = jnp.minimum(a_re, -1e-4)
    lam_i = a_im
    dt = jnp.exp(log_dt)[:, None]
    mag = jnp.exp(lam_r * dt)
    ar = mag * jnp.cos(lam_i * dt)
    ai = mag * jnp.sin(lam_i * dt)
    den = lam_r * lam_r + lam_i * lam_i
    nr = ar - 1.0
    cr = (nr * lam_r + ai * lam_i) / den
    ci = (ai * lam_r - nr * lam_i) / den
    bb_r = cr[..., None] * b_re - ci[..., None] * b_im
    bb_i = cr[..., None] * b_im + ci[..., None] * b_re
    eye = jnp.eye(S5_GROUPS, dtype=F32)

    def blockdiag_in(bb):
        m = eye[:, None, :, None] * jnp.swapaxes(bb, 1, 2)[:, :, None, :]
        return m.reshape(BRANCH_W, S5_LANES).astype(BF16)

    def blockdiag_out(c):
        m = eye[:, None, :, None] * jnp.swapaxes(c, 1, 2)[:, :, None, :]
        return m.reshape(S5_LANES, BRANCH_W).astype(BF16)

    return {'ar': ar.reshape(1, S5_LANES), 'ai': ai.reshape(1, S5_LANES),
            'bbr': blockdiag_in(bb_r), 'bbi': blockdiag_in(bb_i),
            'ccr': blockdiag_out(c_re), 'cci': blockdiag_out(c_im),
            'd': d.reshape(1, BRANCH_W), 'wglu': w_glu.astype(BF16)}


def _sortable(x):
    bits = lax.bitcast_convert_type(x, jnp.int32)
    key = bits ^ ((bits >> 31) & jnp.int32(0x7FFFFFFF))
    return jnp.where(key == -1, 0, key)


def _dsa_kernel(q_ref, qi_ref, misc_ref, hk_ref, bias_ref, tril_ref, o_ref, sc_s, acc_s, *, qb, b_off, n_keys, topk):
    b = pl.program_id(1) + b_off
    nkt = (b + KEY_SUB) // KEY_SUB
    fold = KEY_TILE // FOLD_ROWS
    key_row = _iota((KEY_TILE, qb), 0)
    qpos = b * Q_BLOCK + _iota((1, qb), 1)
    limit = jnp.minimum(((qpos >> 6) + 1) * CHUNK, n_keys)
    low_half = _iota((qb, LANE), 1) < DSA_HEAD_DIM

    def key_rows(ref_cols, j):
        return hk_ref[pl.ds(pl.multiple_of(j * KEY_TILE, KEY_TILE), KEY_TILE), ref_cols:ref_cols + LANE]

    def half_masked(x, hd):
        pair = x[:, (hd // 2) * LANE:(hd // 2 + 1) * LANE]
        return jnp.where(low_half == (hd % 2 == 0), pair, 0.0).astype(BF16)

    misc = misc_ref[...]
    pick = jnp.where(_iota((8, LANE), 1) == MISC_WI + _iota((8, LANE), 0), 1.0, 0.0)
    wi_t = _dot_nt(pick, misc, precision=HIGHEST) * (IDX_DIM ** -0.5 * IDX_HEADS ** -0.5)
    qi = qi_ref[...]
    qi_stack = jnp.concatenate([half_masked(qi, hd) for hd in range(IDX_HEADS)], axis=0)

    def score_block(j, carry):
        z = _dot_nt(key_rows(HK_KI, j), qi_stack)
        sc = jnp.zeros((KEY_TILE, qb), F32)
        for hd in range(IDX_HEADS):
            sc = sc + jnp.maximum(z[:, hd * qb:(hd + 1) * qb], 0.0) * wi_t[hd:hd + 1, :]
        sc = jnp.where(j * KEY_TILE + key_row < limit, sc, -jnp.inf)
        sc_s[j] = _sortable(sc)
        return carry

    lax.fori_loop(0, nkt, score_block, 0)

    def count_ge(cand):
        def body(j, acc):
            return acc + jnp.where(sc_s[j] >= cand, 1.0, 0.0).reshape(fold, FOLD_ROWS, qb).sum(axis=0)
        acc = lax.fori_loop(0, nkt, body, jnp.zeros((FOLD_ROWS, qb), F32))
        return jnp.sum(acc, axis=0, keepdims=True)

    kf = float(topk)
    res = jnp.where(count_ge(jnp.zeros((1, qb), jnp.int32)) >= kf, 0, jnp.iinfo(jnp.int32).min)

    def bit_step(i, res):
        cand = res + (jnp.int32(1) << (30 - i))
        return jnp.where(count_ge(cand) >= kf, cand, res)

    thr = lax.fori_loop(0, 31, bit_step, res)
    need = kf - count_ge(thr + 1)

    tril = tril_ref[...]

    def select_block(j, ties):
        keys = sc_s[j]
        eq = keys == thr
        prefix = _dot(tril, jnp.where(eq, 1.0, 0.0).astype(BF16)) + ties
        sel = ((keys > thr) | (eq & (prefix <= need))) & (j * KEY_TILE + key_row < limit)
        sc_s[j] = lax.bitcast_convert_type(jnp.where(sel, 0.0, -jnp.inf), jnp.int32)
        return prefix[KEY_TILE - 1:KEY_TILE, :]

    lax.fori_loop(0, nkt, select_block, jnp.zeros((1, qb), F32))

    q = q_ref[...] * (DSA_HEAD_DIM ** -0.5 * LOG2E)
    rep = DSA_HEADS // DSA_KV_HEADS
    eye = jnp.where(_iota((LANE, LANE), 0) == _iota((LANE, LANE), 1), 1.0, 0.0)
    value_lane = _iota((KEY_TILE, LANE), 1) < DSA_HEAD_DIM
    q_stack = [jnp.concatenate([half_masked(q, grp * rep + r) for r in range(rep)], axis=0)
               for grp in range(DSA_KV_HEADS)]
    far_bias = [bias_ref[hd, 2, 0:1, 0:qb] for hd in range(DSA_HEADS)]
    acc_s[...] = jnp.zeros_like(acc_s)

    def attend(j, ms, far):
        mask = lax.bitcast_convert_type(sc_s[j], F32)
        new_ms = []
        for grp in range(DSA_KV_HEADS):
            z = _dot_nt(key_rows(grp * LANE, j), q_stack[grp])
            ps, alphas = [], []
            for r in range(rep):
                hd = grp * rep + r
                zh = z[:, r * qb:(r + 1) * qb] + mask
                if not far:
                    bias = [bias_ref[hd, jnp.clip(b - (KEY_SUB * j + s), 0, 2), :, 0:qb] for s in range(KEY_SUB)]
                    zh = zh + jnp.concatenate(bias, axis=0)
                tile_max = jnp.max(zh.reshape(fold, FOLD_ROWS, qb).max(axis=0), axis=0, keepdims=True)
                shift = far_bias[hd] if far else 0.0
                m_new = jnp.maximum(ms[hd], tile_max + shift)
                ps.append(jnp.exp2(zh - (m_new - shift)).astype(BF16))
                alphas.append(jnp.exp2(ms[hd] - m_new))
                new_ms.append(m_new)
            vd = jnp.where(value_lane, key_rows(HK_V + grp * LANE, j), 1.0).astype(BF16)
            acc_s[grp] = (acc_s[grp] * jnp.concatenate(alphas, axis=1)
                          + _dot_tn(vd, jnp.concatenate(ps, axis=1)))
        return tuple(new_ms)

    n_far = jnp.maximum((b - 1) // KEY_SUB, 0)
    ms = tuple(jnp.full((1, qb), -1e30, F32) for _ in range(DSA_HEADS))
    ms = lax.fori_loop(0, n_far, lambda j, m: attend(j, m, True), ms)
    lax.fori_loop(n_far, nkt, lambda j, m: attend(j, m, False), ms)
    for grp in range(DSA_KV_HEADS):
        for pr in range(rep // 2):
            halves = []
            for r in (2 * pr, 2 * pr + 1):
                blk = acc_s[grp, :, r * qb:(r + 1) * qb]
                halves.append(blk[0:DSA_HEAD_DIM] / blk[DSA_HEAD_DIM:DSA_HEAD_DIM + 1])
            o_t = jnp.concatenate(halves, axis=0)
            col = (grp * (rep // 2) + pr) * LANE
            o_ref[:, col:col + LANE] = _dot_tn(o_t, eye, precision=HIGHEST)


def _dsa_call(h, hk, n, t, bias, tril, *, b_off, n_keys, topk):
    qb = min(t, Q_BLOCK)
    nqb = t // qb
    lp = hk.shape[1]
    wq = DSA_HEADS * DSA_HEAD_DIM
    wqi = IDX_HEADS * IDX_DIM
    return pl.pallas_call(
        functools.partial(_dsa_kernel, qb=qb, b_off=b_off, n_keys=n_keys, topk=topk),
        grid=(n, nqb),
        in_specs=[pl.BlockSpec((qb, wq), lambda i, c: (i * nqb + c, PROJ_OFF['b_q'] // wq)),
                  pl.BlockSpec((qb, wqi), lambda i, c: (i * nqb + c, PROJ_OFF['b_qi'] // wqi)),
                  pl.BlockSpec((qb, LANE), lambda i, c: (i * nqb + c, OFF_MISC // LANE)),
                  pl.BlockSpec((None, lp, HK_W), lambda i, c: (i, 0, 0)),
                  pl.BlockSpec((DSA_HEADS, 3, KEY_BLOCK, Q_BLOCK), lambda i, c: (0, 0, 0, 0)),
                  pl.BlockSpec((KEY_TILE, KEY_TILE), lambda i, c: (0, 0))],
        out_specs=pl.BlockSpec((qb, wq), lambda i, c: (i * nqb + c, 0)),
        out_shape=jax.ShapeDtypeStruct((n * t, wq), F32),
        scratch_shapes=[pltpu.VMEM((lp // KEY_TILE, KEY_TILE, qb), jnp.int32),
                        pltpu.VMEM((DSA_KV_HEADS, LANE, (DSA_HEADS // DSA_KV_HEADS) * qb), F32)],
        compiler_params=_params("parallel", "arbitrary"),
        name="dsa_branch",
    )(h, h, h, hk, bias, tril)


def _t5_bucket(rel):
    nb = REL_BUCKETS // 2
    max_exact = nb // 2
    ret = jnp.where(rel > 0, nb, 0)
    dist = jnp.abs(rel)
    distf = jnp.maximum(dist, 1).astype(F32)
    large = max_exact + (jnp.log(distf / max_exact) / math.log(REL_MAX_DIST / max_exact)
                         * (nb - max_exact)).astype(jnp.int32)
    large = jnp.minimum(large, nb - 1)
    return ret + jnp.where(dist < max_exact, dist, large)


def _dsa_bias_tables(rel_bias):
    lc = jnp.arange(KEY_BLOCK, dtype=jnp.int32)[:, None]
    qr = jnp.arange(Q_BLOCK, dtype=jnp.int32)[None, :]
    rel = jnp.stack([lc - qr - back * KEY_BLOCK for back in range(3)], axis=0)
    return jnp.moveaxis(rel_bias[_t5_bucket(rel)].astype(F32), -1, 0) * LOG2E


def _gdn_kernel(raw_ref, misc_ref, s0_ref, conv0_ref, convw_ref, alog_ref, dtb_ref, norm_ref,
                o_ref, s_out, conv_out, xp_s, st_s, *, n, c):
    @pl.when(pl.program_id(0) == 0)
    def _():
        xp_s[:, 0:8, :] = conv0_ref[...]
        st_s[...] = s0_ref[...]

    w = convw_ref[...]
    row = _iota((c, c), 0)
    col = _iota((c, c), 1)
    incl = row >= col
    strict = row > col
    eye = row == col
    tril = jnp.where(incl, 1.0, 0.0)
    eye_f = jnp.where(eye, 1.0, 0.0)
    chains = [(b, hd) for b in range(n) for hd in range(GDN_HEADS)]

    qkv, beta_all, gc_all = [], [], []
    for b in range(n):
        raw = raw_ref[b]
        xp_s[b, 8:8 + c, :] = raw
        y = raw * w[CONV_W - 1:CONV_W, :]
        for i in range(CONV_W - 1):
            y = y + xp_s[b, 5 + i:5 + i + c, :] * w[i:i + 1, :]
        tail = xp_s[b, c:c + 8, :]
        xp_s[b, 0:8, :] = tail
        conv_out[b] = tail
        qkv.append(jax.nn.silu(y))
        misc = misc_ref[b]
        beta_all.append(jax.nn.sigmoid(misc))
        g_all = -jnp.exp(alog_ref[...]) * _softplus(misc + dtb_ref[...])
        gc_all.append(_dot(tril, g_all, precision=HIGHEST))

    qs, ks, vs, betas, gcs, dec_s, dec_i = [], [], [], [], [], [], []
    for b, hd in chains:
        qh = qkv[b][:, hd * GDN_DK:(hd + 1) * GDN_DK]
        kh = qkv[b][:, GDN_QK + hd * GDN_DK:GDN_QK + (hd + 1) * GDN_DK]
        vs.append(qkv[b][:, 2 * GDN_QK + hd * GDN_DV:2 * GDN_QK + (hd + 1) * GDN_DV])
        qs.append(qh * lax.rsqrt(jnp.sum(qh * qh, axis=1, keepdims=True) + RMS_EPS) * (GDN_DK ** -0.5))
        ks.append(kh * lax.rsqrt(jnp.sum(kh * kh, axis=1, keepdims=True) + RMS_EPS))
        betas.append(beta_all[b][:, MISC_BETA + hd:MISC_BETA + hd + 1])
        gc = gc_all[b][:, MISC_A + hd:MISC_A + hd + 1]
        gcs.append(gc)
        gc_b = jnp.broadcast_to(gc, (c, c))
        gc_row = jnp.sum(jnp.where(eye, gc_b, 0.0), axis=0, keepdims=True)
        diff = gc_b - gc_row
        dec_s.append(jnp.where(strict, jnp.exp(jnp.where(strict, diff, 0.0)), 0.0))
        dec_i.append(jnp.where(incl, jnp.exp(jnp.where(incl, diff, 0.0)), 0.0))
    kb = [k.astype(BF16) for k in ks]
    qb = [q.astype(BF16) for q in qs]
    pw = [-(beta * _dot_nt(k, k) * d) for beta, k, d in zip(betas, kb, dec_s)]
    inv = [eye_f + p for p in pw]
    for _ in range(int(math.log2(c)) - 1):
        pw = [_dot3(p, p) for p in pw]
        inv = [i + _dot3(i, p) for i, p in zip(inv, pw)]
    egc = [jnp.exp(gc) for gc in gcs]
    sol = [_dot3(i, jnp.concatenate([(beta * e) * k, beta * v], axis=1))
           for i, beta, e, k, v in zip(inv, betas, egc, ks, vs)]
    s_old = [st_s[b, hd] for b, hd in chains]
    sb = [s.astype(BF16) for s in s_old]
    vb = [(x[:, GDN_DK:] - _dot(x[:, :GDN_DK].astype(BF16), s)).astype(BF16) for x, s in zip(sol, sb)]
    attn = [(_dot_nt(q, k) * d).astype(BF16) for q, k, d in zip(qb, kb, dec_i)]
    outs = [e * _dot(q, s) + _dot(a, v) for e, q, s, a, v in zip(egc, qb, sb, attn, vb)]
    for (b, hd), gc, k, v, s, o in zip(chains, gcs, ks, vb, s_old, outs):
        g_last = gc[c - 1:c, :]
        st_s[b, hd] = jnp.exp(g_last) * s + _dot_tn((k * jnp.exp(g_last - gc)).astype(BF16), v)
        o = o * lax.rsqrt(jnp.mean(o * o, axis=1, keepdims=True) + RMS_EPS) * norm_ref[...]
        o_ref[b, :, hd * GDN_DV:(hd + 1) * GDN_DV] = o
    s_out[...] = st_s[...]


def _gdn_call(h3, s0, conv0, gp):
    n, t, _ = h3.shape
    c = min(t, CHUNK)
    const = lambda shape: pl.BlockSpec(shape, lambda j: (0,) * len(shape))
    state = const((n, GDN_HEADS, GDN_DK, GDN_DV))
    conv = const((n, 8, GDN_CONV_CH))
    return pl.pallas_call(
        functools.partial(_gdn_kernel, n=n, c=c),
        grid=(t // c,),
        in_specs=[pl.BlockSpec((n, c, GDN_CONV_CH), lambda j: (0, j, PROJ_OFF['c_qkv'] // GDN_CONV_CH)),
                  pl.BlockSpec((n, c, LANE), lambda j: (0, j, OFF_MISC // LANE)),
                  state, conv,
                  const((CONV_W, GDN_CONV_CH)), const((1, LANE)), const((1, LANE)), const((1, GDN_DV))],
        out_specs=[pl.BlockSpec((n, c, GDN_VW), lambda j: (0, j, 0)), state, conv],
        out_shape=[jax.ShapeDtypeStruct((n, t, GDN_VW), F32),
                   jax.ShapeDtypeStruct((n, GDN_HEADS, GDN_DK, GDN_DV), F32),
                   jax.ShapeDtypeStruct((n, 8, GDN_CONV_CH), F32)],
        scratch_shapes=[pltpu.VMEM((n, c + 8, GDN_CONV_CH), F32),
                        pltpu.VMEM((n, GDN_HEADS, GDN_DK, GDN_DV), F32)],
        compiler_params=_params("arbitrary"),
        name="gdn_branch",
    )(h3, h3, s0, conv0, gp['convw'], gp['alog'], gp['dtb'], gp['norm'])


def _lane_place(v, off):
    return jnp.zeros((1, LANE), F32).at[0, off:off + v.shape[0]].set(v.astype(F32))


def _gla_kernel(qk_ref, v_ref, misc_ref, s0_ref, wg_ref, wgt_ref, bg_ref, bgc_ref, norm_ref, hexp_ref, gsum_ref,
                o_ref, s_out, st_s, *, n, tc):
    blk = GLA_BLOCK
    wk = GLA_HEADS * GLA_DK

    @pl.when(pl.program_id(0) == 0)
    def _():
        st_s[...] = s0_ref[...]

    row = _iota((blk, blk), 0)
    col = _iota((blk, blk), 1)
    tril = jnp.where(row >= col, 1.0, 0.0)
    jrow = _iota((blk, wk), 0)
    lane_k = _iota((blk, wk), 1)
    head_mask = [(lane_k >= hd * GLA_DK) & (lane_k < (hd + 1) * GLA_DK) for hd in range(GLA_HEADS)]
    chains = range(n)

    def block(sb, carry):
        r0 = pl.multiple_of(sb * blk, blk)
        qk = [qk_ref[b, pl.ds(r0, blk), :] for b in chains]
        q = [x[:, :wk] * (GLA_DK ** -0.5) for x in qk]
        k = [x[:, wk:] for x in qk]
        v = [v_ref[b, pl.ds(r0, blk), :] for b in chains]
        mb = [misc_ref[b, pl.ds(r0, blk), :].astype(BF16) for b in chains]
        lg = [jax.nn.log_sigmoid(_dot(m, wg_ref[...]) + bg_ref[...]) * (1.0 / GLA_TAU) for m in mb]
        lg_t = [jax.nn.log_sigmoid(_dot_nt(wgt_ref[...], m) + bgc_ref[:, 0:blk]) * (1.0 / GLA_TAU)
                for m in mb]
        bc = [_dot(tril, x, precision=HIGHEST) for x in lg]
        dmat = []
        for b in chains:
            slabs = []
            for i in range(blk):
                keep = jrow <= i
                dec = jnp.where(keep, jnp.exp(jnp.where(keep, bc[b][i:i + 1, :] - bc[b], 0.0)), 0.0)
                slabs.append(dec * q[b][i:i + 1, :] * k[b])
            dmat.append(jnp.concatenate(slabs, axis=0).astype(BF16))
        a_exp = [_dot(d, hexp_ref[...]) for d in dmat]
        prod = [a * jnp.concatenate([x] * blk, axis=0) for a, x in zip(a_exp, v)]
        o = [_dot3(gsum_ref[...], p) for p in prod]
        qe = [x * jnp.exp(c) for x, c in zip(q, bc)]
        kt = [x * jnp.exp(c[blk - 1:blk, :] - c) for x, c in zip(k, bc)]
        s = [st_s[b] for b in chains]
        sbf = [x.astype(BF16) for x in s]
        for b in chains:
            upd = jnp.zeros_like(s[b])
            parts = []
            for hd in range(GLA_HEADS):
                parts.append(_dot(jnp.where(head_mask[hd], qe[b], 0.0).astype(BF16), sbf[b]))
                vh = v[b][:, hd * GLA_DV:(hd + 1) * GLA_DV].astype(BF16)
                upd = upd + _dot_tn(jnp.where(head_mask[hd], kt[b], 0.0).astype(BF16), vh)
            ob = o[b] + jnp.concatenate(parts, axis=1)
            decay = jnp.exp(jnp.sum(lg_t[b], axis=1, keepdims=True))
            st_s[b] = decay * s[b] + upd
            for hd in range(GLA_HEADS):
                oh = ob[:, hd * GLA_DV:(hd + 1) * GLA_DV]
                oh = oh * lax.rsqrt(jnp.mean(oh * oh, axis=1, keepdims=True) + RMS_EPS) * norm_ref[...]
                o_ref[b, pl.ds(r0, blk), hd * GLA_DV:(hd + 1) * GLA_DV] = oh
        return carry

    lax.fori_loop(0, tc // blk, block, 0)
    s_out[...] = st_s[...]


def _gla_call(h3, s0, gp):
    n, t, _ = h3.shape
    tc = min(t, 128)
    wk = GLA_HEADS * GLA_DK
    wv = GLA_HEADS * GLA_DV
    const = lambda shape: pl.BlockSpec(shape, lambda j: (0,) * len(shape))
    state = const((n, wk, GLA_DV))
    return pl.pallas_call(
        functools.partial(_gla_kernel, n=n, tc=tc),
        grid=(t // tc,),
        in_specs=[pl.BlockSpec((n, tc, 2 * wk), lambda j: (0, j, PROJ_OFF['d_q'] // (2 * wk))),
                  pl.BlockSpec((n, tc, wv), lambda j: (0, j, PROJ_OFF['d_v'] // wv)),
                  pl.BlockSpec((n, tc, LANE), lambda j: (0, j, OFF_MISC // LANE)),
                  state,
                  const((LANE, wk)), const((wk, LANE)), const((1, wk)), const((wk, LANE)), const((1, GLA_DV)),
                  const((wk, wv)), const((GLA_BLOCK, GLA_BLOCK * GLA_BLOCK))],
        out_specs=[pl.BlockSpec((n, tc, wv), lambda j: (0, j, 0)), state],
        out_shape=[jax.ShapeDtypeStruct((n, t, wv), F32),
                   jax.ShapeDtypeStruct((n, wk, GLA_DV), F32)],
        scratch_shapes=[pltpu.VMEM((n, wk, GLA_DV), F32)],
        compiler_params=_params("arbitrary"),
        name="gla_branch",
    )(h3, h3, h3, s0, gp['wg'], gp['wgt'], gp['bg'], gp['bgc'], gp['norm'], gp['hexp'], gp['gsum'])


def _gla_prepare(w_g2, b_g, norm):
    wk = GLA_HEADS * GLA_DK
    wg = jnp.zeros((LANE, wk), F32).at[MISC_G:MISC_G + GLA_GATE_RANK].set(w_g2).astype(BF16)
    ck = jnp.arange(wk)[:, None] // GLA_DK
    cv = jnp.arange(GLA_HEADS * GLA_DV)[None, :] // GLA_DV
    ij = jnp.arange(GLA_BLOCK * GLA_BLOCK)[None, :] // GLA_BLOCK
    return {'wg': wg, 'wgt': wg.T, 'bg': b_g.reshape(1, wk).astype(F32),
            'bgc': jnp.broadcast_to(b_g.astype(F32)[:, None], (wk, LANE)),
            'norm': norm.reshape(1, GLA_DV).astype(F32),
            'hexp': (ck == cv).astype(BF16),
            'gsum': (jnp.arange(GLA_BLOCK)[:, None] == ij).astype(F32)}


def _merge_kernel(x_ref, ya_ref, yb_ref, yc_ref, yd_ref, ga_ref, gb_ref, gc_ref, gd_ref, mg_ref,
                  wb_ref, wo_ref, lng_ref, lnb_ref, o_ref):
    mixed = None
    for br, (y_ref, g_ref) in enumerate(((ya_ref, ga_ref), (yb_ref, gb_ref), (yc_ref, gc_ref), (yd_ref, gd_ref))):
        act = (y_ref[...] * jax.nn.silu(g_ref[...])).astype(BF16)
        term = jax.nn.sigmoid(mg_ref[:, br * D_MODEL:(br + 1) * D_MODEL]) * _dot(act, wb_ref[br])
        mixed = term if mixed is None else mixed + term
    z = DN_ALPHA * x_ref[...] + _dot(mixed.astype(BF16), wo_ref[...])
    mu = jnp.mean(z, axis=1, keepdims=True)
    zc = z - mu
    var = jnp.mean(zc * zc, axis=1, keepdims=True)
    o_ref[...] = zc * lax.rsqrt(var + LN_EPS) * lng_ref[...] + lnb_ref[...]


def _merge_call(x, h, ya, yb, yc, yd, wb, wo, lng, lnb):
    rows = x.shape[0]
    tm = min(rows, 256)
    bw = BRANCH_W
    ybs = pl.BlockSpec((tm, bw), lambda i: (i, 0))
    gate = lambda name: pl.BlockSpec((tm, bw), lambda i: (i, PROJ_OFF[name] // bw))
    return pl.pallas_call(
        _merge_kernel,
        grid=(rows // tm,),
        in_specs=[pl.BlockSpec((tm, D_MODEL), lambda i: (i, 0)), ybs, ybs, ybs, ybs,
                  gate('a_gate'), gate('b_gate'), gate('c_gate'), gate('d_gate'),
                  pl.BlockSpec((tm, N_BRANCH * D_MODEL), lambda i: (i, 0)),
                  pl.BlockSpec((N_BRANCH, bw, D_MODEL), lambda i: (0, 0, 0)),
                  pl.BlockSpec((D_MODEL, D_MODEL), lambda i: (0, 0)),
                  pl.BlockSpec((1, D_MODEL), lambda i: (0, 0)),
                  pl.BlockSpec((1, D_MODEL), lambda i: (0, 0))],
        out_specs=pl.BlockSpec((tm, D_MODEL), lambda i: (i, 0)),
        out_shape=jax.ShapeDtypeStruct((rows, D_MODEL), F32),
        compiler_params=_params("parallel"),
        name="merge_norm",
    )(x, ya, yb, yc, yd, h, h, h, h, h, wb, wo, lng, lnb)


def _split_w_in(w_in):
    out, off = {}, 0
    for name, width in IN_LAYOUT:
        out[name] = w_in[:, off:off + width]
        off += width
    return out


def _layer_weights(w_in):
    cols = _split_w_in(w_in)
    w_proj = jnp.concatenate([cols[name] for name in PROJ_ORDER]
                             + [jnp.zeros((D_MODEL, MISC_PAD), w_in.dtype)], axis=1).astype(BF16)
    k, v, ki = cols['b_k'], cols['b_v'], cols['b_ki']
    dh = DSA_HEAD_DIM
    w_key = jnp.concatenate([k[:, :dh], k[:, :dh], k[:, dh:], k[:, dh:],
                             v[:, :dh], v[:, :dh], v[:, dh:], v[:, dh:], ki, ki], axis=1).astype(BF16)
    return w_proj, w_key


def _dup_keys(k, v, ki):
    return jnp.concatenate([k[:, :, 0], k[:, :, 0], k[:, :, 1], k[:, :, 1],
                            v[:, :, 0], v[:, :, 0], v[:, :, 1], v[:, :, 1], ki, ki], axis=-1).astype(BF16)


def _layer(x, lw, st):
    n, t, _ = x.shape
    x2 = x.reshape(n * t, D_MODEL)
    h = _matmul(x2, lw['w_proj'], F32, tn=PROJ_W // 9)
    hk_new = _matmul(x2, lw['w_key'], BF16, tn=HK_W).reshape(n, t, HK_W)
    if st is None:
        hk, n_keys, b_off = hk_new, t, 0
        s5r = jnp.zeros((n, 1, S5_LANES), F32)
        s5i = jnp.zeros((n, 1, S5_LANES), F32)
        gdn0 = jnp.zeros((n, GDN_HEADS, GDN_DK, GDN_DV), F32)
        conv0 = jnp.zeros((n, 8, GDN_CONV_CH), F32)
        gla0 = jnp.zeros((n, GLA_HEADS * GLA_DK, GLA_DV), F32)
    else:
        past = st['k'].shape[1]
        n_keys = past + t
        b_off = past // Q_BLOCK
        lp = -(-n_keys // KEY_TILE) * KEY_TILE
        hk = jnp.concatenate([_dup_keys(st['k'], st['v'], st['kidx']), hk_new,
                              jnp.zeros((n, lp - n_keys, HK_W), BF16)], axis=1)
        s5r = st['s5_re'].reshape(n, 1, S5_LANES)
        s5i = st['s5_im'].reshape(n, 1, S5_LANES)
        gdn0 = st['gdn']
        conv0 = jnp.concatenate([jnp.zeros((n, 8 - (CONV_W - 1), GDN_CONV_CH), F32), st['gdn_conv']], axis=1)
        gla0 = st['gla'].reshape(n, GLA_HEADS * GLA_DK, GLA_DV)
    topk = min(DSA_TOPK_MAX, n_keys // 4)

    ya, s5r_new, s5i_new = _s5_call(h, n, t, s5r, s5i, lw['s5'])
    yb = _dsa_call(h, hk, n, t, lw['bias'], lw['tril'], b_off=b_off, n_keys=n_keys, topk=topk)
    h3 = h.reshape(n, t, PROJ_W)
    yc, gdn_new, conv_new = _gdn_call(h3, gdn0, conv0, lw['gdn'])
    yd, gla_new = _gla_call(h3, gla0, lw['gla'])
    yc = yc.reshape(n * t, GDN_VW)
    yd = yd.reshape(n * t, GLA_HEADS * GLA_DV)
    y = _merge_call(x2, h, ya, yb, yc, yd, lw['w_branch'], lw['w_out'], lw['ln_g'], lw['ln_b'])

    def cols(name, width):
        return h[:, PROJ_OFF[name]:PROJ_OFF[name] + width].reshape(n, t, width)

    new = {'k': cols('b_k', DSA_KV_HEADS * DSA_HEAD_DIM).reshape(n, t, DSA_KV_HEADS, DSA_HEAD_DIM),
           'v': cols('b_v', DSA_KV_HEADS * DSA_HEAD_DIM).reshape(n, t, DSA_KV_HEADS, DSA_HEAD_DIM),
           'kidx': cols('b_ki', IDX_DIM),
           's5_re': s5r_new.reshape(n, S5_GROUPS, S5_STATE), 's5_im': s5i_new.reshape(n, S5_GROUPS, S5_STATE),
           'gdn': gdn_new, 'gdn_conv': conv_new[:, 8 - (CONV_W - 1):],
           'gla': gla_new.reshape(n, GLA_HEADS, GLA_DK, GLA_DV)}
    return y.reshape(n, t, D_MODEL), new


def kernel(x_prompt, x_sample, cache_k, cache_v, cache_kidx, state_s5_re, state_s5_im, state_gdn, state_gdn_conv, state_gla, w_in, w_branch, w_out, ln_g, ln_b, rel_bias, s5_a_re, s5_a_im, s5_log_dt, s5_b_re, s5_b_im, s5_c_re, s5_c_im, s5_d, s5_w_glu, gdn_conv, gdn_a_log, gdn_dt_bias, gdn_norm, gla_w_g2, gla_b_g, gla_norm):
    bias = _dsa_bias_tables(rel_bias)
    tril = (jnp.arange(KEY_TILE)[:, None] >= jnp.arange(KEY_TILE)[None, :]).astype(BF16)
    yp, ys = x_prompt, x_sample
    new_p, new_s = [], []
    for l in range(DEPTH):
        w_proj, w_key = _layer_weights(w_in[l])
        lw = {'w_proj': w_proj, 'w_key': w_key, 'bias': bias, 'tril': tril,
              'w_branch': w_branch[l].astype(BF16), 'w_out': w_out[l].astype(BF16),
              'ln_g': ln_g[l].reshape(1, D_MODEL), 'ln_b': ln_b[l].reshape(1, D_MODEL),
              's5': _s5_prepare(s5_a_re[l], s5_a_im[l], s5_log_dt[l], s5_b_re[l], s5_b_im[l],
                                s5_c_re[l], s5_c_im[l], s5_d[l], s5_w_glu[l]),
              'gdn': {'convw': gdn_conv[l], 'alog': _lane_place(gdn_a_log[l], MISC_A),
                      'dtb': _lane_place(gdn_dt_bias[l], MISC_A), 'norm': gdn_norm[l].reshape(1, GDN_DV)},
              'gla': _gla_prepare(gla_w_g2[l], gla_b_g[l], gla_norm[l])}
        st = {'k': cache_k[l], 'v': cache_v[l], 'kidx': cache_kidx[l], 's5_re': state_s5_re[l],
              's5_im': state_s5_im[l], 'gdn': state_gdn[l], 'gdn_conv': state_gdn_conv[l], 'gla': state_gla[l]}
        yp, stp = _layer(yp, lw, None)
        ys, sts = _layer(ys, lw, st)
        new_p.append(stp)
        new_s.append(sts)
    names = ('k', 'v', 'kidx', 's5_re', 's5_im', 'gdn', 'gdn_conv', 'gla')
    stack = lambda states, name: jnp.stack([s[name] for s in states], axis=0)
    return ((yp, ys) + tuple(stack(new_p, nm) for nm in names) + tuple(stack(new_s, nm) for nm in names))
```

```python
import functools
import math

import jax
import jax.numpy as jnp
from jax import lax
from jax.experimental import pallas as pl
from jax.experimental.pallas import tpu as pltpu

F32 = jnp.float32
BF16 = jnp.bfloat16
HIGHEST = lax.Precision.HIGHEST

D_MODEL = 1024
DEPTH = 2
CHUNK = 64
N_BRANCH = 4
BRANCH_W = 512
S5_GROUP = 16
S5_GROUPS = BRANCH_W // S5_GROUP
S5_STATE = 64
S5_LANES = S5_GROUPS * S5_STATE
DSA_HEADS = 8
DSA_KV_HEADS = 2
DSA_HEAD_DIM = 64
IDX_HEADS = 4
IDX_DIM = 64
DSA_TOPK_MAX = 256
REL_BUCKETS = 32
REL_MAX_DIST = 128
GDN_HEADS = 4
GDN_DK = 128
GDN_DV = 128
GDN_QK = GDN_HEADS * GDN_DK
GDN_VW = GDN_HEADS * GDN_DV
CONV_W = 4
GDN_CONV_CH = 2 * GDN_QK + GDN_VW
GLA_HEADS = 4
GLA_DK = 64
GLA_DV = 128
GLA_GATE_RANK = 16
GLA_TAU = 16.0
GLA_BLOCK = 16
LN_EPS = 1e-5
RMS_EPS = 1e-6
DN_ALPHA = (2 * DEPTH) ** 0.25
LOG2E = math.log2(math.e)

LANE = 128
KEY_BLOCK = 128
KEY_SUB = 4
KEY_TILE = KEY_SUB * KEY_BLOCK
FOLD_ROWS = 64
Q_BLOCK = 128
VMEM_LIMIT = 56 * 1024 * 1024

IN_LAYOUT = (
    ('a_u', BRANCH_W), ('a_gate', BRANCH_W),
    ('b_q', DSA_HEADS * DSA_HEAD_DIM), ('b_k', DSA_KV_HEADS * DSA_HEAD_DIM),
    ('b_v', DSA_KV_HEADS * DSA_HEAD_DIM), ('b_qi', IDX_HEADS * IDX_DIM), ('b_ki', IDX_DIM),
    ('b_wi', IDX_HEADS), ('b_gate', BRANCH_W),
    ('c_qkv', GDN_CONV_CH), ('c_beta', GDN_HEADS), ('c_a', GDN_HEADS), ('c_gate', BRANCH_W),
    ('d_q', GLA_HEADS * GLA_DK), ('d_k', GLA_HEADS * GLA_DK), ('d_v', GLA_HEADS * GLA_DV),
    ('d_g', GLA_GATE_RANK), ('d_gate', BRANCH_W),
    ('merge', N_BRANCH * D_MODEL),
)

PROJ_ORDER = ('merge', 'a_u', 'a_gate', 'b_q', 'b_qi', 'b_k', 'b_v', 'c_qkv', 'b_gate', 'c_gate',
              'd_v', 'd_gate', 'd_q', 'd_k', 'b_ki', 'b_wi', 'c_beta', 'c_a', 'd_g')
MISC_WI = IDX_DIM
MISC_BETA = MISC_WI + IDX_HEADS
MISC_A = MISC_BETA + GDN_HEADS
MISC_G = MISC_A + GDN_HEADS
MISC_PAD = LANE - (MISC_G + GLA_GATE_RANK)


def _proj_offsets():
    widths = dict(IN_LAYOUT)
    off, out = 0, {}
    for name in PROJ_ORDER:
        out[name] = off
        off += widths[name]
    return out, off + MISC_PAD


PROJ_OFF, PROJ_W = _proj_offsets()
OFF_MISC = PROJ_OFF['b_ki']
HK_W = 5 * LANE
HK_V = 2 * LANE
HK_KI = 4 * LANE


def _dot(a, b, precision=None):
    return jnp.dot(a, b, preferred_element_type=F32, precision=precision)


def _dot_nt(a, b, precision=None):
    return lax.dot_general(a, b, (((1,), (1,)), ((), ())), preferred_element_type=F32, precision=precision)


def _dot_tn(a, b, precision=None):
    return lax.dot_general(a, b, (((0,), (0,)), ((), ())), preferred_element_type=F32, precision=precision)


def _split_bf16(x):
    hi = x.astype(BF16)
    return hi, (x - hi.astype(F32)).astype(BF16)


def _dot3(a, b):
    a_hi, a_lo = _split_bf16(a)
    b_hi, b_lo = _split_bf16(b)
    return _dot(a_hi, b_hi) + (_dot(a_hi, b_lo) + _dot(a_lo, b_hi))


def _params(*sem):
    return pltpu.CompilerParams(dimension_semantics=sem, vmem_limit_bytes=VMEM_LIMIT)


def _softplus(x):
    return jnp.maximum(x, 0.0) + jnp.log1p(jnp.exp(-jnp.abs(x)))


def _iota(shape, dim):
    return lax.broadcasted_iota(jnp.int32, shape, dim)


def _mm_kernel(x_ref, w_ref, o_ref):
    o_ref[...] = _dot(x_ref[...].astype(BF16), w_ref[...]).astype(o_ref.dtype)


def _matmul(x, w, out_dtype, tn):
    rows, kdim = x.shape
    width = w.shape[1]
    tm = min(rows, 1024)
    return pl.pallas_call(
        _mm_kernel,
        grid=(rows // tm, width // tn),
        in_specs=[pl.BlockSpec((tm, kdim), lambda i, j: (i, 0)),
                  pl.BlockSpec((kdim, tn), lambda i, j: (0, j))],
        out_specs=pl.BlockSpec((tm, tn), lambda i, j: (i, j)),
        out_shape=jax.ShapeDtypeStruct((rows, width), out_dtype),
        compiler_params=_params("parallel", "parallel"),
        name="proj_matmul",
    )(x, w)


def _s5_kernel(u_ref, h0r_ref, h0i_ref, ar_ref, ai_ref, bbr_ref, bbi_ref, ccr_ref, cci_ref, d_ref, wglu_ref,
               ya_ref, hr_out, hi_out, xr_s, xi_s, cr_s, ci_s, *, tc):
    @pl.when(pl.program_id(1) == 0)
    def _():
        cr_s[...] = h0r_ref[...]
        ci_s[...] = h0i_ref[...]

    u = u_ref[...]
    ub = u.astype(BF16)
    xr_s[...] = _dot(ub, bbr_ref[...])
    xi_s[...] = _dot(ub, bbi_ref[...])
    ar = ar_ref[...]
    ai = ai_ref[...]

    def step(t, carry):
        hr, hi = carry
        nhr = ar * hr - ai * hi + xr_s[pl.ds(t, 1), :]
        nhi = ar * hi + ai * hr + xi_s[pl.ds(t, 1), :]
        xr_s[pl.ds(t, 1), :] = nhr
        xi_s[pl.ds(t, 1), :] = nhi
        return nhr, nhi

    hr, hi = lax.fori_loop(0, tc, step, (cr_s[...], ci_s[...]), unroll=8)
    cr_s[...] = hr
    ci_s[...] = hi
    hr_out[...] = hr
    hi_out[...] = hi
    y = (_dot(xr_s[...].astype(BF16), ccr_ref[...]) - _dot(xi_s[...].astype(BF16), cci_ref[...])
         + d_ref[...] * u)
    y = jax.nn.gelu(y)
    g = _dot(y.astype(BF16), wglu_ref[...])
    ya_ref[...] = g[:, :BRANCH_W] * jax.nn.sigmoid(g[:, BRANCH_W:])


def _s5_call(h, n, t, h0r, h0i, sp):
    tc = min(t, 256)
    nc = t // tc
    const = lambda shape: pl.BlockSpec(shape, lambda i, c: (0,) * len(shape))
    state = pl.BlockSpec((None, 1, S5_LANES), lambda i, c: (i, 0, 0))
    return pl.pallas_call(
        functools.partial(_s5_kernel, tc=tc),
        grid=(n, nc),
        in_specs=[pl.BlockSpec((tc, BRANCH_W), lambda i, c: (i * nc + c, PROJ_OFF['a_u'] // BRANCH_W)),
                  state, state,
                  const((1, S5_LANES)), const((1, S5_LANES)),
                  const((BRANCH_W, S5_LANES)), const((BRANCH_W, S5_LANES)),
                  const((S5_LANES, BRANCH_W)), const((S5_LANES, BRANCH_W)),
                  const((1, BRANCH_W)), const((BRANCH_W, 2 * BRANCH_W))],
        out_specs=[pl.BlockSpec((tc, BRANCH_W), lambda i, c: (i * nc + c, 0)), state, state],
        out_shape=[jax.ShapeDtypeStruct((n * t, BRANCH_W), F32),
                   jax.ShapeDtypeStruct((n, 1, S5_LANES), F32),
                   jax.ShapeDtypeStruct((n, 1, S5_LANES), F32)],
        scratch_shapes=[pltpu.VMEM((tc, S5_LANES), F32), pltpu.VMEM((tc, S5_LANES), F32),
                        pltpu.VMEM((1, S5_LANES), F32), pltpu.VMEM((1, S5_LANES), F32)],
        compiler_params=_params("parallel", "arbitrary"),
        name="s5_branch",
    )(h, h0r, h0i, sp['ar'], sp['ai'], sp['bbr'], sp['bbi'], sp['ccr'], sp['cci'], sp['d'], sp['wglu'])


def _s5_prepare(a_re, a_im, log_dt, b_re, b_im, c_re, c_im, d, w_glu):
    lam_r = jnp.minimum(a_re, -1e-4)
    lam_i = a_im
    dt = jnp.exp(log_dt)[:, None]
    mag = jnp.exp(lam_r * dt)
    ar = mag * jnp.cos(lam_i * dt)
    ai = mag * jnp.sin(lam_i * dt)
    den = lam_r * lam_r + lam_i * lam_i
    nr = ar - 1.0
    cr = (nr * lam_r + ai * lam_i) / den
    ci = (ai * lam_r - nr * lam_i) / den
    bb_r = cr[..., None] * b_re - ci[..., None] * b_im
    bb_i = cr[..., None] * b_im + ci[..., None] * b_re
    eye = jnp.eye(S5_GROUPS, dtype=F32)

    def blockdiag_in(bb):
        m = eye[:, None, :, None] * jnp.swapaxes(bb, 1, 2)[:, :, None, :]
        return m.reshape(BRANCH_W, S5_LANES).astype(BF16)

    def blockdiag_out(c):
        m = eye[:, None, :, None] * jnp.swapaxes(c, 1, 2)[:, :, None, :]
        return m.reshape(S5_LANES, BRANCH_W).astype(BF16)

    return {'ar': ar.reshape(1, S5_LANES), 'ai': ai.reshape(1, S5_LANES),
            'bbr': blockdiag_in(bb_r), 'bbi': blockdiag_in(bb_i),
            'ccr': blockdiag_out(c_re), 'cci': blockdiag_out(c_im),
            'd': d.reshape(1, BRANCH_W), 'wglu': w_glu.astype(BF16)}


def _sortable(x):
    bits = lax.bitcast_convert_type(x, jnp.int32)
    key = bits ^ ((bits >> 31) & jnp.int32(0x7FFFFFFF))
    return jnp.where(key == -1, 0, key)


def _dsa_kernel(q_ref, qi_ref, misc_ref, hk_ref, bias_ref, tril_ref, o_ref, sc_s, acc_s, *, qb, b_off, n_keys, topk):
    b = pl.program_id(1) + b_off
    nkt = (b + KEY_SUB) // KEY_SUB
    fold = KEY_TILE // FOLD_ROWS
    key_row = _iota((KEY_TILE, qb), 0)
    qpos = b * Q_BLOCK + _iota((1, qb), 1)
    limit = jnp.minimum(((qpos >> 6) + 1) * CHUNK, n_keys)
    low_half = _iota((qb, LANE), 1) < DSA_HEAD_DIM

    def key_rows(ref_cols, j):
        return hk_ref[pl.ds(pl.multiple_of(j * KEY_TILE, KEY_TILE), KEY_TILE), ref_cols:ref_cols + LANE]

    def half_masked(x, hd):
        pair = x[:, (hd // 2) * LANE:(hd // 2 + 1) * LANE]
        return jnp.where(low_half == (hd % 2 == 0), pair, 0.0).astype(BF16)

    misc = misc_ref[...]
    pick = jnp.where(_iota((8, LANE), 1) == MISC_WI + _iota((8, LANE), 0), 1.0, 0.0)
    wi_t = _dot_nt(pick, misc, precision=HIGHEST) * (IDX_DIM ** -0.5 * IDX_HEADS ** -0.5)
    qi = qi_ref[...]
    qi_stack = jnp.concatenate([half_masked(qi, hd) for hd in range(IDX_HEADS)], axis=0)

    def score_block(j, carry):
        z = _dot_nt(key_rows(HK_KI, j), qi_stack)
        sc = jnp.zeros((KEY_TILE, qb), F32)
        for hd in range(IDX_HEADS):
            sc = sc + jnp.maximum(z[:, hd * qb:(hd + 1) * qb], 0.0) * wi_t[hd:hd + 1, :]
        sc = jnp.where(j * KEY_TILE + key_row < limit, sc, -jnp.inf)
        sc_s[j] = _sortable(sc)
        return carry

    lax.fori_loop(0, nkt, score_block, 0)

    def count_ge(cand):
        def body(j, acc):
            return acc + jnp.where(sc_s[j] >= cand, 1.0, 0.0).reshape(fold, FOLD_ROWS, qb).sum(axis=0)
        acc = lax.fori_loop(0, nkt, body, jnp.zeros((FOLD_ROWS, qb), F32))
        return jnp.sum(acc, axis=0, keepdims=True)

    kf = float(topk)
    res = jnp.where(count_ge(jnp.zeros((1, qb), jnp.int32)) >= kf, 0, jnp.iinfo(jnp.int32).min)

    def bit_step(i, res):
        cand = res + (jnp.int32(1) << (30 - i))
        return jnp.where(count_ge(cand) >= kf, cand, res)

    thr = lax.fori_loop(0, 31, bit_step, res)
    need = kf - count_ge(thr + 1)

    tril = tril_ref[...]

    def select_block(j, ties):
        keys = sc_s[j]
        eq = keys == thr
        eq_b = jnp.where(eq, 1.0, 0.0).astype(BF16)
        parts, carry = [], ties
        for s in range(KEY_SUB):
            part = _dot(tril, eq_b[s * KEY_BLOCK:(s + 1) * KEY_BLOCK]) + carry
            parts.append(part)
            carry = part[KEY_BLOCK - 1:KEY_BLOCK, :]
        prefix = jnp.concatenate(parts, axis=0)
        sel = ((keys > thr) | (eq & (prefix <= need))) & (j * KEY_TILE + key_row < limit)
        sc_s[j] = lax.bitcast_convert_type(jnp.where(sel, 0.0, -jnp.inf), jnp.int32)
        return prefix[KEY_TILE - 1:KEY_TILE, :]

    lax.fori_loop(0, nkt, select_block, jnp.zeros((1, qb), F32))

    q = q_ref[...] * (DSA_HEAD_DIM ** -0.5 * LOG2E)
    rep = DSA_HEADS // DSA_KV_HEADS
    eye = jnp.where(_iota((LANE, LANE), 0) == _iota((LANE, LANE), 1), 1.0, 0.0)
    value_lane = _iota((KEY_TILE, LANE), 1) < DSA_HEAD_DIM
    n_pairs = DSA_HEADS // 2
    q_pair = [jnp.concatenate([half_masked(q, 2 * c), half_masked(q, 2 * c + 1)], axis=0) for c in range(n_pairs)]
    far_bias = [bias_ref[hd, 2, 0:1, 0:qb] for hd in range(DSA_HEADS)]
    acc_s[...] = jnp.zeros_like(acc_s)

    def attend(j, ms, far):
        mask = lax.bitcast_convert_type(sc_s[j], F32)
        kd = [key_rows(grp * LANE, j) for grp in range(DSA_KV_HEADS)]
        zs = [_dot_nt(kd[(2 * c) // rep], q_pair[c]) for c in range(n_pairs)]
        new_ms, ps, alphas = [], [], []
        for hd in range(DSA_HEADS):
            r = hd % 2
            zh = zs[hd // 2][:, r * qb:(r + 1) * qb] + mask
            if not far:
                bias = [bias_ref[hd, jnp.clip(b - (KEY_SUB * j + s), 0, 2), :, 0:qb] for s in range(KEY_SUB)]
                zh = zh + jnp.concatenate(bias, axis=0)
            tile_max = jnp.max(zh.reshape(fold, FOLD_ROWS, qb).max(axis=0), axis=0, keepdims=True)
            shift = far_bias[hd] if far else 0.0
            m_new = jnp.maximum(ms[hd], tile_max + shift)
            ps.append(jnp.exp2(zh - (m_new - shift)).astype(BF16))
            alphas.append(jnp.exp2(ms[hd] - m_new))
            new_ms.append(m_new)
        vd = [jnp.where(value_lane, key_rows(HK_V + grp * LANE, j), 1.0).astype(BF16)
              for grp in range(DSA_KV_HEADS)]
        for c in range(n_pairs):
            acc_s[c] = (acc_s[c] * jnp.concatenate(alphas[2 * c:2 * c + 2], axis=1)
                        + _dot_tn(vd[(2 * c) // rep], jnp.concatenate(ps[2 * c:2 * c + 2], axis=1)))
        return tuple(new_ms)

    n_far = jnp.maximum((b - 1) // KEY_SUB, 0)
    ms = tuple(jnp.full((1, qb), -1e30, F32) for _ in range(DSA_HEADS))
    ms = lax.fori_loop(0, n_far, lambda j, m: attend(j, m, True), ms)
    lax.fori_loop(n_far, nkt, lambda j, m: attend(j, m, False), ms)
    for c in range(n_pairs):
        halves = []
        for r in range(2):
            blk = acc_s[c, :, r * qb:(r + 1) * qb]
            halves.append(blk[0:DSA_HEAD_DIM] / blk[DSA_HEAD_DIM:DSA_HEAD_DIM + 1])
        o_t = jnp.concatenate(halves, axis=0)
        o_ref[:, c * LANE:(c + 1) * LANE] = _dot_tn(o_t, eye, precision=HIGHEST)


def _dsa_call(h, hk, n, t, bias, tril, *, b_off, n_keys, topk):
    qb = min(t, Q_BLOCK)
    nqb = t // qb
    lp = hk.shape[1]
    wq = DSA_HEADS * DSA_HEAD_DIM
    wqi = IDX_HEADS * IDX_DIM
    return pl.pallas_call(
        functools.partial(_dsa_kernel, qb=qb, b_off=b_off, n_keys=n_keys, topk=topk),
        grid=(n, nqb),
        in_specs=[pl.BlockSpec((qb, wq), lambda i, c: (i * nqb + c, PROJ_OFF['b_q'] // wq)),
                  pl.BlockSpec((qb, wqi), lambda i, c: (i * nqb + c, PROJ_OFF['b_qi'] // wqi)),
                  pl.BlockSpec((qb, LANE), lambda i, c: (i * nqb + c, OFF_MISC // LANE)),
                  pl.BlockSpec((None, lp, HK_W), lambda i, c: (i, 0, 0)),
                  pl.BlockSpec((DSA_HEADS, 3, KEY_BLOCK, Q_BLOCK), lambda i, c: (0, 0, 0, 0)),
                  pl.BlockSpec((KEY_BLOCK, KEY_BLOCK), lambda i, c: (0, 0))],
        out_specs=pl.BlockSpec((qb, wq), lambda i, c: (i * nqb + c, 0)),
        out_shape=jax.ShapeDtypeStruct((n * t, wq), F32),
        scratch_shapes=[pltpu.VMEM((lp // KEY_TILE, KEY_TILE, qb), jnp.int32),
                        pltpu.VMEM((DSA_HEADS // 2, LANE, 2 * qb), F32)],
        compiler_params=_params("parallel", "arbitrary"),
        name="dsa_branch",
    )(h, h, h, hk, bias, tril)


def _t5_bucket(rel):
    nb = REL_BUCKETS // 2
    max_exact = nb // 2
    ret = jnp.where(rel > 0, nb, 0)
    dist = jnp.abs(rel)
    distf = jnp.maximum(dist, 1).astype(F32)
    large = max_exact + (jnp.log(distf / max_exact) / math.log(REL_MAX_DIST / max_exact)
                         * (nb - max_exact)).astype(jnp.int32)
    large = jnp.minimum(large, nb - 1)
    return ret + jnp.where(dist < max_exact, dist, large)


def _dsa_bias_tables(rel_bias):
    lc = jnp.arange(KEY_BLOCK, dtype=jnp.int32)[:, None]
    qr = jnp.arange(Q_BLOCK, dtype=jnp.int32)[None, :]
    rel = jnp.stack([lc - qr - back * KEY_BLOCK for back in range(3)], axis=0)
    return jnp.moveaxis(rel_bias[_t5_bucket(rel)].astype(F32), -1, 0) * LOG2E


def _gdn_kernel(raw_ref, misc_ref, s0_ref, conv0_ref, convw_ref, alog_ref, dtb_ref, norm_ref,
                o_ref, s_out, conv_out, xp_s, st_s, *, n, c):
    @pl.when(pl.program_id(0) == 0)
    def _():
        xp_s[:, 0:8, :] = conv0_ref[...]
        st_s[...] = s0_ref[...]

    w = convw_ref[...]
    row = _iota((c, c), 0)
    col = _iota((c, c), 1)
    incl = row >= col
    strict = row > col
    eye = row == col
    tril = jnp.where(incl, 1.0, 0.0)
    eye_f = jnp.where(eye, 1.0, 0.0)
    chains = [(b, hd) for b in range(n) for hd in range(GDN_HEADS)]

    qkv, beta_all, gc_all = [], [], []
    for b in range(n):
        raw = raw_ref[b]
        xp_s[b, 8:8 + c, :] = raw
        y = raw * w[CONV_W - 1:CONV_W, :]
        for i in range(CONV_W - 1):
            y = y + xp_s[b, 5 + i:5 + i + c, :] * w[i:i + 1, :]
        tail = xp_s[b, c:c + 8, :]
        xp_s[b, 0:8, :] = tail
        conv_out[b] = tail
        qkv.append(jax.nn.silu(y))
        misc = misc_ref[b]
        beta_all.append(jax.nn.sigmoid(misc))
        g_all = -jnp.exp(alog_ref[...]) * _softplus(misc + dtb_ref[...])
        gc_all.append(_dot(tril, g_all, precision=HIGHEST))

    qs, ks, vs, betas, gcs, dec_s, dec_i = [], [], [], [], [], [], []
    for b, hd in chains:
        qh = qkv[b][:, hd * GDN_DK:(hd + 1) * GDN_DK]
        kh = qkv[b][:, GDN_QK + hd * GDN_DK:GDN_QK + (hd + 1) * GDN_DK]
        vs.append(qkv[b][:, 2 * GDN_QK + hd * GDN_DV:2 * GDN_QK + (hd + 1) * GDN_DV])
        qs.append(qh * lax.rsqrt(jnp.sum(qh * qh, axis=1, keepdims=True) + RMS_EPS) * (GDN_DK ** -0.5))
        ks.append(kh * lax.rsqrt(jnp.sum(kh * kh, axis=1, keepdims=True) + RMS_EPS))
        betas.append(beta_all[b][:, MISC_BETA + hd:MISC_BETA + hd + 1])
        gc = gc_all[b][:, MISC_A + hd:MISC_A + hd + 1]
        gcs.append(gc)
        gc_b = jnp.broadcast_to(gc, (c, c))
        gc_row = jnp.sum(jnp.where(eye, gc_b, 0.0), axis=0, keepdims=True)
        diff = gc_b - gc_row
        dec = jnp.exp(diff)
        dec_s.append(jnp.where(strict, dec, 0.0))
        dec_i.append(jnp.where(incl, dec, 0.0))
    kb = [k.astype(BF16) for k in ks]
    qb = [q.astype(BF16) for q in qs]
    pw = [-(beta * _dot_nt(k, k) * d) for beta, k, d in zip(betas, kb, dec_s)]
    inv = [eye_f + p for p in pw]
    for _ in range(int(math.log2(c)) - 1):
        pw = [_dot3(p, p) for p in pw]
        inv = [i + _dot3(i, p) for i, p in zip(inv, pw)]
    egc = [jnp.exp(gc) for gc in gcs]
    sol = [_dot3(i, jnp.concatenate([(beta * e) * k, beta * v], axis=1))
           for i, beta, e, k, v in zip(inv, betas, egc, ks, vs)]
    s_old = [st_s[b, hd] for b, hd in chains]
    sb = [s.astype(BF16) for s in s_old]
    vb = [(x[:, GDN_DK:] - _dot(x[:, :GDN_DK].astype(BF16), s)).astype(BF16) for x, s in zip(sol, sb)]
    attn = [(_dot_nt(q, k) * d).astype(BF16) for q, k, d in zip(qb, kb, dec_i)]
    outs = [e * _dot(q, s) + _dot(a, v) for e, q, s, a, v in zip(egc, qb, sb, attn, vb)]
    for (b, hd), gc, k, v, s, o in zip(chains, gcs, ks, vb, s_old, outs):
        g_last = gc[c - 1:c, :]
        st_s[b, hd] = jnp.exp(g_last) * s + _dot_tn((k * jnp.exp(g_last - gc)).astype(BF16), v)
        o = o * lax.rsqrt(jnp.mean(o * o, axis=1, keepdims=True) + RMS_EPS) * norm_ref[...]
        o_ref[b, :, hd * GDN_DV:(hd + 1) * GDN_DV] = o
    s_out[...] = st_s[...]


def _gdn_call(h3, s0, conv0, gp):
    n, t, _ = h3.shape
    c = min(t, CHUNK)
    const = lambda shape: pl.BlockSpec(shape, lambda j: (0,) * len(shape))
    state = const((n, GDN_HEADS, GDN_DK, GDN_DV))
    conv = const((n, 8, GDN_CONV_CH))
    return pl.pallas_call(
        functools.partial(_gdn_kernel, n=n, c=c),
        grid=(t // c,),
        in_specs=[pl.BlockSpec((n, c, GDN_CONV_CH), lambda j: (0, j, PROJ_OFF['c_qkv'] // GDN_CONV_CH)),
                  pl.BlockSpec((n, c, LANE), lambda j: (0, j, OFF_MISC // LANE)),
                  state, conv,
                  const((CONV_W, GDN_CONV_CH)), const((1, LANE)), const((1, LANE)), const((1, GDN_DV))],
        out_specs=[pl.BlockSpec((n, c, GDN_VW), lambda j: (0, j, 0)), state, conv],
        out_shape=[jax.ShapeDtypeStruct((n, t, GDN_VW), F32),
                   jax.ShapeDtypeStruct((n, GDN_HEADS, GDN_DK, GDN_DV), F32),
                   jax.ShapeDtypeStruct((n, 8, GDN_CONV_CH), F32)],
        scratch_shapes=[pltpu.VMEM((n, c + 8, GDN_CONV_CH), F32),
                        pltpu.VMEM((n, GDN_HEADS, GDN_DK, GDN_DV), F32)],
        compiler_params=_params("arbitrary"),
        name="gdn_branch",
    )(h3, h3, s0, conv0, gp['convw'], gp['alog'], gp['dtb'], gp['norm'])


def _lane_place(v, off):
    return jnp.zeros((1, LANE), F32).at[0, off:off + v.shape[0]].set(v.astype(F32))


def _gla_kernel(qk_ref, v_ref, misc_ref, s0_ref, wg_ref, bg_ref, norm_ref, hexp_ref, gsum_ref,
                o_ref, s_out, st_s, *, n, tc):
    blk = GLA_BLOCK
    wk = GLA_HEADS * GLA_DK

    @pl.when(pl.program_id(0) == 0)
    def _():
        st_s[...] = s0_ref[...]

    row = _iota((blk, blk), 0)
    col = _iota((blk, blk), 1)
    tril = jnp.where(row >= col, 1.0, 0.0)
    jrow = _iota((blk, wk), 0)
    lane_k = _iota((blk, wk), 1)
    head_mask = [(lane_k >= hd * GLA_DK) & (lane_k < (hd + 1) * GLA_DK) for hd in range(GLA_HEADS)]
    chains = range(n)

    def block(sb, carry):
        r0 = pl.multiple_of(sb * blk, blk)
        qk = [qk_ref[b, pl.ds(r0, blk), :] for b in chains]
        q = [x[:, :wk] * (GLA_DK ** -0.5) for x in qk]
        k = [x[:, wk:] for x in qk]
        v = [v_ref[b, pl.ds(r0, blk), :] for b in chains]
        mb = [misc_ref[b, pl.ds(r0, blk), :].astype(BF16) for b in chains]
        lg = [jax.nn.log_sigmoid(_dot(m, wg_ref[...]) + bg_ref[...]) * (1.0 / GLA_TAU) for m in mb]
        bc = [_dot(tril, x, precision=HIGHEST) for x in lg]
        dmat = []
        for b in chains:
            slabs = []
            for i in range(blk):
                keep = jrow <= i
                dec = jnp.where(keep, jnp.exp(bc[b][i:i + 1, :] - bc[b]), 0.0)
                slabs.append(dec * q[b][i:i + 1, :] * k[b])
            dmat.append(jnp.concatenate(slabs, axis=0).astype(BF16))
        a_exp = [_dot(d, hexp_ref[...]) for d in dmat]
        prod = [a * jnp.concatenate([x] * blk, axis=0) for a, x in zip(a_exp, v)]
        o = [_dot(gsum_ref[...], p.astype(BF16)) for p in prod]
        qe = [x * jnp.exp(c) for x, c in zip(q, bc)]
        kt = [x * jnp.exp(c[blk - 1:blk, :] - c) for x, c in zip(k, bc)]
        s = [st_s[b] for b in chains]
        sbf = [x.astype(BF16) for x in s]
        for b in chains:
            upd = jnp.zeros_like(s[b])
            parts = []
            for hd in range(GLA_HEADS):
                parts.append(_dot(jnp.where(head_mask[hd], qe[b], 0.0).astype(BF16), sbf[b]))
                vh = v[b][:, hd * GLA_DV:(hd + 1) * GLA_DV].astype(BF16)
                upd = upd + _dot_tn(jnp.where(head_mask[hd], kt[b], 0.0).astype(BF16), vh)
            ob = o[b] + jnp.concatenate(parts, axis=1)
            b_hi, b_lo = _split_bf16(jnp.broadcast_to(bc[b][blk - 1:blk, :], (8, wk)))
            eighth = jnp.full((8, GLA_DV), 0.125, BF16)
            decay = jnp.exp(_dot_tn(b_hi, eighth) + _dot_tn(b_lo, eighth))
            st_s[b] = decay * s[b] + upd
            for hd in range(GLA_HEADS):
                oh = ob[:, hd * GLA_DV:(hd + 1) * GLA_DV]
                oh = oh * lax.rsqrt(jnp.mean(oh * oh, axis=1, keepdims=True) + RMS_EPS) * norm_ref[...]
                o_ref[b, pl.ds(r0, blk), hd * GLA_DV:(hd + 1) * GLA_DV] = oh
        return carry

    lax.fori_loop(0, tc // blk, block, 0, unroll=2 if tc // blk > 1 else 1)
    s_out[...] = st_s[...]


def _gla_call(h3, s0, gp):
    n, t, _ = h3.shape
    tc = min(t, 128)
    wk = GLA_HEADS * GLA_DK
    wv = GLA_HEADS * GLA_DV
    const = lambda shape: pl.BlockSpec(shape, lambda j: (0,) * len(shape))
    state = const((n, wk, GLA_DV))
    return pl.pallas_call(
        functools.partial(_gla_kernel, n=n, tc=tc),
        grid=(t // tc,),
        in_specs=[pl.BlockSpec((n, tc, 2 * wk), lambda j: (0, j, PROJ_OFF['d_q'] // (2 * wk))),
                  pl.BlockSpec((n, tc, wv), lambda j: (0, j, PROJ_OFF['d_v'] // wv)),
                  pl.BlockSpec((n, tc, LANE), lambda j: (0, j, OFF_MISC // LANE)),
                  state,
                  const((LANE, wk)), const((1, wk)), const((1, GLA_DV)),
                  const((wk, wv)), const((GLA_BLOCK, GLA_BLOCK * GLA_BLOCK))],
        out_specs=[pl.BlockSpec((n, tc, wv), lambda j: (0, j, 0)), state],
        out_shape=[jax.ShapeDtypeStruct((n, t, wv), F32),
                   jax.ShapeDtypeStruct((n, wk, GLA_DV), F32)],
        scratch_shapes=[pltpu.VMEM((n, wk, GLA_DV), F32)],
        compiler_params=_params("arbitrary"),
        name="gla_branch",
    )(h3, h3, h3, s0, gp['wg'], gp['bg'], gp['norm'], gp['hexp'], gp['gsum'])


def _gla_prepare(w_g2, b_g, norm):
    wk = GLA_HEADS * GLA_DK
    wg = jnp.zeros((LANE, wk), F32).at[MISC_G:MISC_G + GLA_GATE_RANK].set(w_g2).astype(BF16)
    ck = jnp.arange(wk)[:, None] // GLA_DK
    cv = jnp.arange(GLA_HEADS * GLA_DV)[None, :] // GLA_DV
    ij = jnp.arange(GLA_BLOCK * GLA_BLOCK)[None, :] // GLA_BLOCK
    return {'wg': wg, 'bg': b_g.reshape(1, wk).astype(F32),
            'norm': norm.reshape(1, GLA_DV).astype(F32),
            'hexp': (ck == cv).astype(BF16),
            'gsum': (jnp.arange(GLA_BLOCK)[:, None] == ij).astype(BF16)}


def _merge_kernel(x_ref, ya_ref, yb_ref, yc_ref, yd_ref, ga_ref, gb_ref, gc_ref, gd_ref, mg_ref,
                  wb_ref, wo_ref, lng_ref, lnb_ref, o_ref):
    mixed = None
    for br, (y_ref, g_ref) in enumerate(((ya_ref, ga_ref), (yb_ref, gb_ref), (yc_ref, gc_ref), (yd_ref, gd_ref))):
        act = (y_ref[...] * jax.nn.silu(g_ref[...])).astype(BF16)
        term = jax.nn.sigmoid(mg_ref[:, br * D_MODEL:(br + 1) * D_MODEL]) * _dot(act, wb_ref[br])
        mixed = term if mixed is None else mixed + term
    z = DN_ALPHA * x_ref[...] + _dot(mixed.astype(BF16), wo_ref[...])
    mu = jnp.mean(z, axis=1, keepdims=True)
    zc = z - mu
    var = jnp.mean(zc * zc, axis=1, keepdims=True)
    o_ref[...] = zc * lax.rsqrt(var + LN_EPS) * lng_ref[...] + lnb_ref[...]


def _merge_call(x, h, ya, yb, yc, yd, wb, wo, lng, lnb):
    rows = x.shape[0]
    tm = min(rows, 256)
    bw = BRANCH_W
    ybs = pl.BlockSpec((tm, bw), lambda i: (i, 0))
    gate = lambda name: pl.BlockSpec((tm, bw), lambda i: (i, PROJ_OFF[name] // bw))
    return pl.pallas_call(
        _merge_kernel,
        grid=(rows // tm,),
        in_specs=[pl.BlockSpec((tm, D_MODEL), lambda i: (i, 0)), ybs, ybs, ybs, ybs,
                  gate('a_gate'), gate('b_gate'), gate('c_gate'), gate('d_gate'),
                  pl.BlockSpec((tm, N_BRANCH * D_MODEL), lambda i: (i, 0)),
                  pl.BlockSpec((N_BRANCH, bw, D_MODEL), lambda i: (0, 0, 0)),
                  pl.BlockSpec((D_MODEL, D_MODEL), lambda i: (0, 0)),
                  pl.BlockSpec((1, D_MODEL), lambda i: (0, 0)),
                  pl.BlockSpec((1, D_MODEL), lambda i: (0, 0))],
        out_specs=pl.BlockSpec((tm, D_MODEL), lambda i: (i, 0)),
        out_shape=jax.ShapeDtypeStruct((rows, D_MODEL), F32),
        compiler_params=_params("parallel"),
        name="merge_norm",
    )(x, ya, yb, yc, yd, h, h, h, h, h, wb, wo, lng, lnb)


def _split_w_in(w_in):
    out, off = {}, 0
    for name, width in IN_LAYOUT:
        out[name] = w_in[:, off:off + width]
        off += width
    return out


def _regroup_kernel(w_ref, proj_ref, key_ref):
    cols = _split_w_in(w_ref[...])
    pad = jnp.zeros((w_ref.shape[0], MISC_PAD), F32)
    proj_ref[...] = jnp.concatenate([cols[name] for name in PROJ_ORDER] + [pad], axis=1).astype(BF16)
    k, v, ki = cols['b_k'], cols['b_v'], cols['b_ki']
    dh = DSA_HEAD_DIM
    key_ref[...] = jnp.concatenate([k[:, :dh], k[:, :dh], k[:, dh:], k[:, dh:],
                                    v[:, :dh], v[:, :dh], v[:, dh:], v[:, dh:], ki, ki], axis=1).astype(BF16)


def _regroup_weights(w_in):
    depth, rows, width = w_in.shape
    tr = 128
    return pl.pallas_call(
        _regroup_kernel,
        grid=(depth, rows // tr),
        in_specs=[pl.BlockSpec((None, tr, width), lambda l, i: (l, i, 0))],
        out_specs=[pl.BlockSpec((None, tr, PROJ_W), lambda l, i: (l, i, 0)),
                   pl.BlockSpec((None, tr, HK_W), lambda l, i: (l, i, 0))],
        out_shape=[jax.ShapeDtypeStruct((depth, rows, PROJ_W), BF16),
                   jax.ShapeDtypeStruct((depth, rows, HK_W), BF16)],
        compiler_params=_params("parallel", "parallel"),
        name="regroup_weights",
    )(w_in)


def _dup_keys(k, v, ki):
    return jnp.concatenate([k[:, :, 0], k[:, :, 0], k[:, :, 1], k[:, :, 1],
                            v[:, :, 0], v[:, :, 0], v[:, :, 1], v[:, :, 1], ki, ki], axis=-1).astype(BF16)


def _layer(x, lw, st):
    n, t, _ = x.shape
    x2 = x.reshape(n * t, D_MODEL)
    h = _matmul(x2, lw['w_proj'], F32, tn=PROJ_W // 9)
    hk_new = _matmul(x2, lw['w_key'], BF16, tn=HK_W).reshape(n, t, HK_W)
    if st is None:
        hk, n_keys, b_off = hk_new, t, 0
        s5r = jnp.zeros((n, 1, S5_LANES), F32)
        s5i = jnp.zeros((n, 1, S5_LANES), F32)
        gdn0 = jnp.zeros((n, GDN_HEADS, GDN_DK, GDN_DV), F32)
        conv0 = jnp.zeros((n, 8, GDN_CONV_CH), F32)
        gla0 = jnp.zeros((n, GLA_HEADS * GLA_DK, GLA_DV), F32)
    else:
        past = st['k'].shape[1]
        n_keys = past + t
        b_off = past // Q_BLOCK
        lp = -(-n_keys // KEY_TILE) * KEY_TILE
        hk = jnp.concatenate([_dup_keys(st['k'], st['v'], st['kidx']), hk_new,
                              jnp.zeros((n, lp - n_keys, HK_W), BF16)], axis=1)
        s5r = st['s5_re'].reshape(n, 1, S5_LANES)
        s5i = st['s5_im'].reshape(n, 1, S5_LANES)
        gdn0 = st['gdn']
        conv0 = jnp.concatenate([jnp.zeros((n, 8 - (CONV_W - 1), GDN_CONV_CH), F32), st['gdn_conv']], axis=1)
        gla0 = st['gla'].reshape(n, GLA_HEADS * GLA_DK, GLA_DV)
    topk = min(DSA_TOPK_MAX, n_keys // 4)

    ya, s5r_new, s5i_new = _s5_call(h, n, t, s5r, s5i, lw['s5'])
    yb = _dsa_call(h, hk, n, t, lw['bias'], lw['tril'], b_off=b_off, n_keys=n_keys, topk=topk)
    h3 = h.reshape(n, t, PROJ_W)
    yc, gdn_new, conv_new = _gdn_call(h3, gdn0, conv0, lw['gdn'])
    yd, gla_new = _gla_call(h3, gla0, lw['gla'])
    yc = yc.reshape(n * t, GDN_VW)
    yd = yd.reshape(n * t, GLA_HEADS * GLA_DV)
    y = _merge_call(x2, h, ya, yb, yc, yd, lw['w_branch'], lw['w_out'], lw['ln_g'], lw['ln_b'])

    def cols(name, width):
        return h[:, PROJ_OFF[name]:PROJ_OFF[name] + width].reshape(n, t, width)

    new = {'k': cols('b_k', DSA_KV_HEADS * DSA_HEAD_DIM).reshape(n, t, DSA_KV_HEADS, DSA_HEAD_DIM),
           'v': cols('b_v', DSA_KV_HEADS * DSA_HEAD_DIM).reshape(n, t, DSA_KV_HEADS, DSA_HEAD_DIM),
           'kidx': cols('b_ki', IDX_DIM),
           's5_re': s5r_new.reshape(n, S5_GROUPS, S5_STATE), 's5_im': s5i_new.reshape(n, S5_GROUPS, S5_STATE),
           'gdn': gdn_new, 'gdn_conv': conv_new[:, 8 - (CONV_W - 1):],
           'gla': gla_new.reshape(n, GLA_HEADS, GLA_DK, GLA_DV)}
    return y.reshape(n, t, D_MODEL), new


def kernel(x_prompt, x_sample, cache_k, cache_v, cache_kidx, state_s5_re, state_s5_im, state_gdn, state_gdn_conv, state_gla, w_in, w_branch, w_out, ln_g, ln_b, rel_bias, s5_a_re, s5_a_im, s5_log_dt, s5_b_re, s5_b_im, s5_c_re, s5_c_im, s5_d, s5_w_glu, gdn_conv, gdn_a_log, gdn_dt_bias, gdn_norm, gla_w_g2, gla_b_g, gla_norm):
    bias = _dsa_bias_tables(rel_bias)
    tril = (jnp.arange(KEY_BLOCK)[:, None] >= jnp.arange(KEY_BLOCK)[None, :]).astype(BF16)
    yp, ys = x_prompt, x_sample
    new_p, new_s = [], []
    w_proj, w_key = _regroup_weights(w_in)
    for l in range(DEPTH):
        lw = {'w_proj': w_proj[l], 'w_key': w_key[l], 'bias': bias, 'tril': tril,
              'w_branch': w_branch[l].astype(BF16), 'w_out': w_out[l].astype(BF16),
              'ln_g': ln_g[l].reshape(1, D_MODEL), 'ln_b': ln_b[l].reshape(1, D_MODEL),
              's5': _s5_prepare(s5_a_re[l], s5_a_im[l], s5_log_dt[l], s5_b_re[l], s5_b_im[l],
                                s5_c_re[l], s5_c_im[l], s5_d[l], s5_w_glu[l]),
              'gdn': {'convw': gdn_conv[l], 'alog': _lane_place(gdn_a_log[l], MISC_A),
                      'dtb': _lane_place(gdn_dt_bias[l], MISC_A), 'norm': gdn_norm[l].reshape(1, GDN_DV)},
              'gla': _gla_prepare(gla_w_g2[l], gla_b_g[l], gla_norm[l])}
        st = {'k': cache_k[l], 'v': cache_v[l], 'kidx': cache_kidx[l], 's5_re': state_s5_re[l],
              's5_im': state_s5_im[l], 'gdn': state_gdn[l], 'gdn_conv': state_gdn_conv[l], 'gla': state_gla[l]}
        yp, stp = _layer(yp, lw, None)
        ys, sts = _layer(ys, lw, st)
        new_p.append(stp)
        new_s.append(sts)
    names = ('k', 'v', 'kidx', 's5_re', 's5_im', 'gdn', 'gdn_conv', 'gla')
    stack = lambda states, name: jnp.stack([s[name] for s in states], axis=0)
    return ((yp, ys) + tuple(stack(new_p, nm) for nm in names) + tuple(stack(new_s, nm) for nm in names))
```

```python
import functools
import math

import jax
import jax.numpy as jnp
from jax import lax
from jax.experimental import pallas as pl
from jax.experimental.pallas import tpu as pltpu

F32 = jnp.float32
BF16 = jnp.bfloat16
HIGHEST = lax.Precision.HIGHEST

D_MODEL = 1024
DEPTH = 2
CHUNK = 64
N_BRANCH = 4
BRANCH_W = 512
S5_GROUP = 16
S5_GROUPS = BRANCH_W // S5_GROUP
S5_STATE = 64
S5_LANES = S5_GROUPS * S5_STATE
DSA_HEADS = 8
DSA_KV_HEADS = 2
DSA_HEAD_DIM = 64
IDX_HEADS = 4
IDX_DIM = 64
DSA_TOPK_MAX = 256
REL_BUCKETS = 32
REL_MAX_DIST = 128
GDN_HEADS = 4
GDN_DK = 128
GDN_DV = 128
GDN_QK = GDN_HEADS * GDN_DK
GDN_VW = GDN_HEADS * GDN_DV
CONV_W = 4
GDN_CONV_CH = 2 * GDN_QK + GDN_VW
GLA_HEADS = 4
GLA_DK = 64
GLA_DV = 128
GLA_GATE_RANK = 16
GLA_TAU = 16.0
GLA_BLOCK = 16
LN_EPS = 1e-5
RMS_EPS = 1e-6
DN_ALPHA = (2 * DEPTH) ** 0.25
LOG2E = math.log2(math.e)

LANE = 128
KEY_BLOCK = 128
KEY_SUB = 4
KEY_TILE = KEY_SUB * KEY_BLOCK
FOLD_ROWS = 64
Q_BLOCK = 128
VMEM_LIMIT = 56 * 1024 * 1024

IN_LAYOUT = (
    ('a_u', BRANCH_W), ('a_gate', BRANCH_W),
    ('b_q', DSA_HEADS * DSA_HEAD_DIM), ('b_k', DSA_KV_HEADS * DSA_HEAD_DIM),
    ('b_v', DSA_KV_HEADS * DSA_HEAD_DIM), ('b_qi', IDX_HEADS * IDX_DIM), ('b_ki', IDX_DIM),
    ('b_wi', IDX_HEADS), ('b_gate', BRANCH_W),
    ('c_qkv', GDN_CONV_CH), ('c_beta', GDN_HEADS), ('c_a', GDN_HEADS), ('c_gate', BRANCH_W),
    ('d_q', GLA_HEADS * GLA_DK), ('d_k', GLA_HEADS * GLA_DK), ('d_v', GLA_HEADS * GLA_DV),
    ('d_g', GLA_GATE_RANK), ('d_gate', BRANCH_W),
    ('merge', N_BRANCH * D_MODEL),
)

PROJ_ORDER = ('merge', 'a_u', 'a_gate', 'b_q', 'b_qi', 'b_k', 'b_v', 'c_qkv', 'b_gate', 'c_gate',
              'd_v', 'd_gate', 'd_q', 'd_k', 'b_ki', 'b_wi', 'c_beta', 'c_a', 'd_g')
MISC_WI = IDX_DIM
MISC_BETA = MISC_WI + IDX_HEADS
MISC_A = MISC_BETA + GDN_HEADS
MISC_G = MISC_A + GDN_HEADS
MISC_PAD = LANE - (MISC_G + GLA_GATE_RANK)


def _proj_offsets():
    widths = dict(IN_LAYOUT)
    off, out = 0, {}
    for name in PROJ_ORDER:
        out[name] = off
        off += widths[name]
    return out, off + MISC_PAD


PROJ_OFF, PROJ_W = _proj_offsets()
OFF_MISC = PROJ_OFF['b_ki']
HK_W = 5 * LANE
HK_V = 2 * LANE
HK_KI = 4 * LANE


def _dot(a, b, precision=None):
    return jnp.dot(a, b, preferred_element_type=F32, precision=precision)


def _dot_nt(a, b, precision=None):
    return lax.dot_general(a, b, (((1,), (1,)), ((), ())), preferred_element_type=F32, precision=precision)


def _dot_tn(a, b, precision=None):
    return lax.dot_general(a, b, (((0,), (0,)), ((), ())), preferred_element_type=F32, precision=precision)


def _split_bf16(x):
    hi = x.astype(BF16)
    return hi, (x - hi.astype(F32)).astype(BF16)


def _dot3(a, b):
    a_hi, a_lo = _split_bf16(a)
    b_hi, b_lo = _split_bf16(b)
    return _dot(a_hi, b_hi) + (_dot(a_hi, b_lo) + _dot(a_lo, b_hi))


def _params(*sem):
    return pltpu.CompilerParams(dimension_semantics=sem, vmem_limit_bytes=VMEM_LIMIT)


def _softplus(x):
    return jnp.maximum(x, 0.0) + jnp.log1p(jnp.exp(-jnp.abs(x)))


def _iota(shape, dim):
    return lax.broadcasted_iota(jnp.int32, shape, dim)


def _mm_kernel(x_ref, w_ref, o_ref):
    o_ref[...] = _dot(x_ref[...].astype(BF16), w_ref[...]).astype(o_ref.dtype)


def _matmul(x, w, out_dtype, tn):
    rows, kdim = x.shape
    width = w.shape[1]
    tm = min(rows, 1024)
    return pl.pallas_call(
        _mm_kernel,
        grid=(rows // tm, width // tn),
        in_specs=[pl.BlockSpec((tm, kdim), lambda i, j: (i, 0)),
                  pl.BlockSpec((kdim, tn), lambda i, j: (0, j))],
        out_specs=pl.BlockSpec((tm, tn), lambda i, j: (i, j)),
        out_shape=jax.ShapeDtypeStruct((rows, width), out_dtype),
        compiler_params=_params("parallel", "parallel"),
        name="proj_matmul",
    )(x, w)


def _s5_kernel(u_ref, h0r_ref, h0i_ref, ar_ref, ai_ref, bbr_ref, bbi_ref, ccr_ref, cci_ref, d_ref, wglu_ref,
               ya_ref, hr_out, hi_out, xr_s, xi_s, cr_s, ci_s, *, tc):
    @pl.when(pl.program_id(1) == 0)
    def _():
        cr_s[...] = h0r_ref[...]
        ci_s[...] = h0i_ref[...]

    u = u_ref[...]
    ub = u.astype(BF16)
    xr_s[...] = _dot(ub, bbr_ref[...])
    xi_s[...] = _dot(ub, bbi_ref[...])
    ar = ar_ref[...]
    ai = ai_ref[...]

    def step(t, carry):
        hr, hi = carry
        nhr = ar * hr - ai * hi + xr_s[pl.ds(t, 1), :]
        nhi = ar * hi + ai * hr + xi_s[pl.ds(t, 1), :]
        xr_s[pl.ds(t, 1), :] = nhr
        xi_s[pl.ds(t, 1), :] = nhi
        return nhr, nhi

    hr, hi = lax.fori_loop(0, tc, step, (cr_s[...], ci_s[...]), unroll=8)
    cr_s[...] = hr
    ci_s[...] = hi
    hr_out[...] = hr
    hi_out[...] = hi
    y = (_dot(xr_s[...].astype(BF16), ccr_ref[...]) - _dot(xi_s[...].astype(BF16), cci_ref[...])
         + d_ref[...] * u)
    y = jax.nn.gelu(y)
    g = _dot(y.astype(BF16), wglu_ref[...])
    ya_ref[...] = g[:, :BRANCH_W] * jax.nn.sigmoid(g[:, BRANCH_W:])


def _s5_call(h, n, t, h0r, h0i, sp):
    tc = min(t, 256)
    nc = t // tc
    const = lambda shape: pl.BlockSpec(shape, lambda i, c: (0,) * len(shape))
    state = pl.BlockSpec((None, 1, S5_LANES), lambda i, c: (i, 0, 0))
    return pl.pallas_call(
        functools.partial(_s5_kernel, tc=tc),
        grid=(n, nc),
        in_specs=[pl.BlockSpec((tc, BRANCH_W), lambda i, c: (i * nc + c, PROJ_OFF['a_u'] // BRANCH_W)),
                  state, state,
                  const((1, S5_LANES)), const((1, S5_LANES)),
                  const((BRANCH_W, S5_LANES)), const((BRANCH_W, S5_LANES)),
                  const((S5_LANES, BRANCH_W)), const((S5_LANES, BRANCH_W)),
                  const((1, BRANCH_W)), const((BRANCH_W, 2 * BRANCH_W))],
        out_specs=[pl.BlockSpec((tc, BRANCH_W), lambda i, c: (i * nc + c, 0)), state, state],
        out_shape=[jax.ShapeDtypeStruct((n * t, BRANCH_W), F32),
                   jax.ShapeDtypeStruct((n, 1, S5_LANES), F32),
                   jax.ShapeDtypeStruct((n, 1, S5_LANES), F32)],
        scratch_shapes=[pltpu.VMEM((tc, S5_LANES), F32), pltpu.VMEM((tc, S5_LANES), F32),
                        pltpu.VMEM((1, S5_LANES), F32), pltpu.VMEM((1, S5_LANES), F32)],
        compiler_params=_params("parallel", "arbitrary"),
        name="s5_branch",
    )(h, h0r, h0i, sp['ar'], sp['ai'], sp['bbr'], sp['bbi'], sp['ccr'], sp['cci'], sp['d'], sp['wglu'])


def _s5_prepare(a_re, a_im, log_dt, b_re, b_im, c_re, c_im, d, w_glu):
    lam_r = jnp.minimum(a_re, -1e-4)
    lam_i = a_im
    dt = jnp.exp(log_dt)[:, None]
    mag = jnp.exp(lam_r * dt)
    ar = mag * jnp.cos(lam_i * dt)
    ai = mag * jnp.sin(lam_i * dt)
    den = lam_r * lam_r + lam_i * lam_i
    nr = ar - 1.0
    cr = (nr * lam_r + ai * lam_i) / den
    ci = (ai * lam_r - nr * lam_i) / den
    bb_r = cr[..., None] * b_re - ci[..., None] * b_im
    bb_i = cr[..., None] * b_im + ci[..., None] * b_re
    eye = jnp.eye(S5_GROUPS, dtype=F32)

    def blockdiag_in(bb):
        m = eye[:, None, :, None] * jnp.swapaxes(bb, 1, 2)[:, :, None, :]
        return m.reshape(BRANCH_W, S5_LANES).astype(BF16)

    def blockdiag_out(c):
        m = eye[:, None, :, None] * jnp.swapaxes(c, 1, 2)[:, :, None, :]
        return m.reshape(S5_LANES, BRANCH_W).astype(BF16)

    return {'ar': ar.reshape(1, S5_LANES), 'ai': ai.reshape(1, S5_LANES),
            'bbr': blockdiag_in(bb_r), 'bbi': blockdiag_in(bb_i),
            'ccr': blockdiag_out(c_re), 'cci': blockdiag_out(c_im),
            'd': d.reshape(1, BRANCH_W), 'wglu': w_glu.astype(BF16)}


def _sortable(x):
    bits = lax.bitcast_convert_type(x, jnp.int32)
    key = bits ^ ((bits >> 31) & jnp.int32(0x7FFFFFFF))
    return jnp.where(key == -1, 0, key)


def _dsa_kernel(q_ref, qi_ref, misc_ref, hk_ref, bias_ref, tril_ref, o_ref, sc_s, pk_s, acc_s, *, qb, b_off, n_keys, topk):
    b = pl.program_id(1) + b_off
    nkt = (b + KEY_SUB) // KEY_SUB
    fold = KEY_TILE // FOLD_ROWS
    key_row = _iota((KEY_TILE, qb), 0)
    qpos = b * Q_BLOCK + _iota((1, qb), 1)
    limit = jnp.minimum(((qpos >> 6) + 1) * CHUNK, n_keys)
    low_half = _iota((qb, LANE), 1) < DSA_HEAD_DIM

    def key_rows(ref_cols, j):
        return hk_ref[pl.ds(pl.multiple_of(j * KEY_TILE, KEY_TILE), KEY_TILE), ref_cols:ref_cols + LANE]

    def half_masked(x, hd):
        pair = x[:, (hd // 2) * LANE:(hd // 2 + 1) * LANE]
        return jnp.where(low_half == (hd % 2 == 0), pair, 0.0).astype(BF16)

    misc = misc_ref[...]
    pick = jnp.where(_iota((8, LANE), 1) == MISC_WI + _iota((8, LANE), 0), 1.0, 0.0)
    wi_t = _dot_nt(pick, misc, precision=HIGHEST) * (IDX_DIM ** -0.5 * IDX_HEADS ** -0.5)
    qi = qi_ref[...]
    qi_stack = jnp.concatenate([half_masked(qi, hd) for hd in range(IDX_HEADS)], axis=0)

    def pack16(v):
        half = KEY_TILE // 2
        return (v[0:half] & 0xFFFF) | (v[half:] << 16)

    def score_block(j, carry):
        z = _dot_nt(key_rows(HK_KI, j), qi_stack)
        sc = jnp.zeros((KEY_TILE, qb), F32)
        for hd in range(IDX_HEADS):
            sc = sc + jnp.maximum(z[:, hd * qb:(hd + 1) * qb], 0.0) * wi_t[hd:hd + 1, :]
        sc = jnp.where(j * KEY_TILE + key_row < limit, sc, -jnp.inf)
        key = _sortable(sc)
        sc_s[j] = key
        pk_s[j] = pack16(key >> 16)
        return carry

    lax.fori_loop(0, nkt, score_block, 0)

    def count_ge(cand):
        def body(j, acc):
            return acc + jnp.where(sc_s[j] >= cand, 1.0, 0.0).reshape(fold, FOLD_ROWS, qb).sum(axis=0)
        acc = lax.fori_loop(0, nkt, body, jnp.zeros((FOLD_ROWS, qb), F32))
        return jnp.sum(acc, axis=0, keepdims=True)

    one16 = jnp.ones((KEY_TILE, qb), jnp.int16)
    zero16 = jnp.zeros((KEY_TILE, qb), jnp.int16)

    def count16_ge(cand):
        word = jnp.broadcast_to((cand & 0xFFFF) | (cand << 16), (8, qb))
        c16 = pltpu.bitcast(word, jnp.int16)[0:1, :]

        def body(j, acc):
            ind = jnp.where(pltpu.bitcast(pk_s[j], jnp.int16) >= c16, one16, zero16)
            for i in range(fold):
                acc = acc + ind[i * FOLD_ROWS:(i + 1) * FOLD_ROWS]
            return acc
        acc = lax.fori_loop(0, nkt, body, jnp.zeros((FOLD_ROWS, qb), jnp.int16))
        return jnp.sum(acc.astype(F32), axis=0, keepdims=True)

    def radix_search(start, want):
        def bit_step(i, res):
            cand = res + (jnp.int32(1) << (14 - i))
            return jnp.where(count16_ge(cand) >= want, cand, res)
        return lax.fori_loop(0, 15, bit_step, start)

    kf = float(topk)
    top = radix_search(jnp.where(count16_ge(jnp.zeros((1, qb), jnp.int32)) >= kf, 0, -32768), kf)
    int16_max = jnp.iinfo(jnp.int16).max
    above = jnp.where(top == int16_max, 0.0, count16_ge(jnp.minimum(top + 1, int16_max)))
    want_low = kf - above

    def pack_low(j, carry):
        key = sc_s[j]
        pk_s[j] = pack16(jnp.where((key >> 16) == top, (key >> 1) & 0x7FFF, -1))
        return carry

    lax.fori_loop(0, nkt, pack_low, 0)
    low = radix_search(jnp.zeros((1, qb), jnp.int32), want_low)
    base = (top << 16) | (low << 1)
    thr = jnp.where(count_ge(base | 1) >= kf, base | 1, base)
    need = kf - count_ge(thr + 1)

    tril = tril_ref[...]

    def select_block(j, ties):
        keys = sc_s[j]
        eq = keys == thr
        eq_b = jnp.where(eq, 1.0, 0.0).astype(BF16)
        parts, carry = [], ties
        for s in range(KEY_SUB):
            part = _dot(tril, eq_b[s * KEY_BLOCK:(s + 1) * KEY_BLOCK]) + carry
            parts.append(part)
            carry = part[KEY_BLOCK - 1:KEY_BLOCK, :]
        prefix = jnp.concatenate(parts, axis=0)
        sel = ((keys > thr) | (eq & (prefix <= need))) & (j * KEY_TILE + key_row < limit)
        sc_s[j] = lax.bitcast_convert_type(jnp.where(sel, 0.0, -jnp.inf), jnp.int32)
        return prefix[KEY_TILE - 1:KEY_TILE, :]

    lax.fori_loop(0, nkt, select_block, jnp.zeros((1, qb), F32))

    q = q_ref[...] * (DSA_HEAD_DIM ** -0.5 * LOG2E)
    rep = DSA_HEADS // DSA_KV_HEADS
    eye = jnp.where(_iota((LANE, LANE), 0) == _iota((LANE, LANE), 1), 1.0, 0.0)
    value_lane = _iota((KEY_TILE, LANE), 1) < DSA_HEAD_DIM
    n_pairs = DSA_HEADS // 2
    q_pair = [jnp.concatenate([half_masked(q, 2 * c), half_masked(q, 2 * c + 1)], axis=0) for c in range(n_pairs)]
    far_bias = [bias_ref[hd, 2, 0:1, 0:qb] for hd in range(DSA_HEADS)]
    acc_s[...] = jnp.zeros_like(acc_s)

    def attend(j, ms, far):
        mask = lax.bitcast_convert_type(sc_s[j], F32)
        kd = [key_rows(grp * LANE, j) for grp in range(DSA_KV_HEADS)]
        zs = [_dot_nt(kd[(2 * c) // rep], q_pair[c]) for c in range(n_pairs)]
        new_ms, ps, alphas = [], [], []
        for hd in range(DSA_HEADS):
            r = hd % 2
            zh = zs[hd // 2][:, r * qb:(r + 1) * qb] + mask
            if not far:
                bias = [bias_ref[hd, jnp.clip(b - (KEY_SUB * j + s), 0, 2), :, 0:qb] for s in range(KEY_SUB)]
                zh = zh + jnp.concatenate(bias, axis=0)
            tile_max = jnp.max(zh.reshape(fold, FOLD_ROWS, qb).max(axis=0), axis=0, keepdims=True)
            shift = far_bias[hd] if far else 0.0
            m_new = jnp.maximum(ms[hd], tile_max + shift)
            ps.append(jnp.exp2(zh - (m_new - shift)).astype(BF16))
            alphas.append(jnp.exp2(ms[hd] - m_new))
            new_ms.append(m_new)
        vd = [jnp.where(value_lane, key_rows(HK_V + grp * LANE, j), 1.0).astype(BF16)
              for grp in range(DSA_KV_HEADS)]
        for c in range(n_pairs):
            acc_s[c] = (acc_s[c] * jnp.concatenate(alphas[2 * c:2 * c + 2], axis=1)
                        + _dot_tn(vd[(2 * c) // rep], jnp.concatenate(ps[2 * c:2 * c + 2], axis=1)))
        return tuple(new_ms)

    n_far = jnp.maximum((b - 1) // KEY_SUB, 0)
    ms = tuple(jnp.full((1, qb), -1e30, F32) for _ in range(DSA_HEADS))
    ms = lax.fori_loop(0, n_far, lambda j, m: attend(j, m, True), ms)
    lax.fori_loop(n_far, nkt, lambda j, m: attend(j, m, False), ms)
    for c in range(n_pairs):
        halves = []
        for r in range(2):
            blk = acc_s[c, :, r * qb:(r + 1) * qb]
            halves.append(blk[0:DSA_HEAD_DIM] / blk[DSA_HEAD_DIM:DSA_HEAD_DIM + 1])
        o_t = jnp.concatenate(halves, axis=0)
        o_ref[:, c * LANE:(c + 1) * LANE] = _dot_tn(o_t, eye, precision=HIGHEST)


def _dsa_call(h, hk, n, t, bias, tril, *, b_off, n_keys, topk):
    qb = min(t, Q_BLOCK)
    nqb = t // qb
    lp = hk.shape[1]
    wq = DSA_HEADS * DSA_HEAD_DIM
    wqi = IDX_HEADS * IDX_DIM
    return pl.pallas_call(
        functools.partial(_dsa_kernel, qb=qb, b_off=b_off, n_keys=n_keys, topk=topk),
        grid=(n, nqb),
        in_specs=[pl.BlockSpec((qb, wq), lambda i, c: (i * nqb + c, PROJ_OFF['b_q'] // wq)),
                  pl.BlockSpec((qb, wqi), lambda i, c: (i * nqb + c, PROJ_OFF['b_qi'] // wqi)),
                  pl.BlockSpec((qb, LANE), lambda i, c: (i * nqb + c, OFF_MISC // LANE)),
                  pl.BlockSpec((None, lp, HK_W), lambda i, c: (i, 0, 0)),
                  pl.BlockSpec((DSA_HEADS, 3, KEY_BLOCK, Q_BLOCK), lambda i, c: (0, 0, 0, 0)),
                  pl.BlockSpec((KEY_BLOCK, KEY_BLOCK), lambda i, c: (0, 0))],
        out_specs=pl.BlockSpec((qb, wq), lambda i, c: (i * nqb + c, 0)),
        out_shape=jax.ShapeDtypeStruct((n * t, wq), F32),
        scratch_shapes=[pltpu.VMEM((lp // KEY_TILE, KEY_TILE, qb), jnp.int32),
                        pltpu.VMEM((lp // KEY_TILE, KEY_TILE // 2, qb), jnp.int32),
                        pltpu.VMEM((DSA_HEADS // 2, LANE, 2 * qb), F32)],
        compiler_params=_params("parallel", "arbitrary"),
        name="dsa_branch",
    )(h, h, h, hk, bias, tril)


def _t5_bucket(rel):
    nb = REL_BUCKETS // 2
    max_exact = nb // 2
    ret = jnp.where(rel > 0, nb, 0)
    dist = jnp.abs(rel)
    distf = jnp.maximum(dist, 1).astype(F32)
    large = max_exact + (jnp.log(distf / max_exact) / math.log(REL_MAX_DIST / max_exact)
                         * (nb - max_exact)).astype(jnp.int32)
    large = jnp.minimum(large, nb - 1)
    return ret + jnp.where(dist < max_exact, dist, large)


def _dsa_bias_tables(rel_bias):
    lc = jnp.arange(KEY_BLOCK, dtype=jnp.int32)[:, None]
    qr = jnp.arange(Q_BLOCK, dtype=jnp.int32)[None, :]
    rel = jnp.stack([lc - qr - back * KEY_BLOCK for back in range(3)], axis=0)
    onehot = jax.nn.one_hot(_t5_bucket(rel), REL_BUCKETS, dtype=F32)
    bias = jnp.einsum('klqb,bh->hklq', onehot, rel_bias.astype(F32), precision=HIGHEST)
    return bias * LOG2E


def _gdn_kernel(raw_ref, misc_ref, s0_ref, conv0_ref, convw_ref, alog_ref, dtb_ref, norm_ref,
                o_ref, s_out, conv_out, xp_s, st_s, *, n, c):
    @pl.when(pl.program_id(0) == 0)
    def _():
        xp_s[:, 0:8, :] = conv0_ref[...]
        st_s[...] = s0_ref[...]

    w = convw_ref[...]
    row = _iota((c, c), 0)
    col = _iota((c, c), 1)
    incl = row >= col
    strict = row > col
    eye = row == col
    tril = jnp.where(incl, 1.0, 0.0)
    eye_f = jnp.where(eye, 1.0, 0.0)
    chains = [(b, hd) for b in range(n) for hd in range(GDN_HEADS)]

    qkv, beta_all, gc_all = [], [], []
    for b in range(n):
        raw = raw_ref[b]
        xp_s[b, 8:8 + c, :] = raw
        y = raw * w[CONV_W - 1:CONV_W, :]
        for i in range(CONV_W - 1):
            y = y + xp_s[b, 5 + i:5 + i + c, :] * w[i:i + 1, :]
        tail = xp_s[b, c:c + 8, :]
        xp_s[b, 0:8, :] = tail
        conv_out[b] = tail
        qkv.append(jax.nn.silu(y))
        misc = misc_ref[b]
        beta_all.append(jax.nn.sigmoid(misc))
        g_all = -jnp.exp(alog_ref[...]) * _softplus(misc + dtb_ref[...])
        gc_all.append(_dot(tril, g_all, precision=HIGHEST))

    qs, ks, vs, betas, gcs, dec_s, dec_i = [], [], [], [], [], [], []
    for b, hd in chains:
        qh = qkv[b][:, hd * GDN_DK:(hd + 1) * GDN_DK]
        kh = qkv[b][:, GDN_QK + hd * GDN_DK:GDN_QK + (hd + 1) * GDN_DK]
        vs.append(qkv[b][:, 2 * GDN_QK + hd * GDN_DV:2 * GDN_QK + (hd + 1) * GDN_DV])
        qs.append(qh * lax.rsqrt(jnp.sum(qh * qh, axis=1, keepdims=True) + RMS_EPS) * (GDN_DK ** -0.5))
        ks.append(kh * lax.rsqrt(jnp.sum(kh * kh, axis=1, keepdims=True) + RMS_EPS))
        betas.append(beta_all[b][:, MISC_BETA + hd:MISC_BETA + hd + 1])
        gc = gc_all[b][:, MISC_A + hd:MISC_A + hd + 1]
        gcs.append(gc)
        gc_b = jnp.broadcast_to(gc, (c, c))
        gc_row = jnp.sum(jnp.where(eye, gc_b, 0.0), axis=0, keepdims=True)
        diff = gc_b - gc_row
        dec = jnp.exp(diff)
        dec_s.append(jnp.where(strict, dec, 0.0))
        dec_i.append(jnp.where(incl, dec, 0.0))
    kb = [k.astype(BF16) for k in ks]
    qb = [q.astype(BF16) for q in qs]
    pw = [-(beta * _dot_nt(k, k) * d) for beta, k, d in zip(betas, kb, dec_s)]
    inv = [eye_f + p for p in pw]
    for _ in range(int(math.log2(c)) - 1):
        pw = [_dot3(p, p) for p in pw]
        inv = [i + _dot3(i, p) for i, p in zip(inv, pw)]
    egc = [jnp.exp(gc) for gc in gcs]
    sol = [_dot3(i, jnp.concatenate([(beta * e) * k, beta * v], axis=1))
           for i, beta, e, k, v in zip(inv, betas, egc, ks, vs)]
    s_old = [st_s[b, hd] for b, hd in chains]
    sb = [s.astype(BF16) for s in s_old]
    vb = [(x[:, GDN_DK:] - _dot(x[:, :GDN_DK].astype(BF16), s)).astype(BF16) for x, s in zip(sol, sb)]
    attn = [(_dot_nt(q, k) * d).astype(BF16) for q, k, d in zip(qb, kb, dec_i)]
    outs = [e * _dot(q, s) + _dot(a, v) for e, q, s, a, v in zip(egc, qb, sb, attn, vb)]
    for (b, hd), gc, k, v, s, o in zip(chains, gcs, ks, vb, s_old, outs):
        g_last = gc[c - 1:c, :]
        st_s[b, hd] = jnp.exp(g_last) * s + _dot_tn((k * jnp.exp(g_last - gc)).astype(BF16), v)
        o = o * lax.rsqrt(jnp.mean(o * o, axis=1, keepdims=True) + RMS_EPS) * norm_ref[...]
        o_ref[b, :, hd * GDN_DV:(hd + 1) * GDN_DV] = o
    s_out[...] = st_s[...]


def _gdn_call(h3, s0, conv0, gp):
    n, t, _ = h3.shape
    c = min(t, CHUNK)
    const = lambda shape: pl.BlockSpec(shape, lambda j: (0,) * len(shape))
    state = const((n, GDN_HEADS, GDN_DK, GDN_DV))
    conv = const((n, 8, GDN_CONV_CH))
    return pl.pallas_call(
        functools.partial(_gdn_kernel, n=n, c=c),
        grid=(t // c,),
        in_specs=[pl.BlockSpec((n, c, GDN_CONV_CH), lambda j: (0, j, PROJ_OFF['c_qkv'] // GDN_CONV_CH)),
                  pl.BlockSpec((n, c, LANE), lambda j: (0, j, OFF_MISC // LANE)),
                  state, conv,
                  const((CONV_W, GDN_CONV_CH)), const((1, LANE)), const((1, LANE)), const((1, GDN_DV))],
        out_specs=[pl.BlockSpec((n, c, GDN_VW), lambda j: (0, j, 0)), state, conv],
        out_shape=[jax.ShapeDtypeStruct((n, t, GDN_VW), F32),
                   jax.ShapeDtypeStruct((n, GDN_HEADS, GDN_DK, GDN_DV), F32),
                   jax.ShapeDtypeStruct((n, 8, GDN_CONV_CH), F32)],
        scratch_shapes=[pltpu.VMEM((n, c + 8, GDN_CONV_CH), F32),
                        pltpu.VMEM((n, GDN_HEADS, GDN_DK, GDN_DV), F32)],
        compiler_params=_params("arbitrary"),
        name="gdn_branch",
    )(h3, h3, s0, conv0, gp['convw'], gp['alog'], gp['dtb'], gp['norm'])


def _lane_place(v, off):
    return jnp.zeros((1, LANE), F32).at[0, off:off + v.shape[0]].set(v.astype(F32))


def _gla_kernel(qk_ref, v_ref, misc_ref, s0_ref, wg_ref, bg_ref, norm_ref, hexp_ref, gsum_ref,
                o_ref, s_out, st_s, *, n, tc):
    blk = GLA_BLOCK
    wk = GLA_HEADS * GLA_DK

    @pl.when(pl.program_id(0) == 0)
    def _():
        st_s[...] = s0_ref[...]

    row = _iota((blk, blk), 0)
    col = _iota((blk, blk), 1)
    tril = jnp.where(row >= col, 1.0, 0.0)
    jrow = _iota((blk, wk), 0)
    lane_k = _iota((blk, wk), 1)
    head_mask = [(lane_k >= hd * GLA_DK) & (lane_k < (hd + 1) * GLA_DK) for hd in range(GLA_HEADS)]
    chains = range(n)

    def block(sb, carry):
        r0 = pl.multiple_of(sb * blk, blk)
        qk = [qk_ref[b, pl.ds(r0, blk), :] for b in chains]
        q = [x[:, :wk] * (GLA_DK ** -0.5) for x in qk]
        k = [x[:, wk:] for x in qk]
        v = [v_ref[b, pl.ds(r0, blk), :] for b in chains]
        mb = [misc_ref[b, pl.ds(r0, blk), :].astype(BF16) for b in chains]
        lg = [jax.nn.log_sigmoid(_dot(m, wg_ref[...]) + bg_ref[...]) * (1.0 / GLA_TAU) for m in mb]
        bc = [_dot(tril, x, precision=HIGHEST) for x in lg]
        dmat = []
        for b in chains:
            slabs = []
            for i in range(blk):
                keep = jrow <= i
                dec = jnp.where(keep, jnp.exp(bc[b][i:i + 1, :] - bc[b]), 0.0)
                slabs.append(dec * q[b][i:i + 1, :] * k[b])
            dmat.append(jnp.concatenate(slabs, axis=0).astype(BF16))
        a_exp = [_dot(d, hexp_ref[...]) for d in dmat]
        prod = [a * jnp.concatenate([x] * blk, axis=0) for a, x in zip(a_exp, v)]
        o = [_dot(gsum_ref[...], p.astype(BF16)) for p in prod]
        qe = [x * jnp.exp(c) for x, c in zip(q, bc)]
        kt = [x * jnp.exp(c[blk - 1:blk, :] - c) for x, c in zip(k, bc)]
        s = [st_s[b] for b in chains]
        sbf = [x.astype(BF16) for x in s]
        for b in chains:
            upd = jnp.zeros_like(s[b])
            parts = []
            for hd in range(GLA_HEADS):
                parts.append(_dot(jnp.where(head_mask[hd], qe[b], 0.0).astype(BF16), sbf[b]))
                vh = v[b][:, hd * GLA_DV:(hd + 1) * GLA_DV].astype(BF16)
                upd = upd + _dot_tn(jnp.where(head_mask[hd], kt[b], 0.0).astype(BF16), vh)
            ob = o[b] + jnp.concatenate(parts, axis=1)
            b_hi, b_lo = _split_bf16(jnp.broadcast_to(bc[b][blk - 1:blk, :], (8, wk)))
            eighth = jnp.full((8, GLA_DV), 0.125, BF16)
            decay = jnp.exp(_dot_tn(b_hi, eighth) + _dot_tn(b_lo, eighth))
            st_s[b] = decay * s[b] + upd
            for hd in range(GLA_HEADS):
                oh = ob[:, hd * GLA_DV:(hd + 1) * GLA_DV]
                oh = oh * lax.rsqrt(jnp.mean(oh * oh, axis=1, keepdims=True) + RMS_EPS) * norm_ref[...]
                o_ref[b, pl.ds(r0, blk), hd * GLA_DV:(hd + 1) * GLA_DV] = oh
        return carry

    lax.fori_loop(0, tc // blk, block, 0, unroll=2 if tc // blk > 1 else 1)
    s_out[...] = st_s[...]


def _gla_call(h3, s0, gp):
    n, t, _ = h3.shape
    tc = min(t, 128)
    wk = GLA_HEADS * GLA_DK
    wv = GLA_HEADS * GLA_DV
    const = lambda shape: pl.BlockSpec(shape, lambda j: (0,) * len(shape))
    state = const((n, wk, GLA_DV))
    return pl.pallas_call(
        functools.partial(_gla_kernel, n=n, tc=tc),
        grid=(t // tc,),
        in_specs=[pl.BlockSpec((n, tc, 2 * wk), lambda j: (0, j, PROJ_OFF['d_q'] // (2 * wk))),
                  pl.BlockSpec((n, tc, wv), lambda j: (0, j, PROJ_OFF['d_v'] // wv)),
                  pl.BlockSpec((n, tc, LANE), lambda j: (0, j, OFF_MISC // LANE)),
                  state,
                  const((LANE, wk)), const((1, wk)), const((1, GLA_DV)),
                  const((wk, wv)), const((GLA_BLOCK, GLA_BLOCK * GLA_BLOCK))],
        out_specs=[pl.BlockSpec((n, tc, wv), lambda j: (0, j, 0)), state],
        out_shape=[jax.ShapeDtypeStruct((n, t, wv), F32),
                   jax.ShapeDtypeStruct((n, wk, GLA_DV), F32)],
        scratch_shapes=[pltpu.VMEM((n, wk, GLA_DV), F32)],
        compiler_params=_params("arbitrary"),
        name="gla_branch",
    )(h3, h3, h3, s0, gp['wg'], gp['bg'], gp['norm'], gp['hexp'], gp['gsum'])


def _gla_prepare(w_g2, b_g, norm):
    wk = GLA_HEADS * GLA_DK
    wg = jnp.zeros((LANE, wk), F32).at[MISC_G:MISC_G + GLA_GATE_RANK].set(w_g2).astype(BF16)
    ck = jnp.arange(wk)[:, None] // GLA_DK
    cv = jnp.arange(GLA_HEADS * GLA_DV)[None, :] // GLA_DV
    ij = jnp.arange(GLA_BLOCK * GLA_BLOCK)[None, :] // GLA_BLOCK
    return {'wg': wg, 'bg': b_g.reshape(1, wk).astype(F32),
            'norm': norm.reshape(1, GLA_DV).astype(F32),
            'hexp': (ck == cv).astype(BF16),
            'gsum': (jnp.arange(GLA_BLOCK)[:, None] == ij).astype(BF16)}


def _merge_kernel(x_ref, ya_ref, yb_ref, yc_ref, yd_ref, ga_ref, gb_ref, gc_ref, gd_ref, mg_ref,
                  wb_ref, wo_ref, lng_ref, lnb_ref, o_ref):
    mixed = None
    for br, (y_ref, g_ref) in enumerate(((ya_ref, ga_ref), (yb_ref, gb_ref), (yc_ref, gc_ref), (yd_ref, gd_ref))):
        act = (y_ref[...] * jax.nn.silu(g_ref[...])).astype(BF16)
        term = jax.nn.sigmoid(mg_ref[:, br * D_MODEL:(br + 1) * D_MODEL]) * _dot(act, wb_ref[br])
        mixed = term if mixed is None else mixed + term
    z = DN_ALPHA * x_ref[...] + _dot(mixed.astype(BF16), wo_ref[...])
    mu = jnp.mean(z, axis=1, keepdims=True)
    zc = z - mu
    var = jnp.mean(zc * zc, axis=1, keepdims=True)
    o_ref[...] = zc * lax.rsqrt(var + LN_EPS) * lng_ref[...] + lnb_ref[...]


def _merge_call(x, h, ya, yb, yc, yd, wb, wo, lng, lnb):
    rows = x.shape[0]
    tm = min(rows, 256)
    bw = BRANCH_W
    ybs = pl.BlockSpec((tm, bw), lambda i: (i, 0))
    gate = lambda name: pl.BlockSpec((tm, bw), lambda i: (i, PROJ_OFF[name] // bw))
    return pl.pallas_call(
        _merge_kernel,
        grid=(rows // tm,),
        in_specs=[pl.BlockSpec((tm, D_MODEL), lambda i: (i, 0)), ybs, ybs, ybs, ybs,
                  gate('a_gate'), gate('b_gate'), gate('c_gate'), gate('d_gate'),
                  pl.BlockSpec((tm, N_BRANCH * D_MODEL), lambda i: (i, 0)),
                  pl.BlockSpec((N_BRANCH, bw, D_MODEL), lambda i: (0, 0, 0)),
                  pl.BlockSpec((D_MODEL, D_MODEL), lambda i: (0, 0)),
                  pl.BlockSpec((1, D_MODEL), lambda i: (0, 0)),
                  pl.BlockSpec((1, D_MODEL), lambda i: (0, 0))],
        out_specs=pl.BlockSpec((tm, D_MODEL), lambda i: (i, 0)),
        out_shape=jax.ShapeDtypeStruct((rows, D_MODEL), F32),
        compiler_params=_params("parallel"),
        name="merge_norm",
    )(x, ya, yb, yc, yd, h, h, h, h, h, wb, wo, lng, lnb)


def _split_w_in(w_in):
    out, off = {}, 0
    for name, width in IN_LAYOUT:
        out[name] = w_in[:, off:off + width]
        off += width
    return out


def _regroup_kernel(w_ref, proj_ref, key_ref):
    cols = _split_w_in(w_ref[...])
    pad = jnp.zeros((w_ref.shape[0], MISC_PAD), F32)
    proj_ref[...] = jnp.concatenate([cols[name] for name in PROJ_ORDER] + [pad], axis=1).astype(BF16)
    k, v, ki = cols['b_k'], cols['b_v'], cols['b_ki']
    dh = DSA_HEAD_DIM
    key_ref[...] = jnp.concatenate([k[:, :dh], k[:, :dh], k[:, dh:], k[:, dh:],
                                    v[:, :dh], v[:, :dh], v[:, dh:], v[:, dh:], ki, ki], axis=1).astype(BF16)


def _regroup_weights(w_in):
    depth, rows, width = w_in.shape
    tr = 128
    return pl.pallas_call(
        _regroup_kernel,
        grid=(depth, rows // tr),
        in_specs=[pl.BlockSpec((None, tr, width), lambda l, i: (l, i, 0))],
        out_specs=[pl.BlockSpec((None, tr, PROJ_W), lambda l, i: (l, i, 0)),
                   pl.BlockSpec((None, tr, HK_W), lambda l, i: (l, i, 0))],
        out_shape=[jax.ShapeDtypeStruct((depth, rows, PROJ_W), BF16),
                   jax.ShapeDtypeStruct((depth, rows, HK_W), BF16)],
        compiler_params=_params("parallel", "parallel"),
        name="regroup_weights",
    )(w_in)


def _dup_keys(k, v, ki):
    return jnp.concatenate([k[:, :, 0], k[:, :, 0], k[:, :, 1], k[:, :, 1],
                            v[:, :, 0], v[:, :, 0], v[:, :, 1], v[:, :, 1], ki, ki], axis=-1).astype(BF16)


def _layer(x, lw, st):
    n, t, _ = x.shape
    x2 = x.reshape(n * t, D_MODEL)
    h = _matmul(x2, lw['w_proj'], F32, tn=PROJ_W // 9)
    hk_new = _matmul(x2, lw['w_key'], BF16, tn=HK_W).reshape(n, t, HK_W)
    if st is None:
        hk, n_keys, b_off = hk_new, t, 0
        s5r = jnp.zeros((n, 1, S5_LANES), F32)
        s5i = jnp.zeros((n, 1, S5_LANES), F32)
        gdn0 = jnp.zeros((n, GDN_HEADS, GDN_DK, GDN_DV), F32)
        conv0 = jnp.zeros((n, 8, GDN_CONV_CH), F32)
        gla0 = jnp.zeros((n, GLA_HEADS * GLA_DK, GLA_DV), F32)
    else:
        past = st['k'].shape[1]
        n_keys = past + t
        b_off = past // Q_BLOCK
        lp = -(-n_keys // KEY_TILE) * KEY_TILE
        hk = jnp.concatenate([_dup_keys(st['k'], st['v'], st['kidx']), hk_new,
                              jnp.zeros((n, lp - n_keys, HK_W), BF16)], axis=1)
        s5r = st['s5_re'].reshape(n, 1, S5_LANES)
        s5i = st['s5_im'].reshape(n, 1, S5_LANES)
        gdn0 = st['gdn']
        conv0 = jnp.concatenate([jnp.zeros((n, 8 - (CONV_W - 1), GDN_CONV_CH), F32), st['gdn_conv']], axis=1)
        gla0 = st['gla'].reshape(n, GLA_HEADS * GLA_DK, GLA_DV)
    topk = min(DSA_TOPK_MAX, n_keys // 4)

    ya, s5r_new, s5i_new = _s5_call(h, n, t, s5r, s5i, lw['s5'])
    yb = _dsa_call(h, hk, n, t, lw['bias'], lw['tril'], b_off=b_off, n_keys=n_keys, topk=topk)
    h3 = h.reshape(n, t, PROJ_W)
    yc, gdn_new, conv_new = _gdn_call(h3, gdn0, conv0, lw['gdn'])
    yd, gla_new = _gla_call(h3, gla0, lw['gla'])
    yc = yc.reshape(n * t, GDN_VW)
    yd = yd.reshape(n * t, GLA_HEADS * GLA_DV)
    y = _merge_call(x2, h, ya, yb, yc, yd, lw['w_branch'], lw['w_out'], lw['ln_g'], lw['ln_b'])

    def cols(name, width):
        return h[:, PROJ_OFF[name]:PROJ_OFF[name] + width].reshape(n, t, width)

    new = {'k': cols('b_k', DSA_KV_HEADS * DSA_HEAD_DIM).reshape(n, t, DSA_KV_HEADS, DSA_HEAD_DIM),
           'v': cols('b_v', DSA_KV_HEADS * DSA_HEAD_DIM).reshape(n, t, DSA_KV_HEADS, DSA_HEAD_DIM),
           'kidx': cols('b_ki', IDX_DIM),
           's5_re': s5r_new.reshape(n, S5_GROUPS, S5_STATE), 's5_im': s5i_new.reshape(n, S5_GROUPS, S5_STATE),
           'gdn': gdn_new, 'gdn_conv': conv_new[:, 8 - (CONV_W - 1):],
           'gla': gla_new.reshape(n, GLA_HEADS, GLA_DK, GLA_DV)}
    return y.reshape(n, t, D_MODEL), new


def kernel(x_prompt, x_sample, cache_k, cache_v, cache_kidx, state_s5_re, state_s5_im, state_gdn, state_gdn_conv, state_gla, w_in, w_branch, w_out, ln_g, ln_b, rel_bias, s5_a_re, s5_a_im, s5_log_dt, s5_b_re, s5_b_im, s5_c_re, s5_c_im, s5_d, s5_w_glu, gdn_conv, gdn_a_log, gdn_dt_bias, gdn_norm, gla_w_g2, gla_b_g, gla_norm):
    bias = _dsa_bias_tables(rel_bias)
    tril = (jnp.arange(KEY_BLOCK)[:, None] >= jnp.arange(KEY_BLOCK)[None, :]).astype(BF16)
    yp, ys = x_prompt, x_sample
    new_p, new_s = [], []
    w_proj, w_key = _regroup_weights(w_in)
    for l in range(DEPTH):
        lw = {'w_proj': w_proj[l], 'w_key': w_key[l], 'bias': bias, 'tril': tril,
              'w_branch': w_branch[l].astype(BF16), 'w_out': w_out[l].astype(BF16),
              'ln_g': ln_g[l].reshape(1, D_MODEL), 'ln_b': ln_b[l].reshape(1, D_MODEL),
              's5': _s5_prepare(s5_a_re[l], s5_a_im[l], s5_log_dt[l], s5_b_re[l], s5_b_im[l],
                                s5_c_re[l], s5_c_im[l], s5_d[l], s5_w_glu[l]),
              'gdn': {'convw': gdn_conv[l], 'alog': _lane_place(gdn_a_log[l], MISC_A),
                      'dtb': _lane_place(gdn_dt_bias[l], MISC_A), 'norm': gdn_norm[l].reshape(1, GDN_DV)},
              'gla': _gla_prepare(gla_w_g2[l], gla_b_g[l], gla_norm[l])}
        st = {'k': cache_k[l], 'v': cache_v[l], 'kidx': cache_kidx[l], 's5_re': state_s5_re[l],
              's5_im': state_s5_im[l], 'gdn': state_gdn[l], 'gdn_conv': state_gdn_conv[l], 'gla': state_gla[l]}
        yp, stp = _layer(yp, lw, None)
        ys, sts = _layer(ys, lw, st)
        new_p.append(stp)
        new_s.append(sts)
    names = ('k', 'v', 'kidx', 's5_re', 's5_im', 'gdn', 'gdn_conv', 'gla')
    stack = lambda states, name: jnp.stack([s[name] for s in states], axis=0)
    return ((yp, ys) + tuple(stack(new_p, nm) for nm in names) + tuple(stack(new_s, nm) for nm in names))
```

```python
import functools
import math

import jax
import jax.numpy as jnp
from jax import lax
from jax.experimental import pallas as pl
from jax.experimental.pallas import tpu as pltpu

F32 = jnp.float32
BF16 = jnp.bfloat16
HIGHEST = lax.Precision.HIGHEST

D_MODEL = 1024
DEPTH = 2
CHUNK = 64
N_BRANCH = 4
BRANCH_W = 512
S5_GROUP = 16
S5_GROUPS = BRANCH_W // S5_GROUP
S5_STATE = 64
S5_LANES = S5_GROUPS * S5_STATE
DSA_HEADS = 8
DSA_KV_HEADS = 2
DSA_HEAD_DIM = 64
IDX_HEADS = 4
IDX_DIM = 64
DSA_TOPK_MAX = 256
REL_BUCKETS = 32
REL_MAX_DIST = 128
GDN_HEADS = 4
GDN_DK = 128
GDN_DV = 128
GDN_QK = GDN_HEADS * GDN_DK
GDN_VW = GDN_HEADS * GDN_DV
CONV_W = 4
GDN_CONV_CH = 2 * GDN_QK + GDN_VW
GLA_HEADS = 4
GLA_DK = 64
GLA_DV = 128
GLA_GATE_RANK = 16
GLA_TAU = 16.0
GLA_BLOCK = 16
LN_EPS = 1e-5
RMS_EPS = 1e-6
DN_ALPHA = (2 * DEPTH) ** 0.25
LOG2E = math.log2(math.e)

LANE = 128
KEY_BLOCK = 128
KEY_SUB = 4
KEY_TILE = KEY_SUB * KEY_BLOCK
SOFTMAX_CEIL = 2.0 ** 120
SOFTMAX_FLOOR = 2.0 ** -60
FOLD_ROWS = 64
Q_BLOCK = 128
VMEM_LIMIT = 56 * 1024 * 1024

IN_LAYOUT = (
    ('a_u', BRANCH_W), ('a_gate', BRANCH_W),
    ('b_q', DSA_HEADS * DSA_HEAD_DIM), ('b_k', DSA_KV_HEADS * DSA_HEAD_DIM),
    ('b_v', DSA_KV_HEADS * DSA_HEAD_DIM), ('b_qi', IDX_HEADS * IDX_DIM), ('b_ki', IDX_DIM),
    ('b_wi', IDX_HEADS), ('b_gate', BRANCH_W),
    ('c_qkv', GDN_CONV_CH), ('c_beta', GDN_HEADS), ('c_a', GDN_HEADS), ('c_gate', BRANCH_W),
    ('d_q', GLA_HEADS * GLA_DK), ('d_k', GLA_HEADS * GLA_DK), ('d_v', GLA_HEADS * GLA_DV),
    ('d_g', GLA_GATE_RANK), ('d_gate', BRANCH_W),
    ('merge', N_BRANCH * D_MODEL),
)

PROJ_ORDER = ('merge', 'a_u', 'a_gate', 'b_q', 'b_qi', 'b_k', 'b_v', 'c_qkv', 'b_gate', 'c_gate',
              'd_v', 'd_gate', 'd_q', 'd_k', 'b_ki', 'b_wi', 'c_beta', 'c_a', 'd_g')
MISC_WI = IDX_DIM
MISC_BETA = MISC_WI + IDX_HEADS
MISC_A = MISC_BETA + GDN_HEADS
MISC_G = MISC_A + GDN_HEADS
MISC_PAD = LANE - (MISC_G + GLA_GATE_RANK)


def _proj_offsets():
    widths = dict(IN_LAYOUT)
    off, out = 0, {}
    for name in PROJ_ORDER:
        out[name] = off
        off += widths[name]
    return out, off + MISC_PAD


PROJ_OFF, PROJ_W = _proj_offsets()
OFF_MISC = PROJ_OFF['b_ki']
HK_W = 5 * LANE
HK_V = 2 * LANE
HK_KI = 4 * LANE


def _dot(a, b, precision=None):
    return jnp.dot(a, b, preferred_element_type=F32, precision=precision)


def _dot_nt(a, b, precision=None):
    return lax.dot_general(a, b, (((1,), (1,)), ((), ())), preferred_element_type=F32, precision=precision)


def _dot_tn(a, b, precision=None):
    return lax.dot_general(a, b, (((0,), (0,)), ((), ())), preferred_element_type=F32, precision=precision)


def _split_bf16(x):
    hi = x.astype(BF16)
    return hi, (x - hi.astype(F32)).astype(BF16)


def _dot3(a, b):
    a_hi, a_lo = _split_bf16(a)
    b_hi, b_lo = _split_bf16(b)
    return _dot(a_hi, b_hi) + (_dot(a_hi, b_lo) + _dot(a_lo, b_hi))


def _params(*sem):
    return pltpu.CompilerParams(dimension_semantics=sem, vmem_limit_bytes=VMEM_LIMIT)


def _softplus(x):
    return jnp.maximum(x, 0.0) + jnp.log1p(jnp.exp(-jnp.abs(x)))


def _iota(shape, dim):
    return lax.broadcasted_iota(jnp.int32, shape, dim)


def _mm_kernel(x_ref, w_ref, o_ref):
    o_ref[...] = _dot(x_ref[...].astype(BF16), w_ref[...]).astype(o_ref.dtype)


def _matmul(x, w, out_dtype, tn):
    rows, kdim = x.shape
    width = w.shape[1]
    tm = min(rows, 1024)
    return pl.pallas_call(
        _mm_kernel,
        grid=(rows // tm, width // tn),
        in_specs=[pl.BlockSpec((tm, kdim), lambda i, j: (i, 0)),
                  pl.BlockSpec((kdim, tn), lambda i, j: (0, j))],
        out_specs=pl.BlockSpec((tm, tn), lambda i, j: (i, j)),
        out_shape=jax.ShapeDtypeStruct((rows, width), out_dtype),
        compiler_params=_params("parallel", "parallel"),
        name="proj_matmul",
    )(x, w)


def _s5_kernel(u_ref, h0r_ref, h0i_ref, ar_ref, ai_ref, bbr_ref, bbi_ref, ccr_ref, cci_ref, d_ref, wglu_ref,
               ya_ref, hr_out, hi_out, xr_s, xi_s, cr_s, ci_s, *, tc):
    @pl.when(pl.program_id(1) == 0)
    def _():
        cr_s[...] = h0r_ref[...]
        ci_s[...] = h0i_ref[...]

    u = u_ref[...]
    ub = u.astype(BF16)
    xr_s[...] = _dot(ub, bbr_ref[...])
    xi_s[...] = _dot(ub, bbi_ref[...])
    ar = ar_ref[...]
    ai = ai_ref[...]

    def step(t, carry):
        hr, hi = carry
        nhr = ar * hr - ai * hi + xr_s[pl.ds(t, 1), :]
        nhi = ar * hi + ai * hr + xi_s[pl.ds(t, 1), :]
        xr_s[pl.ds(t, 1), :] = nhr
        xi_s[pl.ds(t, 1), :] = nhi
        return nhr, nhi

    hr, hi = lax.fori_loop(0, tc, step, (cr_s[...], ci_s[...]), unroll=8)
    cr_s[...] = hr
    ci_s[...] = hi
    hr_out[...] = hr
    hi_out[...] = hi
    y = (_dot(xr_s[...].astype(BF16), ccr_ref[...]) - _dot(xi_s[...].astype(BF16), cci_ref[...])
         + d_ref[...] * u)
    y = jax.nn.gelu(y)
    g = _dot(y.astype(BF16), wglu_ref[...])
    ya_ref[...] = g[:, :BRANCH_W] * jax.nn.sigmoid(g[:, BRANCH_W:])


def _s5_call(h, n, t, h0r, h0i, sp):
    tc = min(t, 256)
    nc = t // tc
    const = lambda shape: pl.BlockSpec(shape, lambda i, c: (0,) * len(shape))
    state = pl.BlockSpec((None, 1, S5_LANES), lambda i, c: (i, 0, 0))
    return pl.pallas_call(
        functools.partial(_s5_kernel, tc=tc),
        grid=(n, nc),
        in_specs=[pl.BlockSpec((tc, BRANCH_W), lambda i, c: (i * nc + c, PROJ_OFF['a_u'] // BRANCH_W)),
                  state, state,
                  const((1, S5_LANES)), const((1, S5_LANES)),
                  const((BRANCH_W, S5_LANES)), const((BRANCH_W, S5_LANES)),
                  const((S5_LANES, BRANCH_W)), const((S5_LANES, BRANCH_W)),
                  const((1, BRANCH_W)), const((BRANCH_W, 2 * BRANCH_W))],
        out_specs=[pl.BlockSpec((tc, BRANCH_W), lambda i, c: (i * nc + c, 0)), state, state],
        out_shape=[jax.ShapeDtypeStruct((n * t, BRANCH_W), F32),
                   jax.ShapeDtypeStruct((n, 1, S5_LANES), F32),
                   jax.ShapeDtypeStruct((n, 1, S5_LANES), F32)],
        scratch_shapes=[pltpu.VMEM((tc, S5_LANES), F32), pltpu.VMEM((tc, S5_LANES), F32),
                        pltpu.VMEM((1, S5_LANES), F32), pltpu.VMEM((1, S5_LANES), F32)],
        compiler_params=_params("parallel", "arbitrary"),
        name="s5_branch",
    )(h, h0r, h0i, sp['ar'], sp['ai'], sp['bbr'], sp['bbi'], sp['ccr'], sp['cci'], sp['d'], sp['wglu'])


def _s5_prepare(a_re, a_im, log_dt, b_re, b_im, c_re, c_im, d, w_glu):
    lam_r = jnp.minimum(a_re, -1e-4)
    lam_i = a_im
    dt = jnp.exp(log_dt)[:, None]
    mag = jnp.exp(lam_r * dt)
    ar = mag * jnp.cos(lam_i * dt)
    ai = mag * jnp.sin(lam_i * dt)
    den = lam_r * lam_r + lam_i * lam_i
    nr = ar - 1.0
    cr = (nr * lam_r + ai * lam_i) / den
    ci = (ai * lam_r - nr * lam_i) / den
    bb_r = cr[..., None] * b_re - ci[..., None] * b_im
    bb_i = cr[..., None] * b_im + ci[..., None] * b_re
    eye = jnp.eye(S5_GROUPS, dtype=F32)

    def blockdiag_in(bb):
        m = eye[:, None, :, None] * jnp.swapaxes(bb, 1, 2)[:, :, None, :]
        return m.reshape(BRANCH_W, S5_LANES).astype(BF16)

    def blockdiag_out(c):
        m = eye[:, None, :, None] * jnp.swapaxes(c, 1, 2)[:, :, None, :]
        return m.reshape(S5_LANES, BRANCH_W).astype(BF16)

    return {'ar': ar.reshape(1, S5_LANES), 'ai': ai.reshape(1, S5_LANES),
            'bbr': blockdiag_in(bb_r), 'bbi': blockdiag_in(bb_i),
            'ccr': blockdiag_out(c_re), 'cci': blockdiag_out(c_im),
            'd': d.reshape(1, BRANCH_W), 'wglu': w_glu.astype(BF16)}


def _sortable(x):
    bits = lax.bitcast_convert_type(x, jnp.int32)
    key = bits ^ ((bits >> 31) & jnp.int32(0x7FFFFFFF))
    return jnp.where(key == -1, 0, key)


def _dsa_kernel(q_ref, qi_ref, misc_ref, hk_ref, bias_ref, tril_ref, o_ref, sc_s, pk_s, acc_s, far_s, *, qb, b_off, n_keys, topk):
    b = pl.program_id(1) + b_off
    nkt = (b + KEY_SUB) // KEY_SUB
    fold = KEY_TILE // FOLD_ROWS
    key_row = _iota((KEY_TILE, qb), 0)
    qpos = b * Q_BLOCK + _iota((1, qb), 1)
    limit = jnp.minimum(((qpos >> 6) + 1) * CHUNK, n_keys)
    low_half = _iota((qb, LANE), 1) < DSA_HEAD_DIM

    def key_rows(ref_cols, j):
        return hk_ref[pl.ds(pl.multiple_of(j * KEY_TILE, KEY_TILE), KEY_TILE), ref_cols:ref_cols + LANE]

    def half_masked(x, hd):
        pair = x[:, (hd // 2) * LANE:(hd // 2 + 1) * LANE]
        return jnp.where(low_half == (hd % 2 == 0), pair, 0.0).astype(BF16)

    misc = misc_ref[...]
    pick = jnp.where(_iota((8, LANE), 1) == MISC_WI + _iota((8, LANE), 0), 1.0, 0.0)
    wi_t = _dot_nt(pick, misc, precision=HIGHEST) * (IDX_DIM ** -0.5 * IDX_HEADS ** -0.5)
    qi = qi_ref[...]
    qi_stack = jnp.concatenate([half_masked(qi, hd) for hd in range(IDX_HEADS)], axis=0)

    def pack16(v):
        half = KEY_TILE // 2
        return (v[0:half] & 0xFFFF) | (v[half:] << 16)

    def score_block(j, carry):
        z = _dot_nt(key_rows(HK_KI, j), qi_stack)
        sc = jnp.zeros((KEY_TILE, qb), F32)
        for hd in range(IDX_HEADS):
            sc = sc + jnp.maximum(z[:, hd * qb:(hd + 1) * qb], 0.0) * wi_t[hd:hd + 1, :]
        sc = jnp.where(j * KEY_TILE + key_row < limit, sc, -jnp.inf)
        key = _sortable(sc)
        sc_s[j] = key
        pk_s[j] = pack16(key >> 16)
        return carry

    lax.fori_loop(0, nkt, score_block, 0)

    def count_ge(cand):
        def body(j, acc):
            return acc + jnp.where(sc_s[j] >= cand, 1.0, 0.0).reshape(fold, FOLD_ROWS, qb).sum(axis=0)
        acc = lax.fori_loop(0, nkt, body, jnp.zeros((FOLD_ROWS, qb), F32))
        return jnp.sum(acc, axis=0, keepdims=True)

    one16 = jnp.ones((KEY_TILE, qb), jnp.int16)
    zero16 = jnp.zeros((KEY_TILE, qb), jnp.int16)

    def count16_ge(cand):
        word = jnp.broadcast_to((cand & 0xFFFF) | (cand << 16), (8, qb))
        c16 = pltpu.bitcast(word, jnp.int16)[0:1, :]

        def body(j, acc):
            ind = jnp.where(pltpu.bitcast(pk_s[j], jnp.int16) >= c16, one16, zero16)
            for i in range(fold):
                acc = acc + ind[i * FOLD_ROWS:(i + 1) * FOLD_ROWS]
            return acc
        acc = lax.fori_loop(0, nkt, body, jnp.zeros((FOLD_ROWS, qb), jnp.int16))
        return jnp.sum(acc.astype(F32), axis=0, keepdims=True)

    def radix_search(start, want):
        def bit_step(i, res):
            cand = res + (jnp.int32(1) << (14 - i))
            return jnp.where(count16_ge(cand) >= want, cand, res)
        return lax.fori_loop(0, 15, bit_step, start)

    kf = float(topk)
    top = radix_search(jnp.where(count16_ge(jnp.zeros((1, qb), jnp.int32)) >= kf, 0, -32768), kf)
    int16_max = jnp.iinfo(jnp.int16).max
    above = jnp.where(top == int16_max, 0.0, count16_ge(jnp.minimum(top + 1, int16_max)))
    want_low = kf - above

    def pack_low(j, carry):
        key = sc_s[j]
        pk_s[j] = pack16(jnp.where((key >> 16) == top, (key >> 1) & 0x7FFF, -1))
        return carry

    lax.fori_loop(0, nkt, pack_low, 0)
    low = radix_search(jnp.zeros((1, qb), jnp.int32), want_low)
    base = (top << 16) | (low << 1)
    thr = jnp.where(count_ge(base | 1) >= kf, base | 1, base)
    need = kf - count_ge(thr + 1)

    tril = tril_ref[...]

    def select_block(j, ties):
        keys = sc_s[j]
        eq = keys == thr
        eq_b = jnp.where(eq, 1.0, 0.0).astype(BF16)
        parts, carry = [], ties
        for s in range(KEY_SUB):
            part = _dot(tril, eq_b[s * KEY_BLOCK:(s + 1) * KEY_BLOCK]) + carry
            parts.append(part)
            carry = part[KEY_BLOCK - 1:KEY_BLOCK, :]
        prefix = jnp.concatenate(parts, axis=0)
        sel = ((keys > thr) | (eq & (prefix <= need))) & (j * KEY_TILE + key_row < limit)
        sc_s[j] = lax.bitcast_convert_type(jnp.where(sel, 0.0, -jnp.inf), jnp.int32)
        return prefix[KEY_TILE - 1:KEY_TILE, :]

    lax.fori_loop(0, nkt, select_block, jnp.zeros((1, qb), F32))

    q = q_ref[...] * (DSA_HEAD_DIM ** -0.5 * LOG2E)
    rep = DSA_HEADS // DSA_KV_HEADS
    eye = jnp.where(_iota((LANE, LANE), 0) == _iota((LANE, LANE), 1), 1.0, 0.0)
    value_lane = _iota((KEY_TILE, LANE), 1) < DSA_HEAD_DIM
    n_pairs = DSA_HEADS // 2
    q_pair = [jnp.concatenate([half_masked(q, 2 * c), half_masked(q, 2 * c + 1)], axis=0) for c in range(n_pairs)]
    far_bias = [bias_ref[hd, 2, 0:1, 0:qb] for hd in range(DSA_HEADS)]
    n_far = jnp.maximum((b - 1) // KEY_SUB, 0)

    def tile_operands(j):
        mask = lax.bitcast_convert_type(sc_s[j], F32)
        kd = [key_rows(grp * LANE, j) for grp in range(DSA_KV_HEADS)]
        zs = [_dot_nt(kd[(2 * c) // rep], q_pair[c]) for c in range(n_pairs)]
        vd = [jnp.where(value_lane, key_rows(HK_V + grp * LANE, j), 1.0).astype(BF16)
              for grp in range(DSA_KV_HEADS)]
        return mask, zs, vd

    def near_bias(hd, j):
        bias = [bias_ref[hd, jnp.clip(b - (KEY_SUB * j + s), 0, 2), :, 0:qb] for s in range(KEY_SUB)]
        return jnp.concatenate(bias, axis=0)

    def plain_tile(j, carry, far):
        mask, zs, vd = tile_operands(j)
        ps = []
        for hd in range(DSA_HEADS):
            r = hd % 2
            zh = zs[hd // 2][:, r * qb:(r + 1) * qb] + mask
            if not far:
                zh = zh + near_bias(hd, j)
            ps.append(jnp.exp2(zh).astype(BF16))
        acc = far_s if far else acc_s
        for c in range(n_pairs):
            acc[c] += _dot_tn(vd[(2 * c) // rep], jnp.concatenate(ps[2 * c:2 * c + 2], axis=1))
        return carry

    acc_s[...] = jnp.zeros_like(acc_s)
    far_s[...] = jnp.zeros_like(far_s)
    lax.fori_loop(0, n_far, lambda j, c: plain_tile(j, c, True), 0)
    lax.fori_loop(n_far, nkt, lambda j, c: plain_tile(j, c, False), 0)
    healthy = jnp.ones((1, 1), F32)
    for c in range(n_pairs):
        scale = jnp.concatenate([jnp.exp2(far_bias[2 * c]), jnp.exp2(far_bias[2 * c + 1])], axis=1)
        total = far_s[c] * scale + acc_s[c]
        acc_s[c] = total
        norm = total[DSA_HEAD_DIM:DSA_HEAD_DIM + 1, :]
        ok = (jnp.abs(total) < SOFTMAX_CEIL) & (norm > SOFTMAX_FLOOR)
        healthy = jnp.minimum(healthy, jnp.min(jnp.where(ok, 1.0, 0.0), keepdims=True))

    def attend(j, ms, far):
        mask, zs, vd = tile_operands(j)
        new_ms, ps, alphas = [], [], []
        for hd in range(DSA_HEADS):
            r = hd % 2
            zh = zs[hd // 2][:, r * qb:(r + 1) * qb] + mask
            if not far:
                zh = zh + near_bias(hd, j)
            tile_max = jnp.max(zh.reshape(fold, FOLD_ROWS, qb).max(axis=0), axis=0, keepdims=True)
            shift = far_bias[hd] if far else 0.0
            m_new = jnp.maximum(ms[hd], tile_max + shift)
            ps.append(jnp.exp2(zh - (m_new - shift)).astype(BF16))
            alphas.append(jnp.exp2(ms[hd] - m_new))
            new_ms.append(m_new)
        for c in range(n_pairs):
            acc_s[c] = (acc_s[c] * jnp.concatenate(alphas[2 * c:2 * c + 2], axis=1)
                        + _dot_tn(vd[(2 * c) // rep], jnp.concatenate(ps[2 * c:2 * c + 2], axis=1)))
        return tuple(new_ms)

    @pl.when(healthy[0, 0] < 0.5)
    def _():
        acc_s[...] = jnp.zeros_like(acc_s)
        ms = tuple(jnp.full((1, qb), -1e30, F32) for _ in range(DSA_HEADS))
        ms = lax.fori_loop(0, n_far, lambda j, m: attend(j, m, True), ms)
        lax.fori_loop(n_far, nkt, lambda j, m: attend(j, m, False), ms)

    for c in range(n_pairs):
        halves = []
        for r in range(2):
            blk = acc_s[c, :, r * qb:(r + 1) * qb]
            halves.append(blk[0:DSA_HEAD_DIM] / blk[DSA_HEAD_DIM:DSA_HEAD_DIM + 1])
        o_t = jnp.concatenate(halves, axis=0)
        o_ref[:, c * LANE:(c + 1) * LANE] = _dot_tn(o_t, eye, precision=HIGHEST)


def _dsa_call(h, hk, n, t, bias, tril, *, b_off, n_keys, topk):
    qb = min(t, Q_BLOCK)
    nqb = t // qb
    lp = hk.shape[1]
    wq = DSA_HEADS * DSA_HEAD_DIM
    wqi = IDX_HEADS * IDX_DIM
    return pl.pallas_call(
        functools.partial(_dsa_kernel, qb=qb, b_off=b_off, n_keys=n_keys, topk=topk),
        grid=(n, nqb),
        in_specs=[pl.BlockSpec((qb, wq), lambda i, c: (i * nqb + c, PROJ_OFF['b_q'] // wq)),
                  pl.BlockSpec((qb, wqi), lambda i, c: (i * nqb + c, PROJ_OFF['b_qi'] // wqi)),
                  pl.BlockSpec((qb, LANE), lambda i, c: (i * nqb + c, OFF_MISC // LANE)),
                  pl.BlockSpec((None, lp, HK_W), lambda i, c: (i, 0, 0)),
                  pl.BlockSpec((DSA_HEADS, 3, KEY_BLOCK, Q_BLOCK), lambda i, c: (0, 0, 0, 0)),
                  pl.BlockSpec((KEY_BLOCK, KEY_BLOCK), lambda i, c: (0, 0))],
        out_specs=pl.BlockSpec((qb, wq), lambda i, c: (i * nqb + c, 0)),
        out_shape=jax.ShapeDtypeStruct((n * t, wq), F32),
        scratch_shapes=[pltpu.VMEM((lp // KEY_TILE, KEY_TILE, qb), jnp.int32),
                        pltpu.VMEM((lp // KEY_TILE, KEY_TILE // 2, qb), jnp.int32),
                        pltpu.VMEM((DSA_HEADS // 2, LANE, 2 * qb), F32),
                        pltpu.VMEM((DSA_HEADS // 2, LANE, 2 * qb), F32)],
        compiler_params=_params("parallel", "arbitrary"),
        name="dsa_branch",
    )(h, h, h, hk, bias, tril)


def _t5_bucket(rel):
    nb = REL_BUCKETS // 2
    max_exact = nb // 2
    ret = jnp.where(rel > 0, nb, 0)
    dist = jnp.abs(rel)
    distf = jnp.maximum(dist, 1).astype(F32)
    large = max_exact + (jnp.log(distf / max_exact) / math.log(REL_MAX_DIST / max_exact)
                         * (nb - max_exact)).astype(jnp.int32)
    large = jnp.minimum(large, nb - 1)
    return ret + jnp.where(dist < max_exact, dist, large)


def _dsa_bias_tables(rel_bias):
    lc = jnp.arange(KEY_BLOCK, dtype=jnp.int32)[:, None]
    qr = jnp.arange(Q_BLOCK, dtype=jnp.int32)[None, :]
    rel = jnp.stack([lc - qr - back * KEY_BLOCK for back in range(3)], axis=0)
    onehot = jax.nn.one_hot(_t5_bucket(rel), REL_BUCKETS, dtype=F32)
    bias = jnp.einsum('klqb,bh->hklq', onehot, rel_bias.astype(F32), precision=HIGHEST)
    return bias * LOG2E


def _gdn_kernel(raw_ref, misc_ref, s0_ref, conv0_ref, convw_ref, alog_ref, dtb_ref, norm_ref,
                o_ref, s_out, conv_out, xp_s, st_s, *, n, c):
    @pl.when(pl.program_id(0) == 0)
    def _():
        xp_s[:, 0:8, :] = conv0_ref[...]
        st_s[...] = s0_ref[...]

    w = convw_ref[...]
    row = _iota((c, c), 0)
    col = _iota((c, c), 1)
    incl = row >= col
    strict = row > col
    eye = row == col
    tril = jnp.where(incl, 1.0, 0.0)
    eye_f = jnp.where(eye, 1.0, 0.0)
    chains = [(b, hd) for b in range(n) for hd in range(GDN_HEADS)]

    qkv, beta_all, gc_all = [], [], []
    for b in range(n):
        raw = raw_ref[b]
        xp_s[b, 8:8 + c, :] = raw
        y = raw * w[CONV_W - 1:CONV_W, :]
        for i in range(CONV_W - 1):
            y = y + xp_s[b, 5 + i:5 + i + c, :] * w[i:i + 1, :]
        tail = xp_s[b, c:c + 8, :]
        xp_s[b, 0:8, :] = tail
        conv_out[b] = tail
        qkv.append(jax.nn.silu(y))
        misc = misc_ref[b]
        beta_all.append(jax.nn.sigmoid(misc))
        g_all = -jnp.exp(alog_ref[...]) * _softplus(misc + dtb_ref[...])
        gc_all.append(_dot(tril, g_all, precision=HIGHEST))

    qs, ks, vs, betas, gcs, dec_s, dec_i = [], [], [], [], [], [], []
    for b, hd in chains:
        qh = qkv[b][:, hd * GDN_DK:(hd + 1) * GDN_DK]
        kh = qkv[b][:, GDN_QK + hd * GDN_DK:GDN_QK + (hd + 1) * GDN_DK]
        vs.append(qkv[b][:, 2 * GDN_QK + hd * GDN_DV:2 * GDN_QK + (hd + 1) * GDN_DV])
        qs.append(qh * lax.rsqrt(jnp.sum(qh * qh, axis=1, keepdims=True) + RMS_EPS) * (GDN_DK ** -0.5))
        ks.append(kh * lax.rsqrt(jnp.sum(kh * kh, axis=1, keepdims=True) + RMS_EPS))
        betas.append(beta_all[b][:, MISC_BETA + hd:MISC_BETA + hd + 1])
        gc = gc_all[b][:, MISC_A + hd:MISC_A + hd + 1]
        gcs.append(gc)
        gc_b = jnp.broadcast_to(gc, (c, c))
        gc_row = jnp.sum(jnp.where(eye, gc_b, 0.0), axis=0, keepdims=True)
        diff = gc_b - gc_row
        dec = jnp.exp(diff)
        dec_s.append(jnp.where(strict, dec, 0.0))
        dec_i.append(jnp.where(incl, dec, 0.0))
    kb = [k.astype(BF16) for k in ks]
    qb = [q.astype(BF16) for q in qs]
    pw = [-(beta * _dot_nt(k, k) * d) for beta, k, d in zip(betas, kb, dec_s)]
    inv = [eye_f + p for p in pw]
    for _ in range(int(math.log2(c)) - 1):
        pw = [_dot3(p, p) for p in pw]
        inv = [i + _dot3(i, p) for i, p in zip(inv, pw)]
    egc = [jnp.exp(gc) for gc in gcs]
    sol = [_dot3(i, jnp.concatenate([(beta * e) * k, beta * v], axis=1))
           for i, beta, e, k, v in zip(inv, betas, egc, ks, vs)]
    s_old = [st_s[b, hd] for b, hd in chains]
    sb = [s.astype(BF16) for s in s_old]
    vb = [(x[:, GDN_DK:] - _dot(x[:, :GDN_DK].astype(BF16), s)).astype(BF16) for x, s in zip(sol, sb)]
    attn = [(_dot_nt(q, k) * d).astype(BF16) for q, k, d in zip(qb, kb, dec_i)]
    outs = [e * _dot(q, s) + _dot(a, v) for e, q, s, a, v in zip(egc, qb, sb, attn, vb)]
    for (b, hd), gc, k, v, s, o in zip(chains, gcs, ks, vb, s_old, outs):
        g_last = gc[c - 1:c, :]
        st_s[b, hd] = jnp.exp(g_last) * s + _dot_tn((k * jnp.exp(g_last - gc)).astype(BF16), v)
        o = o * lax.rsqrt(jnp.mean(o * o, axis=1, keepdims=True) + RMS_EPS) * norm_ref[...]
        o_ref[b, :, hd * GDN_DV:(hd + 1) * GDN_DV] = o
    s_out[...] = st_s[...]


def _gdn_call(h3, s0, conv0, gp):
    n, t, _ = h3.shape
    c = min(t, CHUNK)
    const = lambda shape: pl.BlockSpec(shape, lambda j: (0,) * len(shape))
    state = const((n, GDN_HEADS, GDN_DK, GDN_DV))
    conv = const((n, 8, GDN_CONV_CH))
    return pl.pallas_call(
        functools.partial(_gdn_kernel, n=n, c=c),
        grid=(t // c,),
        in_specs=[pl.BlockSpec((n, c, GDN_CONV_CH), lambda j: (0, j, PROJ_OFF['c_qkv'] // GDN_CONV_CH)),
                  pl.BlockSpec((n, c, LANE), lambda j: (0, j, OFF_MISC // LANE)),
                  state, conv,
                  const((CONV_W, GDN_CONV_CH)), const((1, LANE)), const((1, LANE)), const((1, GDN_DV))],
        out_specs=[pl.BlockSpec((n, c, GDN_VW), lambda j: (0, j, 0)), state, conv],
        out_shape=[jax.ShapeDtypeStruct((n, t, GDN_VW), F32),
                   jax.ShapeDtypeStruct((n, GDN_HEADS, GDN_DK, GDN_DV), F32),
                   jax.ShapeDtypeStruct((n, 8, GDN_CONV_CH), F32)],
        scratch_shapes=[pltpu.VMEM((n, c + 8, GDN_CONV_CH), F32),
                        pltpu.VMEM((n, GDN_HEADS, GDN_DK, GDN_DV), F32)],
        compiler_params=_params("arbitrary"),
        name="gdn_branch",
    )(h3, h3, s0, conv0, gp['convw'], gp['alog'], gp['dtb'], gp['norm'])


def _lane_place(v, off):
    return jnp.zeros((1, LANE), F32).at[0, off:off + v.shape[0]].set(v.astype(F32))


def _gla_kernel(qk_ref, v_ref, misc_ref, s0_ref, wg_ref, bg_ref, norm_ref, hexp_ref, gsum_ref,
                o_ref, s_out, st_s, *, n, tc):
    blk = GLA_BLOCK
    wk = GLA_HEADS * GLA_DK

    @pl.when(pl.program_id(0) == 0)
    def _():
        st_s[...] = s0_ref[...]

    row = _iota((blk, blk), 0)
    col = _iota((blk, blk), 1)
    tril = jnp.where(row >= col, 1.0, 0.0)
    jrow = _iota((blk, wk), 0)
    lane_k = _iota((blk, wk), 1)
    head_mask = [(lane_k >= hd * GLA_DK) & (lane_k < (hd + 1) * GLA_DK) for hd in range(GLA_HEADS)]
    chains = range(n)

    def block(sb, carry):
        r0 = pl.multiple_of(sb * blk, blk)
        qk = [qk_ref[b, pl.ds(r0, blk), :] for b in chains]
        q = [x[:, :wk] * (GLA_DK ** -0.5) for x in qk]
        k = [x[:, wk:] for x in qk]
        v = [v_ref[b, pl.ds(r0, blk), :] for b in chains]
        mb = [misc_ref[b, pl.ds(r0, blk), :].astype(BF16) for b in chains]
        lg = [jax.nn.log_sigmoid(_dot(m, wg_ref[...]) + bg_ref[...]) * (1.0 / GLA_TAU) for m in mb]
        bc = [_dot(tril, x, precision=HIGHEST) for x in lg]
        dmat = []
        for b in chains:
            slabs = []
            for i in range(blk):
                keep = jrow <= i
                dec = jnp.where(keep, jnp.exp(bc[b][i:i + 1, :] - bc[b]), 0.0)
                slabs.append(dec * q[b][i:i + 1, :] * k[b])
            dmat.append(jnp.concatenate(slabs, axis=0).astype(BF16))
        a_exp = [_dot(d, hexp_ref[...]) for d in dmat]
        prod = [a * jnp.concatenate([x] * blk, axis=0) for a, x in zip(a_exp, v)]
        o = [_dot(gsum_ref[...], p.astype(BF16)) for p in prod]
        qe = [x * jnp.exp(c) for x, c in zip(q, bc)]
        kt = [x * jnp.exp(c[blk - 1:blk, :] - c) for x, c in zip(k, bc)]
        s = [st_s[b] for b in chains]
        sbf = [x.astype(BF16) for x in s]
        for b in chains:
            upd = jnp.zeros_like(s[b])
            parts = []
            for hd in range(GLA_HEADS):
                parts.append(_dot(jnp.where(head_mask[hd], qe[b], 0.0).astype(BF16), sbf[b]))
                vh = v[b][:, hd * GLA_DV:(hd + 1) * GLA_DV].astype(BF16)
                upd = upd + _dot_tn(jnp.where(head_mask[hd], kt[b], 0.0).astype(BF16), vh)
            ob = o[b] + jnp.concatenate(parts, axis=1)
            b_hi, b_lo = _split_bf16(jnp.broadcast_to(bc[b][blk - 1:blk, :], (8, wk)))
            eighth = jnp.full((8, GLA_DV), 0.125, BF16)
            decay = jnp.exp(_dot_tn(b_hi, eighth) + _dot_tn(b_lo, eighth))
            st_s[b] = decay * s[b] + upd
            for hd in range(GLA_HEADS):
                oh = ob[:, hd * GLA_DV:(hd + 1) * GLA_DV]
                oh = oh * lax.rsqrt(jnp.mean(oh * oh, axis=1, keepdims=True) + RMS_EPS) * norm_ref[...]
                o_ref[b, pl.ds(r0, blk), hd * GLA_DV:(hd + 1) * GLA_DV] = oh
        return carry

    lax.fori_loop(0, tc // blk, block, 0, unroll=2 if tc // blk > 1 else 1)
    s_out[...] = st_s[...]


def _gla_call(h3, s0, gp):
    n, t, _ = h3.shape
    tc = min(t, 128)
    wk = GLA_HEADS * GLA_DK
    wv = GLA_HEADS * GLA_DV
    const = lambda shape: pl.BlockSpec(shape, lambda j: (0,) * len(shape))
    state = const((n, wk, GLA_DV))
    return pl.pallas_call(
        functools.partial(_gla_kernel, n=n, tc=tc),
        grid=(t // tc,),
        in_specs=[pl.BlockSpec((n, tc, 2 * wk), lambda j: (0, j, PROJ_OFF['d_q'] // (2 * wk))),
                  pl.BlockSpec((n, tc, wv), lambda j: (0, j, PROJ_OFF['d_v'] // wv)),
                  pl.BlockSpec((n, tc, LANE), lambda j: (0, j, OFF_MISC // LANE)),
                  state,
                  const((LANE, wk)), const((1, wk)), const((1, GLA_DV)),
                  const((wk, wv)), const((GLA_BLOCK, GLA_BLOCK * GLA_BLOCK))],
        out_specs=[pl.BlockSpec((n, tc, wv), lambda j: (0, j, 0)), state],
        out_shape=[jax.ShapeDtypeStruct((n, t, wv), F32),
                   jax.ShapeDtypeStruct((n, wk, GLA_DV), F32)],
        scratch_shapes=[pltpu.VMEM((n, wk, GLA_DV), F32)],
        compiler_params=_params("arbitrary"),
        name="gla_branch",
    )(h3, h3, h3, s0, gp['wg'], gp['bg'], gp['norm'], gp['hexp'], gp['gsum'])


def _gla_prepare(w_g2, b_g, norm):
    wk = GLA_HEADS * GLA_DK
    wg = jnp.zeros((LANE, wk), F32).at[MISC_G:MISC_G + GLA_GATE_RANK].set(w_g2).astype(BF16)
    ck = jnp.arange(wk)[:, None] // GLA_DK
    cv = jnp.arange(GLA_HEADS * GLA_DV)[None, :] // GLA_DV
    ij = jnp.arange(GLA_BLOCK * GLA_BLOCK)[None, :] // GLA_BLOCK
    return {'wg': wg, 'bg': b_g.reshape(1, wk).astype(F32),
            'norm': norm.reshape(1, GLA_DV).astype(F32),
            'hexp': (ck == cv).astype(BF16),
            'gsum': (jnp.arange(GLA_BLOCK)[:, None] == ij).astype(BF16)}


def _merge_kernel(x_ref, ya_ref, yb_ref, yc_ref, yd_ref, ga_ref, gb_ref, gc_ref, gd_ref, mg_ref,
                  wb_ref, wo_ref, lng_ref, lnb_ref, o_ref):
    mixed = None
    for br, (y_ref, g_ref) in enumerate(((ya_ref, ga_ref), (yb_ref, gb_ref), (yc_ref, gc_ref), (yd_ref, gd_ref))):
        act = (y_ref[...] * jax.nn.silu(g_ref[...])).astype(BF16)
        term = jax.nn.sigmoid(mg_ref[:, br * D_MODEL:(br + 1) * D_MODEL]) * _dot(act, wb_ref[br])
        mixed = term if mixed is None else mixed + term
    z = DN_ALPHA * x_ref[...] + _dot(mixed.astype(BF16), wo_ref[...])
    mu = jnp.mean(z, axis=1, keepdims=True)
    zc = z - mu
    var = jnp.mean(zc * zc, axis=1, keepdims=True)
    o_ref[...] = zc * lax.rsqrt(var + LN_EPS) * lng_ref[...] + lnb_ref[...]


def _merge_call(x, h, ya, yb, yc, yd, wb, wo, lng, lnb):
    rows = x.shape[0]
    tm = min(rows, 256)
    bw = BRANCH_W
    ybs = pl.BlockSpec((tm, bw), lambda i: (i, 0))
    gate = lambda name: pl.BlockSpec((tm, bw), lambda i: (i, PROJ_OFF[name] // bw))
    return pl.pallas_call(
        _merge_kernel,
        grid=(rows // tm,),
        in_specs=[pl.BlockSpec((tm, D_MODEL), lambda i: (i, 0)), ybs, ybs, ybs, ybs,
                  gate('a_gate'), gate('b_gate'), gate('c_gate'), gate('d_gate'),
                  pl.BlockSpec((tm, N_BRANCH * D_MODEL), lambda i: (i, 0)),
                  pl.BlockSpec((N_BRANCH, bw, D_MODEL), lambda i: (0, 0, 0)),
                  pl.BlockSpec((D_MODEL, D_MODEL), lambda i: (0, 0)),
                  pl.BlockSpec((1, D_MODEL), lambda i: (0, 0)),
                  pl.BlockSpec((1, D_MODEL), lambda i: (0, 0))],
        out_specs=pl.BlockSpec((tm, D_MODEL), lambda i: (i, 0)),
        out_shape=jax.ShapeDtypeStruct((rows, D_MODEL), F32),
        compiler_params=_params("parallel"),
        name="merge_norm",
    )(x, ya, yb, yc, yd, h, h, h, h, h, wb, wo, lng, lnb)


def _split_w_in(w_in):
    out, off = {}, 0
    for name, width in IN_LAYOUT:
        out[name] = w_in[:, off:off + width]
        off += width
    return out


def _regroup_kernel(w_ref, proj_ref, key_ref):
    cols = _split_w_in(w_ref[...])
    pad = jnp.zeros((w_ref.shape[0], MISC_PAD), F32)
    proj_ref[...] = jnp.concatenate([cols[name] for name in PROJ_ORDER] + [pad], axis=1).astype(BF16)
    k, v, ki = cols['b_k'], cols['b_v'], cols['b_ki']
    dh = DSA_HEAD_DIM
    key_ref[...] = jnp.concatenate([k[:, :dh], k[:, :dh], k[:, dh:], k[:, dh:],
                                    v[:, :dh], v[:, :dh], v[:, dh:], v[:, dh:], ki, ki], axis=1).astype(BF16)


def _regroup_weights(w_in):
    depth, rows, width = w_in.shape
    tr = 128
    return pl.pallas_call(
        _regroup_kernel,
        grid=(depth, rows // tr),
        in_specs=[pl.BlockSpec((None, tr, width), lambda l, i: (l, i, 0))],
        out_specs=[pl.BlockSpec((None, tr, PROJ_W), lambda l, i: (l, i, 0)),
                   pl.BlockSpec((None, tr, HK_W), lambda l, i: (l, i, 0))],
        out_shape=[jax.ShapeDtypeStruct((depth, rows, PROJ_W), BF16),
                   jax.ShapeDtypeStruct((depth, rows, HK_W), BF16)],
        compiler_params=_params("parallel", "parallel"),
        name="regroup_weights",
    )(w_in)


def _dup_keys(k, v, ki):
    return jnp.concatenate([k[:, :, 0], k[:, :, 0], k[:, :, 1], k[:, :, 1],
                            v[:, :, 0], v[:, :, 0], v[:, :, 1], v[:, :, 1], ki, ki], axis=-1).astype(BF16)


def _layer(x, lw, st):
    n, t, _ = x.shape
    x2 = x.reshape(n * t, D_MODEL)
    h = _matmul(x2, lw['w_proj'], F32, tn=PROJ_W // 9)
    hk_new = _matmul(x2, lw['w_key'], BF16, tn=HK_W).reshape(n, t, HK_W)
    if st is None:
        hk, n_keys, b_off = hk_new, t, 0
        s5r = jnp.zeros((n, 1, S5_LANES), F32)
        s5i = jnp.zeros((n, 1, S5_LANES), F32)
        gdn0 = jnp.zeros((n, GDN_HEADS, GDN_DK, GDN_DV), F32)
        conv0 = jnp.zeros((n, 8, GDN_CONV_CH), F32)
        gla0 = jnp.zeros((n, GLA_HEADS * GLA_DK, GLA_DV), F32)
    else:
        past = st['k'].shape[1]
        n_keys = past + t
        b_off = past // Q_BLOCK
        lp = -(-n_keys // KEY_TILE) * KEY_TILE
        hk = jnp.concatenate([_dup_keys(st['k'], st['v'], st['kidx']), hk_new,
                              jnp.zeros((n, lp - n_keys, HK_W), BF16)], axis=1)
        s5r = st['s5_re'].reshape(n, 1, S5_LANES)
        s5i = st['s5_im'].reshape(n, 1, S5_LANES)
        gdn0 = st['gdn']
        conv0 = jnp.concatenate([jnp.zeros((n, 8 - (CONV_W - 1), GDN_CONV_CH), F32), st['gdn_conv']], axis=1)
        gla0 = st['gla'].reshape(n, GLA_HEADS * GLA_DK, GLA_DV)
    topk = min(DSA_TOPK_MAX, n_keys // 4)

    ya, s5r_new, s5i_new = _s5_call(h, n, t, s5r, s5i, lw['s5'])
    yb = _dsa_call(h, hk, n, t, lw['bias'], lw['tril'], b_off=b_off, n_keys=n_keys, topk=topk)
    h3 = h.reshape(n, t, PROJ_W)
    yc, gdn_new, conv_new = _gdn_call(h3, gdn0, conv0, lw['gdn'])
    yd, gla_new = _gla_call(h3, gla0, lw['gla'])
    yc = yc.reshape(n * t, GDN_VW)
    yd = yd.reshape(n * t, GLA_HEADS * GLA_DV)
    y = _merge_call(x2, h, ya, yb, yc, yd, lw['w_branch'], lw['w_out'], lw['ln_g'], lw['ln_b'])

    def cols(name, width):
        return h[:, PROJ_OFF[name]:PROJ_OFF[name] + width].reshape(n, t, width)

    new = {'k': cols('b_k', DSA_KV_HEADS * DSA_HEAD_DIM).reshape(n, t, DSA_KV_HEADS, DSA_HEAD_DIM),
           'v': cols('b_v', DSA_KV_HEADS * DSA_HEAD_DIM).reshape(n, t, DSA_KV_HEADS, DSA_HEAD_DIM),
           'kidx': cols('b_ki', IDX_DIM),
           's5_re': s5r_new.reshape(n, S5_GROUPS, S5_STATE), 's5_im': s5i_new.reshape(n, S5_GROUPS, S5_STATE),
           'gdn': gdn_new, 'gdn_conv': conv_new[:, 8 - (CONV_W - 1):],
           'gla': gla_new.reshape(n, GLA_HEADS, GLA_DK, GLA_DV)}
    return y.reshape(n, t, D_MODEL), new


def kernel(x_prompt, x_sample, cache_k, cache_v, cache_kidx, state_s5_re, state_s5_im, state_gdn, state_gdn_conv, state_gla, w_in, w_branch, w_out, ln_g, ln_b, rel_bias, s5_a_re, s5_a_im, s5_log_dt, s5_b_re, s5_b_im, s5_c_re, s5_c_im, s5_d, s5_w_glu, gdn_conv, gdn_a_log, gdn_dt_bias, gdn_norm, gla_w_g2, gla_b_g, gla_norm):
    bias = _dsa_bias_tables(rel_bias)
    tril = (jnp.arange(KEY_BLOCK)[:, None] >= jnp.arange(KEY_BLOCK)[None, :]).astype(BF16)
    yp, ys = x_prompt, x_sample
    new_p, new_s = [], []
    w_proj, w_key = _regroup_weights(w_in)
    for l in range(DEPTH):
        lw = {'w_proj': w_proj[l], 'w_key': w_key[l], 'bias': bias, 'tril': tril,
              'w_branch': w_branch[l].astype(BF16), 'w_out': w_out[l].astype(BF16),
              'ln_g': ln_g[l].reshape(1, D_MODEL), 'ln_b': ln_b[l].reshape(1, D_MODEL),
              's5': _s5_prepare(s5_a_re[l], s5_a_im[l], s5_log_dt[l], s5_b_re[l], s5_b_im[l],
                                s5_c_re[l], s5_c_im[l], s5_d[l], s5_w_glu[l]),
              'gdn': {'convw': gdn_conv[l], 'alog': _lane_place(gdn_a_log[l], MISC_A),
                      'dtb': _lane_place(gdn_dt_bias[l], MISC_A), 'norm': gdn_norm[l].reshape(1, GDN_DV)},
              'gla': _gla_prepare(gla_w_g2[l], gla_b_g[l], gla_norm[l])}
        st = {'k': cache_k[l], 'v': cache_v[l], 'kidx': cache_kidx[l], 's5_re': state_s5_re[l],
              's5_im': state_s5_im[l], 'gdn': state_gdn[l], 'gdn_conv': state_gdn_conv[l], 'gla': state_gla[l]}
        yp, stp = _layer(yp, lw, None)
        ys, sts = _layer(ys, lw, st)
        new_p.append(stp)
        new_s.append(sts)
    names = ('k', 'v', 'kidx', 's5_re', 's5_im', 'gdn', 'gdn_conv', 'gla')
    stack = lambda states, name: jnp.stack([s[name] for s in states], axis=0)
    return ((yp, ys) + tuple(stack(new_p, nm) for nm in names) + tuple(stack(new_s, nm) for nm in names))
```

```python
import functools
import math

import jax
import jax.numpy as jnp
from jax import lax
from jax.experimental import pallas as pl
from jax.experimental.pallas import tpu as pltpu

F32 = jnp.float32
BF16 = jnp.bfloat16
HIGHEST = lax.Precision.HIGHEST

D_MODEL = 1024
DEPTH = 2
CHUNK = 64
N_BRANCH = 4
BRANCH_W = 512
S5_GROUP = 16
S5_GROUPS = BRANCH_W // S5_GROUP
S5_STATE = 64
S5_LANES = S5_GROUPS * S5_STATE
S5_SPLIT = 2
DSA_HEADS = 8
DSA_KV_HEADS = 2
DSA_HEAD_DIM = 64
IDX_HEADS = 4
IDX_DIM = 64
DSA_TOPK_MAX = 256
REL_BUCKETS = 32
REL_MAX_DIST = 128
GDN_HEADS = 4
GDN_DK = 128
GDN_DV = 128
GDN_QK = GDN_HEADS * GDN_DK
GDN_VW = GDN_HEADS * GDN_DV
CONV_W = 4
GDN_CONV_CH = 2 * GDN_QK + GDN_VW
GLA_HEADS = 4
GLA_DK = 64
GLA_DV = 128
GLA_GATE_RANK = 16
GLA_TAU = 16.0
GLA_BLOCK = 16
LN_EPS = 1e-5
RMS_EPS = 1e-6
DN_ALPHA = (2 * DEPTH) ** 0.25
LOG2E = math.log2(math.e)

LANE = 128
KEY_BLOCK = 128
KEY_SUB = 4
KEY_TILE = KEY_SUB * KEY_BLOCK
SOFTMAX_CEIL = 2.0 ** 120
SOFTMAX_FLOOR = 2.0 ** -60
FOLD_ROWS = 64
Q_BLOCK = 128
VMEM_LIMIT = 56 * 1024 * 1024

IN_LAYOUT = (
    ('a_u', BRANCH_W), ('a_gate', BRANCH_W),
    ('b_q', DSA_HEADS * DSA_HEAD_DIM), ('b_k', DSA_KV_HEADS * DSA_HEAD_DIM),
    ('b_v', DSA_KV_HEADS * DSA_HEAD_DIM), ('b_qi', IDX_HEADS * IDX_DIM), ('b_ki', IDX_DIM),
    ('b_wi', IDX_HEADS), ('b_gate', BRANCH_W),
    ('c_qkv', GDN_CONV_CH), ('c_beta', GDN_HEADS), ('c_a', GDN_HEADS), ('c_gate', BRANCH_W),
    ('d_q', GLA_HEADS * GLA_DK), ('d_k', GLA_HEADS * GLA_DK), ('d_v', GLA_HEADS * GLA_DV),
    ('d_g', GLA_GATE_RANK), ('d_gate', BRANCH_W),
    ('merge', N_BRANCH * D_MODEL),
)

PROJ_ORDER = ('merge', 'a_u', 'a_gate', 'b_q', 'b_qi', 'b_k', 'b_v', 'c_qkv', 'b_gate', 'c_gate',
              'd_v', 'd_gate', 'd_q', 'd_k', 'b_ki', 'b_wi', 'c_beta', 'c_a', 'd_g')
MISC_WI = IDX_DIM
MISC_BETA = MISC_WI + IDX_HEADS
MISC_A = MISC_BETA + GDN_HEADS
MISC_G = MISC_A + GDN_HEADS
MISC_PAD = LANE - (MISC_G + GLA_GATE_RANK)


def _proj_offsets():
    widths = dict(IN_LAYOUT)
    off, out = 0, {}
    for name in PROJ_ORDER:
        out[name] = off
        off += widths[name]
    return out, off + MISC_PAD


PROJ_OFF, PROJ_W = _proj_offsets()
OFF_MISC = PROJ_OFF['b_ki']
HK_W = 5 * LANE
HK_V = 2 * LANE
HK_KI = 4 * LANE


def _dot(a, b, precision=None):
    return jnp.dot(a, b, preferred_element_type=F32, precision=precision)


def _dot_nt(a, b, precision=None):
    return lax.dot_general(a, b, (((1,), (1,)), ((), ())), preferred_element_type=F32, precision=precision)


def _dot_tn(a, b, precision=None):
    return lax.dot_general(a, b, (((0,), (0,)), ((), ())), preferred_element_type=F32, precision=precision)


def _split_bf16(x):
    hi = x.astype(BF16)
    return hi, (x - hi.astype(F32)).astype(BF16)


def _dot3(a, b):
    a_hi, a_lo = _split_bf16(a)
    b_hi, b_lo = _split_bf16(b)
    return _dot(a_hi, b_hi) + (_dot(a_hi, b_lo) + _dot(a_lo, b_hi))


def _params(*sem):
    return pltpu.CompilerParams(dimension_semantics=sem, vmem_limit_bytes=VMEM_LIMIT)


def _softplus(x):
    return jnp.maximum(x, 0.0) + jnp.log1p(jnp.exp(-jnp.abs(x)))


def _iota(shape, dim):
    return lax.broadcasted_iota(jnp.int32, shape, dim)


def _mm_kernel(x_ref, w_ref, o_ref):
    o_ref[...] = _dot(x_ref[...].astype(BF16), w_ref[...]).astype(o_ref.dtype)


def _matmul(x, w, out_dtype, tn):
    rows, kdim = x.shape
    width = w.shape[1]
    tm = min(rows, 1024)
    return pl.pallas_call(
        _mm_kernel,
        grid=(rows // tm, width // tn),
        in_specs=[pl.BlockSpec((tm, kdim), lambda i, j: (i, 0)),
                  pl.BlockSpec((kdim, tn), lambda i, j: (0, j))],
        out_specs=pl.BlockSpec((tm, tn), lambda i, j: (i, j)),
        out_shape=jax.ShapeDtypeStruct((rows, width), out_dtype),
        compiler_params=_params("parallel", "parallel"),
        name="proj_matmul",
    )(x, w)


def _s5_kernel(u_ref, h0r_ref, h0i_ref, ar_ref, ai_ref, bb_ref, cc_ref, d_ref, wglu_ref,
               ya_ref, hr_out, hi_out, xr_s, xi_s, hr_s, hi_s, cr_s, ci_s, *, tc):
    @pl.when(pl.program_id(1) == 0)
    def _():
        cr_s[...] = h0r_ref[...]
        ci_s[...] = h0i_ref[...]

    u = u_ref[...]
    ub = u.astype(BF16)
    uw = BRANCH_W // S5_SPLIT
    sw = S5_LANES // S5_SPLIT
    for part in range(S5_SPLIT):
        x = _dot(ub[:, part * uw:(part + 1) * uw], bb_ref[part])
        xr_s[:, part * sw:(part + 1) * sw] = x[:, :sw]
        xi_s[:, part * sw:(part + 1) * sw] = x[:, sw:]

    ar = ar_ref[...]
    ai = ai_ref[...]

    def step(t, carry):
        hr, hi = carry
        nhr = ar * hr - ai * hi + xr_s[pl.ds(t, 1), :]
        nhi = ar * hi + ai * hr + xi_s[pl.ds(t, 1), :]
        hr_s[pl.ds(t, 1), :] = nhr
        hi_s[pl.ds(t, 1), :] = nhi
        return nhr, nhi

    hr, hi = lax.fori_loop(0, tc, step, (cr_s[...], ci_s[...]), unroll=8)
    cr_s[...] = hr
    ci_s[...] = hi
    hr_out[...] = hr
    hi_out[...] = hi
    ys = []
    for part in range(S5_SPLIT):
        hcat = jnp.concatenate([hr_s[:, part * sw:(part + 1) * sw], hi_s[:, part * sw:(part + 1) * sw]], axis=1)
        ys.append(_dot(hcat.astype(BF16), cc_ref[part]))
    y = jnp.concatenate(ys, axis=1) + d_ref[...] * u
    y = jax.nn.gelu(y)
    g = _dot(y.astype(BF16), wglu_ref[...])
    ya_ref[...] = g[:, :BRANCH_W] * jax.nn.sigmoid(g[:, BRANCH_W:])


def _s5_call(h, n, t, h0r, h0i, sp):
    tc = min(t, 256)
    nc = t // tc
    uw = BRANCH_W // S5_SPLIT
    sw = S5_LANES // S5_SPLIT
    const = lambda shape: pl.BlockSpec(shape, lambda i, c: (0,) * len(shape))
    state = pl.BlockSpec((None, 1, S5_LANES), lambda i, c: (i, 0, 0))
    return pl.pallas_call(
        functools.partial(_s5_kernel, tc=tc),
        grid=(n, nc),
        in_specs=[pl.BlockSpec((tc, BRANCH_W), lambda i, c: (i * nc + c, PROJ_OFF['a_u'] // BRANCH_W)),
                  state, state,
                  const((1, S5_LANES)), const((1, S5_LANES)),
                  const((S5_SPLIT, uw, 2 * sw)), const((S5_SPLIT, 2 * sw, uw)),
                  const((1, BRANCH_W)), const((BRANCH_W, 2 * BRANCH_W))],
        out_specs=[pl.BlockSpec((tc, BRANCH_W), lambda i, c: (i * nc + c, 0)), state, state],
        out_shape=[jax.ShapeDtypeStruct((n * t, BRANCH_W), F32),
                   jax.ShapeDtypeStruct((n, 1, S5_LANES), F32),
                   jax.ShapeDtypeStruct((n, 1, S5_LANES), F32)],
        scratch_shapes=[pltpu.VMEM((tc, S5_LANES), F32), pltpu.VMEM((tc, S5_LANES), F32),
                        pltpu.VMEM((tc, S5_LANES), F32), pltpu.VMEM((tc, S5_LANES), F32),
                        pltpu.VMEM((1, S5_LANES), F32), pltpu.VMEM((1, S5_LANES), F32)],
        compiler_params=_params("parallel", "arbitrary"),
        name="s5_branch",
    )(h, h0r, h0i, sp['ar'], sp['ai'], sp['bb'], sp['cc'], sp['d'], sp['wglu'])


def _s5_prepare(a_re, a_im, log_dt, b_re, b_im, c_re, c_im, d, w_glu):
    lam_r = jnp.minimum(a_re, -1e-4)
    lam_i = a_im
    dt = jnp.exp(log_dt)[:, None]
    mag = jnp.exp(lam_r * dt)
    ar = mag * jnp.cos(lam_i * dt)
    ai = mag * jnp.sin(lam_i * dt)
    den = lam_r * lam_r + lam_i * lam_i
    nr = ar - 1.0
    cr = (nr * lam_r + ai * lam_i) / den
    ci = (ai * lam_r - nr * lam_i) / den
    bb_r = cr[..., None] * b_re - ci[..., None] * b_im
    bb_i = cr[..., None] * b_im + ci[..., None] * b_re
    eye = jnp.eye(S5_GROUPS, dtype=F32)

    def blockdiag_in(bb):
        m = eye[:, None, :, None] * jnp.swapaxes(bb, 1, 2)[:, :, None, :]
        return m.reshape(BRANCH_W, S5_LANES)

    def blockdiag_out(c):
        m = eye[:, None, :, None] * jnp.swapaxes(c, 1, 2)[:, :, None, :]
        return m.reshape(S5_LANES, BRANCH_W)

    uw = BRANCH_W // S5_SPLIT
    sw = S5_LANES // S5_SPLIT
    bbr, bbi = blockdiag_in(bb_r), blockdiag_in(bb_i)
    ccr, cci = blockdiag_out(c_re), blockdiag_out(c_im)
    bb = jnp.stack([jnp.concatenate([m[p * uw:(p + 1) * uw, p * sw:(p + 1) * sw] for m in (bbr, bbi)], axis=1)
                    for p in range(S5_SPLIT)]).astype(BF16)
    cc = jnp.stack([jnp.concatenate([ccr[p * sw:(p + 1) * sw, p * uw:(p + 1) * uw],
                                     -cci[p * sw:(p + 1) * sw, p * uw:(p + 1) * uw]], axis=0)
                    for p in range(S5_SPLIT)]).astype(BF16)
    return {'ar': ar.reshape(1, S5_LANES), 'ai': ai.reshape(1, S5_LANES), 'bb': bb, 'cc': cc,
            'd': d.reshape(1, BRANCH_W), 'wglu': w_glu.astype(BF16)}


def _sortable(x):
    bits = lax.bitcast_convert_type(x, jnp.int32)
    key = bits ^ ((bits >> 31) & jnp.int32(0x7FFFFFFF))
    return jnp.where(key == -1, 0, key)


def _dsa_kernel(q_ref, qi_ref, misc_ref, hk_ref, bias_ref, tril_ref, o_ref, sc_s, pk_s, acc_s, far_s, *, qb, b_off, n_keys, topk):
    b = pl.program_id(1) + b_off
    nkt = (b + KEY_SUB) // KEY_SUB
    fold = KEY_TILE // FOLD_ROWS
    key_row = _iota((KEY_TILE, qb), 0)
    qpos = b * Q_BLOCK + _iota((1, qb), 1)
    limit = jnp.minimum(((qpos >> 6) + 1) * CHUNK, n_keys)
    low_half = _iota((qb, LANE), 1) < DSA_HEAD_DIM

    def key_rows(ref_cols, j):
        return hk_ref[pl.ds(pl.multiple_of(j * KEY_TILE, KEY_TILE), KEY_TILE), ref_cols:ref_cols + LANE]

    def half_masked(x, hd):
        pair = x[:, (hd // 2) * LANE:(hd // 2 + 1) * LANE]
        return jnp.where(low_half == (hd % 2 == 0), pair, 0.0).astype(BF16)

    misc = misc_ref[...]
    pick = jnp.where(_iota((8, LANE), 1) == MISC_WI + _iota((8, LANE), 0), 1.0, 0.0)
    wi_t = _dot_nt(pick, misc, precision=HIGHEST) * (IDX_DIM ** -0.5 * IDX_HEADS ** -0.5)
    qi = qi_ref[...]
    qi_stack = jnp.concatenate([half_masked(qi, hd) for hd in range(IDX_HEADS)], axis=0)

    def pack16(v):
        half = KEY_TILE // 2
        return (v[0:half] & 0xFFFF) | (v[half:] << 16)

    def score_block(j, carry):
        z = _dot_nt(key_rows(HK_KI, j), qi_stack)
        sc = jnp.zeros((KEY_TILE, qb), F32)
        for hd in range(IDX_HEADS):
            sc = sc + jnp.maximum(z[:, hd * qb:(hd + 1) * qb], 0.0) * wi_t[hd:hd + 1, :]
        sc = jnp.where(j * KEY_TILE + key_row < limit, sc, -jnp.inf)
        key = _sortable(sc)
        sc_s[j] = key
        pk_s[j] = pack16(key >> 16)
        return carry

    lax.fori_loop(0, nkt, score_block, 0)

    def count_ge(cand):
        def body(j, acc):
            return acc + jnp.where(sc_s[j] >= cand, 1.0, 0.0).reshape(fold, FOLD_ROWS, qb).sum(axis=0)
        acc = lax.fori_loop(0, nkt, body, jnp.zeros((FOLD_ROWS, qb), F32))
        return jnp.sum(acc, axis=0, keepdims=True)

    one16 = jnp.ones((KEY_TILE, qb), jnp.int16)
    zero16 = jnp.zeros((KEY_TILE, qb), jnp.int16)

    def count16_ge(cand):
        word = jnp.broadcast_to((cand & 0xFFFF) | (cand << 16), (8, qb))
        c16 = pltpu.bitcast(word, jnp.int16)[0:1, :]

        def body(j, acc):
            ind = jnp.where(pltpu.bitcast(pk_s[j], jnp.int16) >= c16, one16, zero16)
            for i in range(fold):
                acc = acc + ind[i * FOLD_ROWS:(i + 1) * FOLD_ROWS]
            return acc
        acc = lax.fori_loop(0, nkt, body, jnp.zeros((FOLD_ROWS, qb), jnp.int16))
        return jnp.sum(acc.astype(F32), axis=0, keepdims=True)

    def radix_search(start, want):
        def bit_step(i, res):
            cand = res + (jnp.int32(1) << (14 - i))
            return jnp.where(count16_ge(cand) >= want, cand, res)
        return lax.fori_loop(0, 15, bit_step, start)

    kf = float(topk)
    top = radix_search(jnp.where(count16_ge(jnp.zeros((1, qb), jnp.int32)) >= kf, 0, -32768), kf)
    int16_max = jnp.iinfo(jnp.int16).max
    above = jnp.where(top == int16_max, 0.0, count16_ge(jnp.minimum(top + 1, int16_max)))
    want_low = kf - above

    def pack_low(j, carry):
        key = sc_s[j]
        pk_s[j] = pack16(jnp.where((key >> 16) == top, (key >> 1) & 0x7FFF, -1))
        return carry

    lax.fori_loop(0, nkt, pack_low, 0)
    low = radix_search(jnp.zeros((1, qb), jnp.int32), want_low)
    base = (top << 16) | (low << 1)
    thr = jnp.where(count_ge(base | 1) >= kf, base | 1, base)
    need = kf - count_ge(thr + 1)

    tril = tril_ref[...]

    def select_block(j, ties):
        keys = sc_s[j]
        eq = keys == thr
        eq_b = jnp.where(eq, 1.0, 0.0).astype(BF16)
        parts, carry = [], ties
        for s in range(KEY_SUB):
            part = _dot(tril, eq_b[s * KEY_BLOCK:(s + 1) * KEY_BLOCK]) + carry
            parts.append(part)
            carry = part[KEY_BLOCK - 1:KEY_BLOCK, :]
        prefix = jnp.concatenate(parts, axis=0)
        sel = ((keys > thr) | (eq & (prefix <= need))) & (j * KEY_TILE + key_row < limit)
        sc_s[j] = lax.bitcast_convert_type(jnp.where(sel, 0.0, -jnp.inf), jnp.int32)
        return prefix[KEY_TILE - 1:KEY_TILE, :]

    lax.fori_loop(0, nkt, select_block, jnp.zeros((1, qb), F32))

    q = q_ref[...] * (DSA_HEAD_DIM ** -0.5 * LOG2E)
    rep = DSA_HEADS // DSA_KV_HEADS
    eye = jnp.where(_iota((LANE, LANE), 0) == _iota((LANE, LANE), 1), 1.0, 0.0)
    value_lane = _iota((KEY_TILE, LANE), 1) < DSA_HEAD_DIM
    n_pairs = DSA_HEADS // 2
    q_pair = [jnp.concatenate([half_masked(q, 2 * c), half_masked(q, 2 * c + 1)], axis=0) for c in range(n_pairs)]
    far_bias = [bias_ref[hd, 2, 0:1, 0:qb] for hd in range(DSA_HEADS)]
    n_far = jnp.maximum((b - 1) // KEY_SUB, 0)

    def tile_operands(j):
        mask = lax.bitcast_convert_type(sc_s[j], F32)
        kd = [key_rows(grp * LANE, j) for grp in range(DSA_KV_HEADS)]
        zs = [_dot_nt(kd[(2 * c) // rep], q_pair[c]) for c in range(n_pairs)]
        vd = [jnp.where(value_lane, key_rows(HK_V + grp * LANE, j), 1.0).astype(BF16)
              for grp in range(DSA_KV_HEADS)]
        return mask, zs, vd

    def near_bias(hd, j):
        bias = [bias_ref[hd, jnp.clip(b - (KEY_SUB * j + s), 0, 2), :, 0:qb] for s in range(KEY_SUB)]
        return jnp.concatenate(bias, axis=0)

    def plain_tile(j, carry, far):
        mask, zs, vd = tile_operands(j)
        ps = []
        for hd in range(DSA_HEADS):
            r = hd % 2
            zh = zs[hd // 2][:, r * qb:(r + 1) * qb] + mask
            if not far:
                zh = zh + near_bias(hd, j)
            ps.append(jnp.exp2(zh).astype(BF16))
        acc = far_s if far else acc_s
        for c in range(n_pairs):
            acc[c] += _dot_tn(vd[(2 * c) // rep], jnp.concatenate(ps[2 * c:2 * c + 2], axis=1))
        return carry

    acc_s[...] = jnp.zeros_like(acc_s)
    far_s[...] = jnp.zeros_like(far_s)
    lax.fori_loop(0, n_far, lambda j, c: plain_tile(j, c, True), 0)
    lax.fori_loop(n_far, nkt, lambda j, c: plain_tile(j, c, False), 0)
    healthy = jnp.ones((1, 1), F32)
    for c in range(n_pairs):
        scale = jnp.concatenate([jnp.exp2(far_bias[2 * c]), jnp.exp2(far_bias[2 * c + 1])], axis=1)
        total = far_s[c] * scale + acc_s[c]
        acc_s[c] = total
        norm = total[DSA_HEAD_DIM:DSA_HEAD_DIM + 1, :]
        ok = (jnp.abs(total) < SOFTMAX_CEIL) & (norm > SOFTMAX_FLOOR)
        healthy = jnp.minimum(healthy, jnp.min(jnp.where(ok, 1.0, 0.0), keepdims=True))

    def attend(j, ms, far):
        mask, zs, vd = tile_operands(j)
        new_ms, ps, alphas = [], [], []
        for hd in range(DSA_HEADS):
            r = hd % 2
            zh = zs[hd // 2][:, r * qb:(r + 1) * qb] + mask
            if not far:
                zh = zh + near_bias(hd, j)
            tile_max = jnp.max(zh.reshape(fold, FOLD_ROWS, qb).max(axis=0), axis=0, keepdims=True)
            shift = far_bias[hd] if far else 0.0
            m_new = jnp.maximum(ms[hd], tile_max + shift)
            ps.append(jnp.exp2(zh - (m_new - shift)).astype(BF16))
            alphas.append(jnp.exp2(ms[hd] - m_new))
            new_ms.append(m_new)
        for c in range(n_pairs):
            acc_s[c] = (acc_s[c] * jnp.concatenate(alphas[2 * c:2 * c + 2], axis=1)
                        + _dot_tn(vd[(2 * c) // rep], jnp.concatenate(ps[2 * c:2 * c + 2], axis=1)))
        return tuple(new_ms)

    @pl.when(healthy[0, 0] < 0.5)
    def _():
        acc_s[...] = jnp.zeros_like(acc_s)
        ms = tuple(jnp.full((1, qb), -1e30, F32) for _ in range(DSA_HEADS))
        ms = lax.fori_loop(0, n_far, lambda j, m: attend(j, m, True), ms)
        lax.fori_loop(n_far, nkt, lambda j, m: attend(j, m, False), ms)

    for c in range(n_pairs):
        halves = []
        for r in range(2):
            blk = acc_s[c, :, r * qb:(r + 1) * qb]
            halves.append(blk[0:DSA_HEAD_DIM] / blk[DSA_HEAD_DIM:DSA_HEAD_DIM + 1])
        o_t = jnp.concatenate(halves, axis=0)
        o_ref[:, c * LANE:(c + 1) * LANE] = _dot_tn(o_t, eye, precision=HIGHEST)


def _dsa_call(h, hk, n, t, bias, tril, *, b_off, n_keys, topk):
    qb = min(t, Q_BLOCK)
    nqb = t // qb
    lp = hk.shape[1]
    wq = DSA_HEADS * DSA_HEAD_DIM
    wqi = IDX_HEADS * IDX_DIM
    return pl.pallas_call(
        functools.partial(_dsa_kernel, qb=qb, b_off=b_off, n_keys=n_keys, topk=topk),
        grid=(n, nqb),
        in_specs=[pl.BlockSpec((qb, wq), lambda i, c: (i * nqb + c, PROJ_OFF['b_q'] // wq)),
                  pl.BlockSpec((qb, wqi), lambda i, c: (i * nqb + c, PROJ_OFF['b_qi'] // wqi)),
                  pl.BlockSpec((qb, LANE), lambda i, c: (i * nqb + c, OFF_MISC // LANE)),
                  pl.BlockSpec((None, lp, HK_W), lambda i, c: (i, 0, 0)),
                  pl.BlockSpec((DSA_HEADS, 3, KEY_BLOCK, Q_BLOCK), lambda i, c: (0, 0, 0, 0)),
                  pl.BlockSpec((KEY_BLOCK, KEY_BLOCK), lambda i, c: (0, 0))],
        out_specs=pl.BlockSpec((qb, wq), lambda i, c: (i * nqb + c, 0)),
        out_shape=jax.ShapeDtypeStruct((n * t, wq), F32),
        scratch_shapes=[pltpu.VMEM((lp // KEY_TILE, KEY_TILE, qb), jnp.int32),
                        pltpu.VMEM((lp // KEY_TILE, KEY_TILE // 2, qb), jnp.int32),
                        pltpu.VMEM((DSA_HEADS // 2, LANE, 2 * qb), F32),
                        pltpu.VMEM((DSA_HEADS // 2, LANE, 2 * qb), F32)],
        compiler_params=_params("parallel", "arbitrary"),
        name="dsa_branch",
    )(h, h, h, hk, bias, tril)


def _t5_bucket(rel):
    nb = REL_BUCKETS // 2
    max_exact = nb // 2
    ret = jnp.where(rel > 0, nb, 0)
    dist = jnp.abs(rel)
    distf = jnp.maximum(dist, 1).astype(F32)
    large = max_exact + (jnp.log(distf / max_exact) / math.log(REL_MAX_DIST / max_exact)
                         * (nb - max_exact)).astype(jnp.int32)
    large = jnp.minimum(large, nb - 1)
    return ret + jnp.where(dist < max_exact, dist, large)


def _dsa_bias_tables(rel_bias):
    lc = jnp.arange(KEY_BLOCK, dtype=jnp.int32)[:, None]
    qr = jnp.arange(Q_BLOCK, dtype=jnp.int32)[None, :]
    rel = jnp.stack([lc - qr - back * KEY_BLOCK for back in range(3)], axis=0)
    onehot = jax.nn.one_hot(_t5_bucket(rel), REL_BUCKETS, dtype=F32)
    bias = jnp.einsum('klqb,bh->hklq', onehot, rel_bias.astype(F32), precision=HIGHEST)
    return bias * LOG2E


def _gdn_kernel(raw_ref, misc_ref, s0_ref, conv0_ref, convw_ref, alog_ref, dtb_ref, norm_ref,
                o_ref, s_out, conv_out, xp_s, st_s, *, n, c):
    @pl.when(pl.program_id(0) == 0)
    def _():
        xp_s[:, 0:8, :] = conv0_ref[...]
        st_s[...] = s0_ref[...]

    w = convw_ref[...]
    row = _iota((c, c), 0)
    col = _iota((c, c), 1)
    incl = row >= col
    strict = row > col
    eye = row == col
    tril = jnp.where(incl, 1.0, 0.0)
    eye_f = jnp.where(eye, 1.0, 0.0)
    chains = [(b, hd) for b in range(n) for hd in range(GDN_HEADS)]

    qkv, beta_all, gc_all = [], [], []
    for b in range(n):
        raw = raw_ref[b]
        xp_s[b, 8:8 + c, :] = raw
        y = raw * w[CONV_W - 1:CONV_W, :]
        for i in range(CONV_W - 1):
            y = y + xp_s[b, 5 + i:5 + i + c, :] * w[i:i + 1, :]
        tail = xp_s[b, c:c + 8, :]
        xp_s[b, 0:8, :] = tail
        conv_out[b] = tail
        qkv.append(jax.nn.silu(y))
        misc = misc_ref[b]
        beta_all.append(jax.nn.sigmoid(misc))
        g_all = -jnp.exp(alog_ref[...]) * _softplus(misc + dtb_ref[...])
        gc_all.append(_dot(tril, g_all, precision=HIGHEST))

    qs, ks, vs, betas, gcs, dec_s, dec_i = [], [], [], [], [], [], []
    for b, hd in chains:
        qh = qkv[b][:, hd * GDN_DK:(hd + 1) * GDN_DK]
        kh = qkv[b][:, GDN_QK + hd * GDN_DK:GDN_QK + (hd + 1) * GDN_DK]
        vs.append(qkv[b][:, 2 * GDN_QK + hd * GDN_DV:2 * GDN_QK + (hd + 1) * GDN_DV])
        qs.append(qh * lax.rsqrt(jnp.sum(qh * qh, axis=1, keepdims=True) + RMS_EPS) * (GDN_DK ** -0.5))
        ks.append(kh * lax.rsqrt(jnp.sum(kh * kh, axis=1, keepdims=True) + RMS_EPS))
        betas.append(beta_all[b][:, MISC_BETA + hd:MISC_BETA + hd + 1])
        gc = gc_all[b][:, MISC_A + hd:MISC_A + hd + 1]
        gcs.append(gc)
        gc_b = jnp.broadcast_to(gc, (c, c))
        gc_row = jnp.sum(jnp.where(eye, gc_b, 0.0), axis=0, keepdims=True)
        diff = gc_b - gc_row
        dec = jnp.exp(diff)
        dec_s.append(jnp.where(strict, dec, 0.0))
        dec_i.append(jnp.where(incl, dec, 0.0))
    kb = [k.astype(BF16) for k in ks]
    qb = [q.astype(BF16) for q in qs]
    pw = [-(beta * _dot_nt(k, k) * d) for beta, k, d in zip(betas, kb, dec_s)]
    inv = [eye_f + p for p in pw]
    for _ in range(int(math.log2(c)) - 1):
        pw = [_dot3(p, p) for p in pw]
        inv = [i + _dot3(i, p) for i, p in zip(inv, pw)]
    egc = [jnp.exp(gc) for gc in gcs]
    sol = [_dot3(i, jnp.concatenate([(beta * e) * k, beta * v], axis=1))
           for i, beta, e, k, v in zip(inv, betas, egc, ks, vs)]
    s_old = [st_s[b, hd] for b, hd in chains]
    sb = [s.astype(BF16) for s in s_old]
    vb = [(x[:, GDN_DK:] - _dot(x[:, :GDN_DK].astype(BF16), s)).astype(BF16) for x, s in zip(sol, sb)]
    attn = [(_dot_nt(q, k) * d).astype(BF16) for q, k, d in zip(qb, kb, dec_i)]
    outs = [e * _dot(q, s) + _dot(a, v) for e, q, s, a, v in zip(egc, qb, sb, attn, vb)]
    for (b, hd), gc, k, v, s, o in zip(chains, gcs, ks, vb, s_old, outs):
        g_last = gc[c - 1:c, :]
        st_s[b, hd] = jnp.exp(g_last) * s + _dot_tn((k * jnp.exp(g_last - gc)).astype(BF16), v)
        o = o * lax.rsqrt(jnp.mean(o * o, axis=1, keepdims=True) + RMS_EPS) * norm_ref[...]
        o_ref[b, :, hd * GDN_DV:(hd + 1) * GDN_DV] = o
    s_out[...] = st_s[...]


def _gdn_call(h3, s0, conv0, gp):
    n, t, _ = h3.shape
    c = min(t, CHUNK)
    const = lambda shape: pl.BlockSpec(shape, lambda j: (0,) * len(shape))
    state = const((n, GDN_HEADS, GDN_DK, GDN_DV))
    conv = const((n, 8, GDN_CONV_CH))
    return pl.pallas_call(
        functools.partial(_gdn_kernel, n=n, c=c),
        grid=(t // c,),
        in_specs=[pl.BlockSpec((n, c, GDN_CONV_CH), lambda j: (0, j, PROJ_OFF['c_qkv'] // GDN_CONV_CH)),
                  pl.BlockSpec((n, c, LANE), lambda j: (0, j, OFF_MISC // LANE)),
                  state, conv,
                  const((CONV_W, GDN_CONV_CH)), const((1, LANE)), const((1, LANE)), const((1, GDN_DV))],
        out_specs=[pl.BlockSpec((n, c, GDN_VW), lambda j: (0, j, 0)), state, conv],
        out_shape=[jax.ShapeDtypeStruct((n, t, GDN_VW), F32),
                   jax.ShapeDtypeStruct((n, GDN_HEADS, GDN_DK, GDN_DV), F32),
                   jax.ShapeDtypeStruct((n, 8, GDN_CONV_CH), F32)],
        scratch_shapes=[pltpu.VMEM((n, c + 8, GDN_CONV_CH), F32),
                        pltpu.VMEM((n, GDN_HEADS, GDN_DK, GDN_DV), F32)],
        compiler_params=_params("arbitrary"),
        name="gdn_branch",
    )(h3, h3, s0, conv0, gp['convw'], gp['alog'], gp['dtb'], gp['norm'])


def _lane_place(v, off):
    return jnp.zeros((1, LANE), F32).at[0, off:off + v.shape[0]].set(v.astype(F32))


def _gla_kernel(qk_ref, v_ref, misc_ref, s0_ref, wg_ref, bg_ref, norm_ref, hexp_ref, gsum_ref,
                o_ref, s_out, st_s, *, n, tc):
    blk = GLA_BLOCK
    wk = GLA_HEADS * GLA_DK

    @pl.when(pl.program_id(0) == 0)
    def _():
        st_s[...] = s0_ref[...]

    row = _iota((blk, blk), 0)
    col = _iota((blk, blk), 1)
    tril = jnp.where(row >= col, 1.0, 0.0)
    jrow = _iota((blk, wk), 0)
    lane_k = _iota((blk, wk), 1)
    head_mask = [(lane_k >= hd * GLA_DK) & (lane_k < (hd + 1) * GLA_DK) for hd in range(GLA_HEADS)]
    chains = range(n)

    def block(sb, carry):
        r0 = pl.multiple_of(sb * blk, blk)
        qk = [qk_ref[b, pl.ds(r0, blk), :] for b in chains]
        q = [x[:, :wk] * (GLA_DK ** -0.5) for x in qk]
        k = [x[:, wk:] for x in qk]
        v = [v_ref[b, pl.ds(r0, blk), :] for b in chains]
        mb = [misc_ref[b, pl.ds(r0, blk), :].astype(BF16) for b in chains]
        lg = [jax.nn.log_sigmoid(_dot(m, wg_ref[...]) + bg_ref[...]) * (1.0 / GLA_TAU) for m in mb]
        bc = [_dot(tril, x, precision=HIGHEST) for x in lg]
        dmat = []
        for b in chains:
            slabs = []
            for i in range(blk):
                keep = jrow <= i
                dec = jnp.where(keep, jnp.exp(bc[b][i:i + 1, :] - bc[b]), 0.0)
                slabs.append(dec * q[b][i:i + 1, :] * k[b])
            dmat.append(jnp.concatenate(slabs, axis=0).astype(BF16))
        a_exp = [_dot(d, hexp_ref[...]) for d in dmat]
        prod = [a * jnp.concatenate([x] * blk, axis=0) for a, x in zip(a_exp, v)]
        o = [_dot(gsum_ref[...], p.astype(BF16)) for p in prod]
        qe = [x * jnp.exp(c) for x, c in zip(q, bc)]
        kt = [x * jnp.exp(c[blk - 1:blk, :] - c) for x, c in zip(k, bc)]
        s = [st_s[b] for b in chains]
        sbf = [x.astype(BF16) for x in s]
        for b in chains:
            upd = jnp.zeros_like(s[b])
            parts = []
            for hd in range(GLA_HEADS):
                parts.append(_dot(jnp.where(head_mask[hd], qe[b], 0.0).astype(BF16), sbf[b]))
                vh = v[b][:, hd * GLA_DV:(hd + 1) * GLA_DV].astype(BF16)
                upd = upd + _dot_tn(jnp.where(head_mask[hd], kt[b], 0.0).astype(BF16), vh)
            ob = o[b] + jnp.concatenate(parts, axis=1)
            b_hi, b_lo = _split_bf16(jnp.broadcast_to(bc[b][blk - 1:blk, :], (8, wk)))
            eighth = jnp.full((8, GLA_DV), 0.125, BF16)
            decay = jnp.exp(_dot_tn(b_hi, eighth) + _dot_tn(b_lo, eighth))
            st_s[b] = decay * s[b] + upd
            for hd in range(GLA_HEADS):
                oh = ob[:, hd * GLA_DV:(hd + 1) * GLA_DV]
                oh = oh * lax.rsqrt(jnp.mean(oh * oh, axis=1, keepdims=True) + RMS_EPS) * norm_ref[...]
                o_ref[b, pl.ds(r0, blk), hd * GLA_DV:(hd + 1) * GLA_DV] = oh
        return carry

    lax.fori_loop(0, tc // blk, block, 0, unroll=2 if tc // blk > 1 else 1)
    s_out[...] = st_s[...]


def _gla_call(h3, s0, gp):
    n, t, _ = h3.shape
    tc = min(t, 128)
    wk = GLA_HEADS * GLA_DK
    wv = GLA_HEADS * GLA_DV
    const = lambda shape: pl.BlockSpec(shape, lambda j: (0,) * len(shape))
    state = const((n, wk, GLA_DV))
    return pl.pallas_call(
        functools.partial(_gla_kernel, n=n, tc=tc),
        grid=(t // tc,),
        in_specs=[pl.BlockSpec((n, tc, 2 * wk), lambda j: (0, j, PROJ_OFF['d_q'] // (2 * wk))),
                  pl.BlockSpec((n, tc, wv), lambda j: (0, j, PROJ_OFF['d_v'] // wv)),
                  pl.BlockSpec((n, tc, LANE), lambda j: (0, j, OFF_MISC // LANE)),
                  state,
                  const((LANE, wk)), const((1, wk)), const((1, GLA_DV)),
                  const((wk, wv)), const((GLA_BLOCK, GLA_BLOCK * GLA_BLOCK))],
        out_specs=[pl.BlockSpec((n, tc, wv), lambda j: (0, j, 0)), state],
        out_shape=[jax.ShapeDtypeStruct((n, t, wv), F32),
                   jax.ShapeDtypeStruct((n, wk, GLA_DV), F32)],
        scratch_shapes=[pltpu.VMEM((n, wk, GLA_DV), F32)],
        compiler_params=_params("arbitrary"),
        name="gla_branch",
    )(h3, h3, h3, s0, gp['wg'], gp['bg'], gp['norm'], gp['hexp'], gp['gsum'])


def _gla_prepare(w_g2, b_g, norm):
    wk = GLA_HEADS * GLA_DK
    wg = jnp.zeros((LANE, wk), F32).at[MISC_G:MISC_G + GLA_GATE_RANK].set(w_g2).astype(BF16)
    ck = jnp.arange(wk)[:, None] // GLA_DK
    cv = jnp.arange(GLA_HEADS * GLA_DV)[None, :] // GLA_DV
    ij = jnp.arange(GLA_BLOCK * GLA_BLOCK)[None, :] // GLA_BLOCK
    return {'wg': wg, 'bg': b_g.reshape(1, wk).astype(F32),
            'norm': norm.reshape(1, GLA_DV).astype(F32),
            'hexp': (ck == cv).astype(BF16),
            'gsum': (jnp.arange(GLA_BLOCK)[:, None] == ij).astype(BF16)}


def _merge_kernel(x_ref, ya_ref, yb_ref, yc_ref, yd_ref, ga_ref, gb_ref, gc_ref, gd_ref, mg_ref,
                  wb_ref, wo_ref, lng_ref, lnb_ref, o_ref):
    mixed = None
    for br, (y_ref, g_ref) in enumerate(((ya_ref, ga_ref), (yb_ref, gb_ref), (yc_ref, gc_ref), (yd_ref, gd_ref))):
        act = (y_ref[...] * jax.nn.silu(g_ref[...])).astype(BF16)
        term = jax.nn.sigmoid(mg_ref[:, br * D_MODEL:(br + 1) * D_MODEL]) * _dot(act, wb_ref[br])
        mixed = term if mixed is None else mixed + term
    z = DN_ALPHA * x_ref[...] + _dot(mixed.astype(BF16), wo_ref[...])
    mu = jnp.mean(z, axis=1, keepdims=True)
    zc = z - mu
    var = jnp.mean(zc * zc, axis=1, keepdims=True)
    o_ref[...] = zc * lax.rsqrt(var + LN_EPS) * lng_ref[...] + lnb_ref[...]


def _merge_call(x, h, ya, yb, yc, yd, wb, wo, lng, lnb):
    rows = x.shape[0]
    tm = min(rows, 256)
    bw = BRANCH_W
    ybs = pl.BlockSpec((tm, bw), lambda i: (i, 0))
    gate = lambda name: pl.BlockSpec((tm, bw), lambda i: (i, PROJ_OFF[name] // bw))
    return pl.pallas_call(
        _merge_kernel,
        grid=(rows // tm,),
        in_specs=[pl.BlockSpec((tm, D_MODEL), lambda i: (i, 0)), ybs, ybs, ybs, ybs,
                  gate('a_gate'), gate('b_gate'), gate('c_gate'), gate('d_gate'),
                  pl.BlockSpec((tm, N_BRANCH * D_MODEL), lambda i: (i, 0)),
                  pl.BlockSpec((N_BRANCH, bw, D_MODEL), lambda i: (0, 0, 0)),
                  pl.BlockSpec((D_MODEL, D_MODEL), lambda i: (0, 0)),
                  pl.BlockSpec((1, D_MODEL), lambda i: (0, 0)),
                  pl.BlockSpec((1, D_MODEL), lambda i: (0, 0))],
        out_specs=pl.BlockSpec((tm, D_MODEL), lambda i: (i, 0)),
        out_shape=jax.ShapeDtypeStruct((rows, D_MODEL), F32),
        compiler_params=_params("parallel"),
        name="merge_norm",
    )(x, ya, yb, yc, yd, h, h, h, h, h, wb, wo, lng, lnb)


def _split_w_in(w_in):
    out, off = {}, 0
    for name, width in IN_LAYOUT:
        out[name] = w_in[:, off:off + width]
        off += width
    return out


def _regroup_kernel(w_ref, proj_ref, key_ref):
    cols = _split_w_in(w_ref[...])
    pad = jnp.zeros((w_ref.shape[0], MISC_PAD), F32)
    proj_ref[...] = jnp.concatenate([cols[name] for name in PROJ_ORDER] + [pad], axis=1).astype(BF16)
    k, v, ki = cols['b_k'], cols['b_v'], cols['b_ki']
    dh = DSA_HEAD_DIM
    key_ref[...] = jnp.concatenate([k[:, :dh], k[:, :dh], k[:, dh:], k[:, dh:],
                                    v[:, :dh], v[:, :dh], v[:, dh:], v[:, dh:], ki, ki], axis=1).astype(BF16)


def _regroup_weights(w_in):
    depth, rows, width = w_in.shape
    tr = 128
    return pl.pallas_call(
        _regroup_kernel,
        grid=(depth, rows // tr),
        in_specs=[pl.BlockSpec((None, tr, width), lambda l, i: (l, i, 0))],
        out_specs=[pl.BlockSpec((None, tr, PROJ_W), lambda l, i: (l, i, 0)),
                   pl.BlockSpec((None, tr, HK_W), lambda l, i: (l, i, 0))],
        out_shape=[jax.ShapeDtypeStruct((depth, rows, PROJ_W), BF16),
                   jax.ShapeDtypeStruct((depth, rows, HK_W), BF16)],
        compiler_params=_params("parallel", "parallel"),
        name="regroup_weights",
    )(w_in)


def _dup_keys(k, v, ki):
    return jnp.concatenate([k[:, :, 0], k[:, :, 0], k[:, :, 1], k[:, :, 1],
                            v[:, :, 0], v[:, :, 0], v[:, :, 1], v[:, :, 1], ki, ki], axis=-1).astype(BF16)


def _layer(x, lw, st):
    n, t, _ = x.shape
    x2 = x.reshape(n * t, D_MODEL)
    h = _matmul(x2, lw['w_proj'], F32, tn=PROJ_W // 9)
    hk_new = _matmul(x2, lw['w_key'], BF16, tn=HK_W).reshape(n, t, HK_W)
    if st is None:
        hk, n_keys, b_off = hk_new, t, 0
        s5r = jnp.zeros((n, 1, S5_LANES), F32)
        s5i = jnp.zeros((n, 1, S5_LANES), F32)
        gdn0 = jnp.zeros((n, GDN_HEADS, GDN_DK, GDN_DV), F32)
        conv0 = jnp.zeros((n, 8, GDN_CONV_CH), F32)
        gla0 = jnp.zeros((n, GLA_HEADS * GLA_DK, GLA_DV), F32)
    else:
        past = st['k'].shape[1]
        n_keys = past + t
        b_off = past // Q_BLOCK
        lp = -(-n_keys // KEY_TILE) * KEY_TILE
        hk = jnp.concatenate([_dup_keys(st['k'], st['v'], st['kidx']), hk_new,
                              jnp.zeros((n, lp - n_keys, HK_W), BF16)], axis=1)
        s5r = st['s5_re'].reshape(n, 1, S5_LANES)
        s5i = st['s5_im'].reshape(n, 1, S5_LANES)
        gdn0 = st['gdn']
        conv0 = jnp.concatenate([jnp.zeros((n, 8 - (CONV_W - 1), GDN_CONV_CH), F32), st['gdn_conv']], axis=1)
        gla0 = st['gla'].reshape(n, GLA_HEADS * GLA_DK, GLA_DV)
    topk = min(DSA_TOPK_MAX, n_keys // 4)

    ya, s5r_new, s5i_new = _s5_call(h, n, t, s5r, s5i, lw['s5'])
    yb = _dsa_call(h, hk, n, t, lw['bias'], lw['tril'], b_off=b_off, n_keys=n_keys, topk=topk)
    h3 = h.reshape(n, t, PROJ_W)
    yc, gdn_new, conv_new = _gdn_call(h3, gdn0, conv0, lw['gdn'])
    yd, gla_new = _gla_call(h3, gla0, lw['gla'])
    yc = yc.reshape(n * t, GDN_VW)
    yd = yd.reshape(n * t, GLA_HEADS * GLA_DV)
    y = _merge_call(x2, h, ya, yb, yc, yd, lw['w_branch'], lw['w_out'], lw['ln_g'], lw['ln_b'])

    def cols(name, width):
        return h[:, PROJ_OFF[name]:PROJ_OFF[name] + width].reshape(n, t, width)

    new = {'k': cols('b_k', DSA_KV_HEADS * DSA_HEAD_DIM).reshape(n, t, DSA_KV_HEADS, DSA_HEAD_DIM),
           'v': cols('b_v', DSA_KV_HEADS * DSA_HEAD_DIM).reshape(n, t, DSA_KV_HEADS, DSA_HEAD_DIM),
           'kidx': cols('b_ki', IDX_DIM),
           's5_re': s5r_new.reshape(n, S5_GROUPS, S5_STATE), 's5_im': s5i_new.reshape(n, S5_GROUPS, S5_STATE),
           'gdn': gdn_new, 'gdn_conv': conv_new[:, 8 - (CONV_W - 1):],
           'gla': gla_new.reshape(n, GLA_HEADS, GLA_DK, GLA_DV)}
    return y.reshape(n, t, D_MODEL), new


def kernel(x_prompt, x_sample, cache_k, cache_v, cache_kidx, state_s5_re, state_s5_im, state_gdn, state_gdn_conv, state_gla, w_in, w_branch, w_out, ln_g, ln_b, rel_bias, s5_a_re, s5_a_im, s5_log_dt, s5_b_re, s5_b_im, s5_c_re, s5_c_im, s5_d, s5_w_glu, gdn_conv, gdn_a_log, gdn_dt_bias, gdn_norm, gla_w_g2, gla_b_g, gla_norm):
    bias = _dsa_bias_tables(rel_bias)
    tril = (jnp.arange(KEY_BLOCK)[:, None] >= jnp.arange(KEY_BLOCK)[None, :]).astype(BF16)
    yp, ys = x_prompt, x_sample
    new_p, new_s = [], []
    w_proj, w_key = _regroup_weights(w_in)
    for l in range(DEPTH):
        lw = {'w_proj': w_proj[l], 'w_key': w_key[l], 'bias': bias, 'tril': tril,
              'w_branch': w_branch[l].astype(BF16), 'w_out': w_out[l].astype(BF16),
              'ln_g': ln_g[l].reshape(1, D_MODEL), 'ln_b': ln_b[l].reshape(1, D_MODEL),
              's5': _s5_prepare(s5_a_re[l], s5_a_im[l], s5_log_dt[l], s5_b_re[l], s5_b_im[l],
                                s5_c_re[l], s5_c_im[l], s5_d[l], s5_w_glu[l]),
              'gdn': {'convw': gdn_conv[l], 'alog': _lane_place(gdn_a_log[l], MISC_A),
                      'dtb': _lane_place(gdn_dt_bias[l], MISC_A), 'norm': gdn_norm[l].reshape(1, GDN_DV)},
              'gla': _gla_prepare(gla_w_g2[l], gla_b_g[l], gla_norm[l])}
        st = {'k': cache_k[l], 'v': cache_v[l], 'kidx': cache_kidx[l], 's5_re': state_s5_re[l],
              's5_im': state_s5_im[l], 'gdn': state_gdn[l], 'gdn_conv': state_gdn_conv[l], 'gla': state_gla[l]}
        yp, stp = _layer(yp, lw, None)
        ys, sts = _layer(ys, lw, st)
        new_p.append(stp)
        new_s.append(sts)
    names = ('k', 'v', 'kidx', 's5_re', 's5_im', 'gdn', 'gdn_conv', 'gla')
    stack = lambda states, name: jnp.stack([s[name] for s in states], axis=0)
    return ((yp, ys) + tuple(stack(new_p, nm) for nm in names) + tuple(stack(new_s, nm) for nm in names))
```

```python
import functools
import math

import jax
import jax.numpy as jnp
from jax import lax
from jax.experimental import pallas as pl
from jax.experimental.pallas import tpu as pltpu

F32 = jnp.float32
BF16 = jnp.bfloat16
HIGHEST = lax.Precision.HIGHEST

D_MODEL = 1024
DEPTH = 2
CHUNK = 64
N_BRANCH = 4
BRANCH_W = 512
S5_GROUP = 16
S5_GROUPS = BRANCH_W // S5_GROUP
S5_STATE = 64
S5_LANES = S5_GROUPS * S5_STATE
S5_SPLIT = 2
DSA_HEADS = 8
DSA_KV_HEADS = 2
DSA_HEAD_DIM = 64
IDX_HEADS = 4
IDX_DIM = 64
DSA_TOPK_MAX = 256
REL_BUCKETS = 32
REL_MAX_DIST = 128
GDN_HEADS = 4
GDN_DK = 128
GDN_DV = 128
GDN_QK = GDN_HEADS * GDN_DK
GDN_VW = GDN_HEADS * GDN_DV
CONV_W = 4
GDN_CONV_CH = 2 * GDN_QK + GDN_VW
GLA_HEADS = 4
GLA_DK = 64
GLA_DV = 128
GLA_GATE_RANK = 16
GLA_TAU = 16.0
GLA_BLOCK = 16
LN_EPS = 1e-5
RMS_EPS = 1e-6
DN_ALPHA = (2 * DEPTH) ** 0.25
LOG2E = math.log2(math.e)

LANE = 128
KEY_BLOCK = 128
KEY_SUB = 4
KEY_TILE = KEY_SUB * KEY_BLOCK
SOFTMAX_CEIL = 2.0 ** 120
SOFTMAX_FLOOR = 2.0 ** -60
FOLD_ROWS = 64
Q_BLOCK = 128
Q_TILE = 256
VMEM_LIMIT = 56 * 1024 * 1024

IN_LAYOUT = (
    ('a_u', BRANCH_W), ('a_gate', BRANCH_W),
    ('b_q', DSA_HEADS * DSA_HEAD_DIM), ('b_k', DSA_KV_HEADS * DSA_HEAD_DIM),
    ('b_v', DSA_KV_HEADS * DSA_HEAD_DIM), ('b_qi', IDX_HEADS * IDX_DIM), ('b_ki', IDX_DIM),
    ('b_wi', IDX_HEADS), ('b_gate', BRANCH_W),
    ('c_qkv', GDN_CONV_CH), ('c_beta', GDN_HEADS), ('c_a', GDN_HEADS), ('c_gate', BRANCH_W),
    ('d_q', GLA_HEADS * GLA_DK), ('d_k', GLA_HEADS * GLA_DK), ('d_v', GLA_HEADS * GLA_DV),
    ('d_g', GLA_GATE_RANK), ('d_gate', BRANCH_W),
    ('merge', N_BRANCH * D_MODEL),
)

PROJ_ORDER = ('merge', 'a_u', 'a_gate', 'b_q', 'b_qi', 'b_k', 'b_v', 'c_qkv', 'b_gate', 'c_gate',
              'd_v', 'd_gate', 'd_q', 'd_k', 'b_ki', 'b_wi', 'c_beta', 'c_a', 'd_g')
MISC_WI = IDX_DIM
MISC_BETA = MISC_WI + IDX_HEADS
MISC_A = MISC_BETA + GDN_HEADS
MISC_G = MISC_A + GDN_HEADS
MISC_PAD = LANE - (MISC_G + GLA_GATE_RANK)


def _proj_offsets():
    widths = dict(IN_LAYOUT)
    off, out = 0, {}
    for name in PROJ_ORDER:
        out[name] = off
        off += widths[name]
    return out, off + MISC_PAD


PROJ_OFF, PROJ_W = _proj_offsets()
OFF_MISC = PROJ_OFF['b_ki']
HK_W = 5 * LANE
HK_V = 2 * LANE
HK_KI = 4 * LANE


def _dot(a, b, precision=None):
    return jnp.dot(a, b, preferred_element_type=F32, precision=precision)


def _dot_nt(a, b, precision=None):
    return lax.dot_general(a, b, (((1,), (1,)), ((), ())), preferred_element_type=F32, precision=precision)


def _dot_tn(a, b, precision=None):
    return lax.dot_general(a, b, (((0,), (0,)), ((), ())), preferred_element_type=F32, precision=precision)


def _split_bf16(x):
    hi = x.astype(BF16)
    return hi, (x - hi.astype(F32)).astype(BF16)


def _dot3(a, b):
    a_hi, a_lo = _split_bf16(a)
    b_hi, b_lo = _split_bf16(b)
    return _dot(a_hi, b_hi) + (_dot(a_hi, b_lo) + _dot(a_lo, b_hi))


def _params(*sem):
    return pltpu.CompilerParams(dimension_semantics=sem, vmem_limit_bytes=VMEM_LIMIT)


def _softplus(x):
    return jnp.maximum(x, 0.0) + jnp.log1p(jnp.exp(-jnp.abs(x)))


def _iota(shape, dim):
    return lax.broadcasted_iota(jnp.int32, shape, dim)


def _mm_kernel(x_ref, w_ref, o_ref):
    o_ref[...] = _dot(x_ref[...].astype(BF16), w_ref[...]).astype(o_ref.dtype)


def _matmul(x, w, out_dtype, tn):
    rows, kdim = x.shape
    width = w.shape[1]
    tm = min(rows, 1024)
    return pl.pallas_call(
        _mm_kernel,
        grid=(rows // tm, width // tn),
        in_specs=[pl.BlockSpec((tm, kdim), lambda i, j: (i, 0)),
                  pl.BlockSpec((kdim, tn), lambda i, j: (0, j))],
        out_specs=pl.BlockSpec((tm, tn), lambda i, j: (i, j)),
        out_shape=jax.ShapeDtypeStruct((rows, width), out_dtype),
        compiler_params=_params("parallel", "parallel"),
        name="proj_matmul",
    )(x, w)


def _s5_kernel(u_ref, h0r_ref, h0i_ref, ar_ref, ai_ref, bb_ref, cc_ref, d_ref, wglu_ref,
               ya_ref, hr_out, hi_out, xr_s, xi_s, hr_s, hi_s, cr_s, ci_s, *, tc):
    @pl.when(pl.program_id(1) == 0)
    def _():
        cr_s[...] = h0r_ref[...]
        ci_s[...] = h0i_ref[...]

    u = u_ref[...]
    ub = u.astype(BF16)
    uw = BRANCH_W // S5_SPLIT
    sw = S5_LANES // S5_SPLIT
    for part in range(S5_SPLIT):
        x = _dot(ub[:, part * uw:(part + 1) * uw], bb_ref[part])
        xr_s[:, part * sw:(part + 1) * sw] = x[:, :sw]
        xi_s[:, part * sw:(part + 1) * sw] = x[:, sw:]

    ar = ar_ref[...]
    ai = ai_ref[...]

    def step(t, carry):
        hr, hi = carry
        nhr = ar * hr - ai * hi + xr_s[pl.ds(t, 1), :]
        nhi = ar * hi + ai * hr + xi_s[pl.ds(t, 1), :]
        hr_s[pl.ds(t, 1), :] = nhr
        hi_s[pl.ds(t, 1), :] = nhi
        return nhr, nhi

    hr, hi = lax.fori_loop(0, tc, step, (cr_s[...], ci_s[...]), unroll=8)
    cr_s[...] = hr
    ci_s[...] = hi
    hr_out[...] = hr
    hi_out[...] = hi
    ys = []
    for part in range(S5_SPLIT):
        hcat = jnp.concatenate([hr_s[:, part * sw:(part + 1) * sw], hi_s[:, part * sw:(part + 1) * sw]], axis=1)
        ys.append(_dot(hcat.astype(BF16), cc_ref[part]))
    y = jnp.concatenate(ys, axis=1) + d_ref[...] * u
    y = jax.nn.gelu(y)
    g = _dot(y.astype(BF16), wglu_ref[...])
    ya_ref[...] = g[:, :BRANCH_W] * jax.nn.sigmoid(g[:, BRANCH_W:])


def _s5_call(h, n, t, h0r, h0i, sp):
    tc = min(t, 256)
    nc = t // tc
    uw = BRANCH_W // S5_SPLIT
    sw = S5_LANES // S5_SPLIT
    const = lambda shape: pl.BlockSpec(shape, lambda i, c: (0,) * len(shape))
    state = pl.BlockSpec((None, 1, S5_LANES), lambda i, c: (i, 0, 0))
    return pl.pallas_call(
        functools.partial(_s5_kernel, tc=tc),
        grid=(n, nc),
        in_specs=[pl.BlockSpec((tc, BRANCH_W), lambda i, c: (i * nc + c, PROJ_OFF['a_u'] // BRANCH_W)),
                  state, state,
                  const((1, S5_LANES)), const((1, S5_LANES)),
                  const((S5_SPLIT, uw, 2 * sw)), const((S5_SPLIT, 2 * sw, uw)),
                  const((1, BRANCH_W)), const((BRANCH_W, 2 * BRANCH_W))],
        out_specs=[pl.BlockSpec((tc, BRANCH_W), lambda i, c: (i * nc + c, 0)), state, state],
        out_shape=[jax.ShapeDtypeStruct((n * t, BRANCH_W), F32),
                   jax.ShapeDtypeStruct((n, 1, S5_LANES), F32),
                   jax.ShapeDtypeStruct((n, 1, S5_LANES), F32)],
        scratch_shapes=[pltpu.VMEM((tc, S5_LANES), F32), pltpu.VMEM((tc, S5_LANES), F32),
                        pltpu.VMEM((tc, S5_LANES), F32), pltpu.VMEM((tc, S5_LANES), F32),
                        pltpu.VMEM((1, S5_LANES), F32), pltpu.VMEM((1, S5_LANES), F32)],
        compiler_params=_params("parallel", "arbitrary"),
        name="s5_branch",
    )(h, h0r, h0i, sp['ar'], sp['ai'], sp['bb'], sp['cc'], sp['d'], sp['wglu'])


def _s5_prepare(a_re, a_im, log_dt, b_re, b_im, c_re, c_im, d, w_glu):
    lam_r = jnp.minimum(a_re, -1e-4)
    lam_i = a_im
    dt = jnp.exp(log_dt)[:, None]
    mag = jnp.exp(lam_r * dt)
    ar = mag * jnp.cos(lam_i * dt)
    ai = mag * jnp.sin(lam_i * dt)
    den = lam_r * lam_r + lam_i * lam_i
    nr = ar - 1.0
    cr = (nr * lam_r + ai * lam_i) / den
    ci = (ai * lam_r - nr * lam_i) / den
    bb_r = cr[..., None] * b_re - ci[..., None] * b_im
    bb_i = cr[..., None] * b_im + ci[..., None] * b_re
    eye = jnp.eye(S5_GROUPS, dtype=F32)

    def blockdiag_in(bb):
        m = eye[:, None, :, None] * jnp.swapaxes(bb, 1, 2)[:, :, None, :]
        return m.reshape(BRANCH_W, S5_LANES)

    def blockdiag_out(c):
        m = eye[:, None, :, None] * jnp.swapaxes(c, 1, 2)[:, :, None, :]
        return m.reshape(S5_LANES, BRANCH_W)

    uw = BRANCH_W // S5_SPLIT
    sw = S5_LANES // S5_SPLIT
    bbr, bbi = blockdiag_in(bb_r), blockdiag_in(bb_i)
    ccr, cci = blockdiag_out(c_re), blockdiag_out(c_im)
    bb = jnp.stack([jnp.concatenate([m[p * uw:(p + 1) * uw, p * sw:(p + 1) * sw] for m in (bbr, bbi)], axis=1)
                    for p in range(S5_SPLIT)]).astype(BF16)
    cc = jnp.stack([jnp.concatenate([ccr[p * sw:(p + 1) * sw, p * uw:(p + 1) * uw],
                                     -cci[p * sw:(p + 1) * sw, p * uw:(p + 1) * uw]], axis=0)
                    for p in range(S5_SPLIT)]).astype(BF16)
    return {'ar': ar.reshape(1, S5_LANES), 'ai': ai.reshape(1, S5_LANES), 'bb': bb, 'cc': cc,
            'd': d.reshape(1, BRANCH_W), 'wglu': w_glu.astype(BF16)}


def _sortable(x):
    bits = lax.bitcast_convert_type(x, jnp.int32)
    key = bits ^ ((bits >> 31) & jnp.int32(0x7FFFFFFF))
    return jnp.where(key == -1, 0, key)


def _dsa_kernel(q_ref, qi_ref, misc_ref, hk_ref, bias_ref, tril_ref, o_ref, sc_s, pk_s, acc_s, far_s, *, qb, b_off, n_keys, topk):
    nsub = max(qb // Q_BLOCK, 1)
    qw = qb // nsub
    b = pl.program_id(1) * nsub + b_off
    nkt = (b + nsub - 1 + KEY_SUB) // KEY_SUB
    fold = KEY_TILE // FOLD_ROWS
    key_row = _iota((KEY_TILE, qb), 0)
    qpos = b * Q_BLOCK + _iota((1, qb), 1)
    limit = jnp.minimum(((qpos >> 6) + 1) * CHUNK, n_keys)
    low_half = _iota((qb, LANE), 1) < DSA_HEAD_DIM

    def key_rows(ref_cols, j):
        return hk_ref[pl.ds(pl.multiple_of(j * KEY_TILE, KEY_TILE), KEY_TILE), ref_cols:ref_cols + LANE]

    def half_masked(x, hd):
        pair = x[:, (hd // 2) * LANE:(hd // 2 + 1) * LANE]
        return jnp.where(low_half == (hd % 2 == 0), pair, 0.0).astype(BF16)

    misc = misc_ref[...]
    pick = jnp.where(_iota((8, LANE), 1) == MISC_WI + _iota((8, LANE), 0), 1.0, 0.0)
    wi_t = _dot_nt(pick, misc, precision=HIGHEST) * (IDX_DIM ** -0.5 * IDX_HEADS ** -0.5)
    qi = qi_ref[...]
    qi_stack = jnp.concatenate([half_masked(qi, hd) for hd in range(IDX_HEADS)], axis=0)

    def pack16(v):
        half = KEY_TILE // 2
        return (v[0:half] & 0xFFFF) | (v[half:] << 16)

    def score_block(j, carry):
        z = _dot_nt(key_rows(HK_KI, j), qi_stack)
        sc = jnp.zeros((KEY_TILE, qb), F32)
        for hd in range(IDX_HEADS):
            sc = sc + jnp.maximum(z[:, hd * qb:(hd + 1) * qb], 0.0) * wi_t[hd:hd + 1, :]
        sc = jnp.where(j * KEY_TILE + key_row < limit, sc, -jnp.inf)
        key = _sortable(sc)
        sc_s[j] = key
        pk_s[j] = pack16(key >> 16)
        return carry

    lax.fori_loop(0, nkt, score_block, 0)

    def count_ge(cand):
        def body(j, acc):
            return acc + jnp.where(sc_s[j] >= cand, 1.0, 0.0).reshape(fold, FOLD_ROWS, qb).sum(axis=0)
        acc = lax.fori_loop(0, nkt, body, jnp.zeros((FOLD_ROWS, qb), F32))
        return jnp.sum(acc, axis=0, keepdims=True)

    one16 = jnp.ones((KEY_TILE, qb), jnp.int16)
    zero16 = jnp.zeros((KEY_TILE, qb), jnp.int16)

    def count16_ge(cand):
        word = jnp.broadcast_to((cand & 0xFFFF) | (cand << 16), (KEY_TILE // 2, qb))
        c16 = pltpu.bitcast(word, jnp.int16)

        def body(j, acc):
            ind = jnp.where(pltpu.bitcast(pk_s[j], jnp.int16) >= c16, one16, zero16)
            for i in range(fold):
                acc = acc + ind[i * FOLD_ROWS:(i + 1) * FOLD_ROWS]
            return acc
        acc = lax.fori_loop(0, nkt, body, jnp.zeros((FOLD_ROWS, qb), jnp.int16))
        return jnp.sum(acc.astype(F32), axis=0, keepdims=True)

    def radix_search(start, want):
        def bit_step(i, res):
            cand = res + (jnp.int32(1) << (14 - i))
            return jnp.where(count16_ge(cand) >= want, cand, res)
        return lax.fori_loop(0, 15, bit_step, start)

    kf = float(topk)
    top = radix_search(jnp.where(count16_ge(jnp.zeros((1, qb), jnp.int32)) >= kf, 0, -32768), kf)
    int16_max = jnp.iinfo(jnp.int16).max
    above = jnp.where(top == int16_max, 0.0, count16_ge(jnp.minimum(top + 1, int16_max)))
    want_low = kf - above

    def pack_low(j, carry):
        key = sc_s[j]
        pk_s[j] = pack16(jnp.where((key >> 16) == top, (key >> 1) & 0x7FFF, -1))
        return carry

    lax.fori_loop(0, nkt, pack_low, 0)
    low = radix_search(jnp.zeros((1, qb), jnp.int32), want_low)
    base = (top << 16) | (low << 1)
    thr = jnp.where(count_ge(base | 1) >= kf, base | 1, base)
    need = kf - count_ge(thr + 1)

    tril = tril_ref[...]

    def select_block(j, ties):
        keys = sc_s[j]
        eq = keys == thr
        eq_b = jnp.where(eq, 1.0, 0.0).astype(BF16)
        parts, carry = [], ties
        for s in range(KEY_SUB):
            part = _dot(tril, eq_b[s * KEY_BLOCK:(s + 1) * KEY_BLOCK]) + carry
            parts.append(part)
            carry = part[KEY_BLOCK - 1:KEY_BLOCK, :]
        prefix = jnp.concatenate(parts, axis=0)
        sel = ((keys > thr) | (eq & (prefix <= need))) & (j * KEY_TILE + key_row < limit)
        sc_s[j] = lax.bitcast_convert_type(jnp.where(sel, 0.0, -jnp.inf), jnp.int32)
        return prefix[KEY_TILE - 1:KEY_TILE, :]

    lax.fori_loop(0, nkt, select_block, jnp.zeros((1, qb), F32))

    q = q_ref[...] * (DSA_HEAD_DIM ** -0.5 * LOG2E)
    rep = DSA_HEADS // DSA_KV_HEADS
    eye = jnp.where(_iota((LANE, LANE), 0) == _iota((LANE, LANE), 1), 1.0, 0.0)
    value_lane = _iota((KEY_TILE, LANE), 1) < DSA_HEAD_DIM
    n_pairs = DSA_HEADS // 2
    q_pair = [jnp.concatenate([half_masked(q, 2 * c), half_masked(q, 2 * c + 1)], axis=0) for c in range(n_pairs)]
    far_bias = [jnp.concatenate([bias_ref[hd, 2, 0:1, 0:qw]] * nsub, axis=1)
                for hd in range(DSA_HEADS)]
    n_far = jnp.maximum((b - 1) // KEY_SUB, 0)

    def tile_operands(j):
        mask = lax.bitcast_convert_type(sc_s[j], F32)
        kd = [key_rows(grp * LANE, j) for grp in range(DSA_KV_HEADS)]
        zs = [_dot_nt(kd[(2 * c) // rep], q_pair[c]) for c in range(n_pairs)]
        vd = [jnp.where(value_lane, key_rows(HK_V + grp * LANE, j), 1.0).astype(BF16)
              for grp in range(DSA_KV_HEADS)]
        return mask, zs, vd

    def near_bias(hd, j):
        cols = [jnp.concatenate([bias_ref[hd, jnp.clip(b + u - (KEY_SUB * j + s), 0, 2), :, 0:qw]
                                 for s in range(KEY_SUB)], axis=0) for u in range(nsub)]
        return jnp.concatenate(cols, axis=1)

    def plain_tile(j, carry, far):
        mask, zs, vd = tile_operands(j)
        ps = []
        for hd in range(DSA_HEADS):
            r = hd % 2
            zh = zs[hd // 2][:, r * qb:(r + 1) * qb] + mask
            if not far:
                zh = zh + near_bias(hd, j)
            ps.append(jnp.exp2(zh).astype(BF16))
        acc = far_s if far else acc_s
        for c in range(n_pairs):
            acc[c] += _dot_tn(vd[(2 * c) // rep], jnp.concatenate(ps[2 * c:2 * c + 2], axis=1))
        return carry

    acc_s[...] = jnp.zeros_like(acc_s)
    far_s[...] = jnp.zeros_like(far_s)
    lax.fori_loop(0, n_far, lambda j, c: plain_tile(j, c, True), 0)
    lax.fori_loop(n_far, nkt, lambda j, c: plain_tile(j, c, False), 0)
    healthy = jnp.ones((1, 1), F32)
    for c in range(n_pairs):
        scale = jnp.concatenate([jnp.exp2(far_bias[2 * c]), jnp.exp2(far_bias[2 * c + 1])], axis=1)
        total = far_s[c] * scale + acc_s[c]
        acc_s[c] = total
        norm = total[DSA_HEAD_DIM:DSA_HEAD_DIM + 1, :]
        ok = (jnp.abs(total) < SOFTMAX_CEIL) & (norm > SOFTMAX_FLOOR)
        healthy = jnp.minimum(healthy, jnp.min(jnp.where(ok, 1.0, 0.0), keepdims=True))

    def attend(j, ms, far):
        mask, zs, vd = tile_operands(j)
        new_ms, ps, alphas = [], [], []
        for hd in range(DSA_HEADS):
            r = hd % 2
            zh = zs[hd // 2][:, r * qb:(r + 1) * qb] + mask
            if not far:
                zh = zh + near_bias(hd, j)
            tile_max = jnp.max(zh.reshape(fold, FOLD_ROWS, qb).max(axis=0), axis=0, keepdims=True)
            shift = far_bias[hd] if far else 0.0
            m_new = jnp.maximum(ms[hd], tile_max + shift)
            ps.append(jnp.exp2(zh - (m_new - shift)).astype(BF16))
            alphas.append(jnp.exp2(ms[hd] - m_new))
            new_ms.append(m_new)
        for c in range(n_pairs):
            acc_s[c] = (acc_s[c] * jnp.concatenate(alphas[2 * c:2 * c + 2], axis=1)
                        + _dot_tn(vd[(2 * c) // rep], jnp.concatenate(ps[2 * c:2 * c + 2], axis=1)))
        return tuple(new_ms)

    @pl.when(healthy[0, 0] < 0.5)
    def _():
        acc_s[...] = jnp.zeros_like(acc_s)
        ms = tuple(jnp.full((1, qb), -1e30, F32) for _ in range(DSA_HEADS))
        ms = lax.fori_loop(0, n_far, lambda j, m: attend(j, m, True), ms)
        lax.fori_loop(n_far, nkt, lambda j, m: attend(j, m, False), ms)

    for c in range(n_pairs):
        halves = []
        for r in range(2):
            blk = acc_s[c, :, r * qb:(r + 1) * qb]
            halves.append(blk[0:DSA_HEAD_DIM] / blk[DSA_HEAD_DIM:DSA_HEAD_DIM + 1])
        o_t = jnp.concatenate(halves, axis=0)
        o_ref[:, c * LANE:(c + 1) * LANE] = _dot_tn(o_t, eye, precision=HIGHEST)


def _dsa_call(h, hk, n, t, bias, tril, *, b_off, n_keys, topk):
    qb = min(t, Q_TILE)
    nqb = t // qb
    lp = hk.shape[1]
    wq = DSA_HEADS * DSA_HEAD_DIM
    wqi = IDX_HEADS * IDX_DIM
    return pl.pallas_call(
        functools.partial(_dsa_kernel, qb=qb, b_off=b_off, n_keys=n_keys, topk=topk),
        grid=(n, nqb),
        in_specs=[pl.BlockSpec((qb, wq), lambda i, c: (i * nqb + c, PROJ_OFF['b_q'] // wq)),
                  pl.BlockSpec((qb, wqi), lambda i, c: (i * nqb + c, PROJ_OFF['b_qi'] // wqi)),
                  pl.BlockSpec((qb, LANE), lambda i, c: (i * nqb + c, OFF_MISC // LANE)),
                  pl.BlockSpec((None, lp, HK_W), lambda i, c: (i, 0, 0)),
                  pl.BlockSpec((DSA_HEADS, 3, KEY_BLOCK, Q_BLOCK), lambda i, c: (0, 0, 0, 0)),
                  pl.BlockSpec((KEY_BLOCK, KEY_BLOCK), lambda i, c: (0, 0))],
        out_specs=pl.BlockSpec((qb, wq), lambda i, c: (i * nqb + c, 0)),
        out_shape=jax.ShapeDtypeStruct((n * t, wq), F32),
        scratch_shapes=[pltpu.VMEM((lp // KEY_TILE, KEY_TILE, qb), jnp.int32),
                        pltpu.VMEM((lp // KEY_TILE, KEY_TILE // 2, qb), jnp.int32),
                        pltpu.VMEM((DSA_HEADS // 2, LANE, 2 * qb), F32),
                        pltpu.VMEM((DSA_HEADS // 2, LANE, 2 * qb), F32)],
        compiler_params=_params("parallel", "arbitrary"),
        name="dsa_branch",
    )(h, h, h, hk, bias, tril)


def _t5_bucket(rel):
    nb = REL_BUCKETS // 2
    max_exact = nb // 2
    ret = jnp.where(rel > 0, nb, 0)
    dist = jnp.abs(rel)
    distf = jnp.maximum(dist, 1).astype(F32)
    large = max_exact + (jnp.log(distf / max_exact) / math.log(REL_MAX_DIST / max_exact)
                         * (nb - max_exact)).astype(jnp.int32)
    large = jnp.minimum(large, nb - 1)
    return ret + jnp.where(dist < max_exact, dist, large)


def _dsa_bias_tables(rel_bias):
    lc = jnp.arange(KEY_BLOCK, dtype=jnp.int32)[:, None]
    qr = jnp.arange(Q_BLOCK, dtype=jnp.int32)[None, :]
    rel = jnp.stack([lc - qr - back * KEY_BLOCK for back in range(3)], axis=0)
    onehot = jax.nn.one_hot(_t5_bucket(rel), REL_BUCKETS, dtype=F32)
    bias = jnp.einsum('klqb,bh->hklq', onehot, rel_bias.astype(F32), precision=HIGHEST)
    return bias * LOG2E


def _gdn_kernel(raw_ref, misc_ref, s0_ref, conv0_ref, convw_ref, alog_ref, dtb_ref, norm_ref,
                o_ref, s_out, conv_out, xp_s, st_s, *, n, c):
    @pl.when(pl.program_id(0) == 0)
    def _():
        xp_s[:, 0:8, :] = conv0_ref[...]
        st_s[...] = s0_ref[...]

    w = convw_ref[...]
    row = _iota((c, c), 0)
    col = _iota((c, c), 1)
    incl = row >= col
    strict = row > col
    eye = row == col
    tril = jnp.where(incl, 1.0, 0.0)
    eye_f = jnp.where(eye, 1.0, 0.0)
    chains = [(b, hd) for b in range(n) for hd in range(GDN_HEADS)]

    qkv, beta_all, gc_all = [], [], []
    for b in range(n):
        raw = raw_ref[b]
        xp_s[b, 8:8 + c, :] = raw
        y = raw * w[CONV_W - 1:CONV_W, :]
        for i in range(CONV_W - 1):
            y = y + xp_s[b, 5 + i:5 + i + c, :] * w[i:i + 1, :]
        tail = xp_s[b, c:c + 8, :]
        xp_s[b, 0:8, :] = tail
        conv_out[b] = tail
        qkv.append(jax.nn.silu(y))
        misc = misc_ref[b]
        beta_all.append(jax.nn.sigmoid(misc))
        g_all = -jnp.exp(alog_ref[...]) * _softplus(misc + dtb_ref[...])
        gc_all.append(_dot(tril, g_all, precision=HIGHEST))

    qs, ks, vs, betas, gcs, dec_s, dec_i = [], [], [], [], [], [], []
    for b, hd in chains:
        qh = qkv[b][:, hd * GDN_DK:(hd + 1) * GDN_DK]
        kh = qkv[b][:, GDN_QK + hd * GDN_DK:GDN_QK + (hd + 1) * GDN_DK]
        vs.append(qkv[b][:, 2 * GDN_QK + hd * GDN_DV:2 * GDN_QK + (hd + 1) * GDN_DV])
        qs.append(qh * lax.rsqrt(jnp.sum(qh * qh, axis=1, keepdims=True) + RMS_EPS) * (GDN_DK ** -0.5))
        ks.append(kh * lax.rsqrt(jnp.sum(kh * kh, axis=1, keepdims=True) + RMS_EPS))
        betas.append(beta_all[b][:, MISC_BETA + hd:MISC_BETA + hd + 1])
        gc = gc_all[b][:, MISC_A + hd:MISC_A + hd + 1]
        gcs.append(gc)
        gc_b = jnp.broadcast_to(gc, (c, c))
        gc_row = jnp.sum(jnp.where(eye, gc_b, 0.0), axis=0, keepdims=True)
        diff = gc_b - gc_row
        dec = jnp.exp(diff)
        dec_s.append(jnp.where(strict, dec, 0.0))
        dec_i.append(jnp.where(incl, dec, 0.0))
    kb = [k.astype(BF16) for k in ks]
    qb = [q.astype(BF16) for q in qs]
    pw = [-(beta * _dot_nt(k, k) * d) for beta, k, d in zip(betas, kb, dec_s)]
    inv = [eye_f + p for p in pw]
    for _ in range(int(math.log2(c)) - 1):
        pw = [_dot3(p, p) for p in pw]
        inv = [i + _dot3(i, p) for i, p in zip(inv, pw)]
    egc = [jnp.exp(gc) for gc in gcs]
    sol = [_dot3(i, jnp.concatenate([(beta * e) * k, beta * v], axis=1))
           for i, beta, e, k, v in zip(inv, betas, egc, ks, vs)]
    s_old = [st_s[b, hd] for b, hd in chains]
    sb = [s.astype(BF16) for s in s_old]
    vb = [(x[:, GDN_DK:] - _dot(x[:, :GDN_DK].astype(BF16), s)).astype(BF16) for x, s in zip(sol, sb)]
    attn = [(_dot_nt(q, k) * d).astype(BF16) for q, k, d in zip(qb, kb, dec_i)]
    outs = [e * _dot(q, s) + _dot(a, v) for e, q, s, a, v in zip(egc, qb, sb, attn, vb)]
    for (b, hd), gc, k, v, s, o in zip(chains, gcs, ks, vb, s_old, outs):
        g_last = gc[c - 1:c, :]
        st_s[b, hd] = jnp.exp(g_last) * s + _dot_tn((k * jnp.exp(g_last - gc)).astype(BF16), v)
        o = o * lax.rsqrt(jnp.mean(o * o, axis=1, keepdims=True) + RMS_EPS) * norm_ref[...]
        o_ref[b, :, hd * GDN_DV:(hd + 1) * GDN_DV] = o
    s_out[...] = st_s[...]


def _gdn_call(h3, s0, conv0, gp):
    n, t, _ = h3.shape
    c = min(t, CHUNK)
    const = lambda shape: pl.BlockSpec(shape, lambda j: (0,) * len(shape))
    state = const((n, GDN_HEADS, GDN_DK, GDN_DV))
    conv = const((n, 8, GDN_CONV_CH))
    return pl.pallas_call(
        functools.partial(_gdn_kernel, n=n, c=c),
        grid=(t // c,),
        in_specs=[pl.BlockSpec((n, c, GDN_CONV_CH), lambda j: (0, j, PROJ_OFF['c_qkv'] // GDN_CONV_CH)),
                  pl.BlockSpec((n, c, LANE), lambda j: (0, j, OFF_MISC // LANE)),
                  state, conv,
                  const((CONV_W, GDN_CONV_CH)), const((1, LANE)), const((1, LANE)), const((1, GDN_DV))],
        out_specs=[pl.BlockSpec((n, c, GDN_VW), lambda j: (0, j, 0)), state, conv],
        out_shape=[jax.ShapeDtypeStruct((n, t, GDN_VW), F32),
                   jax.ShapeDtypeStruct((n, GDN_HEADS, GDN_DK, GDN_DV), F32),
                   jax.ShapeDtypeStruct((n, 8, GDN_CONV_CH), F32)],
        scratch_shapes=[pltpu.VMEM((n, c + 8, GDN_CONV_CH), F32),
                        pltpu.VMEM((n, GDN_HEADS, GDN_DK, GDN_DV), F32)],
        compiler_params=_params("arbitrary"),
        name="gdn_branch",
    )(h3, h3, s0, conv0, gp['convw'], gp['alog'], gp['dtb'], gp['norm'])


def _lane_place(v, off):
    return jnp.zeros((1, LANE), F32).at[0, off:off + v.shape[0]].set(v.astype(F32))


def _gla_kernel(qk_ref, v_ref, misc_ref, s0_ref, wg_ref, bg_ref, norm_ref, hexp_ref, gsum_ref,
                o_ref, s_out, st_s, *, n, tc):
    blk = GLA_BLOCK
    wk = GLA_HEADS * GLA_DK

    @pl.when(pl.program_id(0) == 0)
    def _():
        st_s[...] = s0_ref[...]

    row = _iota((blk, blk), 0)
    col = _iota((blk, blk), 1)
    tril = jnp.where(row >= col, 1.0, 0.0)
    jrow = _iota((blk, wk), 0)
    lane_k = _iota((blk, wk), 1)
    head_mask = [(lane_k >= hd * GLA_DK) & (lane_k < (hd + 1) * GLA_DK) for hd in range(GLA_HEADS)]
    chains = range(n)

    def block(sb, carry):
        r0 = pl.multiple_of(sb * blk, blk)
        qk = [qk_ref[b, pl.ds(r0, blk), :] for b in chains]
        q = [x[:, :wk] * (GLA_DK ** -0.5) for x in qk]
        k = [x[:, wk:] for x in qk]
        v = [v_ref[b, pl.ds(r0, blk), :] for b in chains]
        mb = [misc_ref[b, pl.ds(r0, blk), :].astype(BF16) for b in chains]
        lg = [jax.nn.log_sigmoid(_dot(m, wg_ref[...]) + bg_ref[...]) * (1.0 / GLA_TAU) for m in mb]
        bc = [_dot(tril, x, precision=HIGHEST) for x in lg]
        dmat = []
        for b in chains:
            slabs = []
            for i in range(blk):
                keep = jrow <= i
                dec = jnp.where(keep, jnp.exp(bc[b][i:i + 1, :] - bc[b]), 0.0)
                slabs.append(dec * q[b][i:i + 1, :] * k[b])
            dmat.append(jnp.concatenate(slabs, axis=0).astype(BF16))
        a_exp = [_dot(d, hexp_ref[...]) for d in dmat]
        prod = [a * jnp.concatenate([x] * blk, axis=0) for a, x in zip(a_exp, v)]
        o = [_dot(gsum_ref[...], p.astype(BF16)) for p in prod]
        qe = [x * jnp.exp(c) for x, c in zip(q, bc)]
        kt = [x * jnp.exp(c[blk - 1:blk, :] - c) for x, c in zip(k, bc)]
        s = [st_s[b] for b in chains]
        sbf = [x.astype(BF16) for x in s]
        for b in chains:
            upd = jnp.zeros_like(s[b])
            parts = []
            for hd in range(GLA_HEADS):
                parts.append(_dot(jnp.where(head_mask[hd], qe[b], 0.0).astype(BF16), sbf[b]))
                vh = v[b][:, hd * GLA_DV:(hd + 1) * GLA_DV].astype(BF16)
                upd = upd + _dot_tn(jnp.where(head_mask[hd], kt[b], 0.0).astype(BF16), vh)
            ob = o[b] + jnp.concatenate(parts, axis=1)
            b_hi, b_lo = _split_bf16(jnp.broadcast_to(bc[b][blk - 1:blk, :], (8, wk)))
            eighth = jnp.full((8, GLA_DV), 0.125, BF16)
            decay = jnp.exp(_dot_tn(b_hi, eighth) + _dot_tn(b_lo, eighth))
            st_s[b] = decay * s[b] + upd
            for hd in range(GLA_HEADS):
                oh = ob[:, hd * GLA_DV:(hd + 1) * GLA_DV]
                oh = oh * lax.rsqrt(jnp.mean(oh * oh, axis=1, keepdims=True) + RMS_EPS) * norm_ref[...]
                o_ref[b, pl.ds(r0, blk), hd * GLA_DV:(hd + 1) * GLA_DV] = oh
        return carry

    lax.fori_loop(0, tc // blk, block, 0, unroll=2 if tc // blk > 1 else 1)
    s_out[...] = st_s[...]


def _gla_call(h3, s0, gp):
    n, t, _ = h3.shape
    tc = min(t, 128)
    wk = GLA_HEADS * GLA_DK
    wv = GLA_HEADS * GLA_DV
    const = lambda shape: pl.BlockSpec(shape, lambda j: (0,) * len(shape))
    state = const((n, wk, GLA_DV))
    return pl.pallas_call(
        functools.partial(_gla_kernel, n=n, tc=tc),
        grid=(t // tc,),
        in_specs=[pl.BlockSpec((n, tc, 2 * wk), lambda j: (0, j, PROJ_OFF['d_q'] // (2 * wk))),
                  pl.BlockSpec((n, tc, wv), lambda j: (0, j, PROJ_OFF['d_v'] // wv)),
                  pl.BlockSpec((n, tc, LANE), lambda j: (0, j, OFF_MISC // LANE)),
                  state,
                  const((LANE, wk)), const((1, wk)), const((1, GLA_DV)),
                  const((wk, wv)), const((GLA_BLOCK, GLA_BLOCK * GLA_BLOCK))],
        out_specs=[pl.BlockSpec((n, tc, wv), lambda j: (0, j, 0)), state],
        out_shape=[jax.ShapeDtypeStruct((n, t, wv), F32),
                   jax.ShapeDtypeStruct((n, wk, GLA_DV), F32)],
        scratch_shapes=[pltpu.VMEM((n, wk, GLA_DV), F32)],
        compiler_params=_params("arbitrary"),
        name="gla_branch",
    )(h3, h3, h3, s0, gp['wg'], gp['bg'], gp['norm'], gp['hexp'], gp['gsum'])


def _gla_prepare(w_g2, b_g, norm):
    wk = GLA_HEADS * GLA_DK
    wg = jnp.zeros((LANE, wk), F32).at[MISC_G:MISC_G + GLA_GATE_RANK].set(w_g2).astype(BF16)
    ck = jnp.arange(wk)[:, None] // GLA_DK
    cv = jnp.arange(GLA_HEADS * GLA_DV)[None, :] // GLA_DV
    ij = jnp.arange(GLA_BLOCK * GLA_BLOCK)[None, :] // GLA_BLOCK
    return {'wg': wg, 'bg': b_g.reshape(1, wk).astype(F32),
            'norm': norm.reshape(1, GLA_DV).astype(F32),
            'hexp': (ck == cv).astype(BF16),
            'gsum': (jnp.arange(GLA_BLOCK)[:, None] == ij).astype(BF16)}


def _merge_kernel(x_ref, ya_ref, yb_ref, yc_ref, yd_ref, ga_ref, gb_ref, gc_ref, gd_ref, mg_ref,
                  wb_ref, wo_ref, lng_ref, lnb_ref, o_ref):
    mixed = None
    for br, (y_ref, g_ref) in enumerate(((ya_ref, ga_ref), (yb_ref, gb_ref), (yc_ref, gc_ref), (yd_ref, gd_ref))):
        act = (y_ref[...] * jax.nn.silu(g_ref[...])).astype(BF16)
        term = jax.nn.sigmoid(mg_ref[:, br * D_MODEL:(br + 1) * D_MODEL]) * _dot(act, wb_ref[br])
        mixed = term if mixed is None else mixed + term
    z = DN_ALPHA * x_ref[...] + _dot(mixed.astype(BF16), wo_ref[...])
    mu = jnp.mean(z, axis=1, keepdims=True)
    zc = z - mu
    var = jnp.mean(zc * zc, axis=1, keepdims=True)
    o_ref[...] = zc * lax.rsqrt(var + LN_EPS) * lng_ref[...] + lnb_ref[...]


def _merge_call(x, h, ya, yb, yc, yd, wb, wo, lng, lnb):
    rows = x.shape[0]
    tm = min(rows, 256)
    bw = BRANCH_W
    ybs = pl.BlockSpec((tm, bw), lambda i: (i, 0))
    gate = lambda name: pl.BlockSpec((tm, bw), lambda i: (i, PROJ_OFF[name] // bw))
    return pl.pallas_call(
        _merge_kernel,
        grid=(rows // tm,),
        in_specs=[pl.BlockSpec((tm, D_MODEL), lambda i: (i, 0)), ybs, ybs, ybs, ybs,
                  gate('a_gate'), gate('b_gate'), gate('c_gate'), gate('d_gate'),
                  pl.BlockSpec((tm, N_BRANCH * D_MODEL), lambda i: (i, 0)),
                  pl.BlockSpec((N_BRANCH, bw, D_MODEL), lambda i: (0, 0, 0)),
                  pl.BlockSpec((D_MODEL, D_MODEL), lambda i: (0, 0)),
                  pl.BlockSpec((1, D_MODEL), lambda i: (0, 0)),
                  pl.BlockSpec((1, D_MODEL), lambda i: (0, 0))],
        out_specs=pl.BlockSpec((tm, D_MODEL), lambda i: (i, 0)),
        out_shape=jax.ShapeDtypeStruct((rows, D_MODEL), F32),
        compiler_params=_params("parallel"),
        name="merge_norm",
    )(x, ya, yb, yc, yd, h, h, h, h, h, wb, wo, lng, lnb)


def _split_w_in(w_in):
    out, off = {}, 0
    for name, width in IN_LAYOUT:
        out[name] = w_in[:, off:off + width]
        off += width
    return out


def _regroup_kernel(w_ref, proj_ref, key_ref):
    cols = _split_w_in(w_ref[...])
    pad = jnp.zeros((w_ref.shape[0], MISC_PAD), F32)
    proj_ref[...] = jnp.concatenate([cols[name] for name in PROJ_ORDER] + [pad], axis=1).astype(BF16)
    k, v, ki = cols['b_k'], cols['b_v'], cols['b_ki']
    dh = DSA_HEAD_DIM
    key_ref[...] = jnp.concatenate([k[:, :dh], k[:, :dh], k[:, dh:], k[:, dh:],
                                    v[:, :dh], v[:, :dh], v[:, dh:], v[:, dh:], ki, ki], axis=1).astype(BF16)


def _regroup_weights(w_in):
    depth, rows, width = w_in.shape
    tr = 128
    return pl.pallas_call(
        _regroup_kernel,
        grid=(depth, rows // tr),
        in_specs=[pl.BlockSpec((None, tr, width), lambda l, i: (l, i, 0))],
        out_specs=[pl.BlockSpec((None, tr, PROJ_W), lambda l, i: (l, i, 0)),
                   pl.BlockSpec((None, tr, HK_W), lambda l, i: (l, i, 0))],
        out_shape=[jax.ShapeDtypeStruct((depth, rows, PROJ_W), BF16),
                   jax.ShapeDtypeStruct((depth, rows, HK_W), BF16)],
        compiler_params=_params("parallel", "parallel"),
        name="regroup_weights",
    )(w_in)


def _dup_keys(k, v, ki):
    return jnp.concatenate([k[:, :, 0], k[:, :, 0], k[:, :, 1], k[:, :, 1],
                            v[:, :, 0], v[:, :, 0], v[:, :, 1], v[:, :, 1], ki, ki], axis=-1).astype(BF16)


def _layer(x, lw, st):
    n, t, _ = x.shape
    x2 = x.reshape(n * t, D_MODEL)
    h = _matmul(x2, lw['w_proj'], F32, tn=PROJ_W // 9)
    hk_new = _matmul(x2, lw['w_key'], BF16, tn=HK_W).reshape(n, t, HK_W)
    if st is None:
        hk, n_keys, b_off = hk_new, t, 0
        s5r = jnp.zeros((n, 1, S5_LANES), F32)
        s5i = jnp.zeros((n, 1, S5_LANES), F32)
        gdn0 = jnp.zeros((n, GDN_HEADS, GDN_DK, GDN_DV), F32)
        conv0 = jnp.zeros((n, 8, GDN_CONV_CH), F32)
        gla0 = jnp.zeros((n, GLA_HEADS * GLA_DK, GLA_DV), F32)
    else:
        past = st['k'].shape[1]
        n_keys = past + t
        b_off = past // Q_BLOCK
        lp = -(-n_keys // KEY_TILE) * KEY_TILE
        hk = jnp.concatenate([_dup_keys(st['k'], st['v'], st['kidx']), hk_new,
                              jnp.zeros((n, lp - n_keys, HK_W), BF16)], axis=1)
        s5r = st['s5_re'].reshape(n, 1, S5_LANES)
        s5i = st['s5_im'].reshape(n, 1, S5_LANES)
        gdn0 = st['gdn']
        conv0 = jnp.concatenate([jnp.zeros((n, 8 - (CONV_W - 1), GDN_CONV_CH), F32), st['gdn_conv']], axis=1)
        gla0 = st['gla'].reshape(n, GLA_HEADS * GLA_DK, GLA_DV)
    topk = min(DSA_TOPK_MAX, n_keys // 4)

    ya, s5r_new, s5i_new = _s5_call(h, n, t, s5r, s5i, lw['s5'])
    yb = _dsa_call(h, hk, n, t, lw['bias'], lw['tril'], b_off=b_off, n_keys=n_keys, topk=topk)
    h3 = h.reshape(n, t, PROJ_W)
    yc, gdn_new, conv_new = _gdn_call(h3, gdn0, conv0, lw['gdn'])
    yd, gla_new = _gla_call(h3, gla0, lw['gla'])
    yc = yc.reshape(n * t, GDN_VW)
    yd = yd.reshape(n * t, GLA_HEADS * GLA_DV)
    y = _merge_call(x2, h, ya, yb, yc, yd, lw['w_branch'], lw['w_out'], lw['ln_g'], lw['ln_b'])

    def cols(name, width):
        return h[:, PROJ_OFF[name]:PROJ_OFF[name] + width].reshape(n, t, width)

    new = {'k': cols('b_k', DSA_KV_HEADS * DSA_HEAD_DIM).reshape(n, t, DSA_KV_HEADS, DSA_HEAD_DIM),
           'v': cols('b_v', DSA_KV_HEADS * DSA_HEAD_DIM).reshape(n, t, DSA_KV_HEADS, DSA_HEAD_DIM),
           'kidx': cols('b_ki', IDX_DIM),
           's5_re': s5r_new.reshape(n, S5_GROUPS, S5_STATE), 's5_im': s5i_new.reshape(n, S5_GROUPS, S5_STATE),
           'gdn': gdn_new, 'gdn_conv': conv_new[:, 8 - (CONV_W - 1):],
           'gla': gla_new.reshape(n, GLA_HEADS, GLA_DK, GLA_DV)}
    return y.reshape(n, t, D_MODEL), new


def kernel(x_prompt, x_sample, cache_k, cache_v, cache_kidx, state_s5_re, state_s5_im, state_gdn, state_gdn_conv, state_gla, w_in, w_branch, w_out, ln_g, ln_b, rel_bias, s5_a_re, s5_a_im, s5_log_dt, s5_b_re, s5_b_im, s5_c_re, s5_c_im, s5_d, s5_w_glu, gdn_conv, gdn_a_log, gdn_dt_bias, gdn_norm, gla_w_g2, gla_b_g, gla_norm):
    bias = _dsa_bias_tables(rel_bias)
    tril = (jnp.arange(KEY_BLOCK)[:, None] >= jnp.arange(KEY_BLOCK)[None, :]).astype(BF16)
    yp, ys = x_prompt, x_sample
    new_p, new_s = [], []
    w_proj, w_key = _regroup_weights(w_in)
    for l in range(DEPTH):
        lw = {'w_proj': w_proj[l], 'w_key': w_key[l], 'bias': bias, 'tril': tril,
              'w_branch': w_branch[l].astype(BF16), 'w_out': w_out[l].astype(BF16),
              'ln_g': ln_g[l].reshape(1, D_MODEL), 'ln_b': ln_b[l].reshape(1, D_MODEL),
              's5': _s5_prepare(s5_a_re[l], s5_a_im[l], s5_log_dt[l], s5_b_re[l], s5_b_im[l],
                                s5_c_re[l], s5_c_im[l], s5_d[l], s5_w_glu[l]),
              'gdn': {'convw': gdn_conv[l], 'alog': _lane_place(gdn_a_log[l], MISC_A),
                      'dtb': _lane_place(gdn_dt_bias[l], MISC_A), 'norm': gdn_norm[l].reshape(1, GDN_DV)},
              'gla': _gla_prepare(gla_w_g2[l], gla_b_g[l], gla_norm[l])}
        st = {'k': cache_k[l], 'v': cache_v[l], 'kidx': cache_kidx[l], 's5_re': state_s5_re[l],
              's5_im': state_s5_im[l], 'gdn': state_gdn[l], 'gdn_conv': state_gdn_conv[l], 'gla': state_gla[l]}
        yp, stp = _layer(yp, lw, None)
        ys, sts = _layer(ys, lw, st)
        new_p.append(stp)
        new_s.append(sts)
    names = ('k', 'v', 'kidx', 's5_re', 's5_im', 'gdn', 'gdn_conv', 'gla')
    stack = lambda states, name: jnp.stack([s[name] for s in states], axis=0)
    return ((yp, ys) + tuple(stack(new_p, nm) for nm in names) + tuple(stack(new_s, nm) for nm in names))
```

```python
import functools
import math

import jax
import jax.numpy as jnp
from jax import lax
from jax.experimental import pallas as pl
from jax.experimental.pallas import tpu as pltpu

F32 = jnp.float32
BF16 = jnp.bfloat16
HIGHEST = lax.Precision.HIGHEST

D_MODEL = 1024
DEPTH = 2
CHUNK = 64
N_BRANCH = 4
BRANCH_W = 512
S5_GROUP = 16
S5_GROUPS = BRANCH_W // S5_GROUP
S5_STATE = 64
S5_LANES = S5_GROUPS * S5_STATE
S5_SPLIT = 2
DSA_HEADS = 8
DSA_KV_HEADS = 2
DSA_HEAD_DIM = 64
IDX_HEADS = 4
IDX_DIM = 64
DSA_TOPK_MAX = 256
REL_BUCKETS = 32
REL_MAX_DIST = 128
GDN_HEADS = 4
GDN_DK = 128
GDN_DV = 128
GDN_QK = GDN_HEADS * GDN_DK
GDN_VW = GDN_HEADS * GDN_DV
CONV_W = 4
GDN_CONV_CH = 2 * GDN_QK + GDN_VW
CONV_LEAD = 8 - (CONV_W - 1)
GLA_HEADS = 4
GLA_DK = 64
GLA_DV = 128
GLA_GATE_RANK = 16
GLA_TAU = 16.0
GLA_BLOCK = 16
LN_EPS = 1e-5
RMS_EPS = 1e-6
DN_ALPHA = (2 * DEPTH) ** 0.25
LOG2E = math.log2(math.e)

LANE = 128
SUBLANE = 8
KEY_BLOCK = 128
KEY_SUB = 4
KEY_TILE = KEY_SUB * KEY_BLOCK
SOFTMAX_CEIL = 2.0 ** 120
SOFTMAX_FLOOR = 2.0 ** -60
FOLD_ROWS = 64
Q_BLOCK = 128
Q_TILE = 256
VMEM_LIMIT = 56 * 1024 * 1024

IN_LAYOUT = (
    ('a_u', BRANCH_W), ('a_gate', BRANCH_W),
    ('b_q', DSA_HEADS * DSA_HEAD_DIM), ('b_k', DSA_KV_HEADS * DSA_HEAD_DIM),
    ('b_v', DSA_KV_HEADS * DSA_HEAD_DIM), ('b_qi', IDX_HEADS * IDX_DIM), ('b_ki', IDX_DIM),
    ('b_wi', IDX_HEADS), ('b_gate', BRANCH_W),
    ('c_qkv', GDN_CONV_CH), ('c_beta', GDN_HEADS), ('c_a', GDN_HEADS), ('c_gate', BRANCH_W),
    ('d_q', GLA_HEADS * GLA_DK), ('d_k', GLA_HEADS * GLA_DK), ('d_v', GLA_HEADS * GLA_DV),
    ('d_g', GLA_GATE_RANK), ('d_gate', BRANCH_W),
    ('merge', N_BRANCH * D_MODEL),
)

PROJ_ORDER = ('merge', 'a_u', 'a_gate', 'b_q', 'b_qi', 'b_k', 'b_v', 'c_qkv', 'b_gate', 'c_gate',
              'd_v', 'd_gate', 'd_q', 'd_k', 'b_ki', 'b_wi', 'c_beta', 'c_a', 'd_g')
MISC_WI = IDX_DIM
MISC_BETA = MISC_WI + IDX_HEADS
MISC_A = MISC_BETA + GDN_HEADS
MISC_G = MISC_A + GDN_HEADS
MISC_PAD = LANE - (MISC_G + GLA_GATE_RANK)


def _proj_offsets():
    widths = dict(IN_LAYOUT)
    off, out = 0, {}
    for name in PROJ_ORDER:
        out[name] = off
        off += widths[name]
    return out, off + MISC_PAD


PROJ_OFF, PROJ_W = _proj_offsets()
OFF_MISC = PROJ_OFF['b_ki']
HK_W = 5 * LANE
HK_V = 2 * LANE
HK_KI = 4 * LANE


def _dot(a, b, precision=None):
    return jnp.dot(a, b, preferred_element_type=F32, precision=precision)


def _dot_nt(a, b, precision=None):
    return lax.dot_general(a, b, (((1,), (1,)), ((), ())), preferred_element_type=F32, precision=precision)


def _dot_tn(a, b, precision=None):
    return lax.dot_general(a, b, (((0,), (0,)), ((), ())), preferred_element_type=F32, precision=precision)


def _split_bf16(x):
    hi = x.astype(BF16)
    return hi, (x - hi.astype(F32)).astype(BF16)


def _dot3(a, b):
    a_hi, a_lo = _split_bf16(a)
    b_hi, b_lo = _split_bf16(b)
    return _dot(a_hi, b_hi) + (_dot(a_hi, b_lo) + _dot(a_lo, b_hi))


def _params(*sem):
    return pltpu.CompilerParams(dimension_semantics=sem, vmem_limit_bytes=VMEM_LIMIT)


def _softplus(x):
    return jnp.maximum(x, 0.0) + jnp.log1p(jnp.exp(-jnp.abs(x)))


def _iota(shape, dim):
    return lax.broadcasted_iota(jnp.int32, shape, dim)


def _mm_kernel(x_ref, w_ref, o_ref):
    o_ref[...] = _dot(x_ref[...].astype(BF16), w_ref[...]).astype(o_ref.dtype)


def _matmul(x, w, out_dtype, tn):
    rows, kdim = x.shape
    width = w.shape[1]
    tm = min(rows, 1024)
    return pl.pallas_call(
        _mm_kernel,
        grid=(rows // tm, width // tn),
        in_specs=[pl.BlockSpec((tm, kdim), lambda i, j: (i, 0)),
                  pl.BlockSpec((kdim, tn), lambda i, j: (0, j))],
        out_specs=pl.BlockSpec((tm, tn), lambda i, j: (i, j)),
        out_shape=jax.ShapeDtypeStruct((rows, width), out_dtype),
        compiler_params=_params("parallel", "parallel"),
        name="proj_matmul",
    )(x, w)


def _s5_kernel(u_ref, h0r_ref, h0i_ref, ar_ref, ai_ref, bb_ref, cc_ref, d_ref, wglu_ref,
               ya_ref, hr_out, hi_out, xr_s, xi_s, hr_s, hi_s, cr_s, ci_s, *, tc):
    @pl.when(pl.program_id(1) == 0)
    def _():
        cr_s[...] = h0r_ref[...]
        ci_s[...] = h0i_ref[...]

    u = u_ref[...]
    ub = u.astype(BF16)
    uw = BRANCH_W // S5_SPLIT
    sw = S5_LANES // S5_SPLIT
    for part in range(S5_SPLIT):
        x = _dot(ub[:, part * uw:(part + 1) * uw], bb_ref[part])
        xr_s[:, part * sw:(part + 1) * sw] = x[:, :sw]
        xi_s[:, part * sw:(part + 1) * sw] = x[:, sw:]

    ar = ar_ref[...]
    ai = ai_ref[...]

    def step(t, carry):
        hr, hi = carry
        nhr = ar * hr - ai * hi + xr_s[pl.ds(t, 1), :]
        nhi = ar * hi + ai * hr + xi_s[pl.ds(t, 1), :]
        hr_s[pl.ds(t, 1), :] = nhr
        hi_s[pl.ds(t, 1), :] = nhi
        return nhr, nhi

    hr, hi = lax.fori_loop(0, tc, step, (cr_s[...], ci_s[...]), unroll=8)
    cr_s[...] = hr
    ci_s[...] = hi
    hr_out[...] = hr
    hi_out[...] = hi
    ys = []
    for part in range(S5_SPLIT):
        hcat = jnp.concatenate([hr_s[:, part * sw:(part + 1) * sw], hi_s[:, part * sw:(part + 1) * sw]], axis=1)
        ys.append(_dot(hcat.astype(BF16), cc_ref[part]))
    y = jnp.concatenate(ys, axis=1) + d_ref[...] * u
    y = jax.nn.gelu(y)
    g = _dot(y.astype(BF16), wglu_ref[...])
    ya_ref[...] = g[:, :BRANCH_W] * jax.nn.sigmoid(g[:, BRANCH_W:])


def _s5_call(h, n, t, h0r, h0i, sp):
    tc = min(t, 256)
    nc = t // tc
    uw = BRANCH_W // S5_SPLIT
    sw = S5_LANES // S5_SPLIT
    const = lambda shape: pl.BlockSpec(shape, lambda i, c: (0,) * len(shape))
    state = pl.BlockSpec((None, 1, S5_LANES), lambda i, c: (i, 0, 0))
    return pl.pallas_call(
        functools.partial(_s5_kernel, tc=tc),
        grid=(n, nc),
        in_specs=[pl.BlockSpec((tc, BRANCH_W), lambda i, c: (i * nc + c, PROJ_OFF['a_u'] // BRANCH_W)),
                  state, state,
                  const((1, S5_LANES)), const((1, S5_LANES)),
                  const((S5_SPLIT, uw, 2 * sw)), const((S5_SPLIT, 2 * sw, uw)),
                  const((1, BRANCH_W)), const((BRANCH_W, 2 * BRANCH_W))],
        out_specs=[pl.BlockSpec((tc, BRANCH_W), lambda i, c: (i * nc + c, 0)), state, state],
        out_shape=[jax.ShapeDtypeStruct((n * t, BRANCH_W), F32),
                   jax.ShapeDtypeStruct((n, 1, S5_LANES), F32),
                   jax.ShapeDtypeStruct((n, 1, S5_LANES), F32)],
        scratch_shapes=[pltpu.VMEM((tc, S5_LANES), F32), pltpu.VMEM((tc, S5_LANES), F32),
                        pltpu.VMEM((tc, S5_LANES), F32), pltpu.VMEM((tc, S5_LANES), F32),
                        pltpu.VMEM((1, S5_LANES), F32), pltpu.VMEM((1, S5_LANES), F32)],
        compiler_params=_params("parallel", "arbitrary"),
        name="s5_branch",
    )(h, h0r, h0i, sp['ar'], sp['ai'], sp['bb'], sp['cc'], sp['d'], sp['wglu'])


def _s5_prepare(a_re, a_im, log_dt, b_re, b_im, c_re, c_im, d, w_glu):
    lam_r = jnp.minimum(a_re, -1e-4)
    lam_i = a_im
    dt = jnp.exp(log_dt)[:, None]
    mag = jnp.exp(lam_r * dt)
    ar = mag * jnp.cos(lam_i * dt)
    ai = mag * jnp.sin(lam_i * dt)
    den = lam_r * lam_r + lam_i * lam_i
    nr = ar - 1.0
    cr = (nr * lam_r + ai * lam_i) / den
    ci = (ai * lam_r - nr * lam_i) / den
    bb_r = cr[..., None] * b_re - ci[..., None] * b_im
    bb_i = cr[..., None] * b_im + ci[..., None] * b_re
    eye = jnp.eye(S5_GROUPS, dtype=F32)

    def blockdiag_in(bb):
        m = eye[:, None, :, None] * jnp.swapaxes(bb, 1, 2)[:, :, None, :]
        return m.reshape(BRANCH_W, S5_LANES)

    def blockdiag_out(c):
        m = eye[:, None, :, None] * jnp.swapaxes(c, 1, 2)[:, :, None, :]
        return m.reshape(S5_LANES, BRANCH_W)

    uw = BRANCH_W // S5_SPLIT
    sw = S5_LANES // S5_SPLIT
    bbr, bbi = blockdiag_in(bb_r), blockdiag_in(bb_i)
    ccr, cci = blockdiag_out(c_re), blockdiag_out(c_im)
    bb = jnp.stack([jnp.concatenate([m[p * uw:(p + 1) * uw, p * sw:(p + 1) * sw] for m in (bbr, bbi)], axis=1)
                    for p in range(S5_SPLIT)]).astype(BF16)
    cc = jnp.stack([jnp.concatenate([ccr[p * sw:(p + 1) * sw, p * uw:(p + 1) * uw],
                                     -cci[p * sw:(p + 1) * sw, p * uw:(p + 1) * uw]], axis=0)
                    for p in range(S5_SPLIT)]).astype(BF16)
    return {'ar': ar.reshape(1, S5_LANES), 'ai': ai.reshape(1, S5_LANES), 'bb': bb, 'cc': cc,
            'd': d.reshape(1, BRANCH_W), 'wglu': w_glu.astype(BF16)}


def _sortable(x):
    bits = lax.bitcast_convert_type(x, jnp.int32)
    key = bits ^ ((bits >> 31) & jnp.int32(0x7FFFFFFF))
    return jnp.where(key == -1, 0, key)


def _dsa_kernel(q_ref, qi_ref, misc_ref, hk_ref, bias_ref, tril_ref, o_ref, sc_s, pk_s, acc_s, far_s, *, qb, b_off, n_keys, topk):
    nsub = max(qb // Q_BLOCK, 1)
    qw = qb // nsub
    b = pl.program_id(1) * nsub + b_off
    nkt = (b + nsub - 1 + KEY_SUB) // KEY_SUB
    fold = KEY_TILE // FOLD_ROWS
    key_row = _iota((KEY_TILE, qb), 0)
    qpos = b * Q_BLOCK + _iota((1, qb), 1)
    limit = jnp.minimum(((qpos >> 6) + 1) * CHUNK, n_keys)
    low_half = _iota((qb, LANE), 1) < DSA_HEAD_DIM

    def key_rows(ref_cols, j):
        return hk_ref[pl.ds(pl.multiple_of(j * KEY_TILE, KEY_TILE), KEY_TILE), ref_cols:ref_cols + LANE]

    def half_masked(x, hd):
        pair = x[:, (hd // 2) * LANE:(hd // 2 + 1) * LANE]
        return jnp.where(low_half == (hd % 2 == 0), pair, 0.0).astype(BF16)

    misc = misc_ref[...]
    pick = jnp.where(_iota((SUBLANE, LANE), 1) == MISC_WI + _iota((SUBLANE, LANE), 0), 1.0, 0.0)
    wi_t = _dot_nt(pick, misc, precision=HIGHEST) * (IDX_DIM ** -0.5 * IDX_HEADS ** -0.5)
    qi = qi_ref[...]
    qi_stack = jnp.concatenate([half_masked(qi, hd) for hd in range(IDX_HEADS)], axis=0)

    def pack16(v):
        half = KEY_TILE // 2
        return (v[0:half] & 0xFFFF) | (v[half:] << 16)

    def score_block(j, carry):
        z = _dot_nt(key_rows(HK_KI, j), qi_stack)
        sc = jnp.zeros((KEY_TILE, qb), F32)
        for hd in range(IDX_HEADS):
            sc = sc + jnp.maximum(z[:, hd * qb:(hd + 1) * qb], 0.0) * wi_t[hd:hd + 1, :]
        sc = jnp.where(j * KEY_TILE + key_row < limit, sc, -jnp.inf)
        key = _sortable(sc)
        sc_s[j] = key
        pk_s[j] = pack16(key >> 16)
        return carry

    lax.fori_loop(0, nkt, score_block, 0)

    def count_ge(cand):
        def body(j, acc):
            return acc + jnp.where(sc_s[j] >= cand, 1.0, 0.0).reshape(fold, FOLD_ROWS, qb).sum(axis=0)
        acc = lax.fori_loop(0, nkt, body, jnp.zeros((FOLD_ROWS, qb), F32))
        return jnp.sum(acc, axis=0, keepdims=True)

    one16 = jnp.ones((KEY_TILE, qb), jnp.int16)
    zero16 = jnp.zeros((KEY_TILE, qb), jnp.int16)

    def count16_ge(cand):
        word = jnp.broadcast_to((cand & 0xFFFF) | (cand << 16), (KEY_TILE // 2, qb))
        c16 = pltpu.bitcast(word, jnp.int16)

        def body(j, acc):
            ind = jnp.where(pltpu.bitcast(pk_s[j], jnp.int16) >= c16, one16, zero16)
            for i in range(fold):
                acc = acc + ind[i * FOLD_ROWS:(i + 1) * FOLD_ROWS]
            return acc
        acc = lax.fori_loop(0, nkt, body, jnp.zeros((FOLD_ROWS, qb), jnp.int16))
        return jnp.sum(acc.astype(F32), axis=0, keepdims=True)

    def radix_search(start, want):
        def bit_step(i, res):
            cand = res + (jnp.int32(1) << (14 - i))
            return jnp.where(count16_ge(cand) >= want, cand, res)
        return lax.fori_loop(0, 15, bit_step, start)

    kf = float(topk)
    top = radix_search(jnp.where(count16_ge(jnp.zeros((1, qb), jnp.int32)) >= kf, 0, -32768), kf)
    int16_max = jnp.iinfo(jnp.int16).max
    above = jnp.where(top == int16_max, 0.0, count16_ge(jnp.minimum(top + 1, int16_max)))
    want_low = kf - above

    def pack_low(j, carry):
        key = sc_s[j]
        pk_s[j] = pack16(jnp.where((key >> 16) == top, (key >> 1) & 0x7FFF, -1))
        return carry

    lax.fori_loop(0, nkt, pack_low, 0)
    low = radix_search(jnp.zeros((1, qb), jnp.int32), want_low)
    base = (top << 16) | (low << 1)
    thr = jnp.where(count_ge(base | 1) >= kf, base | 1, base)
    need = kf - count_ge(thr + 1)

    tril = tril_ref[...]

    def select_block(j, ties):
        keys = sc_s[j]
        eq = keys == thr
        eq_b = jnp.where(eq, 1.0, 0.0).astype(BF16)
        parts, carry = [], ties
        for s in range(KEY_SUB):
            part = _dot(tril, eq_b[s * KEY_BLOCK:(s + 1) * KEY_BLOCK]) + carry
            parts.append(part)
            carry = part[KEY_BLOCK - 1:KEY_BLOCK, :]
        prefix = jnp.concatenate(parts, axis=0)
        sel = ((keys > thr) | (eq & (prefix <= need))) & (j * KEY_TILE + key_row < limit)
        sc_s[j] = lax.bitcast_convert_type(jnp.where(sel, 0.0, -jnp.inf), jnp.int32)
        return prefix[KEY_TILE - 1:KEY_TILE, :]

    lax.fori_loop(0, nkt, select_block, jnp.zeros((1, qb), F32))

    q = q_ref[...] * (DSA_HEAD_DIM ** -0.5 * LOG2E)
    rep = DSA_HEADS // DSA_KV_HEADS
    eye = jnp.where(_iota((LANE, LANE), 0) == _iota((LANE, LANE), 1), 1.0, 0.0)
    value_lane = _iota((KEY_TILE, LANE), 1) < DSA_HEAD_DIM
    n_pairs = DSA_HEADS // 2
    q_pair = [jnp.concatenate([half_masked(q, 2 * c), half_masked(q, 2 * c + 1)], axis=0) for c in range(n_pairs)]
    far_bias = [jnp.concatenate([bias_ref[hd, 2, 0:1, 0:qw]] * nsub, axis=1)
                for hd in range(DSA_HEADS)]
    n_far = jnp.maximum((b - 1) // KEY_SUB, 0)

    def tile_operands(j):
        mask = lax.bitcast_convert_type(sc_s[j], F32)
        kd = [key_rows(grp * LANE, j) for grp in range(DSA_KV_HEADS)]
        zs = [_dot_nt(kd[(2 * c) // rep], q_pair[c]) for c in range(n_pairs)]
        vd = [jnp.where(value_lane, key_rows(HK_V + grp * LANE, j), 1.0).astype(BF16)
              for grp in range(DSA_KV_HEADS)]
        return mask, zs, vd

    def near_bias(hd, j):
        cols = [jnp.concatenate([bias_ref[hd, jnp.clip(b + u - (KEY_SUB * j + s), 0, 2), :, 0:qw]
                                 for s in range(KEY_SUB)], axis=0) for u in range(nsub)]
        return jnp.concatenate(cols, axis=1)

    def plain_tile(j, carry, far):
        mask, zs, vd = tile_operands(j)
        ps = []
        for hd in range(DSA_HEADS):
            r = hd % 2
            zh = zs[hd // 2][:, r * qb:(r + 1) * qb] + mask
            if not far:
                zh = zh + near_bias(hd, j)
            ps.append(jnp.exp2(zh).astype(BF16))
        acc = far_s if far else acc_s
        for c in range(n_pairs):
            acc[c] += _dot_tn(vd[(2 * c) // rep], jnp.concatenate(ps[2 * c:2 * c + 2], axis=1))
        return carry

    acc_s[...] = jnp.zeros_like(acc_s)
    far_s[...] = jnp.zeros_like(far_s)
    lax.fori_loop(0, n_far, lambda j, c: plain_tile(j, c, True), 0)
    lax.fori_loop(n_far, nkt, lambda j, c: plain_tile(j, c, False), 0)
    healthy = jnp.ones((1, 1), F32)
    for c in range(n_pairs):
        scale = jnp.concatenate([jnp.exp2(far_bias[2 * c]), jnp.exp2(far_bias[2 * c + 1])], axis=1)
        total = far_s[c] * scale + acc_s[c]
        acc_s[c] = total
        norm = total[DSA_HEAD_DIM:DSA_HEAD_DIM + 1, :]
        ok = (jnp.abs(total) < SOFTMAX_CEIL) & (norm > SOFTMAX_FLOOR)
        healthy = jnp.minimum(healthy, jnp.min(jnp.where(ok, 1.0, 0.0), keepdims=True))

    def attend(j, ms, far):
        mask, zs, vd = tile_operands(j)
        new_ms, ps, alphas = [], [], []
        for hd in range(DSA_HEADS):
            r = hd % 2
            zh = zs[hd // 2][:, r * qb:(r + 1) * qb] + mask
            if not far:
                zh = zh + near_bias(hd, j)
            tile_max = jnp.max(zh.reshape(fold, FOLD_ROWS, qb).max(axis=0), axis=0, keepdims=True)
            shift = far_bias[hd] if far else 0.0
            m_new = jnp.maximum(ms[hd], tile_max + shift)
            ps.append(jnp.exp2(zh - (m_new - shift)).astype(BF16))
            alphas.append(jnp.exp2(ms[hd] - m_new))
            new_ms.append(m_new)
        for c in range(n_pairs):
            acc_s[c] = (acc_s[c] * jnp.concatenate(alphas[2 * c:2 * c + 2], axis=1)
                        + _dot_tn(vd[(2 * c) // rep], jnp.concatenate(ps[2 * c:2 * c + 2], axis=1)))
        return tuple(new_ms)

    @pl.when(healthy[0, 0] < 0.5)
    def _():
        acc_s[...] = jnp.zeros_like(acc_s)
        ms = tuple(jnp.full((1, qb), -1e30, F32) for _ in range(DSA_HEADS))
        ms = lax.fori_loop(0, n_far, lambda j, m: attend(j, m, True), ms)
        lax.fori_loop(n_far, nkt, lambda j, m: attend(j, m, False), ms)

    for c in range(n_pairs):
        halves = []
        for r in range(2):
            blk = acc_s[c, :, r * qb:(r + 1) * qb]
            halves.append(blk[0:DSA_HEAD_DIM] / blk[DSA_HEAD_DIM:DSA_HEAD_DIM + 1])
        o_t = jnp.concatenate(halves, axis=0)
        o_ref[:, c * LANE:(c + 1) * LANE] = _dot_tn(o_t, eye, precision=HIGHEST)


def _dsa_call(h, hk, n, t, bias, tril, *, b_off, n_keys, topk):
    qb = min(t, Q_TILE)
    nqb = t // qb
    lp = hk.shape[1]
    wq = DSA_HEADS * DSA_HEAD_DIM
    wqi = IDX_HEADS * IDX_DIM
    return pl.pallas_call(
        functools.partial(_dsa_kernel, qb=qb, b_off=b_off, n_keys=n_keys, topk=topk),
        grid=(n, nqb),
        in_specs=[pl.BlockSpec((qb, wq), lambda i, c: (i * nqb + c, PROJ_OFF['b_q'] // wq)),
                  pl.BlockSpec((qb, wqi), lambda i, c: (i * nqb + c, PROJ_OFF['b_qi'] // wqi)),
                  pl.BlockSpec((qb, LANE), lambda i, c: (i * nqb + c, OFF_MISC // LANE)),
                  pl.BlockSpec((None, lp, HK_W), lambda i, c: (i, 0, 0)),
                  pl.BlockSpec((DSA_HEADS, 3, KEY_BLOCK, Q_BLOCK), lambda i, c: (0, 0, 0, 0)),
                  pl.BlockSpec((KEY_BLOCK, KEY_BLOCK), lambda i, c: (0, 0))],
        out_specs=pl.BlockSpec((qb, wq), lambda i, c: (i * nqb + c, 0)),
        out_shape=jax.ShapeDtypeStruct((n * t, wq), F32),
        scratch_shapes=[pltpu.VMEM((lp // KEY_TILE, KEY_TILE, qb), jnp.int32),
                        pltpu.VMEM((lp // KEY_TILE, KEY_TILE // 2, qb), jnp.int32),
                        pltpu.VMEM((DSA_HEADS // 2, LANE, 2 * qb), F32),
                        pltpu.VMEM((DSA_HEADS // 2, LANE, 2 * qb), F32)],
        compiler_params=_params("parallel", "arbitrary"),
        name="dsa_branch",
    )(h, h, h, hk, bias, tril)


def _t5_bucket(rel):
    nb = REL_BUCKETS // 2
    max_exact = nb // 2
    ret = jnp.where(rel > 0, nb, 0)
    dist = jnp.abs(rel)
    distf = jnp.maximum(dist, 1).astype(F32)
    large = max_exact + (jnp.log(distf / max_exact) / math.log(REL_MAX_DIST / max_exact)
                         * (nb - max_exact)).astype(jnp.int32)
    large = jnp.minimum(large, nb - 1)
    return ret + jnp.where(dist < max_exact, dist, large)


def _dsa_bias_tables(rel_bias):
    lc = jnp.arange(KEY_BLOCK, dtype=jnp.int32)[:, None]
    qr = jnp.arange(Q_BLOCK, dtype=jnp.int32)[None, :]
    rel = jnp.stack([lc - qr - back * KEY_BLOCK for back in range(3)], axis=0)
    onehot = jax.nn.one_hot(_t5_bucket(rel), REL_BUCKETS, dtype=F32)
    bias = jnp.einsum('klqb,bh->hklq', onehot, rel_bias.astype(F32), precision=HIGHEST)
    return bias * LOG2E


def _gdn_kernel(raw_ref, misc_ref, s0_ref, conv0_ref, convw_ref, alog_ref, dtb_ref, norm_ref,
                o_ref, s_out, conv_out, xp_s, st_s, *, n, c):
    @pl.when(pl.program_id(0) == 0)
    def _():
        xp_s[:, 0:SUBLANE, :] = conv0_ref[...]
        st_s[...] = s0_ref[...]

    w = convw_ref[...]
    row = _iota((c, c), 0)
    col = _iota((c, c), 1)
    incl = row >= col
    strict = row > col
    eye = row == col
    tril = jnp.where(incl, 1.0, 0.0)
    eye_f = jnp.where(eye, 1.0, 0.0)
    chains = [(b, hd) for b in range(n) for hd in range(GDN_HEADS)]

    qkv, beta_all, gc_all = [], [], []
    for b in range(n):
        raw = raw_ref[b]
        xp_s[b, SUBLANE:SUBLANE + c, :] = raw
        y = raw * w[CONV_W - 1:CONV_W, :]
        for i in range(CONV_W - 1):
            y = y + xp_s[b, CONV_LEAD + i:CONV_LEAD + i + c, :] * w[i:i + 1, :]
        tail = xp_s[b, c:c + SUBLANE, :]
        xp_s[b, 0:SUBLANE, :] = tail
        conv_out[b] = tail
        qkv.append(jax.nn.silu(y))
        misc = misc_ref[b]
        beta_all.append(jax.nn.sigmoid(misc))
        g_all = -jnp.exp(alog_ref[...]) * _softplus(misc + dtb_ref[...])
        gc_all.append(_dot(tril, g_all, precision=HIGHEST))

    qs, ks, vs, betas, gcs, dec_s, dec_i = [], [], [], [], [], [], []
    for b, hd in chains:
        qh = qkv[b][:, hd * GDN_DK:(hd + 1) * GDN_DK]
        kh = qkv[b][:, GDN_QK + hd * GDN_DK:GDN_QK + (hd + 1) * GDN_DK]
        vs.append(qkv[b][:, 2 * GDN_QK + hd * GDN_DV:2 * GDN_QK + (hd + 1) * GDN_DV])
        qs.append(qh * lax.rsqrt(jnp.sum(qh * qh, axis=1, keepdims=True) + RMS_EPS) * (GDN_DK ** -0.5))
        ks.append(kh * lax.rsqrt(jnp.sum(kh * kh, axis=1, keepdims=True) + RMS_EPS))
        betas.append(beta_all[b][:, MISC_BETA + hd:MISC_BETA + hd + 1])
        gc = gc_all[b][:, MISC_A + hd:MISC_A + hd + 1]
        gcs.append(gc)
        gc_b = jnp.broadcast_to(gc, (c, c))
        gc_row = jnp.sum(jnp.where(eye, gc_b, 0.0), axis=0, keepdims=True)
        diff = gc_b - gc_row
        dec = jnp.exp(diff)
        dec_s.append(jnp.where(strict, dec, 0.0))
        dec_i.append(jnp.where(incl, dec, 0.0))
    kb = [k.astype(BF16) for k in ks]
    qb = [q.astype(BF16) for q in qs]
    pw = [-(beta * _dot_nt(k, k) * d) for beta, k, d in zip(betas, kb, dec_s)]
    inv = [eye_f + p for p in pw]
    for _ in range(int(math.log2(c)) - 1):
        pw = [_dot3(p, p) for p in pw]
        inv = [i + _dot3(i, p) for i, p in zip(inv, pw)]
    egc = [jnp.exp(gc) for gc in gcs]
    sol = [_dot3(i, jnp.concatenate([(beta * e) * k, beta * v], axis=1))
           for i, beta, e, k, v in zip(inv, betas, egc, ks, vs)]
    s_old = [st_s[b, hd] for b, hd in chains]
    sb = [s.astype(BF16) for s in s_old]
    vb = [(x[:, GDN_DK:] - _dot(x[:, :GDN_DK].astype(BF16), s)).astype(BF16) for x, s in zip(sol, sb)]
    attn = [(_dot_nt(q, k) * d).astype(BF16) for q, k, d in zip(qb, kb, dec_i)]
    outs = [e * _dot(q, s) + _dot(a, v) for e, q, s, a, v in zip(egc, qb, sb, attn, vb)]
    for (b, hd), gc, k, v, s, o in zip(chains, gcs, ks, vb, s_old, outs):
        g_last = gc[c - 1:c, :]
        st_s[b, hd] = jnp.exp(g_last) * s + _dot_tn((k * jnp.exp(g_last - gc)).astype(BF16), v)
        o = o * lax.rsqrt(jnp.mean(o * o, axis=1, keepdims=True) + RMS_EPS) * norm_ref[...]
        o_ref[b, :, hd * GDN_DV:(hd + 1) * GDN_DV] = o
    s_out[...] = st_s[...]


def _gdn_call(h3, s0, conv0, gp):
    n, t, _ = h3.shape
    c = min(t, CHUNK)
    const = lambda shape: pl.BlockSpec(shape, lambda j: (0,) * len(shape))
    state = const((n, GDN_HEADS, GDN_DK, GDN_DV))
    conv = const((n, SUBLANE, GDN_CONV_CH))
    return pl.pallas_call(
        functools.partial(_gdn_kernel, n=n, c=c),
        grid=(t // c,),
        in_specs=[pl.BlockSpec((n, c, GDN_CONV_CH), lambda j: (0, j, PROJ_OFF['c_qkv'] // GDN_CONV_CH)),
                  pl.BlockSpec((n, c, LANE), lambda j: (0, j, OFF_MISC // LANE)),
                  state, conv,
                  const((CONV_W, GDN_CONV_CH)), const((1, LANE)), const((1, LANE)), const((1, GDN_DV))],
        out_specs=[pl.BlockSpec((n, c, GDN_VW), lambda j: (0, j, 0)), state, conv],
        out_shape=[jax.ShapeDtypeStruct((n, t, GDN_VW), F32),
                   jax.ShapeDtypeStruct((n, GDN_HEADS, GDN_DK, GDN_DV), F32),
                   jax.ShapeDtypeStruct((n, SUBLANE, GDN_CONV_CH), F32)],
        scratch_shapes=[pltpu.VMEM((n, c + SUBLANE, GDN_CONV_CH), F32),
                        pltpu.VMEM((n, GDN_HEADS, GDN_DK, GDN_DV), F32)],
        compiler_params=_params("arbitrary"),
        name="gdn_branch",
    )(h3, h3, s0, conv0, gp['convw'], gp['alog'], gp['dtb'], gp['norm'])


def _lane_place(v, off):
    return jnp.zeros((1, LANE), F32).at[0, off:off + v.shape[0]].set(v.astype(F32))


def _gla_kernel(qk_ref, v_ref, misc_ref, s0_ref, wg_ref, bg_ref, norm_ref, hexp_ref, gsum_ref,
                o_ref, s_out, st_s, *, n, tc):
    blk = GLA_BLOCK
    wk = GLA_HEADS * GLA_DK

    @pl.when(pl.program_id(0) == 0)
    def _():
        st_s[...] = s0_ref[...]

    row = _iota((blk, blk), 0)
    col = _iota((blk, blk), 1)
    tril = jnp.where(row >= col, 1.0, 0.0)
    jrow = _iota((blk, wk), 0)
    lane_k = _iota((blk, wk), 1)
    head_mask = [(lane_k >= hd * GLA_DK) & (lane_k < (hd + 1) * GLA_DK) for hd in range(GLA_HEADS)]
    chains = range(n)

    def block(sb, carry):
        r0 = pl.multiple_of(sb * blk, blk)
        qk = [qk_ref[b, pl.ds(r0, blk), :] for b in chains]
        q = [x[:, :wk] * (GLA_DK ** -0.5) for x in qk]
        k = [x[:, wk:] for x in qk]
        v = [v_ref[b, pl.ds(r0, blk), :] for b in chains]
        mb = [misc_ref[b, pl.ds(r0, blk), :].astype(BF16) for b in chains]
        lg = [jax.nn.log_sigmoid(_dot(m, wg_ref[...]) + bg_ref[...]) * (1.0 / GLA_TAU) for m in mb]
        bc = [_dot(tril, x, precision=HIGHEST) for x in lg]
        dmat = []
        for b in chains:
            slabs = []
            for i in range(blk):
                keep = jrow <= i
                dec = jnp.where(keep, jnp.exp(bc[b][i:i + 1, :] - bc[b]), 0.0)
                slabs.append(dec * q[b][i:i + 1, :] * k[b])
            dmat.append(jnp.concatenate(slabs, axis=0).astype(BF16))
        a_exp = [_dot(d, hexp_ref[...]) for d in dmat]
        prod = [a * jnp.concatenate([x] * blk, axis=0) for a, x in zip(a_exp, v)]
        o = [_dot(gsum_ref[...], p.astype(BF16)) for p in prod]
        qe = [x * jnp.exp(c) for x, c in zip(q, bc)]
        kt = [x * jnp.exp(c[blk - 1:blk, :] - c) for x, c in zip(k, bc)]
        s = [st_s[b] for b in chains]
        sbf = [x.astype(BF16) for x in s]
        for b in chains:
            upd = jnp.zeros_like(s[b])
            parts = []
            for hd in range(GLA_HEADS):
                parts.append(_dot_nt(jnp.where(head_mask[hd], qe[b], 0.0).astype(BF16), sbf[b]))
                vh = v[b][:, hd * GLA_DV:(hd + 1) * GLA_DV].astype(BF16)
                upd = upd + _dot_tn(vh, jnp.where(head_mask[hd], kt[b], 0.0).astype(BF16))
            ob = o[b] + jnp.concatenate(parts, axis=1)
            st_s[b] = jnp.exp(bc[b][blk - 1:blk, :]) * s[b] + upd
            for hd in range(GLA_HEADS):
                oh = ob[:, hd * GLA_DV:(hd + 1) * GLA_DV]
                oh = oh * lax.rsqrt(jnp.mean(oh * oh, axis=1, keepdims=True) + RMS_EPS) * norm_ref[...]
                o_ref[b, pl.ds(r0, blk), hd * GLA_DV:(hd + 1) * GLA_DV] = oh
        return carry

    lax.fori_loop(0, tc // blk, block, 0, unroll=2 if tc // blk > 1 else 1)
    s_out[...] = st_s[...]


def _gla_call(h3, s0, gp):
    n, t, _ = h3.shape
    tc = min(t, 128)
    wk = GLA_HEADS * GLA_DK
    wv = GLA_HEADS * GLA_DV
    const = lambda shape: pl.BlockSpec(shape, lambda j: (0,) * len(shape))
    state = const((n, GLA_DV, wk))
    return pl.pallas_call(
        functools.partial(_gla_kernel, n=n, tc=tc),
        grid=(t // tc,),
        in_specs=[pl.BlockSpec((n, tc, 2 * wk), lambda j: (0, j, PROJ_OFF['d_q'] // (2 * wk))),
                  pl.BlockSpec((n, tc, wv), lambda j: (0, j, PROJ_OFF['d_v'] // wv)),
                  pl.BlockSpec((n, tc, LANE), lambda j: (0, j, OFF_MISC // LANE)),
                  state,
                  const((LANE, wk)), const((1, wk)), const((1, GLA_DV)),
                  const((wk, wv)), const((GLA_BLOCK, GLA_BLOCK * GLA_BLOCK))],
        out_specs=[pl.BlockSpec((n, tc, wv), lambda j: (0, j, 0)), state],
        out_shape=[jax.ShapeDtypeStruct((n, t, wv), F32),
                   jax.ShapeDtypeStruct((n, GLA_DV, wk), F32)],
        scratch_shapes=[pltpu.VMEM((n, GLA_DV, wk), F32)],
        compiler_params=_params("arbitrary"),
        name="gla_branch",
    )(h3, h3, h3, s0, gp['wg'], gp['bg'], gp['norm'], gp['hexp'], gp['gsum'])


def _gla_prepare(w_g2, b_g, norm):
    wk = GLA_HEADS * GLA_DK
    wg = jnp.zeros((LANE, wk), F32).at[MISC_G:MISC_G + GLA_GATE_RANK].set(w_g2).astype(BF16)
    ck = jnp.arange(wk)[:, None] // GLA_DK
    cv = jnp.arange(GLA_HEADS * GLA_DV)[None, :] // GLA_DV
    ij = jnp.arange(GLA_BLOCK * GLA_BLOCK)[None, :] // GLA_BLOCK
    return {'wg': wg, 'bg': b_g.reshape(1, wk).astype(F32),
            'norm': norm.reshape(1, GLA_DV).astype(F32),
            'hexp': (ck == cv).astype(BF16),
            'gsum': (jnp.arange(GLA_BLOCK)[:, None] == ij).astype(BF16)}


def _merge_kernel(x_ref, ya_ref, yb_ref, yc_ref, yd_ref, ga_ref, gb_ref, gc_ref, gd_ref, mg_ref,
                  wb_ref, wo_ref, lng_ref, lnb_ref, o_ref):
    mixed = None
    for br, (y_ref, g_ref) in enumerate(((ya_ref, ga_ref), (yb_ref, gb_ref), (yc_ref, gc_ref), (yd_ref, gd_ref))):
        act = (y_ref[...] * jax.nn.silu(g_ref[...])).astype(BF16)
        term = jax.nn.sigmoid(mg_ref[:, br * D_MODEL:(br + 1) * D_MODEL]) * _dot(act, wb_ref[br])
        mixed = term if mixed is None else mixed + term
    z = DN_ALPHA * x_ref[...] + _dot(mixed.astype(BF16), wo_ref[...])
    mu = jnp.mean(z, axis=1, keepdims=True)
    zc = z - mu
    var = jnp.mean(zc * zc, axis=1, keepdims=True)
    o_ref[...] = zc * lax.rsqrt(var + LN_EPS) * lng_ref[...] + lnb_ref[...]


def _merge_call(x, h, ya, yb, yc, yd, wb, wo, lng, lnb):
    rows = x.shape[0]
    tm = min(rows, 256)
    bw = BRANCH_W
    ybs = pl.BlockSpec((tm, bw), lambda i: (i, 0))
    gate = lambda name: pl.BlockSpec((tm, bw), lambda i: (i, PROJ_OFF[name] // bw))
    return pl.pallas_call(
        _merge_kernel,
        grid=(rows // tm,),
        in_specs=[pl.BlockSpec((tm, D_MODEL), lambda i: (i, 0)), ybs, ybs, ybs, ybs,
                  gate('a_gate'), gate('b_gate'), gate('c_gate'), gate('d_gate'),
                  pl.BlockSpec((tm, N_BRANCH * D_MODEL), lambda i: (i, 0)),
                  pl.BlockSpec((N_BRANCH, bw, D_MODEL), lambda i: (0, 0, 0)),
                  pl.BlockSpec((D_MODEL, D_MODEL), lambda i: (0, 0)),
                  pl.BlockSpec((1, D_MODEL), lambda i: (0, 0)),
                  pl.BlockSpec((1, D_MODEL), lambda i: (0, 0))],
        out_specs=pl.BlockSpec((tm, D_MODEL), lambda i: (i, 0)),
        out_shape=jax.ShapeDtypeStruct((rows, D_MODEL), F32),
        compiler_params=_params("parallel"),
        name="merge_norm",
    )(x, ya, yb, yc, yd, h, h, h, h, h, wb, wo, lng, lnb)


def _split_w_in(w_in):
    out, off = {}, 0
    for name, width in IN_LAYOUT:
        out[name] = w_in[:, off:off + width]
        off += width
    return out


def _regroup_kernel(w_ref, proj_ref, key_ref):
    cols = _split_w_in(w_ref[...])
    pad = jnp.zeros((w_ref.shape[0], MISC_PAD), F32)
    proj_ref[...] = jnp.concatenate([cols[name] for name in PROJ_ORDER] + [pad], axis=1).astype(BF16)
    k, v, ki = cols['b_k'], cols['b_v'], cols['b_ki']
    dh = DSA_HEAD_DIM
    key_ref[...] = jnp.concatenate([k[:, :dh], k[:, :dh], k[:, dh:], k[:, dh:],
                                    v[:, :dh], v[:, :dh], v[:, dh:], v[:, dh:], ki, ki], axis=1).astype(BF16)


def _regroup_weights(w_in):
    depth, rows, width = w_in.shape
    tr = 128
    return pl.pallas_call(
        _regroup_kernel,
        grid=(depth, rows // tr),
        in_specs=[pl.BlockSpec((None, tr, width), lambda l, i: (l, i, 0))],
        out_specs=[pl.BlockSpec((None, tr, PROJ_W), lambda l, i: (l, i, 0)),
                   pl.BlockSpec((None, tr, HK_W), lambda l, i: (l, i, 0))],
        out_shape=[jax.ShapeDtypeStruct((depth, rows, PROJ_W), BF16),
                   jax.ShapeDtypeStruct((depth, rows, HK_W), BF16)],
        compiler_params=_params("parallel", "parallel"),
        name="regroup_weights",
    )(w_in)


def _dup_keys(k, v, ki):
    return jnp.concatenate([k[:, :, 0], k[:, :, 0], k[:, :, 1], k[:, :, 1],
                            v[:, :, 0], v[:, :, 0], v[:, :, 1], v[:, :, 1], ki, ki], axis=-1).astype(BF16)


def _layer(x, lw, st):
    n, t, _ = x.shape
    x2 = x.reshape(n * t, D_MODEL)
    h = _matmul(x2, lw['w_proj'], F32, tn=PROJ_W // 9)
    hk_new = _matmul(x2, lw['w_key'], BF16, tn=HK_W).reshape(n, t, HK_W)
    if st is None:
        hk, n_keys, b_off = hk_new, t, 0
        s5r = jnp.zeros((n, 1, S5_LANES), F32)
        s5i = jnp.zeros((n, 1, S5_LANES), F32)
        gdn0 = jnp.zeros((n, GDN_HEADS, GDN_DK, GDN_DV), F32)
        conv0 = jnp.zeros((n, SUBLANE, GDN_CONV_CH), F32)
        gla0 = jnp.zeros((n, GLA_DV, GLA_HEADS * GLA_DK), F32)
    else:
        past = st['k'].shape[1]
        n_keys = past + t
        b_off = past // Q_BLOCK
        lp = -(-n_keys // KEY_TILE) * KEY_TILE
        hk = jnp.concatenate([_dup_keys(st['k'], st['v'], st['kidx']), hk_new,
                              jnp.zeros((n, lp - n_keys, HK_W), BF16)], axis=1)
        s5r = st['s5_re'].reshape(n, 1, S5_LANES)
        s5i = st['s5_im'].reshape(n, 1, S5_LANES)
        gdn0 = st['gdn']
        conv0 = jnp.concatenate([jnp.zeros((n, CONV_LEAD, GDN_CONV_CH), F32), st['gdn_conv']], axis=1)
        gla0 = jnp.moveaxis(st['gla'], 3, 1).reshape(n, GLA_DV, GLA_HEADS * GLA_DK)
    topk = min(DSA_TOPK_MAX, n_keys // 4)

    ya, s5r_new, s5i_new = _s5_call(h, n, t, s5r, s5i, lw['s5'])
    yb = _dsa_call(h, hk, n, t, lw['bias'], lw['tril'], b_off=b_off, n_keys=n_keys, topk=topk)
    h3 = h.reshape(n, t, PROJ_W)
    yc, gdn_new, conv_new = _gdn_call(h3, gdn0, conv0, lw['gdn'])
    yd, gla_new = _gla_call(h3, gla0, lw['gla'])
    yc = yc.reshape(n * t, GDN_VW)
    yd = yd.reshape(n * t, GLA_HEADS * GLA_DV)
    y = _merge_call(x2, h, ya, yb, yc, yd, lw['w_branch'], lw['w_out'], lw['ln_g'], lw['ln_b'])

    def cols(name, width):
        return h[:, PROJ_OFF[name]:PROJ_OFF[name] + width].reshape(n, t, width)

    new = {'k': cols('b_k', DSA_KV_HEADS * DSA_HEAD_DIM).reshape(n, t, DSA_KV_HEADS, DSA_HEAD_DIM),
           'v': cols('b_v', DSA_KV_HEADS * DSA_HEAD_DIM).reshape(n, t, DSA_KV_HEADS, DSA_HEAD_DIM),
           'kidx': cols('b_ki', IDX_DIM),
           's5_re': s5r_new.reshape(n, S5_GROUPS, S5_STATE), 's5_im': s5i_new.reshape(n, S5_GROUPS, S5_STATE),
           'gdn': gdn_new, 'gdn_conv': conv_new[:, CONV_LEAD:],
           'gla': jnp.moveaxis(gla_new.reshape(n, GLA_DV, GLA_HEADS, GLA_DK), 1, 3)}
    return y.reshape(n, t, D_MODEL), new


def kernel(x_prompt, x_sample, cache_k, cache_v, cache_kidx, state_s5_re, state_s5_im, state_gdn, state_gdn_conv, state_gla, w_in, w_branch, w_out, ln_g, ln_b, rel_bias, s5_a_re, s5_a_im, s5_log_dt, s5_b_re, s5_b_im, s5_c_re, s5_c_im, s5_d, s5_w_glu, gdn_conv, gdn_a_log, gdn_dt_bias, gdn_norm, gla_w_g2, gla_b_g, gla_norm):
    bias = _dsa_bias_tables(rel_bias)
    tril = (jnp.arange(KEY_BLOCK)[:, None] >= jnp.arange(KEY_BLOCK)[None, :]).astype(BF16)
    yp, ys = x_prompt, x_sample
    new_p, new_s = [], []
    w_proj, w_key = _regroup_weights(w_in)
    for l in range(DEPTH):
        lw = {'w_proj': w_proj[l], 'w_key': w_key[l], 'bias': bias, 'tril': tril,
              'w_branch': w_branch[l].astype(BF16), 'w_out': w_out[l].astype(BF16),
              'ln_g': ln_g[l].reshape(1, D_MODEL), 'ln_b': ln_b[l].reshape(1, D_MODEL),
              's5': _s5_prepare(s5_a_re[l], s5_a_im[l], s5_log_dt[l], s5_b_re[l], s5_b_im[l],
                                s5_c_re[l], s5_c_im[l], s5_d[l], s5_w_glu[l]),
              'gdn': {'convw': gdn_conv[l], 'alog': _lane_place(gdn_a_log[l], MISC_A),
                      'dtb': _lane_place(gdn_dt_bias[l], MISC_A), 'norm': gdn_norm[l].reshape(1, GDN_DV)},
              'gla': _gla_prepare(gla_w_g2[l], gla_b_g[l], gla_norm[l])}
        st = {'k': cache_k[l], 'v': cache_v[l], 'kidx': cache_kidx[l], 's5_re': state_s5_re[l],
              's5_im': state_s5_im[l], 'gdn': state_gdn[l], 'gdn_conv': state_gdn_conv[l], 'gla': state_gla[l]}
        yp, stp = _layer(yp, lw, None)
        ys, sts = _layer(ys, lw, st)
        new_p.append(stp)
        new_s.append(sts)
    names = ('k', 'v', 'kidx', 's5_re', 's5_im', 'gdn', 'gdn_conv', 'gla')
    stack = lambda states, name: jnp.stack([s[name] for s in states], axis=0)
    return ((yp, ys) + tuple(stack(new_p, nm) for nm in names) + tuple(stack(new_s, nm) for nm in names))
```

```python
import functools
import math

import jax
import jax.numpy as jnp
from jax import lax
from jax.experimental import pallas as pl
from jax.experimental.pallas import tpu as pltpu

F32 = jnp.float32
BF16 = jnp.bfloat16
HIGHEST = lax.Precision.HIGHEST

D_MODEL = 1024
DEPTH = 2
CHUNK = 64
N_BRANCH = 4
BRANCH_W = 512
S5_GROUP = 16
S5_GROUPS = BRANCH_W // S5_GROUP
S5_STATE = 64
S5_LANES = S5_GROUPS * S5_STATE
S5_SPLIT = 2
DSA_HEADS = 8
DSA_KV_HEADS = 2
DSA_HEAD_DIM = 64
IDX_HEADS = 4
IDX_DIM = 64
DSA_TOPK_MAX = 256
REL_BUCKETS = 32
REL_MAX_DIST = 128
GDN_HEADS = 4
GDN_DK = 128
GDN_DV = 128
GDN_QK = GDN_HEADS * GDN_DK
GDN_VW = GDN_HEADS * GDN_DV
CONV_W = 4
GDN_CONV_CH = 2 * GDN_QK + GDN_VW
GDN_CHUNKS_PER_STEP = 2
CONV_LEAD = 8 - (CONV_W - 1)
GLA_HEADS = 4
GLA_DK = 64
GLA_DV = 128
GLA_GATE_RANK = 16
GLA_TAU = 16.0
GLA_BLOCK = 16
LN_EPS = 1e-5
RMS_EPS = 1e-6
DN_ALPHA = (2 * DEPTH) ** 0.25
LOG2E = math.log2(math.e)

LANE = 128
SUBLANE = 8
KEY_BLOCK = 128
KEY_SUB = 4
KEY_TILE = KEY_SUB * KEY_BLOCK
SOFTMAX_CEIL = 2.0 ** 120
SOFTMAX_FLOOR = 2.0 ** -60
FOLD_ROWS = 64
Q_BLOCK = 128
Q_TILE = 256
VMEM_LIMIT = 56 * 1024 * 1024

IN_LAYOUT = (
    ('a_u', BRANCH_W), ('a_gate', BRANCH_W),
    ('b_q', DSA_HEADS * DSA_HEAD_DIM), ('b_k', DSA_KV_HEADS * DSA_HEAD_DIM),
    ('b_v', DSA_KV_HEADS * DSA_HEAD_DIM), ('b_qi', IDX_HEADS * IDX_DIM), ('b_ki', IDX_DIM),
    ('b_wi', IDX_HEADS), ('b_gate', BRANCH_W),
    ('c_qkv', GDN_CONV_CH), ('c_beta', GDN_HEADS), ('c_a', GDN_HEADS), ('c_gate', BRANCH_W),
    ('d_q', GLA_HEADS * GLA_DK), ('d_k', GLA_HEADS * GLA_DK), ('d_v', GLA_HEADS * GLA_DV),
    ('d_g', GLA_GATE_RANK), ('d_gate', BRANCH_W),
    ('merge', N_BRANCH * D_MODEL),
)

PROJ_ORDER = ('merge', 'a_u', 'a_gate', 'b_q', 'b_qi', 'b_k', 'b_v', 'c_qkv', 'b_gate', 'c_gate',
              'd_v', 'd_gate', 'd_q', 'd_k', 'b_ki', 'b_wi', 'c_beta', 'c_a', 'd_g')
MISC_WI = IDX_DIM
MISC_BETA = MISC_WI + IDX_HEADS
MISC_A = MISC_BETA + GDN_HEADS
MISC_G = MISC_A + GDN_HEADS
MISC_PAD = LANE - (MISC_G + GLA_GATE_RANK)


def _proj_offsets():
    widths = dict(IN_LAYOUT)
    off, out = 0, {}
    for name in PROJ_ORDER:
        out[name] = off
        off += widths[name]
    return out, off + MISC_PAD


PROJ_OFF, PROJ_W = _proj_offsets()
OFF_MISC = PROJ_OFF['b_ki']
HK_W = 5 * LANE
HK_V = 2 * LANE
HK_KI = 4 * LANE


def _dot(a, b, precision=None):
    return jnp.dot(a, b, preferred_element_type=F32, precision=precision)


def _dot_nt(a, b, precision=None):
    return lax.dot_general(a, b, (((1,), (1,)), ((), ())), preferred_element_type=F32, precision=precision)


def _dot_tn(a, b, precision=None):
    return lax.dot_general(a, b, (((0,), (0,)), ((), ())), preferred_element_type=F32, precision=precision)


def _split_bf16(x):
    hi = x.astype(BF16)
    return hi, (x - hi.astype(F32)).astype(BF16)


def _dot3(a, b):
    a_hi, a_lo = _split_bf16(a)
    b_hi, b_lo = _split_bf16(b)
    return _dot(a_hi, b_hi) + (_dot(a_hi, b_lo) + _dot(a_lo, b_hi))


def _params(*sem):
    return pltpu.CompilerParams(dimension_semantics=sem, vmem_limit_bytes=VMEM_LIMIT)


def _softplus(x):
    return jnp.maximum(x, 0.0) + jnp.log1p(jnp.exp(-jnp.abs(x)))


def _iota(shape, dim):
    return lax.broadcasted_iota(jnp.int32, shape, dim)


def _mm_kernel(x_ref, w_ref, o_ref):
    o_ref[...] = _dot(x_ref[...].astype(BF16), w_ref[...]).astype(o_ref.dtype)


def _matmul(x, w, out_dtype, tn):
    rows, kdim = x.shape
    width = w.shape[1]
    tm = min(rows, 1024)
    return pl.pallas_call(
        _mm_kernel,
        grid=(rows // tm, width // tn),
        in_specs=[pl.BlockSpec((tm, kdim), lambda i, j: (i, 0)),
                  pl.BlockSpec((kdim, tn), lambda i, j: (0, j))],
        out_specs=pl.BlockSpec((tm, tn), lambda i, j: (i, j)),
        out_shape=jax.ShapeDtypeStruct((rows, width), out_dtype),
        compiler_params=_params("parallel", "parallel"),
        name="proj_matmul",
    )(x, w)


def _s5_kernel(u_ref, h0r_ref, h0i_ref, ar_ref, ai_ref, bb_ref, cc_ref, d_ref, wglu_ref,
               ya_ref, hr_out, hi_out, xr_s, xi_s, hr_s, hi_s, cr_s, ci_s, *, tc):
    @pl.when(pl.program_id(1) == 0)
    def _():
        cr_s[...] = h0r_ref[...]
        ci_s[...] = h0i_ref[...]

    u = u_ref[...]
    ub = u.astype(BF16)
    uw = BRANCH_W // S5_SPLIT
    sw = S5_LANES // S5_SPLIT
    for part in range(S5_SPLIT):
        x = _dot(ub[:, part * uw:(part + 1) * uw], bb_ref[part])
        xr_s[:, part * sw:(part + 1) * sw] = x[:, :sw]
        xi_s[:, part * sw:(part + 1) * sw] = x[:, sw:]

    ar = ar_ref[...]
    ai = ai_ref[...]

    def step(t, carry):
        hr, hi = carry
        nhr = ar * hr - ai * hi + xr_s[pl.ds(t, 1), :]
        nhi = ar * hi + ai * hr + xi_s[pl.ds(t, 1), :]
        hr_s[pl.ds(t, 1), :] = nhr
        hi_s[pl.ds(t, 1), :] = nhi
        return nhr, nhi

    hr, hi = lax.fori_loop(0, tc, step, (cr_s[...], ci_s[...]), unroll=8)
    cr_s[...] = hr
    ci_s[...] = hi
    hr_out[...] = hr
    hi_out[...] = hi
    ys = []
    for part in range(S5_SPLIT):
        hcat = jnp.concatenate([hr_s[:, part * sw:(part + 1) * sw], hi_s[:, part * sw:(part + 1) * sw]], axis=1)
        ys.append(_dot(hcat.astype(BF16), cc_ref[part]))
    y = jnp.concatenate(ys, axis=1) + d_ref[...] * u
    y = jax.nn.gelu(y)
    g = _dot(y.astype(BF16), wglu_ref[...])
    ya_ref[...] = g[:, :BRANCH_W] * jax.nn.sigmoid(g[:, BRANCH_W:])


def _s5_call(h, n, t, h0r, h0i, sp):
    tc = min(t, 256)
    nc = t // tc
    uw = BRANCH_W // S5_SPLIT
    sw = S5_LANES // S5_SPLIT
    const = lambda shape: pl.BlockSpec(shape, lambda i, c: (0,) * len(shape))
    state = pl.BlockSpec((None, 1, S5_LANES), lambda i, c: (i, 0, 0))
    return pl.pallas_call(
        functools.partial(_s5_kernel, tc=tc),
        grid=(n, nc),
        in_specs=[pl.BlockSpec((tc, BRANCH_W), lambda i, c: (i * nc + c, PROJ_OFF['a_u'] // BRANCH_W)),
                  state, state,
                  const((1, S5_LANES)), const((1, S5_LANES)),
                  const((S5_SPLIT, uw, 2 * sw)), const((S5_SPLIT, 2 * sw, uw)),
                  const((1, BRANCH_W)), const((BRANCH_W, 2 * BRANCH_W))],
        out_specs=[pl.BlockSpec((tc, BRANCH_W), lambda i, c: (i * nc + c, 0)), state, state],
        out_shape=[jax.ShapeDtypeStruct((n * t, BRANCH_W), F32),
                   jax.ShapeDtypeStruct((n, 1, S5_LANES), F32),
                   jax.ShapeDtypeStruct((n, 1, S5_LANES), F32)],
        scratch_shapes=[pltpu.VMEM((tc, S5_LANES), F32), pltpu.VMEM((tc, S5_LANES), F32),
                        pltpu.VMEM((tc, S5_LANES), F32), pltpu.VMEM((tc, S5_LANES), F32),
                        pltpu.VMEM((1, S5_LANES), F32), pltpu.VMEM((1, S5_LANES), F32)],
        compiler_params=_params("parallel", "arbitrary"),
        name="s5_branch",
    )(h, h0r, h0i, sp['ar'], sp['ai'], sp['bb'], sp['cc'], sp['d'], sp['wglu'])


def _s5_prepare(a_re, a_im, log_dt, b_re, b_im, c_re, c_im, d, w_glu):
    lam_r = jnp.minimum(a_re, -1e-4)
    lam_i = a_im
    dt = jnp.exp(log_dt)[:, None]
    mag = jnp.exp(lam_r * dt)
    ar = mag * jnp.cos(lam_i * dt)
    ai = mag * jnp.sin(lam_i * dt)
    den = lam_r * lam_r + lam_i * lam_i
    nr = ar - 1.0
    cr = (nr * lam_r + ai * lam_i) / den
    ci = (ai * lam_r - nr * lam_i) / den
    bb_r = cr[..., None] * b_re - ci[..., None] * b_im
    bb_i = cr[..., None] * b_im + ci[..., None] * b_re
    eye = jnp.eye(S5_GROUPS, dtype=F32)

    def blockdiag_in(bb):
        m = eye[:, None, :, None] * jnp.swapaxes(bb, 1, 2)[:, :, None, :]
        return m.reshape(BRANCH_W, S5_LANES)

    def blockdiag_out(c):
        m = eye[:, None, :, None] * jnp.swapaxes(c, 1, 2)[:, :, None, :]
        return m.reshape(S5_LANES, BRANCH_W)

    uw = BRANCH_W // S5_SPLIT
    sw = S5_LANES // S5_SPLIT
    bbr, bbi = blockdiag_in(bb_r), blockdiag_in(bb_i)
    ccr, cci = blockdiag_out(c_re), blockdiag_out(c_im)
    bb = jnp.stack([jnp.concatenate([m[p * uw:(p + 1) * uw, p * sw:(p + 1) * sw] for m in (bbr, bbi)], axis=1)
                    for p in range(S5_SPLIT)]).astype(BF16)
    cc = jnp.stack([jnp.concatenate([ccr[p * sw:(p + 1) * sw, p * uw:(p + 1) * uw],
                                     -cci[p * sw:(p + 1) * sw, p * uw:(p + 1) * uw]], axis=0)
                    for p in range(S5_SPLIT)]).astype(BF16)
    return {'ar': ar.reshape(1, S5_LANES), 'ai': ai.reshape(1, S5_LANES), 'bb': bb, 'cc': cc,
            'd': d.reshape(1, BRANCH_W), 'wglu': w_glu.astype(BF16)}


def _sortable(x):
    bits = lax.bitcast_convert_type(x, jnp.int32)
    key = bits ^ ((bits >> 31) & jnp.int32(0x7FFFFFFF))
    return jnp.where(key == -1, 0, key)


def _dsa_kernel(q_ref, qi_ref, misc_ref, hk_ref, bias_ref, tril_ref, o_ref, sc_s, pk_s, acc_s, far_s, *, qb, b_off, n_keys, topk):
    nsub = max(qb // Q_BLOCK, 1)
    qw = qb // nsub
    b = pl.program_id(1) * nsub + b_off
    nkt = (b + nsub - 1 + KEY_SUB) // KEY_SUB
    fold = KEY_TILE // FOLD_ROWS
    key_row = _iota((KEY_TILE, qb), 0)
    qpos = b * Q_BLOCK + _iota((1, qb), 1)
    limit = jnp.minimum(((qpos >> 6) + 1) * CHUNK, n_keys)
    low_half = _iota((qb, LANE), 1) < DSA_HEAD_DIM

    def key_rows(ref_cols, j):
        return hk_ref[pl.ds(pl.multiple_of(j * KEY_TILE, KEY_TILE), KEY_TILE), ref_cols:ref_cols + LANE]

    def half_masked(x, hd):
        pair = x[:, (hd // 2) * LANE:(hd // 2 + 1) * LANE]
        return jnp.where(low_half == (hd % 2 == 0), pair, 0.0).astype(BF16)

    misc = misc_ref[...]
    pick = jnp.where(_iota((SUBLANE, LANE), 1) == MISC_WI + _iota((SUBLANE, LANE), 0), 1.0, 0.0)
    wi_t = _dot_nt(pick, misc, precision=HIGHEST) * (IDX_DIM ** -0.5 * IDX_HEADS ** -0.5)
    qi = qi_ref[...]
    qi_stack = jnp.concatenate([half_masked(qi, hd) for hd in range(IDX_HEADS)], axis=0)

    def pack16(v):
        half = KEY_TILE // 2
        return (v[0:half] & 0xFFFF) | (v[half:] << 16)

    def score_block(j, carry):
        z = _dot_nt(key_rows(HK_KI, j), qi_stack)
        sc = jnp.zeros((KEY_TILE, qb), F32)
        for hd in range(IDX_HEADS):
            sc = sc + jnp.maximum(z[:, hd * qb:(hd + 1) * qb], 0.0) * wi_t[hd:hd + 1, :]
        sc = jnp.where(j * KEY_TILE + key_row < limit, sc, -jnp.inf)
        key = _sortable(sc)
        sc_s[j] = key
        pk_s[j] = pack16(key >> 16)
        return carry

    lax.fori_loop(0, nkt, score_block, 0)

    def count_ge(cand):
        def body(j, acc):
            return acc + jnp.where(sc_s[j] >= cand, 1.0, 0.0).reshape(fold, FOLD_ROWS, qb).sum(axis=0)
        acc = lax.fori_loop(0, nkt, body, jnp.zeros((FOLD_ROWS, qb), F32))
        return jnp.sum(acc, axis=0, keepdims=True)

    one16 = jnp.ones((KEY_TILE, qb), jnp.int16)
    zero16 = jnp.zeros((KEY_TILE, qb), jnp.int16)

    def count16_ge(cand):
        word = jnp.broadcast_to((cand & 0xFFFF) | (cand << 16), (KEY_TILE // 2, qb))
        c16 = pltpu.bitcast(word, jnp.int16)

        def body(j, acc):
            ind = jnp.where(pltpu.bitcast(pk_s[j], jnp.int16) >= c16, one16, zero16)
            for i in range(fold):
                acc = acc + ind[i * FOLD_ROWS:(i + 1) * FOLD_ROWS]
            return acc
        acc = lax.fori_loop(0, nkt, body, jnp.zeros((FOLD_ROWS, qb), jnp.int16))
        return jnp.sum(acc.astype(F32), axis=0, keepdims=True)

    def radix_search(start, want):
        def bit_step(i, res):
            cand = res + (jnp.int32(1) << (14 - i))
            return jnp.where(count16_ge(cand) >= want, cand, res)
        return lax.fori_loop(0, 15, bit_step, start)

    kf = float(topk)
    top = radix_search(jnp.where(count16_ge(jnp.zeros((1, qb), jnp.int32)) >= kf, 0, -32768), kf)
    int16_max = jnp.iinfo(jnp.int16).max
    above = jnp.where(top == int16_max, 0.0, count16_ge(jnp.minimum(top + 1, int16_max)))
    want_low = kf - above

    def pack_low(j, carry):
        key = sc_s[j]
        pk_s[j] = pack16(jnp.where((key >> 16) == top, (key >> 1) & 0x7FFF, -1))
        return carry

    lax.fori_loop(0, nkt, pack_low, 0)
    low = radix_search(jnp.zeros((1, qb), jnp.int32), want_low)
    base = (top << 16) | (low << 1)
    thr = jnp.where(count_ge(base | 1) >= kf, base | 1, base)
    need = kf - count_ge(thr + 1)

    tril = tril_ref[...]

    def select_block(j, ties):
        keys = sc_s[j]
        eq = keys == thr
        eq_b = jnp.where(eq, 1.0, 0.0).astype(BF16)
        parts, carry = [], ties
        for s in range(KEY_SUB):
            part = _dot(tril, eq_b[s * KEY_BLOCK:(s + 1) * KEY_BLOCK]) + carry
            parts.append(part)
            carry = part[KEY_BLOCK - 1:KEY_BLOCK, :]
        prefix = jnp.concatenate(parts, axis=0)
        sel = ((keys > thr) | (eq & (prefix <= need))) & (j * KEY_TILE + key_row < limit)
        sc_s[j] = lax.bitcast_convert_type(jnp.where(sel, 0.0, -jnp.inf), jnp.int32)
        return prefix[KEY_TILE - 1:KEY_TILE, :]

    lax.fori_loop(0, nkt, select_block, jnp.zeros((1, qb), F32))

    q = q_ref[...] * (DSA_HEAD_DIM ** -0.5 * LOG2E)
    rep = DSA_HEADS // DSA_KV_HEADS
    eye = jnp.where(_iota((LANE, LANE), 0) == _iota((LANE, LANE), 1), 1.0, 0.0)
    value_lane = _iota((KEY_TILE, LANE), 1) < DSA_HEAD_DIM
    n_pairs = DSA_HEADS // 2
    q_pair = [jnp.concatenate([half_masked(q, 2 * c), half_masked(q, 2 * c + 1)], axis=0) for c in range(n_pairs)]
    far_bias = [jnp.concatenate([bias_ref[hd, 2, 0:1, 0:qw]] * nsub, axis=1)
                for hd in range(DSA_HEADS)]
    n_far = jnp.maximum((b - 1) // KEY_SUB, 0)

    def tile_operands(j):
        mask = lax.bitcast_convert_type(sc_s[j], F32)
        kd = [key_rows(grp * LANE, j) for grp in range(DSA_KV_HEADS)]
        zs = [_dot_nt(kd[(2 * c) // rep], q_pair[c]) for c in range(n_pairs)]
        vd = [jnp.where(value_lane, key_rows(HK_V + grp * LANE, j), 1.0).astype(BF16)
              for grp in range(DSA_KV_HEADS)]
        return mask, zs, vd

    def near_bias(hd, j):
        cols = [jnp.concatenate([bias_ref[hd, jnp.clip(b + u - (KEY_SUB * j + s), 0, 2), :, 0:qw]
                                 for s in range(KEY_SUB)], axis=0) for u in range(nsub)]
        return jnp.concatenate(cols, axis=1)

    def plain_tile(j, carry, far):
        mask, zs, vd = tile_operands(j)
        ps = []
        for hd in range(DSA_HEADS):
            r = hd % 2
            zh = zs[hd // 2][:, r * qb:(r + 1) * qb] + mask
            if not far:
                zh = zh + near_bias(hd, j)
            ps.append(jnp.exp2(zh).astype(BF16))
        acc = far_s if far else acc_s
        for c in range(n_pairs):
            acc[c] += _dot_tn(vd[(2 * c) // rep], jnp.concatenate(ps[2 * c:2 * c + 2], axis=1))
        return carry

    acc_s[...] = jnp.zeros_like(acc_s)
    far_s[...] = jnp.zeros_like(far_s)
    lax.fori_loop(0, n_far, lambda j, c: plain_tile(j, c, True), 0)
    lax.fori_loop(n_far, nkt, lambda j, c: plain_tile(j, c, False), 0)
    healthy = jnp.ones((1, 1), F32)
    for c in range(n_pairs):
        scale = jnp.concatenate([jnp.exp2(far_bias[2 * c]), jnp.exp2(far_bias[2 * c + 1])], axis=1)
        total = far_s[c] * scale + acc_s[c]
        acc_s[c] = total
        norm = total[DSA_HEAD_DIM:DSA_HEAD_DIM + 1, :]
        ok = (jnp.abs(total) < SOFTMAX_CEIL) & (norm > SOFTMAX_FLOOR)
        healthy = jnp.minimum(healthy, jnp.min(jnp.where(ok, 1.0, 0.0), keepdims=True))

    def attend(j, ms, far):
        mask, zs, vd = tile_operands(j)
        new_ms, ps, alphas = [], [], []
        for hd in range(DSA_HEADS):
            r = hd % 2
            zh = zs[hd // 2][:, r * qb:(r + 1) * qb] + mask
            if not far:
                zh = zh + near_bias(hd, j)
            tile_max = jnp.max(zh.reshape(fold, FOLD_ROWS, qb).max(axis=0), axis=0, keepdims=True)
            shift = far_bias[hd] if far else 0.0
            m_new = jnp.maximum(ms[hd], tile_max + shift)
            ps.append(jnp.exp2(zh - (m_new - shift)).astype(BF16))
            alphas.append(jnp.exp2(ms[hd] - m_new))
            new_ms.append(m_new)
        for c in range(n_pairs):
            acc_s[c] = (acc_s[c] * jnp.concatenate(alphas[2 * c:2 * c + 2], axis=1)
                        + _dot_tn(vd[(2 * c) // rep], jnp.concatenate(ps[2 * c:2 * c + 2], axis=1)))
        return tuple(new_ms)

    @pl.when(healthy[0, 0] < 0.5)
    def _():
        acc_s[...] = jnp.zeros_like(acc_s)
        ms = tuple(jnp.full((1, qb), -1e30, F32) for _ in range(DSA_HEADS))
        ms = lax.fori_loop(0, n_far, lambda j, m: attend(j, m, True), ms)
        lax.fori_loop(n_far, nkt, lambda j, m: attend(j, m, False), ms)

    for c in range(n_pairs):
        halves = []
        for r in range(2):
            blk = acc_s[c, :, r * qb:(r + 1) * qb]
            halves.append(blk[0:DSA_HEAD_DIM] / blk[DSA_HEAD_DIM:DSA_HEAD_DIM + 1])
        o_t = jnp.concatenate(halves, axis=0)
        o_ref[:, c * LANE:(c + 1) * LANE] = _dot_tn(o_t, eye, precision=HIGHEST)


def _dsa_call(h, hk, n, t, bias, tril, *, b_off, n_keys, topk):
    qb = min(t, Q_TILE)
    nqb = t // qb
    lp = hk.shape[1]
    wq = DSA_HEADS * DSA_HEAD_DIM
    wqi = IDX_HEADS * IDX_DIM
    return pl.pallas_call(
        functools.partial(_dsa_kernel, qb=qb, b_off=b_off, n_keys=n_keys, topk=topk),
        grid=(n, nqb),
        in_specs=[pl.BlockSpec((qb, wq), lambda i, c: (i * nqb + c, PROJ_OFF['b_q'] // wq)),
                  pl.BlockSpec((qb, wqi), lambda i, c: (i * nqb + c, PROJ_OFF['b_qi'] // wqi)),
                  pl.BlockSpec((qb, LANE), lambda i, c: (i * nqb + c, OFF_MISC // LANE)),
                  pl.BlockSpec((None, lp, HK_W), lambda i, c: (i, 0, 0)),
                  pl.BlockSpec((DSA_HEADS, 3, KEY_BLOCK, Q_BLOCK), lambda i, c: (0, 0, 0, 0)),
                  pl.BlockSpec((KEY_BLOCK, KEY_BLOCK), lambda i, c: (0, 0))],
        out_specs=pl.BlockSpec((qb, wq), lambda i, c: (i * nqb + c, 0)),
        out_shape=jax.ShapeDtypeStruct((n * t, wq), F32),
        scratch_shapes=[pltpu.VMEM((lp // KEY_TILE, KEY_TILE, qb), jnp.int32),
                        pltpu.VMEM((lp // KEY_TILE, KEY_TILE // 2, qb), jnp.int32),
                        pltpu.VMEM((DSA_HEADS // 2, LANE, 2 * qb), F32),
                        pltpu.VMEM((DSA_HEADS // 2, LANE, 2 * qb), F32)],
        compiler_params=_params("parallel", "arbitrary"),
        name="dsa_branch",
    )(h, h, h, hk, bias, tril)


def _t5_bucket(rel):
    nb = REL_BUCKETS // 2
    max_exact = nb // 2
    ret = jnp.where(rel > 0, nb, 0)
    dist = jnp.abs(rel)
    distf = jnp.maximum(dist, 1).astype(F32)
    large = max_exact + (jnp.log(distf / max_exact) / math.log(REL_MAX_DIST / max_exact)
                         * (nb - max_exact)).astype(jnp.int32)
    large = jnp.minimum(large, nb - 1)
    return ret + jnp.where(dist < max_exact, dist, large)


def _dsa_bias_tables(rel_bias):
    lc = jnp.arange(KEY_BLOCK, dtype=jnp.int32)[:, None]
    qr = jnp.arange(Q_BLOCK, dtype=jnp.int32)[None, :]
    rel = jnp.stack([lc - qr - back * KEY_BLOCK for back in range(3)], axis=0)
    onehot = jax.nn.one_hot(_t5_bucket(rel), REL_BUCKETS, dtype=F32)
    bias = jnp.einsum('klqb,bh->hklq', onehot, rel_bias.astype(F32), precision=HIGHEST)
    return bias * LOG2E


def _gdn_kernel(raw_ref, misc_ref, s0_ref, conv0_ref, convw_ref, alog_ref, dtb_ref, norm_ref,
                o_ref, s_out, conv_out, xp_s, st_s, *, n, c, nsub):
    rows = nsub * c

    @pl.when(pl.program_id(0) == 0)
    def _():
        xp_s[:, 0:SUBLANE, :] = conv0_ref[...]
        st_s[...] = s0_ref[...]

    w = convw_ref[...]
    row = _iota((c, c), 0)
    col = _iota((c, c), 1)
    incl = row >= col
    strict = row > col
    eye = row == col
    tril = jnp.where(incl, 1.0, 0.0)
    eye_f = jnp.where(eye, 1.0, 0.0)
    chains = [(b, sub, hd) for b in range(n) for sub in range(nsub) for hd in range(GDN_HEADS)]

    qkv, beta_all, gc_all = [], [], []
    for b in range(n):
        raw = raw_ref[b]
        xp_s[b, SUBLANE:SUBLANE + rows, :] = raw
        y = raw * w[CONV_W - 1:CONV_W, :]
        for i in range(CONV_W - 1):
            y = y + xp_s[b, CONV_LEAD + i:CONV_LEAD + i + rows, :] * w[i:i + 1, :]
        tail = xp_s[b, rows:rows + SUBLANE, :]
        xp_s[b, 0:SUBLANE, :] = tail
        conv_out[b] = tail
        qkv.append(jax.nn.silu(y))
        misc = misc_ref[b]
        beta_all.append(jax.nn.sigmoid(misc))
        g_all = -jnp.exp(alog_ref[...]) * _softplus(misc + dtb_ref[...])
        gc_all.append([_dot(tril, g_all[sub * c:(sub + 1) * c], precision=HIGHEST)
                       for sub in range(nsub)])

    qs, ks, vs, betas, gcs, dec_s, dec_i = [], [], [], [], [], [], []
    for b, sub, hd in chains:
        x = qkv[b][sub * c:(sub + 1) * c]
        qh = x[:, hd * GDN_DK:(hd + 1) * GDN_DK]
        kh = x[:, GDN_QK + hd * GDN_DK:GDN_QK + (hd + 1) * GDN_DK]
        vs.append(x[:, 2 * GDN_QK + hd * GDN_DV:2 * GDN_QK + (hd + 1) * GDN_DV])
        qs.append(qh * lax.rsqrt(jnp.sum(qh * qh, axis=1, keepdims=True) + RMS_EPS) * (GDN_DK ** -0.5))
        ks.append(kh * lax.rsqrt(jnp.sum(kh * kh, axis=1, keepdims=True) + RMS_EPS))
        betas.append(beta_all[b][sub * c:(sub + 1) * c, MISC_BETA + hd:MISC_BETA + hd + 1])
        gc = gc_all[b][sub][:, MISC_A + hd:MISC_A + hd + 1]
        gcs.append(gc)
        gc_b = jnp.broadcast_to(gc, (c, c))
        gc_row = jnp.sum(jnp.where(eye, gc_b, 0.0), axis=0, keepdims=True)
        diff = gc_b - gc_row
        dec = jnp.exp(diff)
        dec_s.append(jnp.where(strict, dec, 0.0))
        dec_i.append(jnp.where(incl, dec, 0.0))
    kb = [k.astype(BF16) for k in ks]
    qb = [q.astype(BF16) for q in qs]
    pw = [-(beta * _dot_nt(k, k) * d) for beta, k, d in zip(betas, kb, dec_s)]
    inv = [eye_f + p for p in pw]
    for _ in range(int(math.log2(c)) - 1):
        pw = [_dot3(p, p) for p in pw]
        inv = [i + _dot3(i, p) for i, p in zip(inv, pw)]
    egc = [jnp.exp(gc) for gc in gcs]
    sol = [_dot3(i, jnp.concatenate([(beta * e) * k, beta * v], axis=1))
           for i, beta, e, k, v in zip(inv, betas, egc, ks, vs)]
    attn = [(_dot_nt(q, k) * d).astype(BF16) for q, k, d in zip(qb, kb, dec_i)]
    for sub in range(nsub):
        ids = [i for i, ch in enumerate(chains) if ch[1] == sub]
        s_old = [st_s[chains[i][0], chains[i][2]] for i in ids]
        sb = [s.astype(BF16) for s in s_old]
        vb = [(sol[i][:, GDN_DK:] - _dot(sol[i][:, :GDN_DK].astype(BF16), s)).astype(BF16) for i, s in zip(ids, sb)]
        outs = [egc[i] * _dot(qb[i], s) + _dot(attn[i], v) for i, s, v in zip(ids, sb, vb)]
        for i, v, s, o in zip(ids, vb, s_old, outs):
            b, _, hd = chains[i]
            g_last = gcs[i][c - 1:c, :]
            st_s[b, hd] = jnp.exp(g_last) * s + _dot_tn((ks[i] * jnp.exp(g_last - gcs[i])).astype(BF16), v)
            o = o * lax.rsqrt(jnp.mean(o * o, axis=1, keepdims=True) + RMS_EPS) * norm_ref[...]
            o_ref[b, sub * c:(sub + 1) * c, hd * GDN_DV:(hd + 1) * GDN_DV] = o
    s_out[...] = st_s[...]


def _gdn_call(h3, s0, conv0, gp):
    n, t, _ = h3.shape
    c = min(t, CHUNK)
    nsub = GDN_CHUNKS_PER_STEP if t % (GDN_CHUNKS_PER_STEP * c) == 0 else 1
    rows = nsub * c
    const = lambda shape: pl.BlockSpec(shape, lambda j: (0,) * len(shape))
    state = const((n, GDN_HEADS, GDN_DK, GDN_DV))
    conv = const((n, SUBLANE, GDN_CONV_CH))
    return pl.pallas_call(
        functools.partial(_gdn_kernel, n=n, c=c, nsub=nsub),
        grid=(t // rows,),
        in_specs=[pl.BlockSpec((n, rows, GDN_CONV_CH), lambda j: (0, j, PROJ_OFF['c_qkv'] // GDN_CONV_CH)),
                  pl.BlockSpec((n, rows, LANE), lambda j: (0, j, OFF_MISC // LANE)),
                  state, conv,
                  const((CONV_W, GDN_CONV_CH)), const((1, LANE)), const((1, LANE)), const((1, GDN_DV))],
        out_specs=[pl.BlockSpec((n, rows, GDN_VW), lambda j: (0, j, 0)), state, conv],
        out_shape=[jax.ShapeDtypeStruct((n, t, GDN_VW), F32),
                   jax.ShapeDtypeStruct((n, GDN_HEADS, GDN_DK, GDN_DV), F32),
                   jax.ShapeDtypeStruct((n, SUBLANE, GDN_CONV_CH), F32)],
        scratch_shapes=[pltpu.VMEM((n, rows + SUBLANE, GDN_CONV_CH), F32),
                        pltpu.VMEM((n, GDN_HEADS, GDN_DK, GDN_DV), F32)],
        compiler_params=_params("arbitrary"),
        name="gdn_branch",
    )(h3, h3, s0, conv0, gp['convw'], gp['alog'], gp['dtb'], gp['norm'])


def _lane_place(v, off):
    return jnp.zeros((1, LANE), F32).at[0, off:off + v.shape[0]].set(v.astype(F32))


def _gla_kernel(qk_ref, v_ref, misc_ref, s0_ref, wg_ref, bg_ref, norm_ref, hexp_ref, gsum_ref,
                o_ref, s_out, st_s, *, n, tc):
    blk = GLA_BLOCK
    wk = GLA_HEADS * GLA_DK

    @pl.when(pl.program_id(0) == 0)
    def _():
        st_s[...] = s0_ref[...]

    row = _iota((blk, blk), 0)
    col = _iota((blk, blk), 1)
    tril = jnp.where(row >= col, 1.0, 0.0)
    jrow = _iota((blk, wk), 0)
    lane_k = _iota((blk, wk), 1)
    head_mask = [(lane_k >= hd * GLA_DK) & (lane_k < (hd + 1) * GLA_DK) for hd in range(GLA_HEADS)]
    chains = range(n)

    def block(sb, carry):
        r0 = pl.multiple_of(sb * blk, blk)
        qk = [qk_ref[b, pl.ds(r0, blk), :] for b in chains]
        q = [x[:, :wk] * (GLA_DK ** -0.5) for x in qk]
        k = [x[:, wk:] for x in qk]
        v = [v_ref[b, pl.ds(r0, blk), :] for b in chains]
        mb = [misc_ref[b, pl.ds(r0, blk), :].astype(BF16) for b in chains]
        lg = [jax.nn.log_sigmoid(_dot(m, wg_ref[...]) + bg_ref[...]) * (1.0 / GLA_TAU) for m in mb]
        bc = [_dot(tril, x, precision=HIGHEST) for x in lg]
        dmat = []
        for b in chains:
            slabs = []
            for i in range(blk):
                keep = jrow <= i
                dec = jnp.where(keep, jnp.exp(bc[b][i:i + 1, :] - bc[b]), 0.0)
                slabs.append(dec * q[b][i:i + 1, :] * k[b])
            dmat.append(jnp.concatenate(slabs, axis=0).astype(BF16))
        a_exp = [_dot(d, hexp_ref[...]) for d in dmat]
        prod = [a * jnp.concatenate([x] * blk, axis=0) for a, x in zip(a_exp, v)]
        o = [_dot(gsum_ref[...], p.astype(BF16)) for p in prod]
        qe = [x * jnp.exp(c) for x, c in zip(q, bc)]
        kt = [x * jnp.exp(c[blk - 1:blk, :] - c) for x, c in zip(k, bc)]
        s = [st_s[b] for b in chains]
        sbf = [x.astype(BF16) for x in s]
        for b in chains:
            upd = jnp.zeros_like(s[b])
            parts = []
            for hd in range(GLA_HEADS):
                parts.append(_dot_nt(jnp.where(head_mask[hd], qe[b], 0.0).astype(BF16), sbf[b]))
                vh = v[b][:, hd * GLA_DV:(hd + 1) * GLA_DV].astype(BF16)
                upd = upd + _dot_tn(vh, jnp.where(head_mask[hd], kt[b], 0.0).astype(BF16))
            ob = o[b] + jnp.concatenate(parts, axis=1)
            st_s[b] = jnp.exp(bc[b][blk - 1:blk, :]) * s[b] + upd
            for hd in range(GLA_HEADS):
                oh = ob[:, hd * GLA_DV:(hd + 1) * GLA_DV]
                oh = oh * lax.rsqrt(jnp.mean(oh * oh, axis=1, keepdims=True) + RMS_EPS) * norm_ref[...]
                o_ref[b, pl.ds(r0, blk), hd * GLA_DV:(hd + 1) * GLA_DV] = oh
        return carry

    lax.fori_loop(0, tc // blk, block, 0, unroll=4 if tc // blk > 1 else 1)
    s_out[...] = st_s[...]


def _gla_call(h3, s0, gp):
    n, t, _ = h3.shape
    tc = min(t, 128)
    wk = GLA_HEADS * GLA_DK
    wv = GLA_HEADS * GLA_DV
    const = lambda shape: pl.BlockSpec(shape, lambda j: (0,) * len(shape))
    state = const((n, GLA_DV, wk))
    return pl.pallas_call(
        functools.partial(_gla_kernel, n=n, tc=tc),
        grid=(t // tc,),
        in_specs=[pl.BlockSpec((n, tc, 2 * wk), lambda j: (0, j, PROJ_OFF['d_q'] // (2 * wk))),
                  pl.BlockSpec((n, tc, wv), lambda j: (0, j, PROJ_OFF['d_v'] // wv)),
                  pl.BlockSpec((n, tc, LANE), lambda j: (0, j, OFF_MISC // LANE)),
                  state,
                  const((LANE, wk)), const((1, wk)), const((1, GLA_DV)),
                  const((wk, wv)), const((GLA_BLOCK, GLA_BLOCK * GLA_BLOCK))],
        out_specs=[pl.BlockSpec((n, tc, wv), lambda j: (0, j, 0)), state],
        out_shape=[jax.ShapeDtypeStruct((n, t, wv), F32),
                   jax.ShapeDtypeStruct((n, GLA_DV, wk), F32)],
        scratch_shapes=[pltpu.VMEM((n, GLA_DV, wk), F32)],
        compiler_params=_params("arbitrary"),
        name="gla_branch",
    )(h3, h3, h3, s0, gp['wg'], gp['bg'], gp['norm'], gp['hexp'], gp['gsum'])


def _gla_prepare(w_g2, b_g, norm):
    wk = GLA_HEADS * GLA_DK
    wg = jnp.zeros((LANE, wk), F32).at[MISC_G:MISC_G + GLA_GATE_RANK].set(w_g2).astype(BF16)
    ck = jnp.arange(wk)[:, None] // GLA_DK
    cv = jnp.arange(GLA_HEADS * GLA_DV)[None, :] // GLA_DV
    ij = jnp.arange(GLA_BLOCK * GLA_BLOCK)[None, :] // GLA_BLOCK
    return {'wg': wg, 'bg': b_g.reshape(1, wk).astype(F32),
            'norm': norm.reshape(1, GLA_DV).astype(F32),
            'hexp': (ck == cv).astype(BF16),
            'gsum': (jnp.arange(GLA_BLOCK)[:, None] == ij).astype(BF16)}


def _merge_kernel(x_ref, ya_ref, yb_ref, yc_ref, yd_ref, ga_ref, gb_ref, gc_ref, gd_ref, mg_ref,
                  wb_ref, wo_ref, lng_ref, lnb_ref, o_ref):
    mixed = None
    for br, (y_ref, g_ref) in enumerate(((ya_ref, ga_ref), (yb_ref, gb_ref), (yc_ref, gc_ref), (yd_ref, gd_ref))):
        act = (y_ref[...] * jax.nn.silu(g_ref[...])).astype(BF16)
        term = jax.nn.sigmoid(mg_ref[:, br * D_MODEL:(br + 1) * D_MODEL]) * _dot(act, wb_ref[br])
        mixed = term if mixed is None else mixed + term
    z = DN_ALPHA * x_ref[...] + _dot(mixed.astype(BF16), wo_ref[...])
    mu = jnp.mean(z, axis=1, keepdims=True)
    zc = z - mu
    var = jnp.mean(zc * zc, axis=1, keepdims=True)
    o_ref[...] = zc * lax.rsqrt(var + LN_EPS) * lng_ref[...] + lnb_ref[...]


def _merge_call(x, h, ya, yb, yc, yd, wb, wo, lng, lnb):
    rows = x.shape[0]
    tm = min(rows, 256)
    bw = BRANCH_W
    ybs = pl.BlockSpec((tm, bw), lambda i: (i, 0))
    gate = lambda name: pl.BlockSpec((tm, bw), lambda i: (i, PROJ_OFF[name] // bw))
    return pl.pallas_call(
        _merge_kernel,
        grid=(rows // tm,),
        in_specs=[pl.BlockSpec((tm, D_MODEL), lambda i: (i, 0)), ybs, ybs, ybs, ybs,
                  gate('a_gate'), gate('b_gate'), gate('c_gate'), gate('d_gate'),
                  pl.BlockSpec((tm, N_BRANCH * D_MODEL), lambda i: (i, 0)),
                  pl.BlockSpec((N_BRANCH, bw, D_MODEL), lambda i: (0, 0, 0)),
                  pl.BlockSpec((D_MODEL, D_MODEL), lambda i: (0, 0)),
                  pl.BlockSpec((1, D_MODEL), lambda i: (0, 0)),
                  pl.BlockSpec((1, D_MODEL), lambda i: (0, 0))],
        out_specs=pl.BlockSpec((tm, D_MODEL), lambda i: (i, 0)),
        out_shape=jax.ShapeDtypeStruct((rows, D_MODEL), F32),
        compiler_params=_params("parallel"),
        name="merge_norm",
    )(x, ya, yb, yc, yd, h, h, h, h, h, wb, wo, lng, lnb)


def _split_w_in(w_in):
    out, off = {}, 0
    for name, width in IN_LAYOUT:
        out[name] = w_in[:, off:off + width]
        off += width
    return out


def _regroup_kernel(w_ref, proj_ref, key_ref):
    cols = _split_w_in(w_ref[...])
    pad = jnp.zeros((w_ref.shape[0], MISC_PAD), F32)
    proj_ref[...] = jnp.concatenate([cols[name] for name in PROJ_ORDER] + [pad], axis=1).astype(BF16)
    k, v, ki = cols['b_k'], cols['b_v'], cols['b_ki']
    dh = DSA_HEAD_DIM
    key_ref[...] = jnp.concatenate([k[:, :dh], k[:, :dh], k[:, dh:], k[:, dh:],
                                    v[:, :dh], v[:, :dh], v[:, dh:], v[:, dh:], ki, ki], axis=1).astype(BF16)


def _regroup_weights(w_in):
    depth, rows, width = w_in.shape
    tr = 128
    return pl.pallas_call(
        _regroup_kernel,
        grid=(depth, rows // tr),
        in_specs=[pl.BlockSpec((None, tr, width), lambda l, i: (l, i, 0))],
        out_specs=[pl.BlockSpec((None, tr, PROJ_W), lambda l, i: (l, i, 0)),
                   pl.BlockSpec((None, tr, HK_W), lambda l, i: (l, i, 0))],
        out_shape=[jax.ShapeDtypeStruct((depth, rows, PROJ_W), BF16),
                   jax.ShapeDtypeStruct((depth, rows, HK_W), BF16)],
        compiler_params=_params("parallel", "parallel"),
        name="regroup_weights",
    )(w_in)


def _dup_keys(k, v, ki):
    return jnp.concatenate([k[:, :, 0], k[:, :, 0], k[:, :, 1], k[:, :, 1],
                            v[:, :, 0], v[:, :, 0], v[:, :, 1], v[:, :, 1], ki, ki], axis=-1).astype(BF16)


def _layer(x, lw, st):
    n, t, _ = x.shape
    x2 = x.reshape(n * t, D_MODEL)
    h = _matmul(x2, lw['w_proj'], F32, tn=PROJ_W // 9)
    hk_new = _matmul(x2, lw['w_key'], BF16, tn=HK_W).reshape(n, t, HK_W)
    if st is None:
        hk, n_keys, b_off = hk_new, t, 0
        s5r = jnp.zeros((n, 1, S5_LANES), F32)
        s5i = jnp.zeros((n, 1, S5_LANES), F32)
        gdn0 = jnp.zeros((n, GDN_HEADS, GDN_DK, GDN_DV), F32)
        conv0 = jnp.zeros((n, SUBLANE, GDN_CONV_CH), F32)
        gla0 = jnp.zeros((n, GLA_DV, GLA_HEADS * GLA_DK), F32)
    else:
        past = st['k'].shape[1]
        n_keys = past + t
        b_off = past // Q_BLOCK
        lp = -(-n_keys // KEY_TILE) * KEY_TILE
        hk = jnp.concatenate([_dup_keys(st['k'], st['v'], st['kidx']), hk_new,
                              jnp.zeros((n, lp - n_keys, HK_W), BF16)], axis=1)
        s5r = st['s5_re'].reshape(n, 1, S5_LANES)
        s5i = st['s5_im'].reshape(n, 1, S5_LANES)
        gdn0 = st['gdn']
        conv0 = jnp.concatenate([jnp.zeros((n, CONV_LEAD, GDN_CONV_CH), F32), st['gdn_conv']], axis=1)
        gla0 = jnp.moveaxis(st['gla'], 3, 1).reshape(n, GLA_DV, GLA_HEADS * GLA_DK)
    topk = min(DSA_TOPK_MAX, n_keys // 4)

    ya, s5r_new, s5i_new = _s5_call(h, n, t, s5r, s5i, lw['s5'])
    yb = _dsa_call(h, hk, n, t, lw['bias'], lw['tril'], b_off=b_off, n_keys=n_keys, topk=topk)
    h3 = h.reshape(n, t, PROJ_W)
    yc, gdn_new, conv_new = _gdn_call(h3, gdn0, conv0, lw['gdn'])
    yd, gla_new = _gla_call(h3, gla0, lw['gla'])
    yc = yc.reshape(n * t, GDN_VW)
    yd = yd.reshape(n * t, GLA_HEADS * GLA_DV)
    y = _merge_call(x2, h, ya, yb, yc, yd, lw['w_branch'], lw['w_out'], lw['ln_g'], lw['ln_b'])

    def cols(name, width):
        return h[:, PROJ_OFF[name]:PROJ_OFF[name] + width].reshape(n, t, width)

    new = {'k': cols('b_k', DSA_KV_HEADS * DSA_HEAD_DIM).reshape(n, t, DSA_KV_HEADS, DSA_HEAD_DIM),
           'v': cols('b_v', DSA_KV_HEADS * DSA_HEAD_DIM).reshape(n, t, DSA_KV_HEADS, DSA_HEAD_DIM),
           'kidx': cols('b_ki', IDX_DIM),
           's5_re': s5r_new.reshape(n, S5_GROUPS, S5_STATE), 's5_im': s5i_new.reshape(n, S5_GROUPS, S5_STATE),
           'gdn': gdn_new, 'gdn_conv': conv_new[:, CONV_LEAD:],
           'gla': jnp.moveaxis(gla_new.reshape(n, GLA_DV, GLA_HEADS, GLA_DK), 1, 3)}
    return y.reshape(n, t, D_MODEL), new


def kernel(x_prompt, x_sample, cache_k, cache_v, cache_kidx, state_s5_re, state_s5_im, state_gdn, state_gdn_conv, state_gla, w_in, w_branch, w_out, ln_g, ln_b, rel_bias, s5_a_re, s5_a_im, s5_log_dt, s5_b_re, s5_b_im, s5_c_re, s5_c_im, s5_d, s5_w_glu, gdn_conv, gdn_a_log, gdn_dt_bias, gdn_norm, gla_w_g2, gla_b_g, gla_norm):
    bias = _dsa_bias_tables(rel_bias)
    tril = (jnp.arange(KEY_BLOCK)[:, None] >= jnp.arange(KEY_BLOCK)[None, :]).astype(BF16)
    yp, ys = x_prompt, x_sample
    new_p, new_s = [], []
    w_proj, w_key = _regroup_weights(w_in)
    for l in range(DEPTH):
        lw = {'w_proj': w_proj[l], 'w_key': w_key[l], 'bias': bias, 'tril': tril,
              'w_branch': w_branch[l].astype(BF16), 'w_out': w_out[l].astype(BF16),
              'ln_g': ln_g[l].reshape(1, D_MODEL), 'ln_b': ln_b[l].reshape(1, D_MODEL),
              's5': _s5_prepare(s5_a_re[l], s5_a_im[l], s5_log_dt[l], s5_b_re[l], s5_b_im[l],
                                s5_c_re[l], s5_c_im[l], s5_d[l], s5_w_glu[l]),
              'gdn': {'convw': gdn_conv[l], 'alog': _lane_place(gdn_a_log[l], MISC_A),
                      'dtb': _lane_place(gdn_dt_bias[l], MISC_A), 'norm': gdn_norm[l].reshape(1, GDN_DV)},
              'gla': _gla_prepare(gla_w_g2[l], gla_b_g[l], gla_norm[l])}
        st = {'k': cache_k[l], 'v': cache_v[l], 'kidx': cache_kidx[l], 's5_re': state_s5_re[l],
              's5_im': state_s5_im[l], 'gdn': state_gdn[l], 'gdn_conv': state_gdn_conv[l], 'gla': state_gla[l]}
        yp, stp = _layer(yp, lw, None)
        ys, sts = _layer(ys, lw, st)
        new_p.append(stp)
        new_s.append(sts)
    names = ('k', 'v', 'kidx', 's5_re', 's5_im', 'gdn', 'gdn_conv', 'gla')
    stack = lambda states, name: jnp.stack([s[name] for s in states], axis=0)
    return ((yp, ys) + tuple(stack(new_p, nm) for nm in names) + tuple(stack(new_s, nm) for nm in names))
```

```python
import functools
import math

import jax
import jax.numpy as jnp
from jax import lax
from jax.experimental import pallas as pl
from jax.experimental.pallas import tpu as pltpu

F32 = jnp.float32
BF16 = jnp.bfloat16
HIGHEST = lax.Precision.HIGHEST

D_MODEL = 1024
DEPTH = 2
CHUNK = 64
N_BRANCH = 4
BRANCH_W = 512
S5_GROUP = 16
S5_GROUPS = BRANCH_W // S5_GROUP
S5_STATE = 64
S5_LANES = S5_GROUPS * S5_STATE
S5_SPLIT = 2
DSA_HEADS = 8
DSA_KV_HEADS = 2
DSA_HEAD_DIM = 64
IDX_HEADS = 4
IDX_DIM = 64
DSA_TOPK_MAX = 256
REL_BUCKETS = 32
REL_MAX_DIST = 128
GDN_HEADS = 4
GDN_DK = 128
GDN_DV = 128
GDN_QK = GDN_HEADS * GDN_DK
GDN_VW = GDN_HEADS * GDN_DV
CONV_W = 4
GDN_CONV_CH = 2 * GDN_QK + GDN_VW
GDN_CHUNKS_PER_STEP = 2
CONV_LEAD = 8 - (CONV_W - 1)
GLA_HEADS = 4
GLA_DK = 64
GLA_DV = 128
GLA_GATE_RANK = 16
GLA_TAU = 16.0
GLA_BLOCK = 16
LN_EPS = 1e-5
RMS_EPS = 1e-6
DN_ALPHA = (2 * DEPTH) ** 0.25
LOG2E = math.log2(math.e)

LANE = 128
SUBLANE = 8
KEY_BLOCK = 128
KEY_SUB = 4
KEY_TILE = KEY_SUB * KEY_BLOCK
SOFTMAX_CEIL = 2.0 ** 120
SOFTMAX_FLOOR = 2.0 ** -60
FOLD_ROWS = 64
Q_BLOCK = 128
Q_TILE = 256
VMEM_LIMIT = 56 * 1024 * 1024

IN_LAYOUT = (
    ('a_u', BRANCH_W), ('a_gate', BRANCH_W),
    ('b_q', DSA_HEADS * DSA_HEAD_DIM), ('b_k', DSA_KV_HEADS * DSA_HEAD_DIM),
    ('b_v', DSA_KV_HEADS * DSA_HEAD_DIM), ('b_qi', IDX_HEADS * IDX_DIM), ('b_ki', IDX_DIM),
    ('b_wi', IDX_HEADS), ('b_gate', BRANCH_W),
    ('c_qkv', GDN_CONV_CH), ('c_beta', GDN_HEADS), ('c_a', GDN_HEADS), ('c_gate', BRANCH_W),
    ('d_q', GLA_HEADS * GLA_DK), ('d_k', GLA_HEADS * GLA_DK), ('d_v', GLA_HEADS * GLA_DV),
    ('d_g', GLA_GATE_RANK), ('d_gate', BRANCH_W),
    ('merge', N_BRANCH * D_MODEL),
)

PROJ_ORDER = ('merge', 'a_u', 'a_gate', 'b_q', 'b_qi', 'b_k', 'b_v', 'c_qkv', 'b_gate', 'c_gate',
              'd_v', 'd_gate', 'd_q', 'd_k', 'b_ki', 'b_wi', 'c_beta', 'c_a', 'd_g')
MISC_WI = IDX_DIM
MISC_BETA = MISC_WI + IDX_HEADS
MISC_A = MISC_BETA + GDN_HEADS
MISC_G = MISC_A + GDN_HEADS
MISC_PAD = LANE - (MISC_G + GLA_GATE_RANK)


def _proj_offsets():
    widths = dict(IN_LAYOUT)
    off, out = 0, {}
    for name in PROJ_ORDER:
        out[name] = off
        off += widths[name]
    return out, off + MISC_PAD


PROJ_OFF, PROJ_W = _proj_offsets()
OFF_MISC = PROJ_OFF['b_ki']
HK_W = 5 * LANE
HK_V = 2 * LANE
HK_KI = 4 * LANE


def _dot(a, b, precision=None):
    return jnp.dot(a, b, preferred_element_type=F32, precision=precision)


def _dot_nt(a, b, precision=None):
    return lax.dot_general(a, b, (((1,), (1,)), ((), ())), preferred_element_type=F32, precision=precision)


def _dot_tn(a, b, precision=None):
    return lax.dot_general(a, b, (((0,), (0,)), ((), ())), preferred_element_type=F32, precision=precision)


def _split_bf16(x):
    hi = x.astype(BF16)
    return hi, (x - hi.astype(F32)).astype(BF16)


def _dot3(a, b):
    a_hi, a_lo = _split_bf16(a)
    b_hi, b_lo = _split_bf16(b)
    return _dot(a_hi, b_hi) + (_dot(a_hi, b_lo) + _dot(a_lo, b_hi))


def _params(*sem):
    return pltpu.CompilerParams(dimension_semantics=sem, vmem_limit_bytes=VMEM_LIMIT)


def _softplus(x):
    return jnp.maximum(x, 0.0) + jnp.log1p(jnp.exp(-jnp.abs(x)))


def _iota(shape, dim):
    return lax.broadcasted_iota(jnp.int32, shape, dim)


def _gated(y, gate):
    return (y * jax.nn.silu(gate)).astype(BF16)


def _mm_kernel(x_ref, w_ref, o_ref):
    o_ref[...] = _dot(x_ref[...].astype(BF16), w_ref[...]).astype(o_ref.dtype)


def _matmul(x, w, out_dtype, tn):
    rows, kdim = x.shape
    width = w.shape[1]
    tm = min(rows, 1024)
    return pl.pallas_call(
        _mm_kernel,
        grid=(rows // tm, width // tn),
        in_specs=[pl.BlockSpec((tm, kdim), lambda i, j: (i, 0)),
                  pl.BlockSpec((kdim, tn), lambda i, j: (0, j))],
        out_specs=pl.BlockSpec((tm, tn), lambda i, j: (i, j)),
        out_shape=jax.ShapeDtypeStruct((rows, width), out_dtype),
        compiler_params=_params("parallel", "parallel"),
        name="proj_matmul",
    )(x, w)


def _s5_kernel(u_ref, gate_ref, h0r_ref, h0i_ref, ar_ref, ai_ref, bb_ref, cc_ref, d_ref, wglu_ref,
               ya_ref, hr_out, hi_out, xr_s, xi_s, hr_s, hi_s, cr_s, ci_s, *, tc):
    @pl.when(pl.program_id(1) == 0)
    def _():
        cr_s[...] = h0r_ref[...]
        ci_s[...] = h0i_ref[...]

    u = u_ref[...]
    ub = u.astype(BF16)
    uw = BRANCH_W // S5_SPLIT
    sw = S5_LANES // S5_SPLIT
    for part in range(S5_SPLIT):
        x = _dot(ub[:, part * uw:(part + 1) * uw], bb_ref[part])
        xr_s[:, part * sw:(part + 1) * sw] = x[:, :sw]
        xi_s[:, part * sw:(part + 1) * sw] = x[:, sw:]

    ar = ar_ref[...]
    ai = ai_ref[...]

    def step(t, carry):
        hr, hi = carry
        nhr = ar * hr - ai * hi + xr_s[pl.ds(t, 1), :]
        nhi = ar * hi + ai * hr + xi_s[pl.ds(t, 1), :]
        hr_s[pl.ds(t, 1), :] = nhr
        hi_s[pl.ds(t, 1), :] = nhi
        return nhr, nhi

    hr, hi = lax.fori_loop(0, tc, step, (cr_s[...], ci_s[...]), unroll=8)
    cr_s[...] = hr
    ci_s[...] = hi
    hr_out[...] = hr
    hi_out[...] = hi
    ys = []
    for part in range(S5_SPLIT):
        hcat = jnp.concatenate([hr_s[:, part * sw:(part + 1) * sw], hi_s[:, part * sw:(part + 1) * sw]], axis=1)
        ys.append(_dot(hcat.astype(BF16), cc_ref[part]))
    y = jnp.concatenate(ys, axis=1) + d_ref[...] * u
    y = jax.nn.gelu(y)
    g = _dot(y.astype(BF16), wglu_ref[...])
    ya_ref[...] = _gated(g[:, :BRANCH_W] * jax.nn.sigmoid(g[:, BRANCH_W:]), gate_ref[...])


def _s5_call(h, n, t, h0r, h0i, sp):
    tc = min(t, 256)
    nc = t // tc
    uw = BRANCH_W // S5_SPLIT
    sw = S5_LANES // S5_SPLIT
    const = lambda shape: pl.BlockSpec(shape, lambda i, c: (0,) * len(shape))
    state = pl.BlockSpec((None, 1, S5_LANES), lambda i, c: (i, 0, 0))
    return pl.pallas_call(
        functools.partial(_s5_kernel, tc=tc),
        grid=(n, nc),
        in_specs=[pl.BlockSpec((tc, BRANCH_W), lambda i, c: (i * nc + c, PROJ_OFF['a_u'] // BRANCH_W)),
                  pl.BlockSpec((tc, BRANCH_W), lambda i, c: (i * nc + c, PROJ_OFF['a_gate'] // BRANCH_W)),
                  state, state,
                  const((1, S5_LANES)), const((1, S5_LANES)),
                  const((S5_SPLIT, uw, 2 * sw)), const((S5_SPLIT, 2 * sw, uw)),
                  const((1, BRANCH_W)), const((BRANCH_W, 2 * BRANCH_W))],
        out_specs=[pl.BlockSpec((tc, BRANCH_W), lambda i, c: (i * nc + c, 0)), state, state],
        out_shape=[jax.ShapeDtypeStruct((n * t, BRANCH_W), BF16),
                   jax.ShapeDtypeStruct((n, 1, S5_LANES), F32),
                   jax.ShapeDtypeStruct((n, 1, S5_LANES), F32)],
        scratch_shapes=[pltpu.VMEM((tc, S5_LANES), F32), pltpu.VMEM((tc, S5_LANES), F32),
                        pltpu.VMEM((tc, S5_LANES), F32), pltpu.VMEM((tc, S5_LANES), F32),
                        pltpu.VMEM((1, S5_LANES), F32), pltpu.VMEM((1, S5_LANES), F32)],
        compiler_params=_params("parallel", "arbitrary"),
        name="s5_branch",
    )(h, h, h0r, h0i, sp['ar'], sp['ai'], sp['bb'], sp['cc'], sp['d'], sp['wglu'])


def _s5_prepare(a_re, a_im, log_dt, b_re, b_im, c_re, c_im, d, w_glu):
    lam_r = jnp.minimum(a_re, -1e-4)
    lam_i = a_im
    dt = jnp.exp(log_dt)[:, None]
    mag = jnp.exp(lam_r * dt)
    ar = mag * jnp.cos(lam_i * dt)
    ai = mag * jnp.sin(lam_i * dt)
    den = lam_r * lam_r + lam_i * lam_i
    nr = ar - 1.0
    cr = (nr * lam_r + ai * lam_i) / den
    ci = (ai * lam_r - nr * lam_i) / den
    bb_r = cr[..., None] * b_re - ci[..., None] * b_im
    bb_i = cr[..., None] * b_im + ci[..., None] * b_re
    eye = jnp.eye(S5_GROUPS, dtype=F32)

    def blockdiag_in(bb):
        m = eye[:, None, :, None] * jnp.swapaxes(bb, 1, 2)[:, :, None, :]
        return m.reshape(BRANCH_W, S5_LANES)

    def blockdiag_out(c):
        m = eye[:, None, :, None] * jnp.swapaxes(c, 1, 2)[:, :, None, :]
        return m.reshape(S5_LANES, BRANCH_W)

    uw = BRANCH_W // S5_SPLIT
    sw = S5_LANES // S5_SPLIT
    bbr, bbi = blockdiag_in(bb_r), blockdiag_in(bb_i)
    ccr, cci = blockdiag_out(c_re), blockdiag_out(c_im)
    bb = jnp.stack([jnp.concatenate([m[p * uw:(p + 1) * uw, p * sw:(p + 1) * sw] for m in (bbr, bbi)], axis=1)
                    for p in range(S5_SPLIT)]).astype(BF16)
    cc = jnp.stack([jnp.concatenate([ccr[p * sw:(p + 1) * sw, p * uw:(p + 1) * uw],
                                     -cci[p * sw:(p + 1) * sw, p * uw:(p + 1) * uw]], axis=0)
                    for p in range(S5_SPLIT)]).astype(BF16)
    return {'ar': ar.reshape(1, S5_LANES), 'ai': ai.reshape(1, S5_LANES), 'bb': bb, 'cc': cc,
            'd': d.reshape(1, BRANCH_W), 'wglu': w_glu.astype(BF16)}


def _sortable(x):
    bits = lax.bitcast_convert_type(x, jnp.int32)
    key = bits ^ ((bits >> 31) & jnp.int32(0x7FFFFFFF))
    return jnp.where(key == -1, 0, key)


def _dsa_kernel(q_ref, qi_ref, misc_ref, gate_ref, hk_ref, bias_ref, tril_ref, o_ref, sc_s, pk_s, acc_s, far_s, *,
                qb, b_off, n_keys, topk):
    nsub = max(qb // Q_BLOCK, 1)
    qw = qb // nsub
    b = pl.program_id(1) * nsub + b_off
    nkt = (b + nsub - 1 + KEY_SUB) // KEY_SUB
    fold = KEY_TILE // FOLD_ROWS
    key_row = _iota((KEY_TILE, qb), 0)
    qpos = b * Q_BLOCK + _iota((1, qb), 1)
    limit = jnp.minimum(((qpos >> 6) + 1) * CHUNK, n_keys)
    low_half = _iota((qb, LANE), 1) < DSA_HEAD_DIM

    def key_rows(ref_cols, j):
        return hk_ref[pl.ds(pl.multiple_of(j * KEY_TILE, KEY_TILE), KEY_TILE), ref_cols:ref_cols + LANE]

    def half_masked(x, hd):
        pair = x[:, (hd // 2) * LANE:(hd // 2 + 1) * LANE]
        return jnp.where(low_half == (hd % 2 == 0), pair, 0.0).astype(BF16)

    misc = misc_ref[...]
    pick = jnp.where(_iota((SUBLANE, LANE), 1) == MISC_WI + _iota((SUBLANE, LANE), 0), 1.0, 0.0)
    wi_t = _dot_nt(pick, misc, precision=HIGHEST) * (IDX_DIM ** -0.5 * IDX_HEADS ** -0.5)
    qi = qi_ref[...]
    qi_stack = jnp.concatenate([half_masked(qi, hd) for hd in range(IDX_HEADS)], axis=0)

    def pack16(v):
        half = KEY_TILE // 2
        return (v[0:half] & 0xFFFF) | (v[half:] << 16)

    def score_block(j, carry):
        z = _dot_nt(key_rows(HK_KI, j), qi_stack)
        sc = jnp.zeros((KEY_TILE, qb), F32)
        for hd in range(IDX_HEADS):
            sc = sc + jnp.maximum(z[:, hd * qb:(hd + 1) * qb], 0.0) * wi_t[hd:hd + 1, :]
        sc = jnp.where(j * KEY_TILE + key_row < limit, sc, -jnp.inf)
        key = _sortable(sc)
        sc_s[j] = key
        pk_s[j] = pack16(key >> 16)
        return carry

    lax.fori_loop(0, nkt, score_block, 0)

    def count_ge(cand):
        def body(j, acc):
            return acc + jnp.where(sc_s[j] >= cand, 1.0, 0.0).reshape(fold, FOLD_ROWS, qb).sum(axis=0)
        acc = lax.fori_loop(0, nkt, body, jnp.zeros((FOLD_ROWS, qb), F32))
        return jnp.sum(acc, axis=0, keepdims=True)

    one16 = jnp.ones((KEY_TILE, qb), jnp.int16)
    zero16 = jnp.zeros((KEY_TILE, qb), jnp.int16)

    def count16_ge(cand):
        word = jnp.broadcast_to((cand & 0xFFFF) | (cand << 16), (KEY_TILE // 2, qb))
        c16 = pltpu.bitcast(word, jnp.int16)

        def body(j, acc):
            ind = jnp.where(pltpu.bitcast(pk_s[j], jnp.int16) >= c16, one16, zero16)
            for i in range(fold):
                acc = acc + ind[i * FOLD_ROWS:(i + 1) * FOLD_ROWS]
            return acc
        acc = lax.fori_loop(0, nkt, body, jnp.zeros((FOLD_ROWS, qb), jnp.int16))
        return jnp.sum(acc.astype(F32), axis=0, keepdims=True)

    def radix_search(start, want):
        def bit_step(i, res):
            cand = res + (jnp.int32(1) << (14 - i))
            return jnp.where(count16_ge(cand) >= want, cand, res)
        return lax.fori_loop(0, 15, bit_step, start)

    kf = float(topk)
    top = radix_search(jnp.where(count16_ge(jnp.zeros((1, qb), jnp.int32)) >= kf, 0, -32768), kf)
    int16_max = jnp.iinfo(jnp.int16).max
    above = jnp.where(top == int16_max, 0.0, count16_ge(jnp.minimum(top + 1, int16_max)))
    want_low = kf - above

    def pack_low(j, carry):
        key = sc_s[j]
        pk_s[j] = pack16(jnp.where((key >> 16) == top, (key >> 1) & 0x7FFF, -1))
        return carry

    lax.fori_loop(0, nkt, pack_low, 0)
    low = radix_search(jnp.zeros((1, qb), jnp.int32), want_low)
    base = (top << 16) | (low << 1)
    thr = jnp.where(count_ge(base | 1) >= kf, base | 1, base)
    need = kf - count_ge(thr + 1)

    tril = tril_ref[...]

    def select_block(j, ties):
        keys = sc_s[j]
        eq = keys == thr
        eq_b = jnp.where(eq, 1.0, 0.0).astype(BF16)
        parts, carry = [], ties
        for s in range(KEY_SUB):
            part = _dot(tril, eq_b[s * KEY_BLOCK:(s + 1) * KEY_BLOCK]) + carry
            parts.append(part)
            carry = part[KEY_BLOCK - 1:KEY_BLOCK, :]
        prefix = jnp.concatenate(parts, axis=0)
        sel = ((keys > thr) | (eq & (prefix <= need))) & (j * KEY_TILE + key_row < limit)
        sc_s[j] = lax.bitcast_convert_type(jnp.where(sel, 0.0, -jnp.inf), jnp.int32)
        return prefix[KEY_TILE - 1:KEY_TILE, :]

    lax.fori_loop(0, nkt, select_block, jnp.zeros((1, qb), F32))

    q = q_ref[...] * (DSA_HEAD_DIM ** -0.5 * LOG2E)
    rep = DSA_HEADS // DSA_KV_HEADS
    eye = jnp.where(_iota((LANE, LANE), 0) == _iota((LANE, LANE), 1), 1.0, 0.0)
    value_lane = _iota((KEY_TILE, LANE), 1) < DSA_HEAD_DIM
    n_pairs = DSA_HEADS // 2
    q_pair = [jnp.concatenate([half_masked(q, 2 * c), half_masked(q, 2 * c + 1)], axis=0) for c in range(n_pairs)]
    far_bias = [jnp.concatenate([bias_ref[hd, 2, 0:1, 0:qw]] * nsub, axis=1)
                for hd in range(DSA_HEADS)]
    n_far = jnp.maximum((b - 1) // KEY_SUB, 0)

    def tile_operands(j):
        mask = lax.bitcast_convert_type(sc_s[j], F32)
        kd = [key_rows(grp * LANE, j) for grp in range(DSA_KV_HEADS)]
        zs = [_dot_nt(kd[(2 * c) // rep], q_pair[c]) for c in range(n_pairs)]
        vd = [jnp.where(value_lane, key_rows(HK_V + grp * LANE, j), 1.0).astype(BF16)
              for grp in range(DSA_KV_HEADS)]
        return mask, zs, vd

    def near_bias(hd, j):
        cols = [jnp.concatenate([bias_ref[hd, jnp.clip(b + u - (KEY_SUB * j + s), 0, 2), :, 0:qw]
                                 for s in range(KEY_SUB)], axis=0) for u in range(nsub)]
        return jnp.concatenate(cols, axis=1)

    def plain_tile(j, carry, far):
        mask, zs, vd = tile_operands(j)
        ps = []
        for hd in range(DSA_HEADS):
            r = hd % 2
            zh = zs[hd // 2][:, r * qb:(r + 1) * qb] + mask
            if not far:
                zh = zh + near_bias(hd, j)
            ps.append(jnp.exp2(zh).astype(BF16))
        acc = far_s if far else acc_s
        for c in range(n_pairs):
            acc[c] += _dot_tn(vd[(2 * c) // rep], jnp.concatenate(ps[2 * c:2 * c + 2], axis=1))
        return carry

    acc_s[...] = jnp.zeros_like(acc_s)
    far_s[...] = jnp.zeros_like(far_s)
    lax.fori_loop(0, n_far, lambda j, c: plain_tile(j, c, True), 0)
    lax.fori_loop(n_far, nkt, lambda j, c: plain_tile(j, c, False), 0)
    healthy = jnp.ones((1, 1), F32)
    for c in range(n_pairs):
        scale = jnp.concatenate([jnp.exp2(far_bias[2 * c]), jnp.exp2(far_bias[2 * c + 1])], axis=1)
        total = far_s[c] * scale + acc_s[c]
        acc_s[c] = total
        norm = total[DSA_HEAD_DIM:DSA_HEAD_DIM + 1, :]
        ok = (jnp.abs(total) < SOFTMAX_CEIL) & (norm > SOFTMAX_FLOOR)
        healthy = jnp.minimum(healthy, jnp.min(jnp.where(ok, 1.0, 0.0), keepdims=True))

    def attend(j, ms, far):
        mask, zs, vd = tile_operands(j)
        new_ms, ps, alphas = [], [], []
        for hd in range(DSA_HEADS):
            r = hd % 2
            zh = zs[hd // 2][:, r * qb:(r + 1) * qb] + mask
            if not far:
                zh = zh + near_bias(hd, j)
            tile_max = jnp.max(zh.reshape(fold, FOLD_ROWS, qb).max(axis=0), axis=0, keepdims=True)
            shift = far_bias[hd] if far else 0.0
            m_new = jnp.maximum(ms[hd], tile_max + shift)
            ps.append(jnp.exp2(zh - (m_new - shift)).astype(BF16))
            alphas.append(jnp.exp2(ms[hd] - m_new))
            new_ms.append(m_new)
        for c in range(n_pairs):
            acc_s[c] = (acc_s[c] * jnp.concatenate(alphas[2 * c:2 * c + 2], axis=1)
                        + _dot_tn(vd[(2 * c) // rep], jnp.concatenate(ps[2 * c:2 * c + 2], axis=1)))
        return tuple(new_ms)

    @pl.when(healthy[0, 0] < 0.5)
    def _():
        acc_s[...] = jnp.zeros_like(acc_s)
        ms = tuple(jnp.full((1, qb), -1e30, F32) for _ in range(DSA_HEADS))
        ms = lax.fori_loop(0, n_far, lambda j, m: attend(j, m, True), ms)
        lax.fori_loop(n_far, nkt, lambda j, m: attend(j, m, False), ms)

    for c in range(n_pairs):
        halves = []
        for r in range(2):
            blk = acc_s[c, :, r * qb:(r + 1) * qb]
            halves.append(blk[0:DSA_HEAD_DIM] / blk[DSA_HEAD_DIM:DSA_HEAD_DIM + 1])
        o_t = jnp.concatenate(halves, axis=0)
        o_ref[:, c * LANE:(c + 1) * LANE] = _gated(_dot_tn(o_t, eye, precision=HIGHEST),
                                                   gate_ref[:, c * LANE:(c + 1) * LANE])


def _dsa_call(h, hk, n, t, bias, tril, *, b_off, n_keys, topk):
    qb = min(t, Q_TILE)
    nqb = t // qb
    lp = hk.shape[1]
    wq = DSA_HEADS * DSA_HEAD_DIM
    wqi = IDX_HEADS * IDX_DIM
    return pl.pallas_call(
        functools.partial(_dsa_kernel, qb=qb, b_off=b_off, n_keys=n_keys, topk=topk),
        grid=(n, nqb),
        in_specs=[pl.BlockSpec((qb, wq), lambda i, c: (i * nqb + c, PROJ_OFF['b_q'] // wq)),
                  pl.BlockSpec((qb, wqi), lambda i, c: (i * nqb + c, PROJ_OFF['b_qi'] // wqi)),
                  pl.BlockSpec((qb, LANE), lambda i, c: (i * nqb + c, OFF_MISC // LANE)),
                  pl.BlockSpec((qb, BRANCH_W), lambda i, c: (i * nqb + c, PROJ_OFF['b_gate'] // BRANCH_W)),
                  pl.BlockSpec((None, lp, HK_W), lambda i, c: (i, 0, 0)),
                  pl.BlockSpec((DSA_HEADS, 3, KEY_BLOCK, Q_BLOCK), lambda i, c: (0, 0, 0, 0)),
                  pl.BlockSpec((KEY_BLOCK, KEY_BLOCK), lambda i, c: (0, 0))],
        out_specs=pl.BlockSpec((qb, wq), lambda i, c: (i * nqb + c, 0)),
        out_shape=jax.ShapeDtypeStruct((n * t, wq), BF16),
        scratch_shapes=[pltpu.VMEM((lp // KEY_TILE, KEY_TILE, qb), jnp.int32),
                        pltpu.VMEM((lp // KEY_TILE, KEY_TILE // 2, qb), jnp.int32),
                        pltpu.VMEM((DSA_HEADS // 2, LANE, 2 * qb), F32),
                        pltpu.VMEM((DSA_HEADS // 2, LANE, 2 * qb), F32)],
        compiler_params=_params("parallel", "arbitrary"),
        name="dsa_branch",
    )(h, h, h, h, hk, bias, tril)


def _t5_bucket(rel):
    nb = REL_BUCKETS // 2
    max_exact = nb // 2
    ret = jnp.where(rel > 0, nb, 0)
    dist = jnp.abs(rel)
    distf = jnp.maximum(dist, 1).astype(F32)
    large = max_exact + (jnp.log(distf / max_exact) / math.log(REL_MAX_DIST / max_exact)
                         * (nb - max_exact)).astype(jnp.int32)
    large = jnp.minimum(large, nb - 1)
    return ret + jnp.where(dist < max_exact, dist, large)


def _dsa_bias_tables(rel_bias):
    lc = jnp.arange(KEY_BLOCK, dtype=jnp.int32)[:, None]
    qr = jnp.arange(Q_BLOCK, dtype=jnp.int32)[None, :]
    rel = jnp.stack([lc - qr - back * KEY_BLOCK for back in range(3)], axis=0)
    onehot = jax.nn.one_hot(_t5_bucket(rel), REL_BUCKETS, dtype=F32)
    bias = jnp.einsum('klqb,bh->hklq', onehot, rel_bias.astype(F32), precision=HIGHEST)
    return bias * LOG2E


def _gdn_kernel(raw_ref, misc_ref, gate_ref, s0_ref, conv0_ref, convw_ref, alog_ref, dtb_ref, norm_ref,
                o_ref, s_out, conv_out, xp_s, st_s, *, n, c, nsub):
    rows = nsub * c

    @pl.when(pl.program_id(0) == 0)
    def _():
        xp_s[:, 0:SUBLANE, :] = conv0_ref[...]
        st_s[...] = s0_ref[...]

    w = convw_ref[...]
    row = _iota((c, c), 0)
    col = _iota((c, c), 1)
    incl = row >= col
    strict = row > col
    eye = row == col
    tril = jnp.where(incl, 1.0, 0.0)
    eye_f = jnp.where(eye, 1.0, 0.0)
    chains = [(b, sub, hd) for b in range(n) for sub in range(nsub) for hd in range(GDN_HEADS)]

    qkv, beta_all, gc_all = [], [], []
    for b in range(n):
        raw = raw_ref[b]
        xp_s[b, SUBLANE:SUBLANE + rows, :] = raw
        y = raw * w[CONV_W - 1:CONV_W, :]
        for i in range(CONV_W - 1):
            y = y + xp_s[b, CONV_LEAD + i:CONV_LEAD + i + rows, :] * w[i:i + 1, :]
        tail = xp_s[b, rows:rows + SUBLANE, :]
        xp_s[b, 0:SUBLANE, :] = tail
        conv_out[b] = tail
        qkv.append(jax.nn.silu(y))
        misc = misc_ref[b]
        beta_all.append(jax.nn.sigmoid(misc))
        g_all = -jnp.exp(alog_ref[...]) * _softplus(misc + dtb_ref[...])
        gc_all.append([_dot(tril, g_all[sub * c:(sub + 1) * c], precision=HIGHEST)
                       for sub in range(nsub)])

    qs, ks, vs, betas, gcs, dec_s, dec_i = [], [], [], [], [], [], []
    for b, sub, hd in chains:
        x = qkv[b][sub * c:(sub + 1) * c]
        qh = x[:, hd * GDN_DK:(hd + 1) * GDN_DK]
        kh = x[:, GDN_QK + hd * GDN_DK:GDN_QK + (hd + 1) * GDN_DK]
        vs.append(x[:, 2 * GDN_QK + hd * GDN_DV:2 * GDN_QK + (hd + 1) * GDN_DV])
        qs.append(qh * lax.rsqrt(jnp.sum(qh * qh, axis=1, keepdims=True) + RMS_EPS) * (GDN_DK ** -0.5))
        ks.append(kh * lax.rsqrt(jnp.sum(kh * kh, axis=1, keepdims=True) + RMS_EPS))
        betas.append(beta_all[b][sub * c:(sub + 1) * c, MISC_BETA + hd:MISC_BETA + hd + 1])
        gc = gc_all[b][sub][:, MISC_A + hd:MISC_A + hd + 1]
        gcs.append(gc)
        gc_b = jnp.broadcast_to(gc, (c, c))
        gc_row = jnp.sum(jnp.where(eye, gc_b, 0.0), axis=0, keepdims=True)
        diff = gc_b - gc_row
        dec = jnp.exp(diff)
        dec_s.append(jnp.where(strict, dec, 0.0))
        dec_i.append(jnp.where(incl, dec, 0.0))
    kb = [k.astype(BF16) for k in ks]
    qb = [q.astype(BF16) for q in qs]
    pw = [-(beta * _dot_nt(k, k) * d) for beta, k, d in zip(betas, kb, dec_s)]
    inv = [eye_f + p for p in pw]
    for _ in range(int(math.log2(c)) - 1):
        pw = [_dot3(p, p) for p in pw]
        inv = [i + _dot3(i, p) for i, p in zip(inv, pw)]
    egc = [jnp.exp(gc) for gc in gcs]
    sol = [_dot3(i, jnp.concatenate([(beta * e) * k, beta * v], axis=1))
           for i, beta, e, k, v in zip(inv, betas, egc, ks, vs)]
    attn = [(_dot_nt(q, k) * d).astype(BF16) for q, k, d in zip(qb, kb, dec_i)]
    for sub in range(nsub):
        ids = [i for i, ch in enumerate(chains) if ch[1] == sub]
        s_old = [st_s[chains[i][0], chains[i][2]] for i in ids]
        sb = [s.astype(BF16) for s in s_old]
        vb = [(sol[i][:, GDN_DK:] - _dot(sol[i][:, :GDN_DK].astype(BF16), s)).astype(BF16) for i, s in zip(ids, sb)]
        outs = [egc[i] * _dot(qb[i], s) + _dot(attn[i], v) for i, s, v in zip(ids, sb, vb)]
        for i, v, s, o in zip(ids, vb, s_old, outs):
            b, _, hd = chains[i]
            g_last = gcs[i][c - 1:c, :]
            st_s[b, hd] = jnp.exp(g_last) * s + _dot_tn((ks[i] * jnp.exp(g_last - gcs[i])).astype(BF16), v)
            o = o * lax.rsqrt(jnp.mean(o * o, axis=1, keepdims=True) + RMS_EPS) * norm_ref[...]
            o_ref[b, sub * c:(sub + 1) * c, hd * GDN_DV:(hd + 1) * GDN_DV] = _gated(
                o, gate_ref[b, sub * c:(sub + 1) * c, hd * GDN_DV:(hd + 1) * GDN_DV])
    s_out[...] = st_s[...]


def _gdn_call(h3, s0, conv0, gp):
    n, t, _ = h3.shape
    c = min(t, CHUNK)
    nsub = GDN_CHUNKS_PER_STEP if t % (GDN_CHUNKS_PER_STEP * c) == 0 else 1
    rows = nsub * c
    const = lambda shape: pl.BlockSpec(shape, lambda j: (0,) * len(shape))
    state = const((n, GDN_HEADS, GDN_DK, GDN_DV))
    conv = const((n, SUBLANE, GDN_CONV_CH))
    return pl.pallas_call(
        functools.partial(_gdn_kernel, n=n, c=c, nsub=nsub),
        grid=(t // rows,),
        in_specs=[pl.BlockSpec((n, rows, GDN_CONV_CH), lambda j: (0, j, PROJ_OFF['c_qkv'] // GDN_CONV_CH)),
                  pl.BlockSpec((n, rows, LANE), lambda j: (0, j, OFF_MISC // LANE)),
                  pl.BlockSpec((n, rows, BRANCH_W), lambda j: (0, j, PROJ_OFF['c_gate'] // BRANCH_W)),
                  state, conv,
                  const((CONV_W, GDN_CONV_CH)), const((1, LANE)), const((1, LANE)), const((1, GDN_DV))],
        out_specs=[pl.BlockSpec((n, rows, GDN_VW), lambda j: (0, j, 0)), state, conv],
        out_shape=[jax.ShapeDtypeStruct((n, t, GDN_VW), BF16),
                   jax.ShapeDtypeStruct((n, GDN_HEADS, GDN_DK, GDN_DV), F32),
                   jax.ShapeDtypeStruct((n, SUBLANE, GDN_CONV_CH), F32)],
        scratch_shapes=[pltpu.VMEM((n, rows + SUBLANE, GDN_CONV_CH), F32),
                        pltpu.VMEM((n, GDN_HEADS, GDN_DK, GDN_DV), F32)],
        compiler_params=_params("arbitrary"),
        name="gdn_branch",
    )(h3, h3, h3, s0, conv0, gp['convw'], gp['alog'], gp['dtb'], gp['norm'])


def _lane_place(v, off):
    return jnp.zeros((1, LANE), F32).at[0, off:off + v.shape[0]].set(v.astype(F32))


def _gla_kernel(qk_ref, v_ref, misc_ref, gate_ref, s0_ref, wg_ref, bg_ref, norm_ref, hexp_ref, gsum_ref,
                o_ref, s_out, st_s, *, n, tc):
    blk = GLA_BLOCK
    wk = GLA_HEADS * GLA_DK

    @pl.when(pl.program_id(0) == 0)
    def _():
        st_s[...] = s0_ref[...]

    row = _iota((blk, blk), 0)
    col = _iota((blk, blk), 1)
    tril = jnp.where(row >= col, 1.0, 0.0)
    jrow = _iota((blk, wk), 0)
    lane_k = _iota((blk, wk), 1)
    head_mask = [(lane_k >= hd * GLA_DK) & (lane_k < (hd + 1) * GLA_DK) for hd in range(GLA_HEADS)]
    chains = range(n)

    def block(sb, carry):
        r0 = pl.multiple_of(sb * blk, blk)
        qk = [qk_ref[b, pl.ds(r0, blk), :] for b in chains]
        q = [x[:, :wk] * (GLA_DK ** -0.5) for x in qk]
        k = [x[:, wk:] for x in qk]
        v = [v_ref[b, pl.ds(r0, blk), :] for b in chains]
        mb = [misc_ref[b, pl.ds(r0, blk), :].astype(BF16) for b in chains]
        lg = [jax.nn.log_sigmoid(_dot(m, wg_ref[...]) + bg_ref[...]) * (1.0 / GLA_TAU) for m in mb]
        bc = [_dot(tril, x, precision=HIGHEST) for x in lg]
        dmat = []
        for b in chains:
            slabs = []
            for i in range(blk):
                keep = jrow <= i
                dec = jnp.where(keep, jnp.exp(bc[b][i:i + 1, :] - bc[b]), 0.0)
                slabs.append(dec * q[b][i:i + 1, :] * k[b])
            dmat.append(jnp.concatenate(slabs, axis=0).astype(BF16))
        a_exp = [_dot(d, hexp_ref[...]) for d in dmat]
        prod = [a * jnp.concatenate([x] * blk, axis=0) for a, x in zip(a_exp, v)]
        o = [_dot(gsum_ref[...], p.astype(BF16)) for p in prod]
        qe = [x * jnp.exp(c) for x, c in zip(q, bc)]
        kt = [x * jnp.exp(c[blk - 1:blk, :] - c) for x, c in zip(k, bc)]
        s = [st_s[b] for b in chains]
        sbf = [x.astype(BF16) for x in s]
        for b in chains:
            upd = jnp.zeros_like(s[b])
            parts = []
            for hd in range(GLA_HEADS):
                parts.append(_dot_nt(jnp.where(head_mask[hd], qe[b], 0.0).astype(BF16), sbf[b]))
                vh = v[b][:, hd * GLA_DV:(hd + 1) * GLA_DV].astype(BF16)
                upd = upd + _dot_tn(vh, jnp.where(head_mask[hd], kt[b], 0.0).astype(BF16))
            ob = o[b] + jnp.concatenate(parts, axis=1)
            st_s[b] = jnp.exp(bc[b][blk - 1:blk, :]) * s[b] + upd
            for hd in range(GLA_HEADS):
                oh = ob[:, hd * GLA_DV:(hd + 1) * GLA_DV]
                oh = oh * lax.rsqrt(jnp.mean(oh * oh, axis=1, keepdims=True) + RMS_EPS) * norm_ref[...]
                o_ref[b, pl.ds(r0, blk), hd * GLA_DV:(hd + 1) * GLA_DV] = _gated(
                    oh, gate_ref[b, pl.ds(r0, blk), hd * GLA_DV:(hd + 1) * GLA_DV])
        return carry

    lax.fori_loop(0, tc // blk, block, 0, unroll=4 if tc // blk > 1 else 1)
    s_out[...] = st_s[...]


def _gla_call(h3, s0, gp):
    n, t, _ = h3.shape
    tc = min(t, 128)
    wk = GLA_HEADS * GLA_DK
    wv = GLA_HEADS * GLA_DV
    const = lambda shape: pl.BlockSpec(shape, lambda j: (0,) * len(shape))
    state = const((n, GLA_DV, wk))
    return pl.pallas_call(
        functools.partial(_gla_kernel, n=n, tc=tc),
        grid=(t // tc,),
        in_specs=[pl.BlockSpec((n, tc, 2 * wk), lambda j: (0, j, PROJ_OFF['d_q'] // (2 * wk))),
                  pl.BlockSpec((n, tc, wv), lambda j: (0, j, PROJ_OFF['d_v'] // wv)),
                  pl.BlockSpec((n, tc, LANE), lambda j: (0, j, OFF_MISC // LANE)),
                  pl.BlockSpec((n, tc, BRANCH_W), lambda j: (0, j, PROJ_OFF['d_gate'] // BRANCH_W)),
                  state,
                  const((LANE, wk)), const((1, wk)), const((1, GLA_DV)),
                  const((wk, wv)), const((GLA_BLOCK, GLA_BLOCK * GLA_BLOCK))],
        out_specs=[pl.BlockSpec((n, tc, wv), lambda j: (0, j, 0)), state],
        out_shape=[jax.ShapeDtypeStruct((n, t, wv), BF16),
                   jax.ShapeDtypeStruct((n, GLA_DV, wk), F32)],
        scratch_shapes=[pltpu.VMEM((n, GLA_DV, wk), F32)],
        compiler_params=_params("arbitrary"),
        name="gla_branch",
    )(h3, h3, h3, h3, s0, gp['wg'], gp['bg'], gp['norm'], gp['hexp'], gp['gsum'])


def _gla_prepare(w_g2, b_g, norm):
    wk = GLA_HEADS * GLA_DK
    wg = jnp.zeros((LANE, wk), F32).at[MISC_G:MISC_G + GLA_GATE_RANK].set(w_g2).astype(BF16)
    ck = jnp.arange(wk)[:, None] // GLA_DK
    cv = jnp.arange(GLA_HEADS * GLA_DV)[None, :] // GLA_DV
    ij = jnp.arange(GLA_BLOCK * GLA_BLOCK)[None, :] // GLA_BLOCK
    return {'wg': wg, 'bg': b_g.reshape(1, wk).astype(F32),
            'norm': norm.reshape(1, GLA_DV).astype(F32),
            'hexp': (ck == cv).astype(BF16),
            'gsum': (jnp.arange(GLA_BLOCK)[:, None] == ij).astype(BF16)}


def _merge_kernel(x_ref, ya_ref, yb_ref, yc_ref, yd_ref, mg_ref, wb_ref, wo_ref, lng_ref, lnb_ref, o_ref):
    mixed = None
    for br, act_ref in enumerate((ya_ref, yb_ref, yc_ref, yd_ref)):
        term = jax.nn.sigmoid(mg_ref[:, br * D_MODEL:(br + 1) * D_MODEL]) * _dot(act_ref[...], wb_ref[br])
        mixed = term if mixed is None else mixed + term
    z = DN_ALPHA * x_ref[...] + _dot(mixed.astype(BF16), wo_ref[...])
    mu = jnp.mean(z, axis=1, keepdims=True)
    zc = z - mu
    var = jnp.mean(zc * zc, axis=1, keepdims=True)
    o_ref[...] = zc * lax.rsqrt(var + LN_EPS) * lng_ref[...] + lnb_ref[...]


def _merge_call(x, h, ya, yb, yc, yd, wb, wo, lng, lnb):
    rows = x.shape[0]
    tm = min(rows, 256)
    bw = BRANCH_W
    ybs = pl.BlockSpec((tm, bw), lambda i: (i, 0))
    return pl.pallas_call(
        _merge_kernel,
        grid=(rows // tm,),
        in_specs=[pl.BlockSpec((tm, D_MODEL), lambda i: (i, 0)), ybs, ybs, ybs, ybs,
                  pl.BlockSpec((tm, N_BRANCH * D_MODEL), lambda i: (i, 0)),
                  pl.BlockSpec((N_BRANCH, bw, D_MODEL), lambda i: (0, 0, 0)),
                  pl.BlockSpec((D_MODEL, D_MODEL), lambda i: (0, 0)),
                  pl.BlockSpec((1, D_MODEL), lambda i: (0, 0)),
                  pl.BlockSpec((1, D_MODEL), lambda i: (0, 0))],
        out_specs=pl.BlockSpec((tm, D_MODEL), lambda i: (i, 0)),
        out_shape=jax.ShapeDtypeStruct((rows, D_MODEL), F32),
        compiler_params=_params("parallel"),
        name="merge_norm",
    )(x, ya, yb, yc, yd, h, wb, wo, lng, lnb)


def _split_w_in(w_in):
    out, off = {}, 0
    for name, width in IN_LAYOUT:
        out[name] = w_in[:, off:off + width]
        off += width
    return out


def _regroup_kernel(w_ref, proj_ref, key_ref):
    cols = _split_w_in(w_ref[...])
    pad = jnp.zeros((w_ref.shape[0], MISC_PAD), F32)
    proj_ref[...] = jnp.concatenate([cols[name] for name in PROJ_ORDER] + [pad], axis=1).astype(BF16)
    k, v, ki = cols['b_k'], cols['b_v'], cols['b_ki']
    dh = DSA_HEAD_DIM
    key_ref[...] = jnp.concatenate([k[:, :dh], k[:, :dh], k[:, dh:], k[:, dh:],
                                    v[:, :dh], v[:, :dh], v[:, dh:], v[:, dh:], ki, ki], axis=1).astype(BF16)


def _regroup_weights(w_in):
    depth, rows, width = w_in.shape
    tr = 128
    return pl.pallas_call(
        _regroup_kernel,
        grid=(depth, rows // tr),
        in_specs=[pl.BlockSpec((None, tr, width), lambda l, i: (l, i, 0))],
        out_specs=[pl.BlockSpec((None, tr, PROJ_W), lambda l, i: (l, i, 0)),
                   pl.BlockSpec((None, tr, HK_W), lambda l, i: (l, i, 0))],
        out_shape=[jax.ShapeDtypeStruct((depth, rows, PROJ_W), BF16),
                   jax.ShapeDtypeStruct((depth, rows, HK_W), BF16)],
        compiler_params=_params("parallel", "parallel"),
        name="regroup_weights",
    )(w_in)


def _dup_keys(k, v, ki):
    return jnp.concatenate([k[:, :, 0], k[:, :, 0], k[:, :, 1], k[:, :, 1],
                            v[:, :, 0], v[:, :, 0], v[:, :, 1], v[:, :, 1], ki, ki], axis=-1).astype(BF16)


def _layer(x, lw, st):
    n, t, _ = x.shape
    x2 = x.reshape(n * t, D_MODEL)
    h = _matmul(x2, lw['w_proj'], F32, tn=PROJ_W // 9)
    hk_new = _matmul(x2, lw['w_key'], BF16, tn=HK_W).reshape(n, t, HK_W)
    if st is None:
        hk, n_keys, b_off = hk_new, t, 0
        s5r = jnp.zeros((n, 1, S5_LANES), F32)
        s5i = jnp.zeros((n, 1, S5_LANES), F32)
        gdn0 = jnp.zeros((n, GDN_HEADS, GDN_DK, GDN_DV), F32)
        conv0 = jnp.zeros((n, SUBLANE, GDN_CONV_CH), F32)
        gla0 = jnp.zeros((n, GLA_DV, GLA_HEADS * GLA_DK), F32)
    else:
        past = st['k'].shape[1]
        n_keys = past + t
        b_off = past // Q_BLOCK
        lp = -(-n_keys // KEY_TILE) * KEY_TILE
        hk = jnp.concatenate([_dup_keys(st['k'], st['v'], st['kidx']), hk_new,
                              jnp.zeros((n, lp - n_keys, HK_W), BF16)], axis=1)
        s5r = st['s5_re'].reshape(n, 1, S5_LANES)
        s5i = st['s5_im'].reshape(n, 1, S5_LANES)
        gdn0 = st['gdn']
        conv0 = jnp.concatenate([jnp.zeros((n, CONV_LEAD, GDN_CONV_CH), F32), st['gdn_conv']], axis=1)
        gla0 = jnp.moveaxis(st['gla'], 3, 1).reshape(n, GLA_DV, GLA_HEADS * GLA_DK)
    topk = min(DSA_TOPK_MAX, n_keys // 4)

    ya, s5r_new, s5i_new = _s5_call(h, n, t, s5r, s5i, lw['s5'])
    yb = _dsa_call(h, hk, n, t, lw['bias'], lw['tril'], b_off=b_off, n_keys=n_keys, topk=topk)
    h3 = h.reshape(n, t, PROJ_W)
    yc, gdn_new, conv_new = _gdn_call(h3, gdn0, conv0, lw['gdn'])
    yd, gla_new = _gla_call(h3, gla0, lw['gla'])
    yc = yc.reshape(n * t, GDN_VW)
    yd = yd.reshape(n * t, GLA_HEADS * GLA_DV)
    y = _merge_call(x2, h, ya, yb, yc, yd, lw['w_branch'], lw['w_out'], lw['ln_g'], lw['ln_b'])

    def cols(name, width):
        return h[:, PROJ_OFF[name]:PROJ_OFF[name] + width].reshape(n, t, width)

    new = {'k': cols('b_k', DSA_KV_HEADS * DSA_HEAD_DIM).reshape(n, t, DSA_KV_HEADS, DSA_HEAD_DIM),
           'v': cols('b_v', DSA_KV_HEADS * DSA_HEAD_DIM).reshape(n, t, DSA_KV_HEADS, DSA_HEAD_DIM),
           'kidx': cols('b_ki', IDX_DIM),
           's5_re': s5r_new.reshape(n, S5_GROUPS, S5_STATE), 's5_im': s5i_new.reshape(n, S5_GROUPS, S5_STATE),
           'gdn': gdn_new, 'gdn_conv': conv_new[:, CONV_LEAD:],
           'gla': jnp.moveaxis(gla_new.reshape(n, GLA_DV, GLA_HEADS, GLA_DK), 1, 3)}
    return y.reshape(n, t, D_MODEL), new


def kernel(x_prompt, x_sample, cache_k, cache_v, cache_kidx, state_s5_re, state_s5_im, state_gdn, state_gdn_conv, state_gla, w_in, w_branch, w_out, ln_g, ln_b, rel_bias, s5_a_re, s5_a_im, s5_log_dt, s5_b_re, s5_b_im, s5_c_re, s5_c_im, s5_d, s5_w_glu, gdn_conv, gdn_a_log, gdn_dt_bias, gdn_norm, gla_w_g2, gla_b_g, gla_norm):
    bias = _dsa_bias_tables(rel_bias)
    tril = (jnp.arange(KEY_BLOCK)[:, None] >= jnp.arange(KEY_BLOCK)[None, :]).astype(BF16)
    yp, ys = x_prompt, x_sample
    new_p, new_s = [], []
    w_proj, w_key = _regroup_weights(w_in)
    for l in range(DEPTH):
        lw = {'w_proj': w_proj[l], 'w_key': w_key[l], 'bias': bias, 'tril': tril,
              'w_branch': w_branch[l].astype(BF16), 'w_out': w_out[l].astype(BF16),
              'ln_g': ln_g[l].reshape(1, D_MODEL), 'ln_b': ln_b[l].reshape(1, D_MODEL),
              's5': _s5_prepare(s5_a_re[l], s5_a_im[l], s5_log_dt[l], s5_b_re[l], s5_b_im[l],
                                s5_c_re[l], s5_c_im[l], s5_d[l], s5_w_glu[l]),
              'gdn': {'convw': gdn_conv[l], 'alog': _lane_place(gdn_a_log[l], MISC_A),
                      'dtb': _lane_place(gdn_dt_bias[l], MISC_A), 'norm': gdn_norm[l].reshape(1, GDN_DV)},
              'gla': _gla_prepare(gla_w_g2[l], gla_b_g[l], gla_norm[l])}
        st = {'k': cache_k[l], 'v': cache_v[l], 'kidx': cache_kidx[l], 's5_re': state_s5_re[l],
              's5_im': state_s5_im[l], 'gdn': state_gdn[l], 'gdn_conv': state_gdn_conv[l], 'gla': state_gla[l]}
        yp, stp = _layer(yp, lw, None)
        ys, sts = _layer(ys, lw, st)
        new_p.append(stp)
        new_s.append(sts)
    names = ('k', 'v', 'kidx', 's5_re', 's5_im', 'gdn', 'gdn_conv', 'gla')
    stack = lambda states, name: jnp.stack([s[name] for s in states], axis=0)
    return ((yp, ys) + tuple(stack(new_p, nm) for nm in names) + tuple(stack(new_s, nm) for nm in names))
```

```python
import functools
import math

import jax
import jax.numpy as jnp
from jax import lax
from jax.experimental import pallas as pl
from jax.experimental.pallas import tpu as pltpu

F32 = jnp.float32
BF16 = jnp.bfloat16
HIGHEST = lax.Precision.HIGHEST

D_MODEL = 1024
DEPTH = 2
CHUNK = 64
N_BRANCH = 4
BRANCH_W = 512
S5_GROUP = 16
S5_GROUPS = BRANCH_W // S5_GROUP
S5_STATE = 64
S5_LANES = S5_GROUPS * S5_STATE
S5_SPLIT = 2
DSA_HEADS = 8
DSA_KV_HEADS = 2
DSA_HEAD_DIM = 64
IDX_HEADS = 4
IDX_DIM = 64
DSA_TOPK_MAX = 256
REL_BUCKETS = 32
REL_MAX_DIST = 128
GDN_HEADS = 4
GDN_DK = 128
GDN_DV = 128
GDN_QK = GDN_HEADS * GDN_DK
GDN_VW = GDN_HEADS * GDN_DV
CONV_W = 4
GDN_CONV_CH = 2 * GDN_QK + GDN_VW
GDN_CHUNKS_PER_STEP = 2
CONV_LEAD = 8 - (CONV_W - 1)
GLA_HEADS = 4
GLA_DK = 64
GLA_DV = 128
GLA_GATE_RANK = 16
GLA_TAU = 16.0
GLA_BLOCK = 16
LN_EPS = 1e-5
RMS_EPS = 1e-6
DN_ALPHA = (2 * DEPTH) ** 0.25
LOG2E = math.log2(math.e)

LANE = 128
SUBLANE = 8
KEY_BLOCK = 128
KEY_SUB = 4
KEY_TILE = KEY_SUB * KEY_BLOCK
SOFTMAX_CEIL = 2.0 ** 120
SOFTMAX_FLOOR = 2.0 ** -60
FOLD_ROWS = 64
Q_BLOCK = 128
Q_TILE = 256
VMEM_LIMIT = 56 * 1024 * 1024

IN_LAYOUT = (
    ('a_u', BRANCH_W), ('a_gate', BRANCH_W),
    ('b_q', DSA_HEADS * DSA_HEAD_DIM), ('b_k', DSA_KV_HEADS * DSA_HEAD_DIM),
    ('b_v', DSA_KV_HEADS * DSA_HEAD_DIM), ('b_qi', IDX_HEADS * IDX_DIM), ('b_ki', IDX_DIM),
    ('b_wi', IDX_HEADS), ('b_gate', BRANCH_W),
    ('c_qkv', GDN_CONV_CH), ('c_beta', GDN_HEADS), ('c_a', GDN_HEADS), ('c_gate', BRANCH_W),
    ('d_q', GLA_HEADS * GLA_DK), ('d_k', GLA_HEADS * GLA_DK), ('d_v', GLA_HEADS * GLA_DV),
    ('d_g', GLA_GATE_RANK), ('d_gate', BRANCH_W),
    ('merge', N_BRANCH * D_MODEL),
)

PROJ_ORDER = ('merge', 'a_u', 'a_gate', 'b_q', 'b_qi', 'b_k', 'b_v', 'c_qkv', 'b_gate', 'c_gate',
              'd_v', 'd_gate', 'd_q', 'd_k', 'b_ki', 'b_wi', 'c_beta', 'c_a', 'd_g')
MISC_WI = IDX_DIM
MISC_BETA = MISC_WI + IDX_HEADS
MISC_A = MISC_BETA + GDN_HEADS
MISC_G = MISC_A + GDN_HEADS
MISC_PAD = LANE - (MISC_G + GLA_GATE_RANK)


def _proj_offsets():
    widths = dict(IN_LAYOUT)
    off, out = 0, {}
    for name in PROJ_ORDER:
        out[name] = off
        off += widths[name]
    return out, off + MISC_PAD


PROJ_OFF, PROJ_W = _proj_offsets()
OFF_MISC = PROJ_OFF['b_ki']
HK_W = 5 * LANE
HK_V = 2 * LANE
HK_KI = 4 * LANE


def _dot(a, b, precision=None):
    return jnp.dot(a, b, preferred_element_type=F32, precision=precision)


def _dot_nt(a, b, precision=None):
    return lax.dot_general(a, b, (((1,), (1,)), ((), ())), preferred_element_type=F32, precision=precision)


def _dot_tn(a, b, precision=None):
    return lax.dot_general(a, b, (((0,), (0,)), ((), ())), preferred_element_type=F32, precision=precision)


def _split_bf16(x):
    hi = x.astype(BF16)
    return hi, (x - hi.astype(F32)).astype(BF16)


def _dot3(a, b):
    a_hi, a_lo = _split_bf16(a)
    b_hi, b_lo = _split_bf16(b)
    return _dot(a_hi, b_hi) + (_dot(a_hi, b_lo) + _dot(a_lo, b_hi))


def _params(*sem):
    return pltpu.CompilerParams(dimension_semantics=sem, vmem_limit_bytes=VMEM_LIMIT)


def _softplus(x):
    return jnp.maximum(x, 0.0) + jnp.log1p(jnp.exp(-jnp.abs(x)))


def _iota(shape, dim):
    return lax.broadcasted_iota(jnp.int32, shape, dim)


def _gated(y, gate):
    return (y * jax.nn.silu(gate)).astype(BF16)


def _mm_kernel(x_ref, w_ref, o_ref):
    o_ref[...] = _dot(x_ref[...].astype(BF16), w_ref[...]).astype(o_ref.dtype)


def _matmul(x, w, out_dtype, tn):
    rows, kdim = x.shape
    width = w.shape[1]
    tm = min(rows, 1024)
    return pl.pallas_call(
        _mm_kernel,
        grid=(rows // tm, width // tn),
        in_specs=[pl.BlockSpec((tm, kdim), lambda i, j: (i, 0)),
                  pl.BlockSpec((kdim, tn), lambda i, j: (0, j))],
        out_specs=pl.BlockSpec((tm, tn), lambda i, j: (i, j)),
        out_shape=jax.ShapeDtypeStruct((rows, width), out_dtype),
        compiler_params=_params("parallel", "parallel"),
        name="proj_matmul",
    )(x, w)


def _s5_kernel(u_ref, gate_ref, h0r_ref, h0i_ref, ar_ref, ai_ref, bb_ref, cc_ref, d_ref, wglu_ref,
               ya_ref, hr_out, hi_out, xr_s, xi_s, hr_s, hi_s, cr_s, ci_s, *, tc):
    @pl.when(pl.program_id(1) == 0)
    def _():
        cr_s[...] = h0r_ref[...]
        ci_s[...] = h0i_ref[...]

    u = u_ref[...]
    ub = u.astype(BF16)
    uw = BRANCH_W // S5_SPLIT
    sw = S5_LANES // S5_SPLIT
    for part in range(S5_SPLIT):
        x = _dot(ub[:, part * uw:(part + 1) * uw], bb_ref[part])
        xr_s[:, part * sw:(part + 1) * sw] = x[:, :sw]
        xi_s[:, part * sw:(part + 1) * sw] = x[:, sw:]

    ar = ar_ref[...]
    ai = ai_ref[...]

    def step(t, carry):
        hr, hi = carry
        nhr = ar * hr - ai * hi + xr_s[pl.ds(t, 1), :]
        nhi = ar * hi + ai * hr + xi_s[pl.ds(t, 1), :]
        hr_s[pl.ds(t, 1), :] = nhr
        hi_s[pl.ds(t, 1), :] = nhi
        return nhr, nhi

    hr, hi = lax.fori_loop(0, tc, step, (cr_s[...], ci_s[...]), unroll=8)
    cr_s[...] = hr
    ci_s[...] = hi
    hr_out[...] = hr
    hi_out[...] = hi
    ys = []
    for part in range(S5_SPLIT):
        hcat = jnp.concatenate([hr_s[:, part * sw:(part + 1) * sw], hi_s[:, part * sw:(part + 1) * sw]], axis=1)
        ys.append(_dot(hcat.astype(BF16), cc_ref[part]))
    y = jnp.concatenate(ys, axis=1) + d_ref[...] * u
    y = jax.nn.gelu(y)
    g = _dot(y.astype(BF16), wglu_ref[...])
    ya_ref[...] = _gated(g[:, :BRANCH_W] * jax.nn.sigmoid(g[:, BRANCH_W:]), gate_ref[...])


def _s5_call(h, n, t, h0r, h0i, sp):
    tc = min(t, 256)
    nc = t // tc
    uw = BRANCH_W // S5_SPLIT
    sw = S5_LANES // S5_SPLIT
    const = lambda shape: pl.BlockSpec(shape, lambda i, c: (0,) * len(shape))
    state = pl.BlockSpec((None, 1, S5_LANES), lambda i, c: (i, 0, 0))
    return pl.pallas_call(
        functools.partial(_s5_kernel, tc=tc),
        grid=(n, nc),
        in_specs=[pl.BlockSpec((tc, BRANCH_W), lambda i, c: (i * nc + c, PROJ_OFF['a_u'] // BRANCH_W)),
                  pl.BlockSpec((tc, BRANCH_W), lambda i, c: (i * nc + c, PROJ_OFF['a_gate'] // BRANCH_W)),
                  state, state,
                  const((1, S5_LANES)), const((1, S5_LANES)),
                  const((S5_SPLIT, uw, 2 * sw)), const((S5_SPLIT, 2 * sw, uw)),
                  const((1, BRANCH_W)), const((BRANCH_W, 2 * BRANCH_W))],
        out_specs=[pl.BlockSpec((tc, BRANCH_W), lambda i, c: (i * nc + c, 0)), state, state],
        out_shape=[jax.ShapeDtypeStruct((n * t, BRANCH_W), BF16),
                   jax.ShapeDtypeStruct((n, 1, S5_LANES), F32),
                   jax.ShapeDtypeStruct((n, 1, S5_LANES), F32)],
        scratch_shapes=[pltpu.VMEM((tc, S5_LANES), F32), pltpu.VMEM((tc, S5_LANES), F32),
                        pltpu.VMEM((tc, S5_LANES), F32), pltpu.VMEM((tc, S5_LANES), F32),
                        pltpu.VMEM((1, S5_LANES), F32), pltpu.VMEM((1, S5_LANES), F32)],
        compiler_params=_params("parallel", "arbitrary"),
        name="s5_branch",
    )(h, h, h0r, h0i, sp['ar'], sp['ai'], sp['bb'], sp['cc'], sp['d'], sp['wglu'])


def _s5_prepare(a_re, a_im, log_dt, b_re, b_im, c_re, c_im, d, w_glu):
    lam_r = jnp.minimum(a_re, -1e-4)
    lam_i = a_im
    dt = jnp.exp(log_dt)[:, None]
    mag = jnp.exp(lam_r * dt)
    ar = mag * jnp.cos(lam_i * dt)
    ai = mag * jnp.sin(lam_i * dt)
    den = lam_r * lam_r + lam_i * lam_i
    nr = ar - 1.0
    cr = (nr * lam_r + ai * lam_i) / den
    ci = (ai * lam_r - nr * lam_i) / den
    bb_r = cr[..., None] * b_re - ci[..., None] * b_im
    bb_i = cr[..., None] * b_im + ci[..., None] * b_re
    eye = jnp.eye(S5_GROUPS, dtype=F32)

    def blockdiag_in(bb):
        m = eye[:, None, :, None] * jnp.swapaxes(bb, 1, 2)[:, :, None, :]
        return m.reshape(BRANCH_W, S5_LANES)

    def blockdiag_out(c):
        m = eye[:, None, :, None] * jnp.swapaxes(c, 1, 2)[:, :, None, :]
        return m.reshape(S5_LANES, BRANCH_W)

    uw = BRANCH_W // S5_SPLIT
    sw = S5_LANES // S5_SPLIT
    bbr, bbi = blockdiag_in(bb_r), blockdiag_in(bb_i)
    ccr, cci = blockdiag_out(c_re), blockdiag_out(c_im)
    bb = jnp.stack([jnp.concatenate([m[p * uw:(p + 1) * uw, p * sw:(p + 1) * sw] for m in (bbr, bbi)], axis=1)
                    for p in range(S5_SPLIT)]).astype(BF16)
    cc = jnp.stack([jnp.concatenate([ccr[p * sw:(p + 1) * sw, p * uw:(p + 1) * uw],
                                     -cci[p * sw:(p + 1) * sw, p * uw:(p + 1) * uw]], axis=0)
                    for p in range(S5_SPLIT)]).astype(BF16)
    return {'ar': ar.reshape(1, S5_LANES), 'ai': ai.reshape(1, S5_LANES), 'bb': bb, 'cc': cc,
            'd': d.reshape(1, BRANCH_W), 'wglu': w_glu.astype(BF16)}


def _sortable(x):
    bits = lax.bitcast_convert_type(x, jnp.int32)
    key = bits ^ ((bits >> 31) & jnp.int32(0x7FFFFFFF))
    return jnp.where(key == -1, 0, key)


def _dsa_kernel(q_ref, qi_ref, misc_ref, gate_ref, hk_ref, bias_ref, tril_ref, o_ref, sc_s, pk_s, acc_s, far_s, *,
                qb, b_off, n_keys, topk):
    nsub = max(qb // Q_BLOCK, 1)
    qw = qb // nsub
    b = pl.program_id(1) * nsub + b_off
    nkt = (b + nsub - 1 + KEY_SUB) // KEY_SUB
    fold = KEY_TILE // FOLD_ROWS
    key_row = _iota((KEY_TILE, qb), 0)
    qpos = b * Q_BLOCK + _iota((1, qb), 1)
    limit = jnp.minimum(((qpos >> 6) + 1) * CHUNK, n_keys)
    low_half = _iota((qb, LANE), 1) < DSA_HEAD_DIM

    def key_rows(ref_cols, j):
        return hk_ref[pl.ds(pl.multiple_of(j * KEY_TILE, KEY_TILE), KEY_TILE), ref_cols:ref_cols + LANE]

    def half_masked(x, hd):
        pair = x[:, (hd // 2) * LANE:(hd // 2 + 1) * LANE]
        return jnp.where(low_half == (hd % 2 == 0), pair, 0.0).astype(BF16)

    misc = misc_ref[...]
    pick = jnp.where(_iota((SUBLANE, LANE), 1) == MISC_WI + _iota((SUBLANE, LANE), 0), 1.0, 0.0)
    wi_t = _dot_nt(pick, misc, precision=HIGHEST) * (IDX_DIM ** -0.5 * IDX_HEADS ** -0.5)
    qi = qi_ref[...]
    qi_stack = jnp.concatenate([half_masked(qi, hd) for hd in range(IDX_HEADS)], axis=0)

    def pack16(v):
        half = KEY_TILE // 2
        return (v[0:half] & 0xFFFF) | (v[half:] << 16)

    def score_block(j, carry):
        z = _dot_nt(key_rows(HK_KI, j), qi_stack)
        sc = jnp.zeros((KEY_TILE, qb), F32)
        for hd in range(IDX_HEADS):
            sc = sc + jnp.maximum(z[:, hd * qb:(hd + 1) * qb], 0.0) * wi_t[hd:hd + 1, :]
        sc = jnp.where(j * KEY_TILE + key_row < limit, sc, -jnp.inf)
        key = _sortable(sc)
        sc_s[j] = key
        pk_s[j] = pack16(key >> 16)
        return carry

    lax.fori_loop(0, nkt, score_block, 0)

    def count_ge(cand):
        def body(j, acc):
            return acc + jnp.where(sc_s[j] >= cand, 1.0, 0.0).reshape(fold, FOLD_ROWS, qb).sum(axis=0)
        acc = lax.fori_loop(0, nkt, body, jnp.zeros((FOLD_ROWS, qb), F32))
        return jnp.sum(acc, axis=0, keepdims=True)

    one16 = jnp.ones((KEY_TILE, qb), jnp.int16)
    zero16 = jnp.zeros((KEY_TILE, qb), jnp.int16)

    def count16_ge(cand):
        word = jnp.broadcast_to((cand & 0xFFFF) | (cand << 16), (KEY_TILE // 2, qb))
        c16 = pltpu.bitcast(word, jnp.int16)

        def body(j, acc):
            ind = jnp.where(pltpu.bitcast(pk_s[j], jnp.int16) >= c16, one16, zero16)
            for i in range(fold):
                acc = acc + ind[i * FOLD_ROWS:(i + 1) * FOLD_ROWS]
            return acc
        acc = lax.fori_loop(0, nkt, body, jnp.zeros((FOLD_ROWS, qb), jnp.int16))
        return jnp.sum(acc.astype(F32), axis=0, keepdims=True)

    def radix_search(start, want):
        def bit_step(i, res):
            cand = res + (jnp.int32(1) << (14 - i))
            return jnp.where(count16_ge(cand) >= want, cand, res)
        return lax.fori_loop(0, 15, bit_step, start)

    kf = float(topk)
    top = radix_search(jnp.where(count16_ge(jnp.zeros((1, qb), jnp.int32)) >= kf, 0, -32768), kf)
    int16_max = jnp.iinfo(jnp.int16).max
    above = jnp.where(top == int16_max, 0.0, count16_ge(jnp.minimum(top + 1, int16_max)))
    want_low = kf - above

    def pack_low(j, carry):
        key = sc_s[j]
        pk_s[j] = pack16(jnp.where((key >> 16) == top, (key >> 1) & 0x7FFF, -1))
        return carry

    lax.fori_loop(0, nkt, pack_low, 0)
    low = radix_search(jnp.zeros((1, qb), jnp.int32), want_low)
    base = (top << 16) | (low << 1)
    thr = jnp.where(count_ge(base | 1) >= kf, base | 1, base)
    need = kf - count_ge(thr + 1)

    tril = tril_ref[...]

    def select_block(j, ties):
        keys = sc_s[j]
        eq = keys == thr
        eq_b = jnp.where(eq, 1.0, 0.0).astype(BF16)
        parts, carry = [], ties
        for s in range(KEY_SUB):
            part = _dot(tril, eq_b[s * KEY_BLOCK:(s + 1) * KEY_BLOCK]) + carry
            parts.append(part)
            carry = part[KEY_BLOCK - 1:KEY_BLOCK, :]
        prefix = jnp.concatenate(parts, axis=0)
        sel = ((keys > thr) | (eq & (prefix <= need))) & (j * KEY_TILE + key_row < limit)
        sc_s[j] = lax.bitcast_convert_type(jnp.where(sel, 0.0, -jnp.inf), jnp.int32)
        return prefix[KEY_TILE - 1:KEY_TILE, :]

    lax.fori_loop(0, nkt, select_block, jnp.zeros((1, qb), F32))

    q = q_ref[...] * (DSA_HEAD_DIM ** -0.5 * LOG2E)
    rep = DSA_HEADS // DSA_KV_HEADS
    eye = jnp.where(_iota((LANE, LANE), 0) == _iota((LANE, LANE), 1), 1.0, 0.0).astype(BF16)
    value_lane = _iota((KEY_TILE, LANE), 1) < DSA_HEAD_DIM
    n_pairs = DSA_HEADS // 2
    q_pair = [jnp.concatenate([half_masked(q, 2 * c), half_masked(q, 2 * c + 1)], axis=0) for c in range(n_pairs)]
    far_bias = [jnp.concatenate([bias_ref[hd, 2, 0:1, 0:qw]] * nsub, axis=1)
                for hd in range(DSA_HEADS)]
    n_far = jnp.maximum((b - 1) // KEY_SUB, 0)

    def tile_operands(j):
        mask = lax.bitcast_convert_type(sc_s[j], F32)
        kd = [key_rows(grp * LANE, j) for grp in range(DSA_KV_HEADS)]
        zs = [_dot_nt(kd[(2 * c) // rep], q_pair[c]) for c in range(n_pairs)]
        vd = [jnp.where(value_lane, key_rows(HK_V + grp * LANE, j), 1.0).astype(BF16)
              for grp in range(DSA_KV_HEADS)]
        return mask, zs, vd

    def near_bias(hd, j):
        cols = [jnp.concatenate([bias_ref[hd, jnp.clip(b + u - (KEY_SUB * j + s), 0, 2), :, 0:qw]
                                 for s in range(KEY_SUB)], axis=0) for u in range(nsub)]
        return jnp.concatenate(cols, axis=1)

    def plain_tile(j, carry, far):
        mask, zs, vd = tile_operands(j)
        ps = []
        for hd in range(DSA_HEADS):
            r = hd % 2
            zh = zs[hd // 2][:, r * qb:(r + 1) * qb] + mask
            if not far:
                zh = zh + near_bias(hd, j)
            ps.append(jnp.exp2(zh).astype(BF16))
        acc = far_s if far else acc_s
        for c in range(n_pairs):
            acc[c] += _dot_tn(vd[(2 * c) // rep], jnp.concatenate(ps[2 * c:2 * c + 2], axis=1))
        return carry

    acc_s[...] = jnp.zeros_like(acc_s)
    far_s[...] = jnp.zeros_like(far_s)
    lax.fori_loop(0, n_far, lambda j, c: plain_tile(j, c, True), 0)
    lax.fori_loop(n_far, nkt, lambda j, c: plain_tile(j, c, False), 0)
    healthy = jnp.ones((1, 1), F32)
    for c in range(n_pairs):
        scale = jnp.concatenate([jnp.exp2(far_bias[2 * c]), jnp.exp2(far_bias[2 * c + 1])], axis=1)
        total = far_s[c] * scale + acc_s[c]
        acc_s[c] = total
        norm = total[DSA_HEAD_DIM:DSA_HEAD_DIM + 1, :]
        ok = (jnp.abs(total) < SOFTMAX_CEIL) & (norm > SOFTMAX_FLOOR)
        healthy = jnp.minimum(healthy, jnp.min(jnp.where(ok, 1.0, 0.0), keepdims=True))

    def attend(j, ms, far):
        mask, zs, vd = tile_operands(j)
        new_ms, ps, alphas = [], [], []
        for hd in range(DSA_HEADS):
            r = hd % 2
            zh = zs[hd // 2][:, r * qb:(r + 1) * qb] + mask
            if not far:
                zh = zh + near_bias(hd, j)
            tile_max = jnp.max(zh.reshape(fold, FOLD_ROWS, qb).max(axis=0), axis=0, keepdims=True)
            shift = far_bias[hd] if far else 0.0
            m_new = jnp.maximum(ms[hd], tile_max + shift)
            ps.append(jnp.exp2(zh - (m_new - shift)).astype(BF16))
            alphas.append(jnp.exp2(ms[hd] - m_new))
            new_ms.append(m_new)
        for c in range(n_pairs):
            acc_s[c] = (acc_s[c] * jnp.concatenate(alphas[2 * c:2 * c + 2], axis=1)
                        + _dot_tn(vd[(2 * c) // rep], jnp.concatenate(ps[2 * c:2 * c + 2], axis=1)))
        return tuple(new_ms)

    @pl.when(healthy[0, 0] < 0.5)
    def _():
        acc_s[...] = jnp.zeros_like(acc_s)
        ms = tuple(jnp.full((1, qb), -1e30, F32) for _ in range(DSA_HEADS))
        ms = lax.fori_loop(0, n_far, lambda j, m: attend(j, m, True), ms)
        lax.fori_loop(n_far, nkt, lambda j, m: attend(j, m, False), ms)

    for c in range(n_pairs):
        halves = []
        for r in range(2):
            blk = acc_s[c, :, r * qb:(r + 1) * qb]
            halves.append(blk[0:DSA_HEAD_DIM] / blk[DSA_HEAD_DIM:DSA_HEAD_DIM + 1])
        o_t = jnp.concatenate(halves, axis=0)
        o_hi, o_lo = _split_bf16(o_t)
        o_ref[:, c * LANE:(c + 1) * LANE] = _gated(_dot_tn(o_hi, eye) + _dot_tn(o_lo, eye),
                                                   gate_ref[:, c * LANE:(c + 1) * LANE])


def _dsa_call(h, hk, n, t, bias, tril, *, b_off, n_keys, topk):
    qb = min(t, Q_TILE)
    nqb = t // qb
    lp = hk.shape[1]
    wq = DSA_HEADS * DSA_HEAD_DIM
    wqi = IDX_HEADS * IDX_DIM
    return pl.pallas_call(
        functools.partial(_dsa_kernel, qb=qb, b_off=b_off, n_keys=n_keys, topk=topk),
        grid=(n, nqb),
        in_specs=[pl.BlockSpec((qb, wq), lambda i, c: (i * nqb + c, PROJ_OFF['b_q'] // wq)),
                  pl.BlockSpec((qb, wqi), lambda i, c: (i * nqb + c, PROJ_OFF['b_qi'] // wqi)),
                  pl.BlockSpec((qb, LANE), lambda i, c: (i * nqb + c, OFF_MISC // LANE)),
                  pl.BlockSpec((qb, BRANCH_W), lambda i, c: (i * nqb + c, PROJ_OFF['b_gate'] // BRANCH_W)),
                  pl.BlockSpec((None, lp, HK_W), lambda i, c: (i, 0, 0)),
                  pl.BlockSpec((DSA_HEADS, 3, KEY_BLOCK, Q_BLOCK), lambda i, c: (0, 0, 0, 0)),
                  pl.BlockSpec((KEY_BLOCK, KEY_BLOCK), lambda i, c: (0, 0))],
        out_specs=pl.BlockSpec((qb, wq), lambda i, c: (i * nqb + c, 0)),
        out_shape=jax.ShapeDtypeStruct((n * t, wq), BF16),
        scratch_shapes=[pltpu.VMEM((lp // KEY_TILE, KEY_TILE, qb), jnp.int32),
                        pltpu.VMEM((lp // KEY_TILE, KEY_TILE // 2, qb), jnp.int32),
                        pltpu.VMEM((DSA_HEADS // 2, LANE, 2 * qb), F32),
                        pltpu.VMEM((DSA_HEADS // 2, LANE, 2 * qb), F32)],
        compiler_params=_params("parallel", "arbitrary"),
        name="dsa_branch",
    )(h, h, h, h, hk, bias, tril)


def _t5_bucket(rel):
    nb = REL_BUCKETS // 2
    max_exact = nb // 2
    ret = jnp.where(rel > 0, nb, 0)
    dist = jnp.abs(rel)
    distf = jnp.maximum(dist, 1).astype(F32)
    large = max_exact + (jnp.log(distf / max_exact) / math.log(REL_MAX_DIST / max_exact)
                         * (nb - max_exact)).astype(jnp.int32)
    large = jnp.minimum(large, nb - 1)
    return ret + jnp.where(dist < max_exact, dist, large)


def _dsa_bias_tables(rel_bias):
    lc = jnp.arange(KEY_BLOCK, dtype=jnp.int32)[:, None]
    qr = jnp.arange(Q_BLOCK, dtype=jnp.int32)[None, :]
    rel = jnp.stack([lc - qr - back * KEY_BLOCK for back in range(3)], axis=0)
    onehot = jax.nn.one_hot(_t5_bucket(rel), REL_BUCKETS, dtype=F32)
    bias = jnp.einsum('klqb,bh->hklq', onehot, rel_bias.astype(F32), precision=HIGHEST)
    return bias * LOG2E


def _gdn_kernel(raw_ref, misc_ref, gate_ref, s0_ref, conv0_ref, convw_ref, alog_ref, dtb_ref, norm_ref,
                o_ref, s_out, conv_out, xp_s, st_s, *, n, c, nsub):
    rows = nsub * c

    @pl.when(pl.program_id(0) == 0)
    def _():
        xp_s[:, 0:SUBLANE, :] = conv0_ref[...]
        st_s[...] = s0_ref[...]

    w = convw_ref[...]
    row = _iota((c, c), 0)
    col = _iota((c, c), 1)
    incl = row >= col
    strict = row > col
    eye = row == col
    tril = jnp.where(incl, 1.0, 0.0)
    eye_f = jnp.where(eye, 1.0, 0.0)
    chains = [(b, sub, hd) for b in range(n) for sub in range(nsub) for hd in range(GDN_HEADS)]

    qkv, beta_all, gc_all = [], [], []
    for b in range(n):
        raw = raw_ref[b]
        xp_s[b, SUBLANE:SUBLANE + rows, :] = raw
        y = raw * w[CONV_W - 1:CONV_W, :]
        for i in range(CONV_W - 1):
            y = y + xp_s[b, CONV_LEAD + i:CONV_LEAD + i + rows, :] * w[i:i + 1, :]
        tail = xp_s[b, rows:rows + SUBLANE, :]
        xp_s[b, 0:SUBLANE, :] = tail
        conv_out[b] = tail
        qkv.append(jax.nn.silu(y))
        misc = misc_ref[b]
        beta_all.append(jax.nn.sigmoid(misc))
        g_all = -jnp.exp(alog_ref[...]) * _softplus(misc + dtb_ref[...])
        gc_all.append([_dot(tril, g_all[sub * c:(sub + 1) * c], precision=HIGHEST)
                       for sub in range(nsub)])

    qs, ks, vs, betas, gcs, dec_s, dec_i = [], [], [], [], [], [], []
    for b, sub, hd in chains:
        x = qkv[b][sub * c:(sub + 1) * c]
        qh = x[:, hd * GDN_DK:(hd + 1) * GDN_DK]
        kh = x[:, GDN_QK + hd * GDN_DK:GDN_QK + (hd + 1) * GDN_DK]
        vs.append(x[:, 2 * GDN_QK + hd * GDN_DV:2 * GDN_QK + (hd + 1) * GDN_DV])
        qs.append(qh * lax.rsqrt(jnp.sum(qh * qh, axis=1, keepdims=True) + RMS_EPS) * (GDN_DK ** -0.5))
        ks.append(kh * lax.rsqrt(jnp.sum(kh * kh, axis=1, keepdims=True) + RMS_EPS))
        betas.append(beta_all[b][sub * c:(sub + 1) * c, MISC_BETA + hd:MISC_BETA + hd + 1])
        gc = gc_all[b][sub][:, MISC_A + hd:MISC_A + hd + 1]
        gcs.append(gc)
        gc_b = jnp.broadcast_to(gc, (c, c))
        gc_row = jnp.sum(jnp.where(eye, gc_b, 0.0), axis=0, keepdims=True)
        diff = gc_b - gc_row
        dec = jnp.exp(diff)
        dec_s.append(jnp.where(strict, dec, 0.0))
        dec_i.append(jnp.where(incl, dec, 0.0))
    kb = [k.astype(BF16) for k in ks]
    qb = [q.astype(BF16) for q in qs]
    pw = [-(beta * _dot_nt(k, k) * d) for beta, k, d in zip(betas, kb, dec_s)]
    inv = [eye_f + p for p in pw]
    for _ in range(int(math.log2(c)) - 1):
        pw = [_dot3(p, p) for p in pw]
        inv = [i + _dot3(i, p) for i, p in zip(inv, pw)]
    egc = [jnp.exp(gc) for gc in gcs]
    sol = [_dot3(i, jnp.concatenate([(beta * e) * k, beta * v], axis=1))
           for i, beta, e, k, v in zip(inv, betas, egc, ks, vs)]
    attn = [(_dot_nt(q, k) * d).astype(BF16) for q, k, d in zip(qb, kb, dec_i)]
    for sub in range(nsub):
        ids = [i for i, ch in enumerate(chains) if ch[1] == sub]
        s_old = [st_s[chains[i][0], chains[i][2]] for i in ids]
        sb = [s.astype(BF16) for s in s_old]
        vb = [(sol[i][:, GDN_DK:] - _dot(sol[i][:, :GDN_DK].astype(BF16), s)).astype(BF16) for i, s in zip(ids, sb)]
        outs = [egc[i] * _dot(qb[i], s) + _dot(attn[i], v) for i, s, v in zip(ids, sb, vb)]
        for i, v, s, o in zip(ids, vb, s_old, outs):
            b, _, hd = chains[i]
            g_last = gcs[i][c - 1:c, :]
            st_s[b, hd] = jnp.exp(g_last) * s + _dot_tn((ks[i] * jnp.exp(g_last - gcs[i])).astype(BF16), v)
            o = o * lax.rsqrt(jnp.mean(o * o, axis=1, keepdims=True) + RMS_EPS) * norm_ref[...]
            o_ref[b, sub * c:(sub + 1) * c, hd * GDN_DV:(hd + 1) * GDN_DV] = _gated(
                o, gate_ref[b, sub * c:(sub + 1) * c, hd * GDN_DV:(hd + 1) * GDN_DV])
    s_out[...] = st_s[...]


def _gdn_call(h3, s0, conv0, gp):
    n, t, _ = h3.shape
    c = min(t, CHUNK)
    nsub = GDN_CHUNKS_PER_STEP if t % (GDN_CHUNKS_PER_STEP * c) == 0 else 1
    rows = nsub * c
    const = lambda shape: pl.BlockSpec(shape, lambda j: (0,) * len(shape))
    state = const((n, GDN_HEADS, GDN_DK, GDN_DV))
    conv = const((n, SUBLANE, GDN_CONV_CH))
    return pl.pallas_call(
        functools.partial(_gdn_kernel, n=n, c=c, nsub=nsub),
        grid=(t // rows,),
        in_specs=[pl.BlockSpec((n, rows, GDN_CONV_CH), lambda j: (0, j, PROJ_OFF['c_qkv'] // GDN_CONV_CH)),
                  pl.BlockSpec((n, rows, LANE), lambda j: (0, j, OFF_MISC // LANE)),
                  pl.BlockSpec((n, rows, BRANCH_W), lambda j: (0, j, PROJ_OFF['c_gate'] // BRANCH_W)),
                  state, conv,
                  const((CONV_W, GDN_CONV_CH)), const((1, LANE)), const((1, LANE)), const((1, GDN_DV))],
        out_specs=[pl.BlockSpec((n, rows, GDN_VW), lambda j: (0, j, 0)), state, conv],
        out_shape=[jax.ShapeDtypeStruct((n, t, GDN_VW), BF16),
                   jax.ShapeDtypeStruct((n, GDN_HEADS, GDN_DK, GDN_DV), F32),
                   jax.ShapeDtypeStruct((n, SUBLANE, GDN_CONV_CH), F32)],
        scratch_shapes=[pltpu.VMEM((n, rows + SUBLANE, GDN_CONV_CH), F32),
                        pltpu.VMEM((n, GDN_HEADS, GDN_DK, GDN_DV), F32)],
        compiler_params=_params("arbitrary"),
        name="gdn_branch",
    )(h3, h3, h3, s0, conv0, gp['convw'], gp['alog'], gp['dtb'], gp['norm'])


def _lane_place(v, off):
    return jnp.zeros((1, LANE), F32).at[0, off:off + v.shape[0]].set(v.astype(F32))


def _gla_kernel(qk_ref, v_ref, misc_ref, gate_ref, s0_ref, wg_ref, bg_ref, norm_ref, hexp_ref, gsum_ref,
                o_ref, s_out, st_s, *, n, tc):
    blk = GLA_BLOCK
    wk = GLA_HEADS * GLA_DK

    @pl.when(pl.program_id(0) == 0)
    def _():
        st_s[...] = s0_ref[...]

    row = _iota((blk, blk), 0)
    col = _iota((blk, blk), 1)
    tril = jnp.where(row >= col, 1.0, 0.0)
    jrow = _iota((blk, wk), 0)
    lane_k = _iota((blk, wk), 1)
    head_mask = [(lane_k >= hd * GLA_DK) & (lane_k < (hd + 1) * GLA_DK) for hd in range(GLA_HEADS)]
    chains = range(n)

    def block(sb, carry):
        r0 = pl.multiple_of(sb * blk, blk)
        qk = [qk_ref[b, pl.ds(r0, blk), :] for b in chains]
        q = [x[:, :wk] * (GLA_DK ** -0.5) for x in qk]
        k = [x[:, wk:] for x in qk]
        v = [v_ref[b, pl.ds(r0, blk), :] for b in chains]
        mb = [misc_ref[b, pl.ds(r0, blk), :].astype(BF16) for b in chains]
        lg = [jax.nn.log_sigmoid(_dot(m, wg_ref[...]) + bg_ref[...]) * (1.0 / GLA_TAU) for m in mb]
        bc = [_dot(tril, x, precision=HIGHEST) for x in lg]
        dmat = []
        for b in chains:
            slabs = []
            for i in range(blk):
                keep = jrow <= i
                dec = jnp.where(keep, jnp.exp(bc[b][i:i + 1, :] - bc[b]), 0.0)
                slabs.append(dec * q[b][i:i + 1, :] * k[b])
            dmat.append(jnp.concatenate(slabs, axis=0).astype(BF16))
        a_exp = [_dot(d, hexp_ref[...]) for d in dmat]
        prod = [a * jnp.concatenate([x] * blk, axis=0) for a, x in zip(a_exp, v)]
        o = [_dot(gsum_ref[...], p.astype(BF16)) for p in prod]
        qe = [x * jnp.exp(c) for x, c in zip(q, bc)]
        kt = [x * jnp.exp(c[blk - 1:blk, :] - c) for x, c in zip(k, bc)]
        s = [st_s[b] for b in chains]
        sbf = [x.astype(BF16) for x in s]
        for b in chains:
            upd = jnp.zeros_like(s[b])
            parts = []
            for hd in range(GLA_HEADS):
                parts.append(_dot_nt(jnp.where(head_mask[hd], qe[b], 0.0).astype(BF16), sbf[b]))
                vh = v[b][:, hd * GLA_DV:(hd + 1) * GLA_DV].astype(BF16)
                upd = upd + _dot_tn(vh, jnp.where(head_mask[hd], kt[b], 0.0).astype(BF16))
            ob = o[b] + jnp.concatenate(parts, axis=1)
            st_s[b] = jnp.exp(bc[b][blk - 1:blk, :]) * s[b] + upd
            for hd in range(GLA_HEADS):
                oh = ob[:, hd * GLA_DV:(hd + 1) * GLA_DV]
                oh = oh * lax.rsqrt(jnp.mean(oh * oh, axis=1, keepdims=True) + RMS_EPS) * norm_ref[...]
                o_ref[b, pl.ds(r0, blk), hd * GLA_DV:(hd + 1) * GLA_DV] = _gated(
                    oh, gate_ref[b, pl.ds(r0, blk), hd * GLA_DV:(hd + 1) * GLA_DV])
        return carry

    lax.fori_loop(0, tc // blk, block, 0, unroll=4 if tc // blk > 1 else 1)
    s_out[...] = st_s[...]


def _gla_call(h3, s0, gp):
    n, t, _ = h3.shape
    tc = min(t, 128)
    wk = GLA_HEADS * GLA_DK
    wv = GLA_HEADS * GLA_DV
    const = lambda shape: pl.BlockSpec(shape, lambda j: (0,) * len(shape))
    state = const((n, GLA_DV, wk))
    return pl.pallas_call(
        functools.partial(_gla_kernel, n=n, tc=tc),
        grid=(t // tc,),
        in_specs=[pl.BlockSpec((n, tc, 2 * wk), lambda j: (0, j, PROJ_OFF['d_q'] // (2 * wk))),
                  pl.BlockSpec((n, tc, wv), lambda j: (0, j, PROJ_OFF['d_v'] // wv)),
                  pl.BlockSpec((n, tc, LANE), lambda j: (0, j, OFF_MISC // LANE)),
                  pl.BlockSpec((n, tc, BRANCH_W), lambda j: (0, j, PROJ_OFF['d_gate'] // BRANCH_W)),
                  state,
                  const((LANE, wk)), const((1, wk)), const((1, GLA_DV)),
                  const((wk, wv)), const((GLA_BLOCK, GLA_BLOCK * GLA_BLOCK))],
        out_specs=[pl.BlockSpec((n, tc, wv), lambda j: (0, j, 0)), state],
        out_shape=[jax.ShapeDtypeStruct((n, t, wv), BF16),
                   jax.ShapeDtypeStruct((n, GLA_DV, wk), F32)],
        scratch_shapes=[pltpu.VMEM((n, GLA_DV, wk), F32)],
        compiler_params=_params("arbitrary"),
        name="gla_branch",
    )(h3, h3, h3, h3, s0, gp['wg'], gp['bg'], gp['norm'], gp['hexp'], gp['gsum'])


def _gla_prepare(w_g2, b_g, norm):
    wk = GLA_HEADS * GLA_DK
    wg = jnp.zeros((LANE, wk), F32).at[MISC_G:MISC_G + GLA_GATE_RANK].set(w_g2).astype(BF16)
    ck = jnp.arange(wk)[:, None] // GLA_DK
    cv = jnp.arange(GLA_HEADS * GLA_DV)[None, :] // GLA_DV
    ij = jnp.arange(GLA_BLOCK * GLA_BLOCK)[None, :] // GLA_BLOCK
    return {'wg': wg, 'bg': b_g.reshape(1, wk).astype(F32),
            'norm': norm.reshape(1, GLA_DV).astype(F32),
            'hexp': (ck == cv).astype(BF16),
            'gsum': (jnp.arange(GLA_BLOCK)[:, None] == ij).astype(BF16)}


def _merge_kernel(x_ref, ya_ref, yb_ref, yc_ref, yd_ref, mg_ref, wb_ref, wo_ref, lng_ref, lnb_ref, o_ref):
    mixed = None
    for br, act_ref in enumerate((ya_ref, yb_ref, yc_ref, yd_ref)):
        term = jax.nn.sigmoid(mg_ref[:, br * D_MODEL:(br + 1) * D_MODEL]) * _dot(act_ref[...], wb_ref[br])
        mixed = term if mixed is None else mixed + term
    z = DN_ALPHA * x_ref[...] + _dot(mixed.astype(BF16), wo_ref[...])
    mu = jnp.mean(z, axis=1, keepdims=True)
    zc = z - mu
    var = jnp.mean(zc * zc, axis=1, keepdims=True)
    o_ref[...] = zc * lax.rsqrt(var + LN_EPS) * lng_ref[...] + lnb_ref[...]


def _merge_call(x, h, ya, yb, yc, yd, wb, wo, lng, lnb):
    rows = x.shape[0]
    tm = min(rows, 256)
    bw = BRANCH_W
    ybs = pl.BlockSpec((tm, bw), lambda i: (i, 0))
    return pl.pallas_call(
        _merge_kernel,
        grid=(rows // tm,),
        in_specs=[pl.BlockSpec((tm, D_MODEL), lambda i: (i, 0)), ybs, ybs, ybs, ybs,
                  pl.BlockSpec((tm, N_BRANCH * D_MODEL), lambda i: (i, 0)),
                  pl.BlockSpec((N_BRANCH, bw, D_MODEL), lambda i: (0, 0, 0)),
                  pl.BlockSpec((D_MODEL, D_MODEL), lambda i: (0, 0)),
                  pl.BlockSpec((1, D_MODEL), lambda i: (0, 0)),
                  pl.BlockSpec((1, D_MODEL), lambda i: (0, 0))],
        out_specs=pl.BlockSpec((tm, D_MODEL), lambda i: (i, 0)),
        out_shape=jax.ShapeDtypeStruct((rows, D_MODEL), F32),
        compiler_params=_params("parallel"),
        name="merge_norm",
    )(x, ya, yb, yc, yd, h, wb, wo, lng, lnb)


def _split_w_in(w_in):
    out, off = {}, 0
    for name, width in IN_LAYOUT:
        out[name] = w_in[:, off:off + width]
        off += width
    return out


def _regroup_kernel(w_ref, proj_ref, key_ref):
    cols = _split_w_in(w_ref[...])
    pad = jnp.zeros((w_ref.shape[0], MISC_PAD), F32)
    proj_ref[...] = jnp.concatenate([cols[name] for name in PROJ_ORDER] + [pad], axis=1).astype(BF16)
    k, v, ki = cols['b_k'], cols['b_v'], cols['b_ki']
    dh = DSA_HEAD_DIM
    key_ref[...] = jnp.concatenate([k[:, :dh], k[:, :dh], k[:, dh:], k[:, dh:],
                                    v[:, :dh], v[:, :dh], v[:, dh:], v[:, dh:], ki, ki], axis=1).astype(BF16)


def _regroup_weights(w_in):
    depth, rows, width = w_in.shape
    tr = 128
    return pl.pallas_call(
        _regroup_kernel,
        grid=(depth, rows // tr),
        in_specs=[pl.BlockSpec((None, tr, width), lambda l, i: (l, i, 0))],
        out_specs=[pl.BlockSpec((None, tr, PROJ_W), lambda l, i: (l, i, 0)),
                   pl.BlockSpec((None, tr, HK_W), lambda l, i: (l, i, 0))],
        out_shape=[jax.ShapeDtypeStruct((depth, rows, PROJ_W), BF16),
                   jax.ShapeDtypeStruct((depth, rows, HK_W), BF16)],
        compiler_params=_params("parallel", "parallel"),
        name="regroup_weights",
    )(w_in)


def _dup_keys(k, v, ki):
    return jnp.concatenate([k[:, :, 0], k[:, :, 0], k[:, :, 1], k[:, :, 1],
                            v[:, :, 0], v[:, :, 0], v[:, :, 1], v[:, :, 1], ki, ki], axis=-1).astype(BF16)


def _layer(x, lw, st):
    n, t, _ = x.shape
    x2 = x.reshape(n * t, D_MODEL)
    h = _matmul(x2, lw['w_proj'], F32, tn=PROJ_W // 9)
    hk_new = _matmul(x2, lw['w_key'], BF16, tn=HK_W).reshape(n, t, HK_W)
    if st is None:
        hk, n_keys, b_off = hk_new, t, 0
        s5r = jnp.zeros((n, 1, S5_LANES), F32)
        s5i = jnp.zeros((n, 1, S5_LANES), F32)
        gdn0 = jnp.zeros((n, GDN_HEADS, GDN_DK, GDN_DV), F32)
        conv0 = jnp.zeros((n, SUBLANE, GDN_CONV_CH), F32)
        gla0 = jnp.zeros((n, GLA_DV, GLA_HEADS * GLA_DK), F32)
    else:
        past = st['k'].shape[1]
        n_keys = past + t
        b_off = past // Q_BLOCK
        lp = -(-n_keys // KEY_TILE) * KEY_TILE
        hk = jnp.concatenate([_dup_keys(st['k'], st['v'], st['kidx']), hk_new,
                              jnp.zeros((n, lp - n_keys, HK_W), BF16)], axis=1)
        s5r = st['s5_re'].reshape(n, 1, S5_LANES)
        s5i = st['s5_im'].reshape(n, 1, S5_LANES)
        gdn0 = st['gdn']
        conv0 = jnp.concatenate([jnp.zeros((n, CONV_LEAD, GDN_CONV_CH), F32), st['gdn_conv']], axis=1)
        gla0 = jnp.moveaxis(st['gla'], 3, 1).reshape(n, GLA_DV, GLA_HEADS * GLA_DK)
    topk = min(DSA_TOPK_MAX, n_keys // 4)

    ya, s5r_new, s5i_new = _s5_call(h, n, t, s5r, s5i, lw['s5'])
    yb = _dsa_call(h, hk, n, t, lw['bias'], lw['tril'], b_off=b_off, n_keys=n_keys, topk=topk)
    h3 = h.reshape(n, t, PROJ_W)
    yc, gdn_new, conv_new = _gdn_call(h3, gdn0, conv0, lw['gdn'])
    yd, gla_new = _gla_call(h3, gla0, lw['gla'])
    yc = yc.reshape(n * t, GDN_VW)
    yd = yd.reshape(n * t, GLA_HEADS * GLA_DV)
    y = _merge_call(x2, h, ya, yb, yc, yd, lw['w_branch'], lw['w_out'], lw['ln_g'], lw['ln_b'])

    def cols(name, width):
        return h[:, PROJ_OFF[name]:PROJ_OFF[name] + width].reshape(n, t, width)

    new = {'k': cols('b_k', DSA_KV_HEADS * DSA_HEAD_DIM).reshape(n, t, DSA_KV_HEADS, DSA_HEAD_DIM),
           'v': cols('b_v', DSA_KV_HEADS * DSA_HEAD_DIM).reshape(n, t, DSA_KV_HEADS, DSA_HEAD_DIM),
           'kidx': cols('b_ki', IDX_DIM),
           's5_re': s5r_new.reshape(n, S5_GROUPS, S5_STATE), 's5_im': s5i_new.reshape(n, S5_GROUPS, S5_STATE),
           'gdn': gdn_new, 'gdn_conv': conv_new[:, CONV_LEAD:],
           'gla': jnp.moveaxis(gla_new.reshape(n, GLA_DV, GLA_HEADS, GLA_DK), 1, 3)}
    return y.reshape(n, t, D_MODEL), new


def kernel(x_prompt, x_sample, cache_k, cache_v, cache_kidx, state_s5_re, state_s5_im, state_gdn, state_gdn_conv, state_gla, w_in, w_branch, w_out, ln_g, ln_b, rel_bias, s5_a_re, s5_a_im, s5_log_dt, s5_b_re, s5_b_im, s5_c_re, s5_c_im, s5_d, s5_w_glu, gdn_conv, gdn_a_log, gdn_dt_bias, gdn_norm, gla_w_g2, gla_b_g, gla_norm):
    bias = _dsa_bias_tables(rel_bias)
    tril = (jnp.arange(KEY_BLOCK)[:, None] >= jnp.arange(KEY_BLOCK)[None, :]).astype(BF16)
    yp, ys = x_prompt, x_sample
    new_p, new_s = [], []
    w_proj, w_key = _regroup_weights(w_in)
    for l in range(DEPTH):
        lw = {'w_proj': w_proj[l], 'w_key': w_key[l], 'bias': bias, 'tril': tril,
              'w_branch': w_branch[l].astype(BF16), 'w_out': w_out[l].astype(BF16),
              'ln_g': ln_g[l].reshape(1, D_MODEL), 'ln_b': ln_b[l].reshape(1, D_MODEL),
              's5': _s5_prepare(s5_a_re[l], s5_a_im[l], s5_log_dt[l], s5_b_re[l], s5_b_im[l],
                                s5_c_re[l], s5_c_im[l], s5_d[l], s5_w_glu[l]),
              'gdn': {'convw': gdn_conv[l], 'alog': _lane_place(gdn_a_log[l], MISC_A),
                      'dtb': _lane_place(gdn_dt_bias[l], MISC_A), 'norm': gdn_norm[l].reshape(1, GDN_DV)},
              'gla': _gla_prepare(gla_w_g2[l], gla_b_g[l], gla_norm[l])}
        st = {'k': cache_k[l], 'v': cache_v[l], 'kidx': cache_kidx[l], 's5_re': state_s5_re[l],
              's5_im': state_s5_im[l], 'gdn': state_gdn[l], 'gdn_conv': state_gdn_conv[l], 'gla': state_gla[l]}
        yp, stp = _layer(yp, lw, None)
        ys, sts = _layer(ys, lw, st)
        new_p.append(stp)
        new_s.append(sts)
    names = ('k', 'v', 'kidx', 's5_re', 's5_im', 'gdn', 'gdn_conv', 'gla')
    stack = lambda states, name: jnp.stack([s[name] for s in states], axis=0)
    return ((yp, ys) + tuple(stack(new_p, nm) for nm in names) + tuple(stack(new_s, nm) for nm in names))
```

```python
import functools
import math

import jax
import jax.numpy as jnp
from jax import lax
from jax.experimental import pallas as pl
from jax.experimental.pallas import tpu as pltpu

F32 = jnp.float32
BF16 = jnp.bfloat16
HIGHEST = lax.Precision.HIGHEST

D_MODEL = 1024
DEPTH = 2
CHUNK = 64
N_BRANCH = 4
BRANCH_W = 512
S5_GROUP = 16
S5_GROUPS = BRANCH_W // S5_GROUP
S5_STATE = 64
S5_LANES = S5_GROUPS * S5_STATE
S5_SPLIT = 2
DSA_HEADS = 8
DSA_KV_HEADS = 2
DSA_HEAD_DIM = 64
IDX_HEADS = 4
IDX_DIM = 64
DSA_TOPK_MAX = 256
REL_BUCKETS = 32
REL_MAX_DIST = 128
GDN_HEADS = 4
GDN_DK = 128
GDN_DV = 128
GDN_QK = GDN_HEADS * GDN_DK
GDN_VW = GDN_HEADS * GDN_DV
CONV_W = 4
GDN_CONV_CH = 2 * GDN_QK + GDN_VW
GDN_CHUNKS_PER_STEP = 2
CONV_LEAD = 8 - (CONV_W - 1)
GLA_HEADS = 4
GLA_DK = 64
GLA_DV = 128
GLA_GATE_RANK = 16
GLA_TAU = 16.0
GLA_BLOCK = 16
LN_EPS = 1e-5
RMS_EPS = 1e-6
DN_ALPHA = (2 * DEPTH) ** 0.25
LOG2E = math.log2(math.e)

LANE = 128
SUBLANE = 8
KEY_BLOCK = 128
KEY_SUB = 4
KEY_TILE = KEY_SUB * KEY_BLOCK
SOFTMAX_CEIL = 2.0 ** 120
SOFTMAX_FLOOR = 2.0 ** -60
FOLD_ROWS = 64
Q_BLOCK = 128
Q_TILE = 256
VMEM_LIMIT = 56 * 1024 * 1024

IN_LAYOUT = (
    ('a_u', BRANCH_W), ('a_gate', BRANCH_W),
    ('b_q', DSA_HEADS * DSA_HEAD_DIM), ('b_k', DSA_KV_HEADS * DSA_HEAD_DIM),
    ('b_v', DSA_KV_HEADS * DSA_HEAD_DIM), ('b_qi', IDX_HEADS * IDX_DIM), ('b_ki', IDX_DIM),
    ('b_wi', IDX_HEADS), ('b_gate', BRANCH_W),
    ('c_qkv', GDN_CONV_CH), ('c_beta', GDN_HEADS), ('c_a', GDN_HEADS), ('c_gate', BRANCH_W),
    ('d_q', GLA_HEADS * GLA_DK), ('d_k', GLA_HEADS * GLA_DK), ('d_v', GLA_HEADS * GLA_DV),
    ('d_g', GLA_GATE_RANK), ('d_gate', BRANCH_W),
    ('merge', N_BRANCH * D_MODEL),
)

PROJ_ORDER = ('merge', 'a_u', 'a_gate', 'b_q', 'b_qi', 'b_k', 'b_v', 'c_qkv', 'b_gate', 'c_gate',
              'd_v', 'd_gate', 'd_q', 'd_k', 'b_ki', 'b_wi', 'c_beta', 'c_a', 'd_g')
MISC_WI = IDX_DIM
MISC_BETA = MISC_WI + IDX_HEADS
MISC_A = MISC_BETA + GDN_HEADS
MISC_G = MISC_A + GDN_HEADS
MISC_PAD = LANE - (MISC_G + GLA_GATE_RANK)


def _proj_offsets():
    widths = dict(IN_LAYOUT)
    off, out = 0, {}
    for name in PROJ_ORDER:
        out[name] = off
        off += widths[name]
    return out, off + MISC_PAD


PROJ_OFF, PROJ_W = _proj_offsets()
OFF_MISC = PROJ_OFF['b_ki']
HK_W = 5 * LANE
HK_V = 2 * LANE
HK_KI = 4 * LANE


def _dot(a, b, precision=None):
    return jnp.dot(a, b, preferred_element_type=F32, precision=precision)


def _dot_nt(a, b, precision=None):
    return lax.dot_general(a, b, (((1,), (1,)), ((), ())), preferred_element_type=F32, precision=precision)


def _dot_tn(a, b, precision=None):
    return lax.dot_general(a, b, (((0,), (0,)), ((), ())), preferred_element_type=F32, precision=precision)


def _split_bf16(x):
    hi = x.astype(BF16)
    return hi, (x - hi.astype(F32)).astype(BF16)


def _dot3(a, b):
    a_hi, a_lo = _split_bf16(a)
    b_hi, b_lo = _split_bf16(b)
    return _dot(a_hi, b_hi) + (_dot(a_hi, b_lo) + _dot(a_lo, b_hi))


def _params(*sem):
    return pltpu.CompilerParams(dimension_semantics=sem, vmem_limit_bytes=VMEM_LIMIT)


def _softplus(x):
    return jnp.maximum(x, 0.0) + jnp.log1p(jnp.exp(-jnp.abs(x)))


def _iota(shape, dim):
    return lax.broadcasted_iota(jnp.int32, shape, dim)


def _gated(y, gate):
    return (y * jax.nn.silu(gate)).astype(BF16)


def _mm_kernel(x_ref, w_ref, o_ref):
    o_ref[...] = _dot(x_ref[...].astype(BF16), w_ref[...]).astype(o_ref.dtype)


def _matmul(x, w, out_dtype, tn):
    rows, kdim = x.shape
    width = w.shape[1]
    tm = min(rows, 1024)

    def streamed(x_hbm, w_hbm, o_hbm):
        pltpu.emit_pipeline(
            _mm_kernel,
            grid=(rows // tm, width // tn),
            in_specs=[pl.BlockSpec((tm, kdim), lambda i, j: (i, 0),
                                   pipeline_mode=pl.Buffered(2, use_lookahead=True)),
                      pl.BlockSpec((kdim, tn), lambda i, j: (0, j))],
            out_specs=pl.BlockSpec((tm, tn), lambda i, j: (i, j)),
        )(x_hbm, w_hbm, o_hbm)

    return pl.pallas_call(
        streamed,
        in_specs=[pl.BlockSpec(memory_space=pl.ANY), pl.BlockSpec(memory_space=pl.ANY)],
        out_specs=pl.BlockSpec(memory_space=pl.ANY),
        out_shape=jax.ShapeDtypeStruct((rows, width), out_dtype),
        compiler_params=_params(),
        name="proj_matmul",
    )(x, w)


def _s5_kernel(u_ref, gate_ref, h0r_ref, h0i_ref, ar_ref, ai_ref, bb_ref, cc_ref, d_ref, wglu_ref,
               ya_ref, hr_out, hi_out, xr_s, xi_s, hr_s, hi_s, cr_s, ci_s, *, tc):
    @pl.when(pl.program_id(1) == 0)
    def _():
        cr_s[...] = h0r_ref[...]
        ci_s[...] = h0i_ref[...]

    u = u_ref[...]
    ub = u.astype(BF16)
    uw = BRANCH_W // S5_SPLIT
    sw = S5_LANES // S5_SPLIT
    for part in range(S5_SPLIT):
        x = _dot(ub[:, part * uw:(part + 1) * uw], bb_ref[part])
        xr_s[:, part * sw:(part + 1) * sw] = x[:, :sw]
        xi_s[:, part * sw:(part + 1) * sw] = x[:, sw:]

    ar = ar_ref[...]
    ai = ai_ref[...]

    def step(t, carry):
        hr, hi = carry
        nhr = ar * hr - ai * hi + xr_s[pl.ds(t, 1), :]
        nhi = ar * hi + ai * hr + xi_s[pl.ds(t, 1), :]
        hr_s[pl.ds(t, 1), :] = nhr
        hi_s[pl.ds(t, 1), :] = nhi
        return nhr, nhi

    hr, hi = lax.fori_loop(0, tc, step, (cr_s[...], ci_s[...]), unroll=8)
    cr_s[...] = hr
    ci_s[...] = hi
    hr_out[...] = hr
    hi_out[...] = hi
    ys = []
    for part in range(S5_SPLIT):
        hcat = jnp.concatenate([hr_s[:, part * sw:(part + 1) * sw], hi_s[:, part * sw:(part + 1) * sw]], axis=1)
        ys.append(_dot(hcat.astype(BF16), cc_ref[part]))
    y = jnp.concatenate(ys, axis=1) + d_ref[...] * u
    y = jax.nn.gelu(y)
    g = _dot(y.astype(BF16), wglu_ref[...])
    ya_ref[...] = _gated(g[:, :BRANCH_W] * jax.nn.sigmoid(g[:, BRANCH_W:]), gate_ref[...])


def _s5_call(h, n, t, h0r, h0i, sp):
    tc = min(t, 256)
    nc = t // tc
    uw = BRANCH_W // S5_SPLIT
    sw = S5_LANES // S5_SPLIT
    const = lambda shape: pl.BlockSpec(shape, lambda i, c: (0,) * len(shape))
    state = pl.BlockSpec((None, 1, S5_LANES), lambda i, c: (i, 0, 0))
    return pl.pallas_call(
        functools.partial(_s5_kernel, tc=tc),
        grid=(n, nc),
        in_specs=[pl.BlockSpec((tc, BRANCH_W), lambda i, c: (i * nc + c, PROJ_OFF['a_u'] // BRANCH_W)),
                  pl.BlockSpec((tc, BRANCH_W), lambda i, c: (i * nc + c, PROJ_OFF['a_gate'] // BRANCH_W)),
                  state, state,
                  const((1, S5_LANES)), const((1, S5_LANES)),
                  const((S5_SPLIT, uw, 2 * sw)), const((S5_SPLIT, 2 * sw, uw)),
                  const((1, BRANCH_W)), const((BRANCH_W, 2 * BRANCH_W))],
        out_specs=[pl.BlockSpec((tc, BRANCH_W), lambda i, c: (i * nc + c, 0)), state, state],
        out_shape=[jax.ShapeDtypeStruct((n * t, BRANCH_W), BF16),
                   jax.ShapeDtypeStruct((n, 1, S5_LANES), F32),
                   jax.ShapeDtypeStruct((n, 1, S5_LANES), F32)],
        scratch_shapes=[pltpu.VMEM((tc, S5_LANES), F32), pltpu.VMEM((tc, S5_LANES), F32),
                        pltpu.VMEM((tc, S5_LANES), F32), pltpu.VMEM((tc, S5_LANES), F32),
                        pltpu.VMEM((1, S5_LANES), F32), pltpu.VMEM((1, S5_LANES), F32)],
        compiler_params=_params("parallel", "arbitrary"),
        name="s5_branch",
    )(h, h, h0r, h0i, sp['ar'], sp['ai'], sp['bb'], sp['cc'], sp['d'], sp['wglu'])


def _s5_prepare(a_re, a_im, log_dt, b_re, b_im, c_re, c_im, d, w_glu):
    lam_r = jnp.minimum(a_re, -1e-4)
    lam_i = a_im
    dt = jnp.exp(log_dt)[:, None]
    mag = jnp.exp(lam_r * dt)
    ar = mag * jnp.cos(lam_i * dt)
    ai = mag * jnp.sin(lam_i * dt)
    den = lam_r * lam_r + lam_i * lam_i
    nr = ar - 1.0
    cr = (nr * lam_r + ai * lam_i) / den
    ci = (ai * lam_r - nr * lam_i) / den
    bb_r = cr[..., None] * b_re - ci[..., None] * b_im
    bb_i = cr[..., None] * b_im + ci[..., None] * b_re
    eye = jnp.eye(S5_GROUPS, dtype=F32)

    def blockdiag_in(bb):
        m = eye[:, None, :, None] * jnp.swapaxes(bb, 1, 2)[:, :, None, :]
        return m.reshape(BRANCH_W, S5_LANES)

    def blockdiag_out(c):
        m = eye[:, None, :, None] * jnp.swapaxes(c, 1, 2)[:, :, None, :]
        return m.reshape(S5_LANES, BRANCH_W)

    uw = BRANCH_W // S5_SPLIT
    sw = S5_LANES // S5_SPLIT
    bbr, bbi = blockdiag_in(bb_r), blockdiag_in(bb_i)
    ccr, cci = blockdiag_out(c_re), blockdiag_out(c_im)
    bb = jnp.stack([jnp.concatenate([m[p * uw:(p + 1) * uw, p * sw:(p + 1) * sw] for m in (bbr, bbi)], axis=1)
                    for p in range(S5_SPLIT)]).astype(BF16)
    cc = jnp.stack([jnp.concatenate([ccr[p * sw:(p + 1) * sw, p * uw:(p + 1) * uw],
                                     -cci[p * sw:(p + 1) * sw, p * uw:(p + 1) * uw]], axis=0)
                    for p in range(S5_SPLIT)]).astype(BF16)
    return {'ar': ar.reshape(1, S5_LANES), 'ai': ai.reshape(1, S5_LANES), 'bb': bb, 'cc': cc,
            'd': d.reshape(1, BRANCH_W), 'wglu': w_glu.astype(BF16)}


def _sortable(x):
    bits = lax.bitcast_convert_type(x, jnp.int32)
    key = bits ^ ((bits >> 31) & jnp.int32(0x7FFFFFFF))
    return jnp.where(key == -1, 0, key)


def _dsa_kernel(q_ref, qi_ref, misc_ref, gate_ref, hk_ref, bias_ref, tril_ref, o_ref, sc_s, pk_s, acc_s, far_s, *,
                qb, b_off, n_keys, topk):
    nsub = max(qb // Q_BLOCK, 1)
    qw = qb // nsub
    b = pl.program_id(1) * nsub + b_off
    nkt = (b + nsub - 1 + KEY_SUB) // KEY_SUB
    fold = KEY_TILE // FOLD_ROWS
    key_row = _iota((KEY_TILE, qb), 0)
    qpos = b * Q_BLOCK + _iota((1, qb), 1)
    limit = jnp.minimum(((qpos >> 6) + 1) * CHUNK, n_keys)
    low_half = _iota((qb, LANE), 1) < DSA_HEAD_DIM

    def key_rows(ref_cols, j):
        return hk_ref[pl.ds(pl.multiple_of(j * KEY_TILE, KEY_TILE), KEY_TILE), ref_cols:ref_cols + LANE]

    def half_masked(x, hd):
        pair = x[:, (hd // 2) * LANE:(hd // 2 + 1) * LANE]
        return jnp.where(low_half == (hd % 2 == 0), pair, 0.0).astype(BF16)

    misc = misc_ref[...]
    pick = jnp.where(_iota((SUBLANE, LANE), 1) == MISC_WI + _iota((SUBLANE, LANE), 0), 1.0, 0.0)
    wi_t = _dot_nt(pick, misc, precision=HIGHEST) * (IDX_DIM ** -0.5 * IDX_HEADS ** -0.5)
    qi = qi_ref[...]
    qi_stack = jnp.concatenate([half_masked(qi, hd) for hd in range(IDX_HEADS)], axis=0)

    def pack16(v):
        half = KEY_TILE // 2
        return (v[0:half] & 0xFFFF) | (v[half:] << 16)

    def score_block(j, carry):
        z = _dot_nt(key_rows(HK_KI, j), qi_stack)
        sc = jnp.zeros((KEY_TILE, qb), F32)
        for hd in range(IDX_HEADS):
            sc = sc + jnp.maximum(z[:, hd * qb:(hd + 1) * qb], 0.0) * wi_t[hd:hd + 1, :]
        sc = jnp.where(j * KEY_TILE + key_row < limit, sc, -jnp.inf)
        key = _sortable(sc)
        sc_s[j] = key
        pk_s[j] = pack16(key >> 16)
        return carry

    lax.fori_loop(0, nkt, score_block, 0)

    def count_ge(cand):
        def body(j, acc):
            return acc + jnp.where(sc_s[j] >= cand, 1.0, 0.0).reshape(fold, FOLD_ROWS, qb).sum(axis=0)
        acc = lax.fori_loop(0, nkt, body, jnp.zeros((FOLD_ROWS, qb), F32))
        return jnp.sum(acc, axis=0, keepdims=True)

    one16 = jnp.ones((KEY_TILE, qb), jnp.int16)
    zero16 = jnp.zeros((KEY_TILE, qb), jnp.int16)

    def count16_ge(cand):
        word = jnp.broadcast_to((cand & 0xFFFF) | (cand << 16), (KEY_TILE // 2, qb))
        c16 = pltpu.bitcast(word, jnp.int16)

        def body(j, acc):
            ind = jnp.where(pltpu.bitcast(pk_s[j], jnp.int16) >= c16, one16, zero16)
            for i in range(fold):
                acc = acc + ind[i * FOLD_ROWS:(i + 1) * FOLD_ROWS]
            return acc
        acc = lax.fori_loop(0, nkt, body, jnp.zeros((FOLD_ROWS, qb), jnp.int16))
        return jnp.sum(acc.astype(F32), axis=0, keepdims=True)

    def radix_search(start, want):
        def bit_step(i, res):
            cand = res + (jnp.int32(1) << (14 - i))
            return jnp.where(count16_ge(cand) >= want, cand, res)
        return lax.fori_loop(0, 15, bit_step, start)

    kf = float(topk)
    top = radix_search(jnp.where(count16_ge(jnp.zeros((1, qb), jnp.int32)) >= kf, 0, -32768), kf)
    int16_max = jnp.iinfo(jnp.int16).max
    above = jnp.where(top == int16_max, 0.0, count16_ge(jnp.minimum(top + 1, int16_max)))
    want_low = kf - above

    def pack_low(j, carry):
        key = sc_s[j]
        pk_s[j] = pack16(jnp.where((key >> 16) == top, (key >> 1) & 0x7FFF, -1))
        return carry

    lax.fori_loop(0, nkt, pack_low, 0)
    low = radix_search(jnp.zeros((1, qb), jnp.int32), want_low)
    base = (top << 16) | (low << 1)
    thr = jnp.where(count_ge(base | 1) >= kf, base | 1, base)
    need = kf - count_ge(thr + 1)

    tril = tril_ref[...]

    def select_block(j, ties):
        keys = sc_s[j]
        eq = keys == thr
        eq_b = jnp.where(eq, 1.0, 0.0).astype(BF16)
        parts, carry = [], ties
        for s in range(KEY_SUB):
            part = _dot(tril, eq_b[s * KEY_BLOCK:(s + 1) * KEY_BLOCK]) + carry
            parts.append(part)
            carry = part[KEY_BLOCK - 1:KEY_BLOCK, :]
        prefix = jnp.concatenate(parts, axis=0)
        sel = ((keys > thr) | (eq & (prefix <= need))) & (j * KEY_TILE + key_row < limit)
        sc_s[j] = lax.bitcast_convert_type(jnp.where(sel, 0.0, -jnp.inf), jnp.int32)
        return prefix[KEY_TILE - 1:KEY_TILE, :]

    lax.fori_loop(0, nkt, select_block, jnp.zeros((1, qb), F32))

    q = q_ref[...] * (DSA_HEAD_DIM ** -0.5 * LOG2E)
    rep = DSA_HEADS // DSA_KV_HEADS
    eye = jnp.where(_iota((LANE, LANE), 0) == _iota((LANE, LANE), 1), 1.0, 0.0).astype(BF16)
    value_lane = _iota((KEY_TILE, LANE), 1) < DSA_HEAD_DIM
    n_pairs = DSA_HEADS // 2
    q_pair = [jnp.concatenate([half_masked(q, 2 * c), half_masked(q, 2 * c + 1)], axis=0) for c in range(n_pairs)]
    far_bias = [jnp.concatenate([bias_ref[hd, 2, 0:1, 0:qw]] * nsub, axis=1)
                for hd in range(DSA_HEADS)]
    n_far = jnp.maximum((b - 1) // KEY_SUB, 0)

    def tile_operands(j):
        mask = lax.bitcast_convert_type(sc_s[j], F32)
        kd = [key_rows(grp * LANE, j) for grp in range(DSA_KV_HEADS)]
        zs = [_dot_nt(kd[(2 * c) // rep], q_pair[c]) for c in range(n_pairs)]
        vd = [jnp.where(value_lane, key_rows(HK_V + grp * LANE, j), 1.0).astype(BF16)
              for grp in range(DSA_KV_HEADS)]
        return mask, zs, vd

    def near_bias(hd, j):
        cols = [jnp.concatenate([bias_ref[hd, jnp.clip(b + u - (KEY_SUB * j + s), 0, 2), :, 0:qw]
                                 for s in range(KEY_SUB)], axis=0) for u in range(nsub)]
        return jnp.concatenate(cols, axis=1)

    def plain_tile(j, carry, far):
        mask, zs, vd = tile_operands(j)
        ps = []
        for hd in range(DSA_HEADS):
            r = hd % 2
            zh = zs[hd // 2][:, r * qb:(r + 1) * qb] + mask
            if not far:
                zh = zh + near_bias(hd, j)
            ps.append(jnp.exp2(zh).astype(BF16))
        acc = far_s if far else acc_s
        for c in range(n_pairs):
            acc[c] += _dot_tn(vd[(2 * c) // rep], jnp.concatenate(ps[2 * c:2 * c + 2], axis=1))
        return carry

    acc_s[...] = jnp.zeros_like(acc_s)
    far_s[...] = jnp.zeros_like(far_s)
    lax.fori_loop(0, n_far, lambda j, c: plain_tile(j, c, True), 0)
    lax.fori_loop(n_far, nkt, lambda j, c: plain_tile(j, c, False), 0)
    healthy = jnp.ones((1, 1), F32)
    for c in range(n_pairs):
        scale = jnp.concatenate([jnp.exp2(far_bias[2 * c]), jnp.exp2(far_bias[2 * c + 1])], axis=1)
        total = far_s[c] * scale + acc_s[c]
        acc_s[c] = total
        norm = total[DSA_HEAD_DIM:DSA_HEAD_DIM + 1, :]
        ok = (jnp.abs(total) < SOFTMAX_CEIL) & (norm > SOFTMAX_FLOOR)
        healthy = jnp.minimum(healthy, jnp.min(jnp.where(ok, 1.0, 0.0), keepdims=True))

    def attend(j, ms, far):
        mask, zs, vd = tile_operands(j)
        new_ms, ps, alphas = [], [], []
        for hd in range(DSA_HEADS):
            r = hd % 2
            zh = zs[hd // 2][:, r * qb:(r + 1) * qb] + mask
            if not far:
                zh = zh + near_bias(hd, j)
            tile_max = jnp.max(zh.reshape(fold, FOLD_ROWS, qb).max(axis=0), axis=0, keepdims=True)
            shift = far_bias[hd] if far else 0.0
            m_new = jnp.maximum(ms[hd], tile_max + shift)
            ps.append(jnp.exp2(zh - (m_new - shift)).astype(BF16))
            alphas.append(jnp.exp2(ms[hd] - m_new))
            new_ms.append(m_new)
        for c in range(n_pairs):
            acc_s[c] = (acc_s[c] * jnp.concatenate(alphas[2 * c:2 * c + 2], axis=1)
                        + _dot_tn(vd[(2 * c) // rep], jnp.concatenate(ps[2 * c:2 * c + 2], axis=1)))
        return tuple(new_ms)

    @pl.when(healthy[0, 0] < 0.5)
    def _():
        acc_s[...] = jnp.zeros_like(acc_s)
        ms = tuple(jnp.full((1, qb), -1e30, F32) for _ in range(DSA_HEADS))
        ms = lax.fori_loop(0, n_far, lambda j, m: attend(j, m, True), ms)
        lax.fori_loop(n_far, nkt, lambda j, m: attend(j, m, False), ms)

    for c in range(n_pairs):
        halves = []
        for r in range(2):
            blk = acc_s[c, :, r * qb:(r + 1) * qb]
            halves.append(blk[0:DSA_HEAD_DIM] / blk[DSA_HEAD_DIM:DSA_HEAD_DIM + 1])
        o_t = jnp.concatenate(halves, axis=0)
        o_hi, o_lo = _split_bf16(o_t)
        o_ref[:, c * LANE:(c + 1) * LANE] = _gated(_dot_tn(o_hi, eye) + _dot_tn(o_lo, eye),
                                                   gate_ref[:, c * LANE:(c + 1) * LANE])


def _dsa_call(h, hk, n, t, bias, tril, *, b_off, n_keys, topk):
    qb = min(t, Q_TILE)
    nqb = t // qb
    lp = hk.shape[1]
    wq = DSA_HEADS * DSA_HEAD_DIM
    wqi = IDX_HEADS * IDX_DIM
    return pl.pallas_call(
        functools.partial(_dsa_kernel, qb=qb, b_off=b_off, n_keys=n_keys, topk=topk),
        grid=(n, nqb),
        in_specs=[pl.BlockSpec((qb, wq), lambda i, c: (i * nqb + c, PROJ_OFF['b_q'] // wq)),
                  pl.BlockSpec((qb, wqi), lambda i, c: (i * nqb + c, PROJ_OFF['b_qi'] // wqi)),
                  pl.BlockSpec((qb, LANE), lambda i, c: (i * nqb + c, OFF_MISC // LANE)),
                  pl.BlockSpec((qb, BRANCH_W), lambda i, c: (i * nqb + c, PROJ_OFF['b_gate'] // BRANCH_W)),
                  pl.BlockSpec((None, lp, HK_W), lambda i, c: (i, 0, 0)),
                  pl.BlockSpec((DSA_HEADS, 3, KEY_BLOCK, Q_BLOCK), lambda i, c: (0, 0, 0, 0)),
                  pl.BlockSpec((KEY_BLOCK, KEY_BLOCK), lambda i, c: (0, 0))],
        out_specs=pl.BlockSpec((qb, wq), lambda i, c: (i * nqb + c, 0)),
        out_shape=jax.ShapeDtypeStruct((n * t, wq), BF16),
        scratch_shapes=[pltpu.VMEM((lp // KEY_TILE, KEY_TILE, qb), jnp.int32),
                        pltpu.VMEM((lp // KEY_TILE, KEY_TILE // 2, qb), jnp.int32),
                        pltpu.VMEM((DSA_HEADS // 2, LANE, 2 * qb), F32),
                        pltpu.VMEM((DSA_HEADS // 2, LANE, 2 * qb), F32)],
        compiler_params=_params("parallel", "arbitrary"),
        name="dsa_branch",
    )(h, h, h, h, hk, bias, tril)


def _t5_bucket(rel):
    nb = REL_BUCKETS // 2
    max_exact = nb // 2
    ret = jnp.where(rel > 0, nb, 0)
    dist = jnp.abs(rel)
    distf = jnp.maximum(dist, 1).astype(F32)
    large = max_exact + (jnp.log(distf / max_exact) / math.log(REL_MAX_DIST / max_exact)
                         * (nb - max_exact)).astype(jnp.int32)
    large = jnp.minimum(large, nb - 1)
    return ret + jnp.where(dist < max_exact, dist, large)


def _dsa_bias_tables(rel_bias):
    lc = jnp.arange(KEY_BLOCK, dtype=jnp.int32)[:, None]
    qr = jnp.arange(Q_BLOCK, dtype=jnp.int32)[None, :]
    rel = jnp.stack([lc - qr - back * KEY_BLOCK for back in range(3)], axis=0)
    onehot = jax.nn.one_hot(_t5_bucket(rel), REL_BUCKETS, dtype=F32)
    bias = jnp.einsum('klqb,bh->hklq', onehot, rel_bias.astype(F32), precision=HIGHEST)
    return bias * LOG2E


def _gdn_kernel(raw_ref, misc_ref, gate_ref, s0_ref, conv0_ref, convw_ref, alog_ref, dtb_ref, norm_ref,
                o_ref, s_out, conv_out, xp_s, st_s, *, n, c, nsub):
    rows = nsub * c

    @pl.when(pl.program_id(0) == 0)
    def _():
        xp_s[:, 0:SUBLANE, :] = conv0_ref[...]
        st_s[...] = s0_ref[...]

    w = convw_ref[...]
    row = _iota((c, c), 0)
    col = _iota((c, c), 1)
    incl = row >= col
    strict = row > col
    eye = row == col
    tril = jnp.where(incl, 1.0, 0.0)
    eye_f = jnp.where(eye, 1.0, 0.0)
    chains = [(b, sub, hd) for b in range(n) for sub in range(nsub) for hd in range(GDN_HEADS)]

    qkv, beta_all, gc_all = [], [], []
    for b in range(n):
        raw = raw_ref[b]
        xp_s[b, SUBLANE:SUBLANE + rows, :] = raw
        y = raw * w[CONV_W - 1:CONV_W, :]
        for i in range(CONV_W - 1):
            y = y + xp_s[b, CONV_LEAD + i:CONV_LEAD + i + rows, :] * w[i:i + 1, :]
        tail = xp_s[b, rows:rows + SUBLANE, :]
        xp_s[b, 0:SUBLANE, :] = tail
        conv_out[b] = tail
        qkv.append(jax.nn.silu(y))
        misc = misc_ref[b]
        beta_all.append(jax.nn.sigmoid(misc))
        g_all = -jnp.exp(alog_ref[...]) * _softplus(misc + dtb_ref[...])
        gc_all.append([_dot(tril, g_all[sub * c:(sub + 1) * c], precision=HIGHEST)
                       for sub in range(nsub)])

    qs, ks, vs, betas, gcs, dec_s, dec_i = [], [], [], [], [], [], []
    for b, sub, hd in chains:
        x = qkv[b][sub * c:(sub + 1) * c]
        qh = x[:, hd * GDN_DK:(hd + 1) * GDN_DK]
        kh = x[:, GDN_QK + hd * GDN_DK:GDN_QK + (hd + 1) * GDN_DK]
        vs.append(x[:, 2 * GDN_QK + hd * GDN_DV:2 * GDN_QK + (hd + 1) * GDN_DV])
        qs.append(qh * lax.rsqrt(jnp.sum(qh * qh, axis=1, keepdims=True) + RMS_EPS) * (GDN_DK ** -0.5))
        ks.append(kh * lax.rsqrt(jnp.sum(kh * kh, axis=1, keepdims=True) + RMS_EPS))
        betas.append(beta_all[b][sub * c:(sub + 1) * c, MISC_BETA + hd:MISC_BETA + hd + 1])
        gc = gc_all[b][sub][:, MISC_A + hd:MISC_A + hd + 1]
        gcs.append(gc)
        gc_b = jnp.broadcast_to(gc, (c, c))
        gc_row = jnp.sum(jnp.where(eye, gc_b, 0.0), axis=0, keepdims=True)
        diff = gc_b - gc_row
        dec = jnp.exp(diff)
        dec_s.append(jnp.where(strict, dec, 0.0))
        dec_i.append(jnp.where(incl, dec, 0.0))
    kb = [k.astype(BF16) for k in ks]
    qb = [q.astype(BF16) for q in qs]
    pw = [-(beta * _dot_nt(k, k) * d) for beta, k, d in zip(betas, kb, dec_s)]
    inv = [eye_f + p for p in pw]
    for _ in range(int(math.log2(c)) - 1):
        pw = [_dot3(p, p) for p in pw]
        inv = [i + _dot3(i, p) for i, p in zip(inv, pw)]
    egc = [jnp.exp(gc) for gc in gcs]
    sol = [_dot3(i, jnp.concatenate([(beta * e) * k, beta * v], axis=1))
           for i, beta, e, k, v in zip(inv, betas, egc, ks, vs)]
    attn = [(_dot_nt(q, k) * d).astype(BF16) for q, k, d in zip(qb, kb, dec_i)]
    for sub in range(nsub):
        ids = [i for i, ch in enumerate(chains) if ch[1] == sub]
        s_old = [st_s[chains[i][0], chains[i][2]] for i in ids]
        sb = [s.astype(BF16) for s in s_old]
        vb = [(sol[i][:, GDN_DK:] - _dot(sol[i][:, :GDN_DK].astype(BF16), s)).astype(BF16) for i, s in zip(ids, sb)]
        outs = [egc[i] * _dot(qb[i], s) + _dot(attn[i], v) for i, s, v in zip(ids, sb, vb)]
        for i, v, s, o in zip(ids, vb, s_old, outs):
            b, _, hd = chains[i]
            g_last = gcs[i][c - 1:c, :]
            st_s[b, hd] = jnp.exp(g_last) * s + _dot_tn((ks[i] * jnp.exp(g_last - gcs[i])).astype(BF16), v)
            o = o * lax.rsqrt(jnp.mean(o * o, axis=1, keepdims=True) + RMS_EPS) * norm_ref[...]
            o_ref[b, sub * c:(sub + 1) * c, hd * GDN_DV:(hd + 1) * GDN_DV] = _gated(
                o, gate_ref[b, sub * c:(sub + 1) * c, hd * GDN_DV:(hd + 1) * GDN_DV])
    s_out[...] = st_s[...]


def _gdn_call(h3, s0, conv0, gp):
    n, t, _ = h3.shape
    c = min(t, CHUNK)
    nsub = GDN_CHUNKS_PER_STEP if t % (GDN_CHUNKS_PER_STEP * c) == 0 else 1
    rows = nsub * c
    const = lambda shape: pl.BlockSpec(shape, lambda j: (0,) * len(shape))
    state = const((n, GDN_HEADS, GDN_DK, GDN_DV))
    conv = const((n, SUBLANE, GDN_CONV_CH))
    return pl.pallas_call(
        functools.partial(_gdn_kernel, n=n, c=c, nsub=nsub),
        grid=(t // rows,),
        in_specs=[pl.BlockSpec((n, rows, GDN_CONV_CH), lambda j: (0, j, PROJ_OFF['c_qkv'] // GDN_CONV_CH)),
                  pl.BlockSpec((n, rows, LANE), lambda j: (0, j, OFF_MISC // LANE)),
                  pl.BlockSpec((n, rows, BRANCH_W), lambda j: (0, j, PROJ_OFF['c_gate'] // BRANCH_W)),
                  state, conv,
                  const((CONV_W, GDN_CONV_CH)), const((1, LANE)), const((1, LANE)), const((1, GDN_DV))],
        out_specs=[pl.BlockSpec((n, rows, GDN_VW), lambda j: (0, j, 0)), state, conv],
        out_shape=[jax.ShapeDtypeStruct((n, t, GDN_VW), BF16),
                   jax.ShapeDtypeStruct((n, GDN_HEADS, GDN_DK, GDN_DV), F32),
                   jax.ShapeDtypeStruct((n, SUBLANE, GDN_CONV_CH), F32)],
        scratch_shapes=[pltpu.VMEM((n, rows + SUBLANE, GDN_CONV_CH), F32),
                        pltpu.VMEM((n, GDN_HEADS, GDN_DK, GDN_DV), F32)],
        compiler_params=_params("arbitrary"),
        name="gdn_branch",
    )(h3, h3, h3, s0, conv0, gp['convw'], gp['alog'], gp['dtb'], gp['norm'])


def _lane_place(v, off):
    return jnp.zeros((1, LANE), F32).at[0, off:off + v.shape[0]].set(v.astype(F32))


def _gla_kernel(qk_ref, v_ref, misc_ref, gate_ref, s0_ref, wg_ref, bg_ref, norm_ref, hexp_ref, gsum_ref,
                o_ref, s_out, st_s, *, n, tc):
    blk = GLA_BLOCK
    wk = GLA_HEADS * GLA_DK

    @pl.when(pl.program_id(0) == 0)
    def _():
        st_s[...] = s0_ref[...]

    row = _iota((blk, blk), 0)
    col = _iota((blk, blk), 1)
    tril = jnp.where(row >= col, 1.0, 0.0)
    jrow = _iota((blk, wk), 0)
    lane_k = _iota((blk, wk), 1)
    head_mask = [(lane_k >= hd * GLA_DK) & (lane_k < (hd + 1) * GLA_DK) for hd in range(GLA_HEADS)]
    chains = range(n)

    def block(sb, carry):
        r0 = pl.multiple_of(sb * blk, blk)
        qk = [qk_ref[b, pl.ds(r0, blk), :] for b in chains]
        q = [x[:, :wk] * (GLA_DK ** -0.5) for x in qk]
        k = [x[:, wk:] for x in qk]
        v = [v_ref[b, pl.ds(r0, blk), :] for b in chains]
        mb = [misc_ref[b, pl.ds(r0, blk), :].astype(BF16) for b in chains]
        lg = [jax.nn.log_sigmoid(_dot(m, wg_ref[...]) + bg_ref[...]) * (1.0 / GLA_TAU) for m in mb]
        bc = [_dot(tril, x, precision=HIGHEST) for x in lg]
        dmat = []
        for b in chains:
            slabs = []
            for i in range(blk):
                keep = jrow <= i
                dec = jnp.where(keep, jnp.exp(bc[b][i:i + 1, :] - bc[b]), 0.0)
                slabs.append(dec * q[b][i:i + 1, :] * k[b])
            dmat.append(jnp.concatenate(slabs, axis=0).astype(BF16))
        a_exp = [_dot(d, hexp_ref[...]) for d in dmat]
        prod = [a * jnp.concatenate([x] * blk, axis=0) for a, x in zip(a_exp, v)]
        o = [_dot(gsum_ref[...], p.astype(BF16)) for p in prod]
        qe = [x * jnp.exp(c) for x, c in zip(q, bc)]
        kt = [x * jnp.exp(c[blk - 1:blk, :] - c) for x, c in zip(k, bc)]
        s = [st_s[b] for b in chains]
        sbf = [x.astype(BF16) for x in s]
        for b in chains:
            upd = jnp.zeros_like(s[b])
            parts = []
            for hd in range(GLA_HEADS):
                parts.append(_dot_nt(jnp.where(head_mask[hd], qe[b], 0.0).astype(BF16), sbf[b]))
                vh = v[b][:, hd * GLA_DV:(hd + 1) * GLA_DV].astype(BF16)
                upd = upd + _dot_tn(vh, jnp.where(head_mask[hd], kt[b], 0.0).astype(BF16))
            ob = o[b] + jnp.concatenate(parts, axis=1)
            st_s[b] = jnp.exp(bc[b][blk - 1:blk, :]) * s[b] + upd
            for hd in range(GLA_HEADS):
                oh = ob[:, hd * GLA_DV:(hd + 1) * GLA_DV]
                oh = oh * lax.rsqrt(jnp.mean(oh * oh, axis=1, keepdims=True) + RMS_EPS) * norm_ref[...]
                o_ref[b, pl.ds(r0, blk), hd * GLA_DV:(hd + 1) * GLA_DV] = _gated(
                    oh, gate_ref[b, pl.ds(r0, blk), hd * GLA_DV:(hd + 1) * GLA_DV])
        return carry

    lax.fori_loop(0, tc // blk, block, 0, unroll=4 if tc // blk > 1 else 1)
    s_out[...] = st_s[...]


def _gla_call(h3, s0, gp):
    n, t, _ = h3.shape
    tc = min(t, 128)
    wk = GLA_HEADS * GLA_DK
    wv = GLA_HEADS * GLA_DV
    const = lambda shape: pl.BlockSpec(shape, lambda j: (0,) * len(shape))
    state = const((n, GLA_DV, wk))
    return pl.pallas_call(
        functools.partial(_gla_kernel, n=n, tc=tc),
        grid=(t // tc,),
        in_specs=[pl.BlockSpec((n, tc, 2 * wk), lambda j: (0, j, PROJ_OFF['d_q'] // (2 * wk))),
                  pl.BlockSpec((n, tc, wv), lambda j: (0, j, PROJ_OFF['d_v'] // wv)),
                  pl.BlockSpec((n, tc, LANE), lambda j: (0, j, OFF_MISC // LANE)),
                  pl.BlockSpec((n, tc, BRANCH_W), lambda j: (0, j, PROJ_OFF['d_gate'] // BRANCH_W)),
                  state,
                  const((LANE, wk)), const((1, wk)), const((1, GLA_DV)),
                  const((wk, wv)), const((GLA_BLOCK, GLA_BLOCK * GLA_BLOCK))],
        out_specs=[pl.BlockSpec((n, tc, wv), lambda j: (0, j, 0)), state],
        out_shape=[jax.ShapeDtypeStruct((n, t, wv), BF16),
                   jax.ShapeDtypeStruct((n, GLA_DV, wk), F32)],
        scratch_shapes=[pltpu.VMEM((n, GLA_DV, wk), F32)],
        compiler_params=_params("arbitrary"),
        name="gla_branch",
    )(h3, h3, h3, h3, s0, gp['wg'], gp['bg'], gp['norm'], gp['hexp'], gp['gsum'])


def _gla_prepare(w_g2, b_g, norm):
    wk = GLA_HEADS * GLA_DK
    wg = jnp.zeros((LANE, wk), F32).at[MISC_G:MISC_G + GLA_GATE_RANK].set(w_g2).astype(BF16)
    ck = jnp.arange(wk)[:, None] // GLA_DK
    cv = jnp.arange(GLA_HEADS * GLA_DV)[None, :] // GLA_DV
    ij = jnp.arange(GLA_BLOCK * GLA_BLOCK)[None, :] // GLA_BLOCK
    return {'wg': wg, 'bg': b_g.reshape(1, wk).astype(F32),
            'norm': norm.reshape(1, GLA_DV).astype(F32),
            'hexp': (ck == cv).astype(BF16),
            'gsum': (jnp.arange(GLA_BLOCK)[:, None] == ij).astype(BF16)}


def _merge_kernel(x_ref, ya_ref, yb_ref, yc_ref, yd_ref, mg_ref, wb_ref, wo_ref, lng_ref, lnb_ref, o_ref):
    mixed = None
    for br, act_ref in enumerate((ya_ref, yb_ref, yc_ref, yd_ref)):
        term = jax.nn.sigmoid(mg_ref[:, br * D_MODEL:(br + 1) * D_MODEL]) * _dot(act_ref[...], wb_ref[br])
        mixed = term if mixed is None else mixed + term
    z = DN_ALPHA * x_ref[...] + _dot(mixed.astype(BF16), wo_ref[...])
    mu = jnp.mean(z, axis=1, keepdims=True)
    zc = z - mu
    var = jnp.mean(zc * zc, axis=1, keepdims=True)
    o_ref[...] = zc * lax.rsqrt(var + LN_EPS) * lng_ref[...] + lnb_ref[...]


def _merge_call(x, h, ya, yb, yc, yd, wb, wo, lng, lnb):
    rows = x.shape[0]
    tm = min(rows, 256)
    bw = BRANCH_W
    ybs = pl.BlockSpec((tm, bw), lambda i: (i, 0))
    return pl.pallas_call(
        _merge_kernel,
        grid=(rows // tm,),
        in_specs=[pl.BlockSpec((tm, D_MODEL), lambda i: (i, 0)), ybs, ybs, ybs, ybs,
                  pl.BlockSpec((tm, N_BRANCH * D_MODEL), lambda i: (i, 0)),
                  pl.BlockSpec((N_BRANCH, bw, D_MODEL), lambda i: (0, 0, 0)),
                  pl.BlockSpec((D_MODEL, D_MODEL), lambda i: (0, 0)),
                  pl.BlockSpec((1, D_MODEL), lambda i: (0, 0)),
                  pl.BlockSpec((1, D_MODEL), lambda i: (0, 0))],
        out_specs=pl.BlockSpec((tm, D_MODEL), lambda i: (i, 0)),
        out_shape=jax.ShapeDtypeStruct((rows, D_MODEL), F32),
        compiler_params=_params("parallel"),
        name="merge_norm",
    )(x, ya, yb, yc, yd, h, wb, wo, lng, lnb)


def _split_w_in(w_in):
    out, off = {}, 0
    for name, width in IN_LAYOUT:
        out[name] = w_in[:, off:off + width]
        off += width
    return out


def _regroup_kernel(w_ref, proj_ref, key_ref):
    cols = _split_w_in(w_ref[...])
    pad = jnp.zeros((w_ref.shape[0], MISC_PAD), F32)
    proj_ref[...] = jnp.concatenate([cols[name] for name in PROJ_ORDER] + [pad], axis=1).astype(BF16)
    k, v, ki = cols['b_k'], cols['b_v'], cols['b_ki']
    dh = DSA_HEAD_DIM
    key_ref[...] = jnp.concatenate([k[:, :dh], k[:, :dh], k[:, dh:], k[:, dh:],
                                    v[:, :dh], v[:, :dh], v[:, dh:], v[:, dh:], ki, ki], axis=1).astype(BF16)


def _regroup_weights(w_in):
    depth, rows, width = w_in.shape
    tr = 128
    return pl.pallas_call(
        _regroup_kernel,
        grid=(depth, rows // tr),
        in_specs=[pl.BlockSpec((None, tr, width), lambda l, i: (l, i, 0))],
        out_specs=[pl.BlockSpec((None, tr, PROJ_W), lambda l, i: (l, i, 0)),
                   pl.BlockSpec((None, tr, HK_W), lambda l, i: (l, i, 0))],
        out_shape=[jax.ShapeDtypeStruct((depth, rows, PROJ_W), BF16),
                   jax.ShapeDtypeStruct((depth, rows, HK_W), BF16)],
        compiler_params=_params("parallel", "parallel"),
        name="regroup_weights",
    )(w_in)


def _dup_keys(k, v, ki):
    return jnp.concatenate([k[:, :, 0], k[:, :, 0], k[:, :, 1], k[:, :, 1],
                            v[:, :, 0], v[:, :, 0], v[:, :, 1], v[:, :, 1], ki, ki], axis=-1).astype(BF16)


def _layer(x, lw, st):
    n, t, _ = x.shape
    x2 = x.reshape(n * t, D_MODEL)
    h = _matmul(x2, lw['w_proj'], F32, tn=PROJ_W // 9)
    hk_new = _matmul(x2, lw['w_key'], BF16, tn=HK_W).reshape(n, t, HK_W)
    if st is None:
        hk, n_keys, b_off = hk_new, t, 0
        s5r = jnp.zeros((n, 1, S5_LANES), F32)
        s5i = jnp.zeros((n, 1, S5_LANES), F32)
        gdn0 = jnp.zeros((n, GDN_HEADS, GDN_DK, GDN_DV), F32)
        conv0 = jnp.zeros((n, SUBLANE, GDN_CONV_CH), F32)
        gla0 = jnp.zeros((n, GLA_DV, GLA_HEADS * GLA_DK), F32)
    else:
        past = st['k'].shape[1]
        n_keys = past + t
        b_off = past // Q_BLOCK
        lp = -(-n_keys // KEY_TILE) * KEY_TILE
        hk = jnp.concatenate([_dup_keys(st['k'], st['v'], st['kidx']), hk_new,
                              jnp.zeros((n, lp - n_keys, HK_W), BF16)], axis=1)
        s5r = st['s5_re'].reshape(n, 1, S5_LANES)
        s5i = st['s5_im'].reshape(n, 1, S5_LANES)
        gdn0 = st['gdn']
        conv0 = jnp.concatenate([jnp.zeros((n, CONV_LEAD, GDN_CONV_CH), F32), st['gdn_conv']], axis=1)
        gla0 = jnp.moveaxis(st['gla'], 3, 1).reshape(n, GLA_DV, GLA_HEADS * GLA_DK)
    topk = min(DSA_TOPK_MAX, n_keys // 4)

    ya, s5r_new, s5i_new = _s5_call(h, n, t, s5r, s5i, lw['s5'])
    yb = _dsa_call(h, hk, n, t, lw['bias'], lw['tril'], b_off=b_off, n_keys=n_keys, topk=topk)
    h3 = h.reshape(n, t, PROJ_W)
    yc, gdn_new, conv_new = _gdn_call(h3, gdn0, conv0, lw['gdn'])
    yd, gla_new = _gla_call(h3, gla0, lw['gla'])
    yc = yc.reshape(n * t, GDN_VW)
    yd = yd.reshape(n * t, GLA_HEADS * GLA_DV)
    y = _merge_call(x2, h, ya, yb, yc, yd, lw['w_branch'], lw['w_out'], lw['ln_g'], lw['ln_b'])

    def cols(name, width):
        return h[:, PROJ_OFF[name]:PROJ_OFF[name] + width].reshape(n, t, width)

    new = {'k': cols('b_k', DSA_KV_HEADS * DSA_HEAD_DIM).reshape(n, t, DSA_KV_HEADS, DSA_HEAD_DIM),
           'v': cols('b_v', DSA_KV_HEADS * DSA_HEAD_DIM).reshape(n, t, DSA_KV_HEADS, DSA_HEAD_DIM),
           'kidx': cols('b_ki', IDX_DIM),
           's5_re': s5r_new.reshape(n, S5_GROUPS, S5_STATE), 's5_im': s5i_new.reshape(n, S5_GROUPS, S5_STATE),
           'gdn': gdn_new, 'gdn_conv': conv_new[:, CONV_LEAD:],
           'gla': jnp.moveaxis(gla_new.reshape(n, GLA_DV, GLA_HEADS, GLA_DK), 1, 3)}
    return y.reshape(n, t, D_MODEL), new


def kernel(x_prompt, x_sample, cache_k, cache_v, cache_kidx, state_s5_re, state_s5_im, state_gdn, state_gdn_conv, state_gla, w_in, w_branch, w_out, ln_g, ln_b, rel_bias, s5_a_re, s5_a_im, s5_log_dt, s5_b_re, s5_b_im, s5_c_re, s5_c_im, s5_d, s5_w_glu, gdn_conv, gdn_a_log, gdn_dt_bias, gdn_norm, gla_w_g2, gla_b_g, gla_norm):
    bias = _dsa_bias_tables(rel_bias)
    tril = (jnp.arange(KEY_BLOCK)[:, None] >= jnp.arange(KEY_BLOCK)[None, :]).astype(BF16)
    yp, ys = x_prompt, x_sample
    new_p, new_s = [], []
    w_proj, w_key = _regroup_weights(w_in)
    for l in range(DEPTH):
        lw = {'w_proj': w_proj[l], 'w_key': w_key[l], 'bias': bias, 'tril': tril,
              'w_branch': w_branch[l].astype(BF16), 'w_out': w_out[l].astype(BF16),
              'ln_g': ln_g[l].reshape(1, D_MODEL), 'ln_b': ln_b[l].reshape(1, D_MODEL),
              's5': _s5_prepare(s5_a_re[l], s5_a_im[l], s5_log_dt[l], s5_b_re[l], s5_b_im[l],
                                s5_c_re[l], s5_c_im[l], s5_d[l], s5_w_glu[l]),
              'gdn': {'convw': gdn_conv[l], 'alog': _lane_place(gdn_a_log[l], MISC_A),
                      'dtb': _lane_place(gdn_dt_bias[l], MISC_A), 'norm': gdn_norm[l].reshape(1, GDN_DV)},
              'gla': _gla_prepare(gla_w_g2[l], gla_b_g[l], gla_norm[l])}
        st = {'k': cache_k[l], 'v': cache_v[l], 'kidx': cache_kidx[l], 's5_re': state_s5_re[l],
              's5_im': state_s5_im[l], 'gdn': state_gdn[l], 'gdn_conv': state_gdn_conv[l], 'gla': state_gla[l]}
        yp, stp = _layer(yp, lw, None)
        ys, sts = _layer(ys, lw, st)
        new_p.append(stp)
        new_s.append(sts)
    names = ('k', 'v', 'kidx', 's5_re', 's5_im', 'gdn', 'gdn_conv', 'gla')
    stack = lambda states, name: jnp.stack([s[name] for s in states], axis=0)
    return ((yp, ys) + tuple(stack(new_p, nm) for nm in names) + tuple(stack(new_s, nm) for nm in names))
```

```python
import functools
import math

import jax
import jax.numpy as jnp
from jax import lax
from jax.experimental import pallas as pl
from jax.experimental.pallas import tpu as pltpu

F32 = jnp.float32
BF16 = jnp.bfloat16
HIGHEST = lax.Precision.HIGHEST

D_MODEL = 1024
DEPTH = 2
CHUNK = 64
N_BRANCH = 4
BRANCH_W = 512
S5_GROUP = 16
S5_GROUPS = BRANCH_W // S5_GROUP
S5_STATE = 64
S5_LANES = S5_GROUPS * S5_STATE
S5_SPLIT = 2
DSA_HEADS = 8
DSA_KV_HEADS = 2
DSA_HEAD_DIM = 64
IDX_HEADS = 4
IDX_DIM = 64
DSA_TOPK_MAX = 256
REL_BUCKETS = 32
REL_MAX_DIST = 128
GDN_HEADS = 4
GDN_DK = 128
GDN_DV = 128
GDN_QK = GDN_HEADS * GDN_DK
GDN_VW = GDN_HEADS * GDN_DV
CONV_W = 4
GDN_CONV_CH = 2 * GDN_QK + GDN_VW
GDN_CHUNKS_PER_STEP = 2
CONV_LEAD = 8 - (CONV_W - 1)
GLA_HEADS = 4
GLA_DK = 64
GLA_DV = 128
GLA_GATE_RANK = 16
GLA_TAU = 16.0
GLA_BLOCK = 16
LN_EPS = 1e-5
RMS_EPS = 1e-6
DN_ALPHA = (2 * DEPTH) ** 0.25
LOG2E = math.log2(math.e)

LANE = 128
SUBLANE = 8
KEY_BLOCK = 128
KEY_SUB = 4
KEY_TILE = KEY_SUB * KEY_BLOCK
SOFTMAX_CEIL = 2.0 ** 120
SOFTMAX_FLOOR = 2.0 ** -60
FOLD_ROWS = 64
Q_BLOCK = 128
Q_TILE = 256
VMEM_LIMIT = 56 * 1024 * 1024

IN_LAYOUT = (
    ('a_u', BRANCH_W), ('a_gate', BRANCH_W),
    ('b_q', DSA_HEADS * DSA_HEAD_DIM), ('b_k', DSA_KV_HEADS * DSA_HEAD_DIM),
    ('b_v', DSA_KV_HEADS * DSA_HEAD_DIM), ('b_qi', IDX_HEADS * IDX_DIM), ('b_ki', IDX_DIM),
    ('b_wi', IDX_HEADS), ('b_gate', BRANCH_W),
    ('c_qkv', GDN_CONV_CH), ('c_beta', GDN_HEADS), ('c_a', GDN_HEADS), ('c_gate', BRANCH_W),
    ('d_q', GLA_HEADS * GLA_DK), ('d_k', GLA_HEADS * GLA_DK), ('d_v', GLA_HEADS * GLA_DV),
    ('d_g', GLA_GATE_RANK), ('d_gate', BRANCH_W),
    ('merge', N_BRANCH * D_MODEL),
)

PROJ_ORDER = ('merge', 'a_u', 'a_gate', 'b_q', 'b_qi', 'b_k', 'b_v', 'c_qkv', 'b_gate', 'c_gate',
              'd_v', 'd_gate', 'd_q', 'd_k', 'b_ki', 'b_wi', 'c_beta', 'c_a', 'd_g')
MISC_WI = IDX_DIM
MISC_BETA = MISC_WI + IDX_HEADS
MISC_A = MISC_BETA + GDN_HEADS
MISC_G = MISC_A + GDN_HEADS
MISC_PAD = LANE - (MISC_G + GLA_GATE_RANK)


def _proj_offsets():
    widths = dict(IN_LAYOUT)
    off, out = 0, {}
    for name in PROJ_ORDER:
        out[name] = off
        off += widths[name]
    return out, off + MISC_PAD


PROJ_OFF, PROJ_W = _proj_offsets()
OFF_MISC = PROJ_OFF['b_ki']
HK_W = 5 * LANE
HK_V = 2 * LANE
HK_KI = 4 * LANE


def _dot(a, b, precision=None):
    return jnp.dot(a, b, preferred_element_type=F32, precision=precision)


def _dot_nt(a, b, precision=None):
    return lax.dot_general(a, b, (((1,), (1,)), ((), ())), preferred_element_type=F32, precision=precision)


def _dot_tn(a, b, precision=None):
    return lax.dot_general(a, b, (((0,), (0,)), ((), ())), preferred_element_type=F32, precision=precision)


def _split_bf16(x):
    hi = x.astype(BF16)
    return hi, (x - hi.astype(F32)).astype(BF16)


def _dot3(a, b):
    a_hi, a_lo = _split_bf16(a)
    b_hi, b_lo = _split_bf16(b)
    return _dot(a_hi, b_hi) + (_dot(a_hi, b_lo) + _dot(a_lo, b_hi))


def _params(*sem):
    return pltpu.CompilerParams(dimension_semantics=sem, vmem_limit_bytes=VMEM_LIMIT)


def _softplus(x):
    return jnp.maximum(x, 0.0) + jnp.log1p(jnp.exp(-jnp.abs(x)))


def _iota(shape, dim):
    return lax.broadcasted_iota(jnp.int32, shape, dim)


def _gated(y, gate):
    return (y * jax.nn.silu(gate)).astype(BF16)


def _mm_kernel(x_ref, w_ref, o_ref):
    o_ref[...] = _dot(x_ref[...].astype(BF16), w_ref[...]).astype(o_ref.dtype)


def _matmul(x, w, out_dtype, tn, tm=1024):
    rows, kdim = x.shape
    width = w.shape[1]
    tm = min(rows, tm)
    return pl.pallas_call(
        _mm_kernel,
        grid=(rows // tm, width // tn),
        in_specs=[pl.BlockSpec((tm, kdim), lambda i, j: (i, 0)),
                  pl.BlockSpec((kdim, tn), lambda i, j: (0, j))],
        out_specs=pl.BlockSpec((tm, tn), lambda i, j: (i, j)),
        out_shape=jax.ShapeDtypeStruct((rows, width), out_dtype),
        compiler_params=_params("parallel", "parallel"),
        name="proj_matmul",
    )(x, w)


def _s5_kernel(u_ref, gate_ref, h0r_ref, h0i_ref, ar_ref, ai_ref, bb_ref, cc_ref, d_ref, wglu_ref,
               ya_ref, hr_out, hi_out, xr_s, xi_s, hr_s, hi_s, cr_s, ci_s, *, tc):
    @pl.when(pl.program_id(1) == 0)
    def _():
        cr_s[...] = h0r_ref[...]
        ci_s[...] = h0i_ref[...]

    u = u_ref[...]
    ub = u.astype(BF16)
    uw = BRANCH_W // S5_SPLIT
    sw = S5_LANES // S5_SPLIT
    for part in range(S5_SPLIT):
        x = _dot(ub[:, part * uw:(part + 1) * uw], bb_ref[part])
        xr_s[:, part * sw:(part + 1) * sw] = x[:, :sw]
        xi_s[:, part * sw:(part + 1) * sw] = x[:, sw:]

    ar = ar_ref[...]
    ai = ai_ref[...]

    def step(t, carry):
        hr, hi = carry
        nhr = ar * hr - ai * hi + xr_s[pl.ds(t, 1), :]
        nhi = ar * hi + ai * hr + xi_s[pl.ds(t, 1), :]
        hr_s[pl.ds(t, 1), :] = nhr
        hi_s[pl.ds(t, 1), :] = nhi
        return nhr, nhi

    hr, hi = lax.fori_loop(0, tc, step, (cr_s[...], ci_s[...]), unroll=8)
    cr_s[...] = hr
    ci_s[...] = hi
    hr_out[...] = hr
    hi_out[...] = hi
    ys = []
    for part in range(S5_SPLIT):
        hcat = jnp.concatenate([hr_s[:, part * sw:(part + 1) * sw], hi_s[:, part * sw:(part + 1) * sw]], axis=1)
        ys.append(_dot(hcat.astype(BF16), cc_ref[part]))
    y = jnp.concatenate(ys, axis=1) + d_ref[...] * u
    y = jax.nn.gelu(y)
    g = _dot(y.astype(BF16), wglu_ref[...])
    ya_ref[...] = _gated(g[:, :BRANCH_W] * jax.nn.sigmoid(g[:, BRANCH_W:]), gate_ref[...])


def _s5_call(h, n, t, h0r, h0i, sp):
    tc = min(t, 256)
    nc = t // tc
    uw = BRANCH_W // S5_SPLIT
    sw = S5_LANES // S5_SPLIT
    const = lambda shape: pl.BlockSpec(shape, lambda i, c: (0,) * len(shape))
    state = pl.BlockSpec((None, 1, S5_LANES), lambda i, c: (i, 0, 0))
    return pl.pallas_call(
        functools.partial(_s5_kernel, tc=tc),
        grid=(n, nc),
        in_specs=[pl.BlockSpec((tc, BRANCH_W), lambda i, c: (i * nc + c, PROJ_OFF['a_u'] // BRANCH_W)),
                  pl.BlockSpec((tc, BRANCH_W), lambda i, c: (i * nc + c, PROJ_OFF['a_gate'] // BRANCH_W)),
                  state, state,
                  const((1, S5_LANES)), const((1, S5_LANES)),
                  const((S5_SPLIT, uw, 2 * sw)), const((S5_SPLIT, 2 * sw, uw)),
                  const((1, BRANCH_W)), const((BRANCH_W, 2 * BRANCH_W))],
        out_specs=[pl.BlockSpec((tc, BRANCH_W), lambda i, c: (i * nc + c, 0)), state, state],
        out_shape=[jax.ShapeDtypeStruct((n * t, BRANCH_W), BF16),
                   jax.ShapeDtypeStruct((n, 1, S5_LANES), F32),
                   jax.ShapeDtypeStruct((n, 1, S5_LANES), F32)],
        scratch_shapes=[pltpu.VMEM((tc, S5_LANES), F32), pltpu.VMEM((tc, S5_LANES), F32),
                        pltpu.VMEM((tc, S5_LANES), F32), pltpu.VMEM((tc, S5_LANES), F32),
                        pltpu.VMEM((1, S5_LANES), F32), pltpu.VMEM((1, S5_LANES), F32)],
        compiler_params=_params("parallel", "arbitrary"),
        name="s5_branch",
    )(h, h, h0r, h0i, sp['ar'], sp['ai'], sp['bb'], sp['cc'], sp['d'], sp['wglu'])


def _s5_prepare(a_re, a_im, log_dt, b_re, b_im, c_re, c_im, d, w_glu):
    lam_r = jnp.minimum(a_re, -1e-4)
    lam_i = a_im
    dt = jnp.exp(log_dt)[:, None]
    mag = jnp.exp(lam_r * dt)
    ar = mag * jnp.cos(lam_i * dt)
    ai = mag * jnp.sin(lam_i * dt)
    den = lam_r * lam_r + lam_i * lam_i
    nr = ar - 1.0
    cr = (nr * lam_r + ai * lam_i) / den
    ci = (ai * lam_r - nr * lam_i) / den
    bb_r = cr[..., None] * b_re - ci[..., None] * b_im
    bb_i = cr[..., None] * b_im + ci[..., None] * b_re
    eye = jnp.eye(S5_GROUPS, dtype=F32)

    def blockdiag_in(bb):
        m = eye[:, None, :, None] * jnp.swapaxes(bb, 1, 2)[:, :, None, :]
        return m.reshape(BRANCH_W, S5_LANES)

    def blockdiag_out(c):
        m = eye[:, None, :, None] * jnp.swapaxes(c, 1, 2)[:, :, None, :]
        return m.reshape(S5_LANES, BRANCH_W)

    uw = BRANCH_W // S5_SPLIT
    sw = S5_LANES // S5_SPLIT
    bbr, bbi = blockdiag_in(bb_r), blockdiag_in(bb_i)
    ccr, cci = blockdiag_out(c_re), blockdiag_out(c_im)
    bb = jnp.stack([jnp.concatenate([m[p * uw:(p + 1) * uw, p * sw:(p + 1) * sw] for m in (bbr, bbi)], axis=1)
                    for p in range(S5_SPLIT)]).astype(BF16)
    cc = jnp.stack([jnp.concatenate([ccr[p * sw:(p + 1) * sw, p * uw:(p + 1) * uw],
                                     -cci[p * sw:(p + 1) * sw, p * uw:(p + 1) * uw]], axis=0)
                    for p in range(S5_SPLIT)]).astype(BF16)
    return {'ar': ar.reshape(1, S5_LANES), 'ai': ai.reshape(1, S5_LANES), 'bb': bb, 'cc': cc,
            'd': d.reshape(1, BRANCH_W), 'wglu': w_glu.astype(BF16)}


def _sortable(x):
    bits = lax.bitcast_convert_type(x, jnp.int32)
    key = bits ^ ((bits >> 31) & jnp.int32(0x7FFFFFFF))
    return jnp.where(key == -1, 0, key)


def _dsa_kernel(q_ref, qi_ref, misc_ref, gate_ref, hk_ref, bias_ref, tril_ref, o_ref, sc_s, pk_s, acc_s, far_s, *,
                qb, b_off, n_keys, topk):
    nsub = max(qb // Q_BLOCK, 1)
    qw = qb // nsub
    b = pl.program_id(1) * nsub + b_off
    nkt = (b + nsub - 1 + KEY_SUB) // KEY_SUB
    fold = KEY_TILE // FOLD_ROWS
    key_row = _iota((KEY_TILE, qb), 0)
    qpos = b * Q_BLOCK + _iota((1, qb), 1)
    limit = jnp.minimum(((qpos >> 6) + 1) * CHUNK, n_keys)
    low_half = _iota((qb, LANE), 1) < DSA_HEAD_DIM

    def key_rows(ref_cols, j):
        return hk_ref[pl.ds(pl.multiple_of(j * KEY_TILE, KEY_TILE), KEY_TILE), ref_cols:ref_cols + LANE]

    def half_masked(x, hd):
        pair = x[:, (hd // 2) * LANE:(hd // 2 + 1) * LANE]
        return jnp.where(low_half == (hd % 2 == 0), pair, 0.0).astype(BF16)

    misc = misc_ref[...]
    pick = jnp.where(_iota((SUBLANE, LANE), 1) == MISC_WI + _iota((SUBLANE, LANE), 0), 1.0, 0.0)
    wi_t = _dot_nt(pick, misc, precision=HIGHEST) * (IDX_DIM ** -0.5 * IDX_HEADS ** -0.5)
    qi = qi_ref[...]
    qi_stack = jnp.concatenate([half_masked(qi, hd) for hd in range(IDX_HEADS)], axis=0)

    def pack16(v):
        half = KEY_TILE // 2
        return (v[0:half] & 0xFFFF) | (v[half:] << 16)

    def score_block(j, carry):
        z = _dot_nt(key_rows(HK_KI, j), qi_stack)
        sc = jnp.zeros((KEY_TILE, qb), F32)
        for hd in range(IDX_HEADS):
            sc = sc + jnp.maximum(z[:, hd * qb:(hd + 1) * qb], 0.0) * wi_t[hd:hd + 1, :]
        sc = jnp.where(j * KEY_TILE + key_row < limit, sc, -jnp.inf)
        key = _sortable(sc)
        sc_s[j] = key
        pk_s[j] = pack16(key >> 16)
        return carry

    lax.fori_loop(0, nkt, score_block, 0)

    def count_ge(cand):
        def body(j, acc):
            return acc + jnp.where(sc_s[j] >= cand, 1.0, 0.0).reshape(fold, FOLD_ROWS, qb).sum(axis=0)
        acc = lax.fori_loop(0, nkt, body, jnp.zeros((FOLD_ROWS, qb), F32))
        return jnp.sum(acc, axis=0, keepdims=True)

    one16 = jnp.ones((KEY_TILE, qb), jnp.int16)
    zero16 = jnp.zeros((KEY_TILE, qb), jnp.int16)

    def count16_ge(cand):
        word = jnp.broadcast_to((cand & 0xFFFF) | (cand << 16), (KEY_TILE // 2, qb))
        c16 = pltpu.bitcast(word, jnp.int16)

        def body(j, acc):
            ind = jnp.where(pltpu.bitcast(pk_s[j], jnp.int16) >= c16, one16, zero16)
            for i in range(fold):
                acc = acc + ind[i * FOLD_ROWS:(i + 1) * FOLD_ROWS]
            return acc
        acc = lax.fori_loop(0, nkt, body, jnp.zeros((FOLD_ROWS, qb), jnp.int16))
        return jnp.sum(acc.astype(F32), axis=0, keepdims=True)

    def radix_search(start, want):
        def bit_step(i, res):
            cand = res + (jnp.int32(1) << (14 - i))
            return jnp.where(count16_ge(cand) >= want, cand, res)
        return lax.fori_loop(0, 15, bit_step, start)

    kf = float(topk)
    top = radix_search(jnp.where(count16_ge(jnp.zeros((1, qb), jnp.int32)) >= kf, 0, -32768), kf)
    int16_max = jnp.iinfo(jnp.int16).max
    above = jnp.where(top == int16_max, 0.0, count16_ge(jnp.minimum(top + 1, int16_max)))
    want_low = kf - above

    def pack_low(j, carry):
        key = sc_s[j]
        pk_s[j] = pack16(jnp.where((key >> 16) == top, (key >> 1) & 0x7FFF, -1))
        return carry

    lax.fori_loop(0, nkt, pack_low, 0)
    low = radix_search(jnp.zeros((1, qb), jnp.int32), want_low)
    base = (top << 16) | (low << 1)
    thr = jnp.where(count_ge(base | 1) >= kf, base | 1, base)
    need = kf - count_ge(thr + 1)

    tril = tril_ref[...]

    def select_block(j, ties):
        keys = sc_s[j]
        eq = keys == thr
        eq_b = jnp.where(eq, 1.0, 0.0).astype(BF16)
        parts, carry = [], ties
        for s in range(KEY_SUB):
            part = _dot(tril, eq_b[s * KEY_BLOCK:(s + 1) * KEY_BLOCK]) + carry
            parts.append(part)
            carry = part[KEY_BLOCK - 1:KEY_BLOCK, :]
        prefix = jnp.concatenate(parts, axis=0)
        sel = ((keys > thr) | (eq & (prefix <= need))) & (j * KEY_TILE + key_row < limit)
        sc_s[j] = lax.bitcast_convert_type(jnp.where(sel, 0.0, -jnp.inf), jnp.int32)
        return prefix[KEY_TILE - 1:KEY_TILE, :]

    lax.fori_loop(0, nkt, select_block, jnp.zeros((1, qb), F32))

    q = q_ref[...] * (DSA_HEAD_DIM ** -0.5 * LOG2E)
    rep = DSA_HEADS // DSA_KV_HEADS
    eye = jnp.where(_iota((LANE, LANE), 0) == _iota((LANE, LANE), 1), 1.0, 0.0).astype(BF16)
    value_lane = _iota((KEY_TILE, LANE), 1) < DSA_HEAD_DIM
    n_pairs = DSA_HEADS // 2
    q_pair = [jnp.concatenate([half_masked(q, 2 * c), half_masked(q, 2 * c + 1)], axis=0) for c in range(n_pairs)]
    far_bias = [jnp.concatenate([bias_ref[hd, 2, 0:1, 0:qw]] * nsub, axis=1)
                for hd in range(DSA_HEADS)]
    n_far = jnp.maximum((b - 1) // KEY_SUB, 0)

    def tile_operands(j):
        mask = lax.bitcast_convert_type(sc_s[j], F32)
        kd = [key_rows(grp * LANE, j) for grp in range(DSA_KV_HEADS)]
        zs = [_dot_nt(kd[(2 * c) // rep], q_pair[c]) for c in range(n_pairs)]
        vd = [jnp.where(value_lane, key_rows(HK_V + grp * LANE, j), 1.0).astype(BF16)
              for grp in range(DSA_KV_HEADS)]
        return mask, zs, vd

    def near_bias(hd, j):
        cols = [jnp.concatenate([bias_ref[hd, jnp.clip(b + u - (KEY_SUB * j + s), 0, 2), :, 0:qw]
                                 for s in range(KEY_SUB)], axis=0) for u in range(nsub)]
        return jnp.concatenate(cols, axis=1)

    def plain_tile(j, carry, far):
        mask, zs, vd = tile_operands(j)
        ps = []
        for hd in range(DSA_HEADS):
            r = hd % 2
            zh = zs[hd // 2][:, r * qb:(r + 1) * qb] + mask
            if not far:
                zh = zh + near_bias(hd, j)
            ps.append(jnp.exp2(zh).astype(BF16))
        acc = far_s if far else acc_s
        for c in range(n_pairs):
            acc[c] += _dot_tn(vd[(2 * c) // rep], jnp.concatenate(ps[2 * c:2 * c + 2], axis=1))
        return carry

    acc_s[...] = jnp.zeros_like(acc_s)
    far_s[...] = jnp.zeros_like(far_s)
    lax.fori_loop(0, n_far, lambda j, c: plain_tile(j, c, True), 0)
    lax.fori_loop(n_far, nkt, lambda j, c: plain_tile(j, c, False), 0)
    healthy = jnp.ones((1, 1), F32)
    for c in range(n_pairs):
        scale = jnp.concatenate([jnp.exp2(far_bias[2 * c]), jnp.exp2(far_bias[2 * c + 1])], axis=1)
        total = far_s[c] * scale + acc_s[c]
        acc_s[c] = total
        norm = total[DSA_HEAD_DIM:DSA_HEAD_DIM + 1, :]
        ok = (jnp.abs(total) < SOFTMAX_CEIL) & (norm > SOFTMAX_FLOOR)
        healthy = jnp.minimum(healthy, jnp.min(jnp.where(ok, 1.0, 0.0), keepdims=True))

    def attend(j, ms, far):
        mask, zs, vd = tile_operands(j)
        new_ms, ps, alphas = [], [], []
        for hd in range(DSA_HEADS):
            r = hd % 2
            zh = zs[hd // 2][:, r * qb:(r + 1) * qb] + mask
            if not far:
                zh = zh + near_bias(hd, j)
            tile_max = jnp.max(zh.reshape(fold, FOLD_ROWS, qb).max(axis=0), axis=0, keepdims=True)
            shift = far_bias[hd] if far else 0.0
            m_new = jnp.maximum(ms[hd], tile_max + shift)
            ps.append(jnp.exp2(zh - (m_new - shift)).astype(BF16))
            alphas.append(jnp.exp2(ms[hd] - m_new))
            new_ms.append(m_new)
        for c in range(n_pairs):
            acc_s[c] = (acc_s[c] * jnp.concatenate(alphas[2 * c:2 * c + 2], axis=1)
                        + _dot_tn(vd[(2 * c) // rep], jnp.concatenate(ps[2 * c:2 * c + 2], axis=1)))
        return tuple(new_ms)

    @pl.when(healthy[0, 0] < 0.5)
    def _():
        acc_s[...] = jnp.zeros_like(acc_s)
        ms = tuple(jnp.full((1, qb), -1e30, F32) for _ in range(DSA_HEADS))
        ms = lax.fori_loop(0, n_far, lambda j, m: attend(j, m, True), ms)
        lax.fori_loop(n_far, nkt, lambda j, m: attend(j, m, False), ms)

    for c in range(n_pairs):
        halves = []
        for r in range(2):
            blk = acc_s[c, :, r * qb:(r + 1) * qb]
            halves.append(blk[0:DSA_HEAD_DIM] / blk[DSA_HEAD_DIM:DSA_HEAD_DIM + 1])
        o_t = jnp.concatenate(halves, axis=0)
        o_hi, o_lo = _split_bf16(o_t)
        o_ref[:, c * LANE:(c + 1) * LANE] = _gated(_dot_tn(o_hi, eye) + _dot_tn(o_lo, eye),
                                                   gate_ref[:, c * LANE:(c + 1) * LANE])


def _dsa_call(h, hk, n, t, bias, tril, *, b_off, n_keys, topk):
    qb = min(t, Q_TILE)
    nqb = t // qb
    lp = hk.shape[1]
    wq = DSA_HEADS * DSA_HEAD_DIM
    wqi = IDX_HEADS * IDX_DIM
    return pl.pallas_call(
        functools.partial(_dsa_kernel, qb=qb, b_off=b_off, n_keys=n_keys, topk=topk),
        grid=(n, nqb),
        in_specs=[pl.BlockSpec((qb, wq), lambda i, c: (i * nqb + c, PROJ_OFF['b_q'] // wq)),
                  pl.BlockSpec((qb, wqi), lambda i, c: (i * nqb + c, PROJ_OFF['b_qi'] // wqi)),
                  pl.BlockSpec((qb, LANE), lambda i, c: (i * nqb + c, OFF_MISC // LANE)),
                  pl.BlockSpec((qb, BRANCH_W), lambda i, c: (i * nqb + c, PROJ_OFF['b_gate'] // BRANCH_W)),
                  pl.BlockSpec((None, lp, HK_W), lambda i, c: (i, 0, 0)),
                  pl.BlockSpec((DSA_HEADS, 3, KEY_BLOCK, Q_BLOCK), lambda i, c: (0, 0, 0, 0)),
                  pl.BlockSpec((KEY_BLOCK, KEY_BLOCK), lambda i, c: (0, 0))],
        out_specs=pl.BlockSpec((qb, wq), lambda i, c: (i * nqb + c, 0)),
        out_shape=jax.ShapeDtypeStruct((n * t, wq), BF16),
        scratch_shapes=[pltpu.VMEM((lp // KEY_TILE, KEY_TILE, qb), jnp.int32),
                        pltpu.VMEM((lp // KEY_TILE, KEY_TILE // 2, qb), jnp.int32),
                        pltpu.VMEM((DSA_HEADS // 2, LANE, 2 * qb), F32),
                        pltpu.VMEM((DSA_HEADS // 2, LANE, 2 * qb), F32)],
        compiler_params=_params("parallel", "arbitrary"),
        name="dsa_branch",
    )(h, h, h, h, hk, bias, tril)


def _t5_bucket(rel):
    nb = REL_BUCKETS // 2
    max_exact = nb // 2
    ret = jnp.where(rel > 0, nb, 0)
    dist = jnp.abs(rel)
    distf = jnp.maximum(dist, 1).astype(F32)
    large = max_exact + (jnp.log(distf / max_exact) / math.log(REL_MAX_DIST / max_exact)
                         * (nb - max_exact)).astype(jnp.int32)
    large = jnp.minimum(large, nb - 1)
    return ret + jnp.where(dist < max_exact, dist, large)


def _dsa_bias_tables(rel_bias):
    lc = jnp.arange(KEY_BLOCK, dtype=jnp.int32)[:, None]
    qr = jnp.arange(Q_BLOCK, dtype=jnp.int32)[None, :]
    rel = jnp.stack([lc - qr - back * KEY_BLOCK for back in range(3)], axis=0)
    onehot = jax.nn.one_hot(_t5_bucket(rel), REL_BUCKETS, dtype=F32)
    bias = jnp.einsum('klqb,bh->hklq', onehot, rel_bias.astype(F32), precision=HIGHEST)
    return bias * LOG2E


def _gdn_kernel(raw_ref, misc_ref, gate_ref, s0_ref, conv0_ref, convw_ref, alog_ref, dtb_ref, norm_ref,
                o_ref, s_out, conv_out, xp_s, st_s, *, n, c, nsub):
    rows = nsub * c

    @pl.when(pl.program_id(0) == 0)
    def _():
        xp_s[:, 0:SUBLANE, :] = conv0_ref[...]
        st_s[...] = s0_ref[...]

    w = convw_ref[...]
    row = _iota((c, c), 0)
    col = _iota((c, c), 1)
    incl = row >= col
    strict = row > col
    eye = row == col
    tril = jnp.where(incl, 1.0, 0.0)
    eye_f = jnp.where(eye, 1.0, 0.0)
    chains = [(b, sub, hd) for b in range(n) for sub in range(nsub) for hd in range(GDN_HEADS)]

    qkv, beta_all, gc_all = [], [], []
    for b in range(n):
        raw = raw_ref[b]
        xp_s[b, SUBLANE:SUBLANE + rows, :] = raw
        y = raw * w[CONV_W - 1:CONV_W, :]
        for i in range(CONV_W - 1):
            y = y + xp_s[b, CONV_LEAD + i:CONV_LEAD + i + rows, :] * w[i:i + 1, :]
        tail = xp_s[b, rows:rows + SUBLANE, :]
        xp_s[b, 0:SUBLANE, :] = tail
        conv_out[b] = tail
        qkv.append(jax.nn.silu(y))
        misc = misc_ref[b]
        beta_all.append(jax.nn.sigmoid(misc))
        g_all = -jnp.exp(alog_ref[...]) * _softplus(misc + dtb_ref[...])
        gc_all.append([_dot(tril, g_all[sub * c:(sub + 1) * c], precision=HIGHEST)
                       for sub in range(nsub)])

    qs, ks, vs, betas, gcs, dec_s, dec_i = [], [], [], [], [], [], []
    for b, sub, hd in chains:
        x = qkv[b][sub * c:(sub + 1) * c]
        qh = x[:, hd * GDN_DK:(hd + 1) * GDN_DK]
        kh = x[:, GDN_QK + hd * GDN_DK:GDN_QK + (hd + 1) * GDN_DK]
        vs.append(x[:, 2 * GDN_QK + hd * GDN_DV:2 * GDN_QK + (hd + 1) * GDN_DV])
        qs.append(qh * lax.rsqrt(jnp.sum(qh * qh, axis=1, keepdims=True) + RMS_EPS) * (GDN_DK ** -0.5))
        ks.append(kh * lax.rsqrt(jnp.sum(kh * kh, axis=1, keepdims=True) + RMS_EPS))
        betas.append(beta_all[b][sub * c:(sub + 1) * c, MISC_BETA + hd:MISC_BETA + hd + 1])
        gc = gc_all[b][sub][:, MISC_A + hd:MISC_A + hd + 1]
        gcs.append(gc)
        gc_b = jnp.broadcast_to(gc, (c, c))
        gc_row = jnp.sum(jnp.where(eye, gc_b, 0.0), axis=0, keepdims=True)
        diff = gc_b - gc_row
        dec = jnp.exp(diff)
        dec_s.append(jnp.where(strict, dec, 0.0))
        dec_i.append(jnp.where(incl, dec, 0.0))
    kb = [k.astype(BF16) for k in ks]
    qb = [q.astype(BF16) for q in qs]
    pw = [-(beta * _dot_nt(k, k) * d) for beta, k, d in zip(betas, kb, dec_s)]
    inv = [eye_f + p for p in pw]
    for _ in range(int(math.log2(c)) - 1):
        pw = [_dot3(p, p) for p in pw]
        inv = [i + _dot3(i, p) for i, p in zip(inv, pw)]
    egc = [jnp.exp(gc) for gc in gcs]
    sol = [_dot3(i, jnp.concatenate([(beta * e) * k, beta * v], axis=1))
           for i, beta, e, k, v in zip(inv, betas, egc, ks, vs)]
    attn = [(_dot_nt(q, k) * d).astype(BF16) for q, k, d in zip(qb, kb, dec_i)]
    for sub in range(nsub):
        ids = [i for i, ch in enumerate(chains) if ch[1] == sub]
        s_old = [st_s[chains[i][0], chains[i][2]] for i in ids]
        sb = [s.astype(BF16) for s in s_old]
        vb = [(sol[i][:, GDN_DK:] - _dot(sol[i][:, :GDN_DK].astype(BF16), s)).astype(BF16) for i, s in zip(ids, sb)]
        outs = [egc[i] * _dot(qb[i], s) + _dot(attn[i], v) for i, s, v in zip(ids, sb, vb)]
        for i, v, s, o in zip(ids, vb, s_old, outs):
            b, _, hd = chains[i]
            g_last = gcs[i][c - 1:c, :]
            st_s[b, hd] = jnp.exp(g_last) * s + _dot_tn((ks[i] * jnp.exp(g_last - gcs[i])).astype(BF16), v)
            o = o * lax.rsqrt(jnp.mean(o * o, axis=1, keepdims=True) + RMS_EPS) * norm_ref[...]
            o_ref[b, sub * c:(sub + 1) * c, hd * GDN_DV:(hd + 1) * GDN_DV] = _gated(
                o, gate_ref[b, sub * c:(sub + 1) * c, hd * GDN_DV:(hd + 1) * GDN_DV])
    s_out[...] = st_s[...]


def _gdn_call(h3, s0, conv0, gp):
    n, t, _ = h3.shape
    c = min(t, CHUNK)
    nsub = GDN_CHUNKS_PER_STEP if t % (GDN_CHUNKS_PER_STEP * c) == 0 else 1
    rows = nsub * c
    const = lambda shape: pl.BlockSpec(shape, lambda j: (0,) * len(shape))
    state = const((n, GDN_HEADS, GDN_DK, GDN_DV))
    conv = const((n, SUBLANE, GDN_CONV_CH))
    return pl.pallas_call(
        functools.partial(_gdn_kernel, n=n, c=c, nsub=nsub),
        grid=(t // rows,),
        in_specs=[pl.BlockSpec((n, rows, GDN_CONV_CH), lambda j: (0, j, PROJ_OFF['c_qkv'] // GDN_CONV_CH)),
                  pl.BlockSpec((n, rows, LANE), lambda j: (0, j, OFF_MISC // LANE)),
                  pl.BlockSpec((n, rows, BRANCH_W), lambda j: (0, j, PROJ_OFF['c_gate'] // BRANCH_W)),
                  state, conv,
                  const((CONV_W, GDN_CONV_CH)), const((1, LANE)), const((1, LANE)), const((1, GDN_DV))],
        out_specs=[pl.BlockSpec((n, rows, GDN_VW), lambda j: (0, j, 0)), state, conv],
        out_shape=[jax.ShapeDtypeStruct((n, t, GDN_VW), BF16),
                   jax.ShapeDtypeStruct((n, GDN_HEADS, GDN_DK, GDN_DV), F32),
                   jax.ShapeDtypeStruct((n, SUBLANE, GDN_CONV_CH), F32)],
        scratch_shapes=[pltpu.VMEM((n, rows + SUBLANE, GDN_CONV_CH), F32),
                        pltpu.VMEM((n, GDN_HEADS, GDN_DK, GDN_DV), F32)],
        compiler_params=_params("arbitrary"),
        name="gdn_branch",
    )(h3, h3, h3, s0, conv0, gp['convw'], gp['alog'], gp['dtb'], gp['norm'])


def _lane_place(v, off):
    return jnp.zeros((1, LANE), F32).at[0, off:off + v.shape[0]].set(v.astype(F32))


def _gla_kernel(qk_ref, v_ref, misc_ref, gate_ref, s0_ref, wg_ref, bg_ref, norm_ref, hexp_ref, gsum_ref,
                o_ref, s_out, st_s, *, n, tc):
    blk = GLA_BLOCK
    wk = GLA_HEADS * GLA_DK

    @pl.when(pl.program_id(0) == 0)
    def _():
        st_s[...] = s0_ref[...]

    row = _iota((blk, blk), 0)
    col = _iota((blk, blk), 1)
    tril = jnp.where(row >= col, 1.0, 0.0)
    jrow = _iota((blk, wk), 0)
    lane_k = _iota((blk, wk), 1)
    head_mask = [(lane_k >= hd * GLA_DK) & (lane_k < (hd + 1) * GLA_DK) for hd in range(GLA_HEADS)]
    chains = range(n)

    def block(sb, carry):
        r0 = pl.multiple_of(sb * blk, blk)
        qk = [qk_ref[b, pl.ds(r0, blk), :] for b in chains]
        q = [x[:, :wk] * (GLA_DK ** -0.5) for x in qk]
        k = [x[:, wk:] for x in qk]
        v = [v_ref[b, pl.ds(r0, blk), :] for b in chains]
        mb = [misc_ref[b, pl.ds(r0, blk), :].astype(BF16) for b in chains]
        lg = [jax.nn.log_sigmoid(_dot(m, wg_ref[...]) + bg_ref[...]) * (1.0 / GLA_TAU) for m in mb]
        bc = [_dot(tril, x, precision=HIGHEST) for x in lg]
        dmat = []
        for b in chains:
            slabs = []
            for i in range(blk):
                keep = jrow <= i
                dec = jnp.where(keep, jnp.exp(bc[b][i:i + 1, :] - bc[b]), 0.0)
                slabs.append(dec * q[b][i:i + 1, :] * k[b])
            dmat.append(jnp.concatenate(slabs, axis=0).astype(BF16))
        a_exp = [_dot(d, hexp_ref[...]) for d in dmat]
        prod = [a * jnp.concatenate([x] * blk, axis=0) for a, x in zip(a_exp, v)]
        o = [_dot(gsum_ref[...], p.astype(BF16)) for p in prod]
        qe = [x * jnp.exp(c) for x, c in zip(q, bc)]
        kt = [x * jnp.exp(c[blk - 1:blk, :] - c) for x, c in zip(k, bc)]
        s = [st_s[b] for b in chains]
        sbf = [x.astype(BF16) for x in s]
        for b in chains:
            upd = jnp.zeros_like(s[b])
            parts = []
            for hd in range(GLA_HEADS):
                parts.append(_dot_nt(jnp.where(head_mask[hd], qe[b], 0.0).astype(BF16), sbf[b]))
                vh = v[b][:, hd * GLA_DV:(hd + 1) * GLA_DV].astype(BF16)
                upd = upd + _dot_tn(vh, jnp.where(head_mask[hd], kt[b], 0.0).astype(BF16))
            ob = o[b] + jnp.concatenate(parts, axis=1)
            st_s[b] = jnp.exp(bc[b][blk - 1:blk, :]) * s[b] + upd
            for hd in range(GLA_HEADS):
                oh = ob[:, hd * GLA_DV:(hd + 1) * GLA_DV]
                oh = oh * lax.rsqrt(jnp.mean(oh * oh, axis=1, keepdims=True) + RMS_EPS) * norm_ref[...]
                o_ref[b, pl.ds(r0, blk), hd * GLA_DV:(hd + 1) * GLA_DV] = _gated(
                    oh, gate_ref[b, pl.ds(r0, blk), hd * GLA_DV:(hd + 1) * GLA_DV])
        return carry

    lax.fori_loop(0, tc // blk, block, 0, unroll=8 if tc // blk > 1 else 1)
    s_out[...] = st_s[...]


def _gla_call(h3, s0, gp):
    n, t, _ = h3.shape
    tc = min(t, 128)
    wk = GLA_HEADS * GLA_DK
    wv = GLA_HEADS * GLA_DV
    const = lambda shape: pl.BlockSpec(shape, lambda j: (0,) * len(shape))
    state = const((n, GLA_DV, wk))
    return pl.pallas_call(
        functools.partial(_gla_kernel, n=n, tc=tc),
        grid=(t // tc,),
        in_specs=[pl.BlockSpec((n, tc, 2 * wk), lambda j: (0, j, PROJ_OFF['d_q'] // (2 * wk))),
                  pl.BlockSpec((n, tc, wv), lambda j: (0, j, PROJ_OFF['d_v'] // wv)),
                  pl.BlockSpec((n, tc, LANE), lambda j: (0, j, OFF_MISC // LANE)),
                  pl.BlockSpec((n, tc, BRANCH_W), lambda j: (0, j, PROJ_OFF['d_gate'] // BRANCH_W)),
                  state,
                  const((LANE, wk)), const((1, wk)), const((1, GLA_DV)),
                  const((wk, wv)), const((GLA_BLOCK, GLA_BLOCK * GLA_BLOCK))],
        out_specs=[pl.BlockSpec((n, tc, wv), lambda j: (0, j, 0)), state],
        out_shape=[jax.ShapeDtypeStruct((n, t, wv), BF16),
                   jax.ShapeDtypeStruct((n, GLA_DV, wk), F32)],
        scratch_shapes=[pltpu.VMEM((n, GLA_DV, wk), F32)],
        compiler_params=_params("arbitrary"),
        name="gla_branch",
    )(h3, h3, h3, h3, s0, gp['wg'], gp['bg'], gp['norm'], gp['hexp'], gp['gsum'])


def _gla_prepare(w_g2, b_g, norm):
    wk = GLA_HEADS * GLA_DK
    wg = jnp.zeros((LANE, wk), F32).at[MISC_G:MISC_G + GLA_GATE_RANK].set(w_g2).astype(BF16)
    ck = jnp.arange(wk)[:, None] // GLA_DK
    cv = jnp.arange(GLA_HEADS * GLA_DV)[None, :] // GLA_DV
    ij = jnp.arange(GLA_BLOCK * GLA_BLOCK)[None, :] // GLA_BLOCK
    return {'wg': wg, 'bg': b_g.reshape(1, wk).astype(F32),
            'norm': norm.reshape(1, GLA_DV).astype(F32),
            'hexp': (ck == cv).astype(BF16),
            'gsum': (jnp.arange(GLA_BLOCK)[:, None] == ij).astype(BF16)}


def _merge_kernel(x_ref, ya_ref, yb_ref, yc_ref, yd_ref, mg_ref, wb_ref, wo_ref, lng_ref, lnb_ref, o_ref):
    mixed = None
    for br, act_ref in enumerate((ya_ref, yb_ref, yc_ref, yd_ref)):
        term = jax.nn.sigmoid(mg_ref[:, br * D_MODEL:(br + 1) * D_MODEL]) * _dot(act_ref[...], wb_ref[br])
        mixed = term if mixed is None else mixed + term
    z = DN_ALPHA * x_ref[...] + _dot(mixed.astype(BF16), wo_ref[...])
    mu = jnp.mean(z, axis=1, keepdims=True)
    zc = z - mu
    var = jnp.mean(zc * zc, axis=1, keepdims=True)
    o_ref[...] = zc * lax.rsqrt(var + LN_EPS) * lng_ref[...] + lnb_ref[...]


def _merge_call(x, h, ya, yb, yc, yd, wb, wo, lng, lnb):
    rows = x.shape[0]
    tm = min(rows, 256)
    bw = BRANCH_W
    ybs = pl.BlockSpec((tm, bw), lambda i: (i, 0))
    return pl.pallas_call(
        _merge_kernel,
        grid=(rows // tm,),
        in_specs=[pl.BlockSpec((tm, D_MODEL), lambda i: (i, 0)), ybs, ybs, ybs, ybs,
                  pl.BlockSpec((tm, N_BRANCH * D_MODEL), lambda i: (i, 0)),
                  pl.BlockSpec((N_BRANCH, bw, D_MODEL), lambda i: (0, 0, 0)),
                  pl.BlockSpec((D_MODEL, D_MODEL), lambda i: (0, 0)),
                  pl.BlockSpec((1, D_MODEL), lambda i: (0, 0)),
                  pl.BlockSpec((1, D_MODEL), lambda i: (0, 0))],
        out_specs=pl.BlockSpec((tm, D_MODEL), lambda i: (i, 0)),
        out_shape=jax.ShapeDtypeStruct((rows, D_MODEL), F32),
        compiler_params=_params("parallel"),
        name="merge_norm",
    )(x, ya, yb, yc, yd, h, wb, wo, lng, lnb)


def _split_w_in(w_in):
    out, off = {}, 0
    for name, width in IN_LAYOUT:
        out[name] = w_in[:, off:off + width]
        off += width
    return out


def _regroup_kernel(w_ref, proj_ref, key_ref):
    cols = _split_w_in(w_ref[...])
    pad = jnp.zeros((w_ref.shape[0], MISC_PAD), F32)
    proj_ref[...] = jnp.concatenate([cols[name] for name in PROJ_ORDER] + [pad], axis=1).astype(BF16)
    k, v, ki = cols['b_k'], cols['b_v'], cols['b_ki']
    dh = DSA_HEAD_DIM
    key_ref[...] = jnp.concatenate([k[:, :dh], k[:, :dh], k[:, dh:], k[:, dh:],
                                    v[:, :dh], v[:, :dh], v[:, dh:], v[:, dh:], ki, ki], axis=1).astype(BF16)


def _regroup_weights(w_in):
    depth, rows, width = w_in.shape
    tr = 128
    return pl.pallas_call(
        _regroup_kernel,
        grid=(depth, rows // tr),
        in_specs=[pl.BlockSpec((None, tr, width), lambda l, i: (l, i, 0))],
        out_specs=[pl.BlockSpec((None, tr, PROJ_W), lambda l, i: (l, i, 0)),
                   pl.BlockSpec((None, tr, HK_W), lambda l, i: (l, i, 0))],
        out_shape=[jax.ShapeDtypeStruct((depth, rows, PROJ_W), BF16),
                   jax.ShapeDtypeStruct((depth, rows, HK_W), BF16)],
        compiler_params=_params("parallel", "parallel"),
        name="regroup_weights",
    )(w_in)


def _dup_keys(k, v, ki):
    return jnp.concatenate([k[:, :, 0], k[:, :, 0], k[:, :, 1], k[:, :, 1],
                            v[:, :, 0], v[:, :, 0], v[:, :, 1], v[:, :, 1], ki, ki], axis=-1).astype(BF16)


def _layer(x, lw, st):
    n, t, _ = x.shape
    x2 = x.reshape(n * t, D_MODEL)
    h = _matmul(x2, lw['w_proj'], F32, tn=PROJ_W // 3, tm=512)
    hk_new = _matmul(x2, lw['w_key'], BF16, tn=HK_W).reshape(n, t, HK_W)
    if st is None:
        hk, n_keys, b_off = hk_new, t, 0
        s5r = jnp.zeros((n, 1, S5_LANES), F32)
        s5i = jnp.zeros((n, 1, S5_LANES), F32)
        gdn0 = jnp.zeros((n, GDN_HEADS, GDN_DK, GDN_DV), F32)
        conv0 = jnp.zeros((n, SUBLANE, GDN_CONV_CH), F32)
        gla0 = jnp.zeros((n, GLA_DV, GLA_HEADS * GLA_DK), F32)
    else:
        past = st['k'].shape[1]
        n_keys = past + t
        b_off = past // Q_BLOCK
        lp = -(-n_keys // KEY_TILE) * KEY_TILE
        hk = jnp.concatenate([_dup_keys(st['k'], st['v'], st['kidx']), hk_new,
                              jnp.zeros((n, lp - n_keys, HK_W), BF16)], axis=1)
        s5r = st['s5_re'].reshape(n, 1, S5_LANES)
        s5i = st['s5_im'].reshape(n, 1, S5_LANES)
        gdn0 = st['gdn']
        conv0 = jnp.concatenate([jnp.zeros((n, CONV_LEAD, GDN_CONV_CH), F32), st['gdn_conv']], axis=1)
        gla0 = jnp.moveaxis(st['gla'], 3, 1).reshape(n, GLA_DV, GLA_HEADS * GLA_DK)
    topk = min(DSA_TOPK_MAX, n_keys // 4)

    ya, s5r_new, s5i_new = _s5_call(h, n, t, s5r, s5i, lw['s5'])
    yb = _dsa_call(h, hk, n, t, lw['bias'], lw['tril'], b_off=b_off, n_keys=n_keys, topk=topk)
    h3 = h.reshape(n, t, PROJ_W)
    yc, gdn_new, conv_new = _gdn_call(h3, gdn0, conv0, lw['gdn'])
    yd, gla_new = _gla_call(h3, gla0, lw['gla'])
    yc = yc.reshape(n * t, GDN_VW)
    yd = yd.reshape(n * t, GLA_HEADS * GLA_DV)
    y = _merge_call(x2, h, ya, yb, yc, yd, lw['w_branch'], lw['w_out'], lw['ln_g'], lw['ln_b'])

    def cols(name, width):
        return h[:, PROJ_OFF[name]:PROJ_OFF[name] + width].reshape(n, t, width)

    new = {'k': cols('b_k', DSA_KV_HEADS * DSA_HEAD_DIM).reshape(n, t, DSA_KV_HEADS, DSA_HEAD_DIM),
           'v': cols('b_v', DSA_KV_HEADS * DSA_HEAD_DIM).reshape(n, t, DSA_KV_HEADS, DSA_HEAD_DIM),
           'kidx': cols('b_ki', IDX_DIM),
           's5_re': s5r_new.reshape(n, S5_GROUPS, S5_STATE), 's5_im': s5i_new.reshape(n, S5_GROUPS, S5_STATE),
           'gdn': gdn_new, 'gdn_conv': conv_new[:, CONV_LEAD:],
           'gla': jnp.moveaxis(gla_new.reshape(n, GLA_DV, GLA_HEADS, GLA_DK), 1, 3)}
    return y.reshape(n, t, D_MODEL), new


def kernel(x_prompt, x_sample, cache_k, cache_v, cache_kidx, state_s5_re, state_s5_im, state_gdn, state_gdn_conv, state_gla, w_in, w_branch, w_out, ln_g, ln_b, rel_bias, s5_a_re, s5_a_im, s5_log_dt, s5_b_re, s5_b_im, s5_c_re, s5_c_im, s5_d, s5_w_glu, gdn_conv, gdn_a_log, gdn_dt_bias, gdn_norm, gla_w_g2, gla_b_g, gla_norm):
    bias = _dsa_bias_tables(rel_bias)
    tril = (jnp.arange(KEY_BLOCK)[:, None] >= jnp.arange(KEY_BLOCK)[None, :]).astype(BF16)
    yp, ys = x_prompt, x_sample
    new_p, new_s = [], []
    w_proj, w_key = _regroup_weights(w_in)
    for l in range(DEPTH):
        lw = {'w_proj': w_proj[l], 'w_key': w_key[l], 'bias': bias, 'tril': tril,
              'w_branch': w_branch[l].astype(BF16), 'w_out': w_out[l].astype(BF16),
              'ln_g': ln_g[l].reshape(1, D_MODEL), 'ln_b': ln_b[l].reshape(1, D_MODEL),
              's5': _s5_prepare(s5_a_re[l], s5_a_im[l], s5_log_dt[l], s5_b_re[l], s5_b_im[l],
                                s5_c_re[l], s5_c_im[l], s5_d[l], s5_w_glu[l]),
              'gdn': {'convw': gdn_conv[l], 'alog': _lane_place(gdn_a_log[l], MISC_A),
                      'dtb': _lane_place(gdn_dt_bias[l], MISC_A), 'norm': gdn_norm[l].reshape(1, GDN_DV)},
              'gla': _gla_prepare(gla_w_g2[l], gla_b_g[l], gla_norm[l])}
        st = {'k': cache_k[l], 'v': cache_v[l], 'kidx': cache_kidx[l], 's5_re': state_s5_re[l],
              's5_im': state_s5_im[l], 'gdn': state_gdn[l], 'gdn_conv': state_gdn_conv[l], 'gla': state_gla[l]}
        yp, stp = _layer(yp, lw, None)
        ys, sts = _layer(ys, lw, st)
        new_p.append(stp)
        new_s.append(sts)
    names = ('k', 'v', 'kidx', 's5_re', 's5_im', 'gdn', 'gdn_conv', 'gla')
    stack = lambda states, name: jnp.stack([s[name] for s in states], axis=0)
    return ((yp, ys) + tuple(stack(new_p, nm) for nm in names) + tuple(stack(new_s, nm) for nm in names))
```

```python
import functools
import math

import jax
import jax.numpy as jnp
from jax import lax
from jax.experimental import pallas as pl
from jax.experimental.pallas import tpu as pltpu

F32 = jnp.float32
BF16 = jnp.bfloat16
HIGHEST = lax.Precision.HIGHEST

D_MODEL = 1024
DEPTH = 2
CHUNK = 64
N_BRANCH = 4
BRANCH_W = 512
S5_GROUP = 16
S5_GROUPS = BRANCH_W // S5_GROUP
S5_STATE = 64
S5_LANES = S5_GROUPS * S5_STATE
S5_SPLIT = 2
DSA_HEADS = 8
DSA_KV_HEADS = 2
DSA_HEAD_DIM = 64
IDX_HEADS = 4
IDX_DIM = 64
DSA_TOPK_MAX = 256
REL_BUCKETS = 32
REL_MAX_DIST = 128
GDN_HEADS = 4
GDN_DK = 128
GDN_DV = 128
GDN_QK = GDN_HEADS * GDN_DK
GDN_VW = GDN_HEADS * GDN_DV
CONV_W = 4
GDN_CONV_CH = 2 * GDN_QK + GDN_VW
GDN_CHUNKS_PER_STEP = 4
CONV_LEAD = 8 - (CONV_W - 1)
GLA_HEADS = 4
GLA_DK = 64
GLA_DV = 128
GLA_GATE_RANK = 16
GLA_TAU = 16.0
GLA_BLOCK = 16
LN_EPS = 1e-5
RMS_EPS = 1e-6
DN_ALPHA = (2 * DEPTH) ** 0.25
LOG2E = math.log2(math.e)

LANE = 128
SUBLANE = 8
KEY_BLOCK = 128
KEY_SUB = 4
KEY_TILE = KEY_SUB * KEY_BLOCK
SOFTMAX_CEIL = 2.0 ** 120
SOFTMAX_FLOOR = 2.0 ** -60
FOLD_ROWS = 64
Q_BLOCK = 128
Q_TILE = 256
VMEM_LIMIT = 56 * 1024 * 1024

IN_LAYOUT = (
    ('a_u', BRANCH_W), ('a_gate', BRANCH_W),
    ('b_q', DSA_HEADS * DSA_HEAD_DIM), ('b_k', DSA_KV_HEADS * DSA_HEAD_DIM),
    ('b_v', DSA_KV_HEADS * DSA_HEAD_DIM), ('b_qi', IDX_HEADS * IDX_DIM), ('b_ki', IDX_DIM),
    ('b_wi', IDX_HEADS), ('b_gate', BRANCH_W),
    ('c_qkv', GDN_CONV_CH), ('c_beta', GDN_HEADS), ('c_a', GDN_HEADS), ('c_gate', BRANCH_W),
    ('d_q', GLA_HEADS * GLA_DK), ('d_k', GLA_HEADS * GLA_DK), ('d_v', GLA_HEADS * GLA_DV),
    ('d_g', GLA_GATE_RANK), ('d_gate', BRANCH_W),
    ('merge', N_BRANCH * D_MODEL),
)

PROJ_ORDER = ('merge', 'a_u', 'a_gate', 'b_q', 'b_qi', 'b_k', 'b_v', 'c_qkv', 'b_gate', 'c_gate',
              'd_v', 'd_gate', 'd_q', 'd_k', 'b_ki', 'b_wi', 'c_beta', 'c_a', 'd_g')
MISC_WI = IDX_DIM
MISC_BETA = MISC_WI + IDX_HEADS
MISC_A = MISC_BETA + GDN_HEADS
MISC_G = MISC_A + GDN_HEADS
MISC_PAD = LANE - (MISC_G + GLA_GATE_RANK)


def _proj_offsets():
    widths = dict(IN_LAYOUT)
    off, out = 0, {}
    for name in PROJ_ORDER:
        out[name] = off
        off += widths[name]
    return out, off + MISC_PAD


PROJ_OFF, PROJ_W = _proj_offsets()
OFF_MISC = PROJ_OFF['b_ki']
HK_W = 5 * LANE
HK_V = 2 * LANE
HK_KI = 4 * LANE


def _dot(a, b, precision=None):
    return jnp.dot(a, b, preferred_element_type=F32, precision=precision)


def _dot_nt(a, b, precision=None):
    return lax.dot_general(a, b, (((1,), (1,)), ((), ())), preferred_element_type=F32, precision=precision)


def _dot_tn(a, b, precision=None):
    return lax.dot_general(a, b, (((0,), (0,)), ((), ())), preferred_element_type=F32, precision=precision)


def _split_bf16(x):
    hi = x.astype(BF16)
    return hi, (x - hi.astype(F32)).astype(BF16)


def _dot3(a, b):
    a_hi, a_lo = _split_bf16(a)
    b_hi, b_lo = _split_bf16(b)
    return _dot(a_hi, b_hi) + (_dot(a_hi, b_lo) + _dot(a_lo, b_hi))


def _params(*sem):
    return pltpu.CompilerParams(dimension_semantics=sem, vmem_limit_bytes=VMEM_LIMIT)


def _softplus(x):
    return jnp.maximum(x, 0.0) + jnp.log1p(jnp.exp(-jnp.abs(x)))


def _iota(shape, dim):
    return lax.broadcasted_iota(jnp.int32, shape, dim)


def _gated(y, gate):
    return (y * jax.nn.silu(gate)).astype(BF16)


def _mm_kernel(x_ref, w_ref, o_ref):
    o_ref[...] = _dot(x_ref[...].astype(BF16), w_ref[...]).astype(o_ref.dtype)


def _matmul(x, w, out_dtype, tn):
    rows, kdim = x.shape
    width = w.shape[1]
    tm = min(rows, 1024)
    return pl.pallas_call(
        _mm_kernel,
        grid=(rows // tm, width // tn),
        in_specs=[pl.BlockSpec((tm, kdim), lambda i, j: (i, 0)),
                  pl.BlockSpec((kdim, tn), lambda i, j: (0, j))],
        out_specs=pl.BlockSpec((tm, tn), lambda i, j: (i, j)),
        out_shape=jax.ShapeDtypeStruct((rows, width), out_dtype),
        compiler_params=_params("parallel", "parallel"),
        name="proj_matmul",
    )(x, w)


def _s5_kernel(u_ref, gate_ref, h0r_ref, h0i_ref, ar_ref, ai_ref, bb_ref, cc_ref, d_ref, wglu_ref,
               ya_ref, hr_out, hi_out, xr_s, xi_s, hr_s, hi_s, cr_s, ci_s, *, tc):
    @pl.when(pl.program_id(1) == 0)
    def _():
        cr_s[...] = h0r_ref[...]
        ci_s[...] = h0i_ref[...]

    u = u_ref[...]
    ub = u.astype(BF16)
    uw = BRANCH_W // S5_SPLIT
    sw = S5_LANES // S5_SPLIT
    for part in range(S5_SPLIT):
        x = _dot(ub[:, part * uw:(part + 1) * uw], bb_ref[part])
        xr_s[:, part * sw:(part + 1) * sw] = x[:, :sw]
        xi_s[:, part * sw:(part + 1) * sw] = x[:, sw:]

    ar = ar_ref[...]
    ai = ai_ref[...]

    def step(t, carry):
        hr, hi = carry
        nhr = ar * hr - ai * hi + xr_s[pl.ds(t, 1), :]
        nhi = ar * hi + ai * hr + xi_s[pl.ds(t, 1), :]
        hr_s[pl.ds(t, 1), :] = nhr
        hi_s[pl.ds(t, 1), :] = nhi
        return nhr, nhi

    hr, hi = lax.fori_loop(0, tc, step, (cr_s[...], ci_s[...]), unroll=8)
    cr_s[...] = hr
    ci_s[...] = hi
    hr_out[...] = hr
    hi_out[...] = hi
    ys = []
    for part in range(S5_SPLIT):
        hcat = jnp.concatenate([hr_s[:, part * sw:(part + 1) * sw], hi_s[:, part * sw:(part + 1) * sw]], axis=1)
        ys.append(_dot(hcat.astype(BF16), cc_ref[part]))
    y = jnp.concatenate(ys, axis=1) + d_ref[...] * u
    y = jax.nn.gelu(y)
    g = _dot(y.astype(BF16), wglu_ref[...])
    ya_ref[...] = _gated(g[:, :BRANCH_W] * jax.nn.sigmoid(g[:, BRANCH_W:]), gate_ref[...])


def _s5_call(h, n, t, h0r, h0i, sp):
    tc = min(t, 256)
    nc = t // tc
    uw = BRANCH_W // S5_SPLIT
    sw = S5_LANES // S5_SPLIT
    const = lambda shape: pl.BlockSpec(shape, lambda i, c: (0,) * len(shape))
    state = pl.BlockSpec((None, 1, S5_LANES), lambda i, c: (i, 0, 0))
    return pl.pallas_call(
        functools.partial(_s5_kernel, tc=tc),
        grid=(n, nc),
        in_specs=[pl.BlockSpec((tc, BRANCH_W), lambda i, c: (i * nc + c, PROJ_OFF['a_u'] // BRANCH_W)),
                  pl.BlockSpec((tc, BRANCH_W), lambda i, c: (i * nc + c, PROJ_OFF['a_gate'] // BRANCH_W)),
                  state, state,
                  const((1, S5_LANES)), const((1, S5_LANES)),
                  const((S5_SPLIT, uw, 2 * sw)), const((S5_SPLIT, 2 * sw, uw)),
                  const((1, BRANCH_W)), const((BRANCH_W, 2 * BRANCH_W))],
        out_specs=[pl.BlockSpec((tc, BRANCH_W), lambda i, c: (i * nc + c, 0)), state, state],
        out_shape=[jax.ShapeDtypeStruct((n * t, BRANCH_W), BF16),
                   jax.ShapeDtypeStruct((n, 1, S5_LANES), F32),
                   jax.ShapeDtypeStruct((n, 1, S5_LANES), F32)],
        scratch_shapes=[pltpu.VMEM((tc, S5_LANES), F32), pltpu.VMEM((tc, S5_LANES), F32),
                        pltpu.VMEM((tc, S5_LANES), F32), pltpu.VMEM((tc, S5_LANES), F32),
                        pltpu.VMEM((1, S5_LANES), F32), pltpu.VMEM((1, S5_LANES), F32)],
        compiler_params=_params("parallel", "arbitrary"),
        name="s5_branch",
    )(h, h, h0r, h0i, sp['ar'], sp['ai'], sp['bb'], sp['cc'], sp['d'], sp['wglu'])


def _s5_prepare(a_re, a_im, log_dt, b_re, b_im, c_re, c_im, d, w_glu):
    lam_r = jnp.minimum(a_re, -1e-4)
    lam_i = a_im
    dt = jnp.exp(log_dt)[:, None]
    mag = jnp.exp(lam_r * dt)
    ar = mag * jnp.cos(lam_i * dt)
    ai = mag * jnp.sin(lam_i * dt)
    den = lam_r * lam_r + lam_i * lam_i
    nr = ar - 1.0
    cr = (nr * lam_r + ai * lam_i) / den
    ci = (ai * lam_r - nr * lam_i) / den
    bb_r = cr[..., None] * b_re - ci[..., None] * b_im
    bb_i = cr[..., None] * b_im + ci[..., None] * b_re
    eye = jnp.eye(S5_GROUPS, dtype=F32)

    def blockdiag_in(bb):
        m = eye[:, None, :, None] * jnp.swapaxes(bb, 1, 2)[:, :, None, :]
        return m.reshape(BRANCH_W, S5_LANES)

    def blockdiag_out(c):
        m = eye[:, None, :, None] * jnp.swapaxes(c, 1, 2)[:, :, None, :]
        return m.reshape(S5_LANES, BRANCH_W)

    uw = BRANCH_W // S5_SPLIT
    sw = S5_LANES // S5_SPLIT
    bbr, bbi = blockdiag_in(bb_r), blockdiag_in(bb_i)
    ccr, cci = blockdiag_out(c_re), blockdiag_out(c_im)
    bb = jnp.stack([jnp.concatenate([m[p * uw:(p + 1) * uw, p * sw:(p + 1) * sw] for m in (bbr, bbi)], axis=1)
                    for p in range(S5_SPLIT)]).astype(BF16)
    cc = jnp.stack([jnp.concatenate([ccr[p * sw:(p + 1) * sw, p * uw:(p + 1) * uw],
                                     -cci[p * sw:(p + 1) * sw, p * uw:(p + 1) * uw]], axis=0)
                    for p in range(S5_SPLIT)]).astype(BF16)
    return {'ar': ar.reshape(1, S5_LANES), 'ai': ai.reshape(1, S5_LANES), 'bb': bb, 'cc': cc,
            'd': d.reshape(1, BRANCH_W), 'wglu': w_glu.astype(BF16)}


def _sortable(x):
    bits = lax.bitcast_convert_type(x, jnp.int32)
    key = bits ^ ((bits >> 31) & jnp.int32(0x7FFFFFFF))
    return jnp.where(key == -1, 0, key)


def _dsa_kernel(q_ref, qi_ref, misc_ref, gate_ref, hk_ref, bias_ref, tril_ref, o_ref, sc_s, pk_s, acc_s, far_s, *,
                qb, b_off, n_keys, topk):
    nsub = max(qb // Q_BLOCK, 1)
    qw = qb // nsub
    b = pl.program_id(1) * nsub + b_off
    nkt = (b + nsub - 1 + KEY_SUB) // KEY_SUB
    fold = KEY_TILE // FOLD_ROWS
    key_row = _iota((KEY_TILE, qb), 0)
    qpos = b * Q_BLOCK + _iota((1, qb), 1)
    limit = jnp.minimum(((qpos >> 6) + 1) * CHUNK, n_keys)
    low_half = _iota((qb, LANE), 1) < DSA_HEAD_DIM

    def key_rows(ref_cols, j):
        return hk_ref[pl.ds(pl.multiple_of(j * KEY_TILE, KEY_TILE), KEY_TILE), ref_cols:ref_cols + LANE]

    def half_masked(x, hd):
        pair = x[:, (hd // 2) * LANE:(hd // 2 + 1) * LANE]
        return jnp.where(low_half == (hd % 2 == 0), pair, 0.0).astype(BF16)

    misc = misc_ref[...]
    pick = jnp.where(_iota((SUBLANE, LANE), 1) == MISC_WI + _iota((SUBLANE, LANE), 0), 1.0, 0.0)
    wi_t = _dot_nt(pick, misc, precision=HIGHEST) * (IDX_DIM ** -0.5 * IDX_HEADS ** -0.5)
    qi = qi_ref[...]
    qi_stack = jnp.concatenate([half_masked(qi, hd) for hd in range(IDX_HEADS)], axis=0)

    def pack16(v):
        half = KEY_TILE // 2
        return (v[0:half] & 0xFFFF) | (v[half:] << 16)

    def score_block(j, carry):
        z = _dot_nt(key_rows(HK_KI, j), qi_stack)
        sc = jnp.zeros((KEY_TILE, qb), F32)
        for hd in range(IDX_HEADS):
            sc = sc + jnp.maximum(z[:, hd * qb:(hd + 1) * qb], 0.0) * wi_t[hd:hd + 1, :]
        sc = jnp.where(j * KEY_TILE + key_row < limit, sc, -jnp.inf)
        key = _sortable(sc)
        sc_s[j] = key
        pk_s[j] = pack16(key >> 16)
        return carry

    lax.fori_loop(0, nkt, score_block, 0)

    def count_ge(cand):
        def body(j, acc):
            return acc + jnp.where(sc_s[j] >= cand, 1.0, 0.0).reshape(fold, FOLD_ROWS, qb).sum(axis=0)
        acc = lax.fori_loop(0, nkt, body, jnp.zeros((FOLD_ROWS, qb), F32))
        return jnp.sum(acc, axis=0, keepdims=True)

    one16 = jnp.ones((KEY_TILE, qb), jnp.int16)
    zero16 = jnp.zeros((KEY_TILE, qb), jnp.int16)

    def count16_ge(cand):
        word = jnp.broadcast_to((cand & 0xFFFF) | (cand << 16), (KEY_TILE // 2, qb))
        c16 = pltpu.bitcast(word, jnp.int16)

        def body(j, acc):
            ind = jnp.where(pltpu.bitcast(pk_s[j], jnp.int16) >= c16, one16, zero16)
            for i in range(fold):
                acc = acc + ind[i * FOLD_ROWS:(i + 1) * FOLD_ROWS]
            return acc
        acc = lax.fori_loop(0, nkt, body, jnp.zeros((FOLD_ROWS, qb), jnp.int16))
        return jnp.sum(acc.astype(F32), axis=0, keepdims=True)

    def radix_search(start, want):
        def bit_step(i, res):
            cand = res + (jnp.int32(1) << (14 - i))
            return jnp.where(count16_ge(cand) >= want, cand, res)
        return lax.fori_loop(0, 15, bit_step, start)

    kf = float(topk)
    top = radix_search(jnp.where(count16_ge(jnp.zeros((1, qb), jnp.int32)) >= kf, 0, -32768), kf)
    int16_max = jnp.iinfo(jnp.int16).max
    above = jnp.where(top == int16_max, 0.0, count16_ge(jnp.minimum(top + 1, int16_max)))
    want_low = kf - above

    def pack_low(j, carry):
        key = sc_s[j]
        pk_s[j] = pack16(jnp.where((key >> 16) == top, (key >> 1) & 0x7FFF, -1))
        return carry

    lax.fori_loop(0, nkt, pack_low, 0)
    low = radix_search(jnp.zeros((1, qb), jnp.int32), want_low)
    base = (top << 16) | (low << 1)
    thr = jnp.where(count_ge(base | 1) >= kf, base | 1, base)
    need = kf - count_ge(thr + 1)

    tril = tril_ref[...]

    def select_block(j, ties):
        keys = sc_s[j]
        eq = keys == thr
        eq_b = jnp.where(eq, 1.0, 0.0).astype(BF16)
        parts, carry = [], ties
        for s in range(KEY_SUB):
            part = _dot(tril, eq_b[s * KEY_BLOCK:(s + 1) * KEY_BLOCK]) + carry
            parts.append(part)
            carry = part[KEY_BLOCK - 1:KEY_BLOCK, :]
        prefix = jnp.concatenate(parts, axis=0)
        sel = ((keys > thr) | (eq & (prefix <= need))) & (j * KEY_TILE + key_row < limit)
        sc_s[j] = lax.bitcast_convert_type(jnp.where(sel, 0.0, -jnp.inf), jnp.int32)
        return prefix[KEY_TILE - 1:KEY_TILE, :]

    lax.fori_loop(0, nkt, select_block, jnp.zeros((1, qb), F32))

    q = q_ref[...] * (DSA_HEAD_DIM ** -0.5 * LOG2E)
    rep = DSA_HEADS // DSA_KV_HEADS
    eye = jnp.where(_iota((LANE, LANE), 0) == _iota((LANE, LANE), 1), 1.0, 0.0).astype(BF16)
    value_lane = _iota((KEY_TILE, LANE), 1) < DSA_HEAD_DIM
    n_pairs = DSA_HEADS // 2
    q_pair = [jnp.concatenate([half_masked(q, 2 * c), half_masked(q, 2 * c + 1)], axis=0) for c in range(n_pairs)]
    far_bias = [jnp.concatenate([bias_ref[hd, 2, 0:1, 0:qw]] * nsub, axis=1)
                for hd in range(DSA_HEADS)]
    n_far = jnp.maximum((b - 1) // KEY_SUB, 0)

    def tile_operands(j):
        mask = lax.bitcast_convert_type(sc_s[j], F32)
        kd = [key_rows(grp * LANE, j) for grp in range(DSA_KV_HEADS)]
        zs = [_dot_nt(kd[(2 * c) // rep], q_pair[c]) for c in range(n_pairs)]
        vd = [jnp.where(value_lane, key_rows(HK_V + grp * LANE, j), 1.0).astype(BF16)
              for grp in range(DSA_KV_HEADS)]
        return mask, zs, vd

    def near_bias(hd, j):
        cols = [jnp.concatenate([bias_ref[hd, jnp.clip(b + u - (KEY_SUB * j + s), 0, 2), :, 0:qw]
                                 for s in range(KEY_SUB)], axis=0) for u in range(nsub)]
        return jnp.concatenate(cols, axis=1)

    def plain_tile(j, carry, far):
        mask, zs, vd = tile_operands(j)
        ps = []
        for hd in range(DSA_HEADS):
            r = hd % 2
            zh = zs[hd // 2][:, r * qb:(r + 1) * qb] + mask
            if not far:
                zh = zh + near_bias(hd, j)
            ps.append(jnp.exp2(zh).astype(BF16))
        acc = far_s if far else acc_s
        for c in range(n_pairs):
            acc[c] += _dot_tn(vd[(2 * c) // rep], jnp.concatenate(ps[2 * c:2 * c + 2], axis=1))
        return carry

    acc_s[...] = jnp.zeros_like(acc_s)
    far_s[...] = jnp.zeros_like(far_s)
    lax.fori_loop(0, n_far, lambda j, c: plain_tile(j, c, True), 0)
    lax.fori_loop(n_far, nkt, lambda j, c: plain_tile(j, c, False), 0)
    healthy = jnp.ones((1, 1), F32)
    for c in range(n_pairs):
        scale = jnp.concatenate([jnp.exp2(far_bias[2 * c]), jnp.exp2(far_bias[2 * c + 1])], axis=1)
        total = far_s[c] * scale + acc_s[c]
        acc_s[c] = total
        norm = total[DSA_HEAD_DIM:DSA_HEAD_DIM + 1, :]
        ok = (jnp.abs(total) < SOFTMAX_CEIL) & (norm > SOFTMAX_FLOOR)
        healthy = jnp.minimum(healthy, jnp.min(jnp.where(ok, 1.0, 0.0), keepdims=True))

    def attend(j, ms, far):
        mask, zs, vd = tile_operands(j)
        new_ms, ps, alphas = [], [], []
        for hd in range(DSA_HEADS):
            r = hd % 2
            zh = zs[hd // 2][:, r * qb:(r + 1) * qb] + mask
            if not far:
                zh = zh + near_bias(hd, j)
            tile_max = jnp.max(zh.reshape(fold, FOLD_ROWS, qb).max(axis=0), axis=0, keepdims=True)
            shift = far_bias[hd] if far else 0.0
            m_new = jnp.maximum(ms[hd], tile_max + shift)
            ps.append(jnp.exp2(zh - (m_new - shift)).astype(BF16))
            alphas.append(jnp.exp2(ms[hd] - m_new))
            new_ms.append(m_new)
        for c in range(n_pairs):
            acc_s[c] = (acc_s[c] * jnp.concatenate(alphas[2 * c:2 * c + 2], axis=1)
                        + _dot_tn(vd[(2 * c) // rep], jnp.concatenate(ps[2 * c:2 * c + 2], axis=1)))
        return tuple(new_ms)

    @pl.when(healthy[0, 0] < 0.5)
    def _():
        acc_s[...] = jnp.zeros_like(acc_s)
        ms = tuple(jnp.full((1, qb), -1e30, F32) for _ in range(DSA_HEADS))
        ms = lax.fori_loop(0, n_far, lambda j, m: attend(j, m, True), ms)
        lax.fori_loop(n_far, nkt, lambda j, m: attend(j, m, False), ms)

    for c in range(n_pairs):
        halves = []
        for r in range(2):
            blk = acc_s[c, :, r * qb:(r + 1) * qb]
            halves.append(blk[0:DSA_HEAD_DIM] / blk[DSA_HEAD_DIM:DSA_HEAD_DIM + 1])
        o_t = jnp.concatenate(halves, axis=0)
        o_hi, o_lo = _split_bf16(o_t)
        o_ref[:, c * LANE:(c + 1) * LANE] = _gated(_dot_tn(o_hi, eye) + _dot_tn(o_lo, eye),
                                                   gate_ref[:, c * LANE:(c + 1) * LANE])


def _dsa_call(h, hk, n, t, bias, tril, *, b_off, n_keys, topk):
    qb = min(t, Q_TILE)
    nqb = t // qb
    lp = hk.shape[1]
    wq = DSA_HEADS * DSA_HEAD_DIM
    wqi = IDX_HEADS * IDX_DIM
    return pl.pallas_call(
        functools.partial(_dsa_kernel, qb=qb, b_off=b_off, n_keys=n_keys, topk=topk),
        grid=(n, nqb),
        in_specs=[pl.BlockSpec((qb, wq), lambda i, c: (i * nqb + c, PROJ_OFF['b_q'] // wq)),
                  pl.BlockSpec((qb, wqi), lambda i, c: (i * nqb + c, PROJ_OFF['b_qi'] // wqi)),
                  pl.BlockSpec((qb, LANE), lambda i, c: (i * nqb + c, OFF_MISC // LANE)),
                  pl.BlockSpec((qb, BRANCH_W), lambda i, c: (i * nqb + c, PROJ_OFF['b_gate'] // BRANCH_W)),
                  pl.BlockSpec((None, lp, HK_W), lambda i, c: (i, 0, 0)),
                  pl.BlockSpec((DSA_HEADS, 3, KEY_BLOCK, Q_BLOCK), lambda i, c: (0, 0, 0, 0)),
                  pl.BlockSpec((KEY_BLOCK, KEY_BLOCK), lambda i, c: (0, 0))],
        out_specs=pl.BlockSpec((qb, wq), lambda i, c: (i * nqb + c, 0)),
        out_shape=jax.ShapeDtypeStruct((n * t, wq), BF16),
        scratch_shapes=[pltpu.VMEM((lp // KEY_TILE, KEY_TILE, qb), jnp.int32),
                        pltpu.VMEM((lp // KEY_TILE, KEY_TILE // 2, qb), jnp.int32),
                        pltpu.VMEM((DSA_HEADS // 2, LANE, 2 * qb), F32),
                        pltpu.VMEM((DSA_HEADS // 2, LANE, 2 * qb), F32)],
        compiler_params=_params("parallel", "arbitrary"),
        name="dsa_branch",
    )(h, h, h, h, hk, bias, tril)


def _t5_bucket(rel):
    nb = REL_BUCKETS // 2
    max_exact = nb // 2
    ret = jnp.where(rel > 0, nb, 0)
    dist = jnp.abs(rel)
    distf = jnp.maximum(dist, 1).astype(F32)
    large = max_exact + (jnp.log(distf / max_exact) / math.log(REL_MAX_DIST / max_exact)
                         * (nb - max_exact)).astype(jnp.int32)
    large = jnp.minimum(large, nb - 1)
    return ret + jnp.where(dist < max_exact, dist, large)


def _dsa_bias_tables(rel_bias):
    lc = jnp.arange(KEY_BLOCK, dtype=jnp.int32)[:, None]
    qr = jnp.arange(Q_BLOCK, dtype=jnp.int32)[None, :]
    rel = jnp.stack([lc - qr - back * KEY_BLOCK for back in range(3)], axis=0)
    onehot = jax.nn.one_hot(_t5_bucket(rel), REL_BUCKETS, dtype=F32)
    bias = jnp.einsum('klqb,bh->hklq', onehot, rel_bias.astype(F32), precision=HIGHEST)
    return bias * LOG2E


def _gdn_kernel(raw_ref, misc_ref, gate_ref, s0_ref, conv0_ref, convw_ref, alog_ref, dtb_ref, norm_ref,
                o_ref, s_out, conv_out, xp_s, st_s, *, n, c, nsub):
    rows = nsub * c

    @pl.when(pl.program_id(0) == 0)
    def _():
        xp_s[:, 0:SUBLANE, :] = conv0_ref[...]
        st_s[...] = s0_ref[...]

    w = convw_ref[...]
    row = _iota((c, c), 0)
    col = _iota((c, c), 1)
    incl = row >= col
    strict = row > col
    eye = row == col
    tril = jnp.where(incl, 1.0, 0.0)
    eye_f = jnp.where(eye, 1.0, 0.0)
    chains = [(b, sub, hd) for b in range(n) for sub in range(nsub) for hd in range(GDN_HEADS)]

    qkv, beta_all, gc_all = [], [], []
    for b in range(n):
        raw = raw_ref[b]
        xp_s[b, SUBLANE:SUBLANE + rows, :] = raw
        y = raw * w[CONV_W - 1:CONV_W, :]
        for i in range(CONV_W - 1):
            y = y + xp_s[b, CONV_LEAD + i:CONV_LEAD + i + rows, :] * w[i:i + 1, :]
        tail = xp_s[b, rows:rows + SUBLANE, :]
        xp_s[b, 0:SUBLANE, :] = tail
        conv_out[b] = tail
        qkv.append(jax.nn.silu(y))
        misc = misc_ref[b]
        beta_all.append(jax.nn.sigmoid(misc))
        g_all = -jnp.exp(alog_ref[...]) * _softplus(misc + dtb_ref[...])
        gc_all.append([_dot(tril, g_all[sub * c:(sub + 1) * c], precision=HIGHEST)
                       for sub in range(nsub)])

    qs, ks, vs, betas, gcs, dec_s, dec_i = [], [], [], [], [], [], []
    for b, sub, hd in chains:
        x = qkv[b][sub * c:(sub + 1) * c]
        qh = x[:, hd * GDN_DK:(hd + 1) * GDN_DK]
        kh = x[:, GDN_QK + hd * GDN_DK:GDN_QK + (hd + 1) * GDN_DK]
        vs.append(x[:, 2 * GDN_QK + hd * GDN_DV:2 * GDN_QK + (hd + 1) * GDN_DV])
        qs.append(qh * lax.rsqrt(jnp.sum(qh * qh, axis=1, keepdims=True) + RMS_EPS) * (GDN_DK ** -0.5))
        ks.append(kh * lax.rsqrt(jnp.sum(kh * kh, axis=1, keepdims=True) + RMS_EPS))
        betas.append(beta_all[b][sub * c:(sub + 1) * c, MISC_BETA + hd:MISC_BETA + hd + 1])
        gc = gc_all[b][sub][:, MISC_A + hd:MISC_A + hd + 1]
        gcs.append(gc)
        gc_b = jnp.broadcast_to(gc, (c, c))
        gc_row = jnp.sum(jnp.where(eye, gc_b, 0.0), axis=0, keepdims=True)
        diff = gc_b - gc_row
        dec = jnp.exp(diff)
        dec_s.append(jnp.where(strict, dec, 0.0))
        dec_i.append(jnp.where(incl, dec, 0.0))
    kb = [k.astype(BF16) for k in ks]
    qb = [q.astype(BF16) for q in qs]
    pw = [-(beta * _dot_nt(k, k) * d) for beta, k, d in zip(betas, kb, dec_s)]
    inv = [eye_f + p for p in pw]
    for _ in range(int(math.log2(c)) - 1):
        pw = [_dot3(p, p) for p in pw]
        inv = [i + _dot3(i, p) for i, p in zip(inv, pw)]
    egc = [jnp.exp(gc) for gc in gcs]
    sol = [_dot3(i, jnp.concatenate([(beta * e) * k, beta * v], axis=1))
           for i, beta, e, k, v in zip(inv, betas, egc, ks, vs)]
    attn = [(_dot_nt(q, k) * d).astype(BF16) for q, k, d in zip(qb, kb, dec_i)]
    for sub in range(nsub):
        ids = [i for i, ch in enumerate(chains) if ch[1] == sub]
        s_old = [st_s[chains[i][0], chains[i][2]] for i in ids]
        sb = [s.astype(BF16) for s in s_old]
        vb = [(sol[i][:, GDN_DK:] - _dot(sol[i][:, :GDN_DK].astype(BF16), s)).astype(BF16) for i, s in zip(ids, sb)]
        outs = [egc[i] * _dot(qb[i], s) + _dot(attn[i], v) for i, s, v in zip(ids, sb, vb)]
        for i, v, s, o in zip(ids, vb, s_old, outs):
            b, _, hd = chains[i]
            g_last = gcs[i][c - 1:c, :]
            st_s[b, hd] = jnp.exp(g_last) * s + _dot_tn((ks[i] * jnp.exp(g_last - gcs[i])).astype(BF16), v)
            o = o * lax.rsqrt(jnp.mean(o * o, axis=1, keepdims=True) + RMS_EPS) * norm_ref[...]
            o_ref[b, sub * c:(sub + 1) * c, hd * GDN_DV:(hd + 1) * GDN_DV] = _gated(
                o, gate_ref[b, sub * c:(sub + 1) * c, hd * GDN_DV:(hd + 1) * GDN_DV])
    s_out[...] = st_s[...]


def _gdn_call(h3, s0, conv0, gp):
    n, t, _ = h3.shape
    c = min(t, CHUNK)
    nsub = GDN_CHUNKS_PER_STEP if t % (GDN_CHUNKS_PER_STEP * c) == 0 else 1
    rows = nsub * c
    const = lambda shape: pl.BlockSpec(shape, lambda j: (0,) * len(shape))
    state = const((n, GDN_HEADS, GDN_DK, GDN_DV))
    conv = const((n, SUBLANE, GDN_CONV_CH))
    return pl.pallas_call(
        functools.partial(_gdn_kernel, n=n, c=c, nsub=nsub),
        grid=(t // rows,),
        in_specs=[pl.BlockSpec((n, rows, GDN_CONV_CH), lambda j: (0, j, PROJ_OFF['c_qkv'] // GDN_CONV_CH)),
                  pl.BlockSpec((n, rows, LANE), lambda j: (0, j, OFF_MISC // LANE)),
                  pl.BlockSpec((n, rows, BRANCH_W), lambda j: (0, j, PROJ_OFF['c_gate'] // BRANCH_W)),
                  state, conv,
                  const((CONV_W, GDN_CONV_CH)), const((1, LANE)), const((1, LANE)), const((1, GDN_DV))],
        out_specs=[pl.BlockSpec((n, rows, GDN_VW), lambda j: (0, j, 0)), state, conv],
        out_shape=[jax.ShapeDtypeStruct((n, t, GDN_VW), BF16),
                   jax.ShapeDtypeStruct((n, GDN_HEADS, GDN_DK, GDN_DV), F32),
                   jax.ShapeDtypeStruct((n, SUBLANE, GDN_CONV_CH), F32)],
        scratch_shapes=[pltpu.VMEM((n, rows + SUBLANE, GDN_CONV_CH), F32),
                        pltpu.VMEM((n, GDN_HEADS, GDN_DK, GDN_DV), F32)],
        compiler_params=_params("arbitrary"),
        name="gdn_branch",
    )(h3, h3, h3, s0, conv0, gp['convw'], gp['alog'], gp['dtb'], gp['norm'])


def _lane_place(v, off):
    return jnp.zeros((1, LANE), F32).at[0, off:off + v.shape[0]].set(v.astype(F32))


def _gla_kernel(qk_ref, v_ref, misc_ref, gate_ref, s0_ref, wg_ref, bg_ref, norm_ref, hexp_ref, gsum_ref,
                o_ref, s_out, st_s, *, n, tc):
    blk = GLA_BLOCK
    wk = GLA_HEADS * GLA_DK

    @pl.when(pl.program_id(0) == 0)
    def _():
        st_s[...] = s0_ref[...]

    row = _iota((blk, blk), 0)
    col = _iota((blk, blk), 1)
    tril = jnp.where(row >= col, 1.0, 0.0)
    jrow = _iota((blk, wk), 0)
    lane_k = _iota((blk, wk), 1)
    head_mask = [(lane_k >= hd * GLA_DK) & (lane_k < (hd + 1) * GLA_DK) for hd in range(GLA_HEADS)]
    chains = range(n)

    def block(sb, carry):
        r0 = pl.multiple_of(sb * blk, blk)
        qk = [qk_ref[b, pl.ds(r0, blk), :] for b in chains]
        q = [x[:, :wk] * (GLA_DK ** -0.5) for x in qk]
        k = [x[:, wk:] for x in qk]
        v = [v_ref[b, pl.ds(r0, blk), :] for b in chains]
        mb = [misc_ref[b, pl.ds(r0, blk), :].astype(BF16) for b in chains]
        lg = [jax.nn.log_sigmoid(_dot(m, wg_ref[...]) + bg_ref[...]) * (1.0 / GLA_TAU) for m in mb]
        bc = [_dot(tril, x, precision=HIGHEST) for x in lg]
        dmat = []
        for b in chains:
            slabs = []
            for i in range(blk):
                keep = jrow <= i
                dec = jnp.where(keep, jnp.exp(bc[b][i:i + 1, :] - bc[b]), 0.0)
                slabs.append(dec * q[b][i:i + 1, :] * k[b])
            dmat.append(jnp.concatenate(slabs, axis=0).astype(BF16))
        a_exp = [_dot(d, hexp_ref[...]) for d in dmat]
        prod = [a * jnp.concatenate([x] * blk, axis=0) for a, x in zip(a_exp, v)]
        o = [_dot(gsum_ref[...], p.astype(BF16)) for p in prod]
        qe = [x * jnp.exp(c) for x, c in zip(q, bc)]
        kt = [x * jnp.exp(c[blk - 1:blk, :] - c) for x, c in zip(k, bc)]
        s = [st_s[b] for b in chains]
        sbf = [x.astype(BF16) for x in s]
        for b in chains:
            upd = jnp.zeros_like(s[b])
            parts = []
            for hd in range(GLA_HEADS):
                parts.append(_dot_nt(jnp.where(head_mask[hd], qe[b], 0.0).astype(BF16), sbf[b]))
                vh = v[b][:, hd * GLA_DV:(hd + 1) * GLA_DV].astype(BF16)
                upd = upd + _dot_tn(vh, jnp.where(head_mask[hd], kt[b], 0.0).astype(BF16))
            ob = o[b] + jnp.concatenate(parts, axis=1)
            st_s[b] = jnp.exp(bc[b][blk - 1:blk, :]) * s[b] + upd
            for hd in range(GLA_HEADS):
                oh = ob[:, hd * GLA_DV:(hd + 1) * GLA_DV]
                oh = oh * lax.rsqrt(jnp.mean(oh * oh, axis=1, keepdims=True) + RMS_EPS) * norm_ref[...]
                o_ref[b, pl.ds(r0, blk), hd * GLA_DV:(hd + 1) * GLA_DV] = _gated(
                    oh, gate_ref[b, pl.ds(r0, blk), hd * GLA_DV:(hd + 1) * GLA_DV])
        return carry

    lax.fori_loop(0, tc // blk, block, 0, unroll=4 if tc // blk > 1 else 1)
    s_out[...] = st_s[...]


def _gla_call(h3, s0, gp):
    n, t, _ = h3.shape
    tc = min(t, 128)
    wk = GLA_HEADS * GLA_DK
    wv = GLA_HEADS * GLA_DV
    const = lambda shape: pl.BlockSpec(shape, lambda j: (0,) * len(shape))
    state = const((n, GLA_DV, wk))
    return pl.pallas_call(
        functools.partial(_gla_kernel, n=n, tc=tc),
        grid=(t // tc,),
        in_specs=[pl.BlockSpec((n, tc, 2 * wk), lambda j: (0, j, PROJ_OFF['d_q'] // (2 * wk))),
                  pl.BlockSpec((n, tc, wv), lambda j: (0, j, PROJ_OFF['d_v'] // wv)),
                  pl.BlockSpec((n, tc, LANE), lambda j: (0, j, OFF_MISC // LANE)),
                  pl.BlockSpec((n, tc, BRANCH_W), lambda j: (0, j, PROJ_OFF['d_gate'] // BRANCH_W)),
                  state,
                  const((LANE, wk)), const((1, wk)), const((1, GLA_DV)),
                  const((wk, wv)), const((GLA_BLOCK, GLA_BLOCK * GLA_BLOCK))],
        out_specs=[pl.BlockSpec((n, tc, wv), lambda j: (0, j, 0)), state],
        out_shape=[jax.ShapeDtypeStruct((n, t, wv), BF16),
                   jax.ShapeDtypeStruct((n, GLA_DV, wk), F32)],
        scratch_shapes=[pltpu.VMEM((n, GLA_DV, wk), F32)],
        compiler_params=_params("arbitrary"),
        name="gla_branch",
    )(h3, h3, h3, h3, s0, gp['wg'], gp['bg'], gp['norm'], gp['hexp'], gp['gsum'])


def _gla_prepare(w_g2, b_g, norm):
    wk = GLA_HEADS * GLA_DK
    wg = jnp.zeros((LANE, wk), F32).at[MISC_G:MISC_G + GLA_GATE_RANK].set(w_g2).astype(BF16)
    ck = jnp.arange(wk)[:, None] // GLA_DK
    cv = jnp.arange(GLA_HEADS * GLA_DV)[None, :] // GLA_DV
    ij = jnp.arange(GLA_BLOCK * GLA_BLOCK)[None, :] // GLA_BLOCK
    return {'wg': wg, 'bg': b_g.reshape(1, wk).astype(F32),
            'norm': norm.reshape(1, GLA_DV).astype(F32),
            'hexp': (ck == cv).astype(BF16),
            'gsum': (jnp.arange(GLA_BLOCK)[:, None] == ij).astype(BF16)}


def _merge_kernel(x_ref, ya_ref, yb_ref, yc_ref, yd_ref, mg_ref, wb_ref, wo_ref, lng_ref, lnb_ref, o_ref):
    mixed = None
    for br, act_ref in enumerate((ya_ref, yb_ref, yc_ref, yd_ref)):
        term = jax.nn.sigmoid(mg_ref[:, br * D_MODEL:(br + 1) * D_MODEL]) * _dot(act_ref[...], wb_ref[br])
        mixed = term if mixed is None else mixed + term
    z = DN_ALPHA * x_ref[...] + _dot(mixed.astype(BF16), wo_ref[...])
    mu = jnp.mean(z, axis=1, keepdims=True)
    zc = z - mu
    var = jnp.mean(zc * zc, axis=1, keepdims=True)
    o_ref[...] = zc * lax.rsqrt(var + LN_EPS) * lng_ref[...] + lnb_ref[...]


def _merge_call(x, h, ya, yb, yc, yd, wb, wo, lng, lnb):
    rows = x.shape[0]
    tm = min(rows, 256)
    bw = BRANCH_W
    ybs = pl.BlockSpec((tm, bw), lambda i: (i, 0))
    return pl.pallas_call(
        _merge_kernel,
        grid=(rows // tm,),
        in_specs=[pl.BlockSpec((tm, D_MODEL), lambda i: (i, 0)), ybs, ybs, ybs, ybs,
                  pl.BlockSpec((tm, N_BRANCH * D_MODEL), lambda i: (i, 0)),
                  pl.BlockSpec((N_BRANCH, bw, D_MODEL), lambda i: (0, 0, 0)),
                  pl.BlockSpec((D_MODEL, D_MODEL), lambda i: (0, 0)),
                  pl.BlockSpec((1, D_MODEL), lambda i: (0, 0)),
                  pl.BlockSpec((1, D_MODEL), lambda i: (0, 0))],
        out_specs=pl.BlockSpec((tm, D_MODEL), lambda i: (i, 0)),
        out_shape=jax.ShapeDtypeStruct((rows, D_MODEL), F32),
        compiler_params=_params("parallel"),
        name="merge_norm",
    )(x, ya, yb, yc, yd, h, wb, wo, lng, lnb)


def _split_w_in(w_in):
    out, off = {}, 0
    for name, width in IN_LAYOUT:
        out[name] = w_in[:, off:off + width]
        off += width
    return out


def _regroup_kernel(w_ref, proj_ref, key_ref):
    cols = _split_w_in(w_ref[...])
    pad = jnp.zeros((w_ref.shape[0], MISC_PAD), F32)
    proj_ref[...] = jnp.concatenate([cols[name] for name in PROJ_ORDER] + [pad], axis=1).astype(BF16)
    k, v, ki = cols['b_k'], cols['b_v'], cols['b_ki']
    dh = DSA_HEAD_DIM
    key_ref[...] = jnp.concatenate([k[:, :dh], k[:, :dh], k[:, dh:], k[:, dh:],
                                    v[:, :dh], v[:, :dh], v[:, dh:], v[:, dh:], ki, ki], axis=1).astype(BF16)


def _regroup_weights(w_in):
    depth, rows, width = w_in.shape
    tr = 128
    return pl.pallas_call(
        _regroup_kernel,
        grid=(depth, rows // tr),
        in_specs=[pl.BlockSpec((None, tr, width), lambda l, i: (l, i, 0))],
        out_specs=[pl.BlockSpec((None, tr, PROJ_W), lambda l, i: (l, i, 0)),
                   pl.BlockSpec((None, tr, HK_W), lambda l, i: (l, i, 0))],
        out_shape=[jax.ShapeDtypeStruct((depth, rows, PROJ_W), BF16),
                   jax.ShapeDtypeStruct((depth, rows, HK_W), BF16)],
        compiler_params=_params("parallel", "parallel"),
        name="regroup_weights",
    )(w_in)


def _dup_keys(k, v, ki):
    return jnp.concatenate([k[:, :, 0], k[:, :, 0], k[:, :, 1], k[:, :, 1],
                            v[:, :, 0], v[:, :, 0], v[:, :, 1], v[:, :, 1], ki, ki], axis=-1).astype(BF16)


def _layer(x, lw, st):
    n, t, _ = x.shape
    x2 = x.reshape(n * t, D_MODEL)
    h = _matmul(x2, lw['w_proj'], F32, tn=PROJ_W // 9)
    hk_new = _matmul(x2, lw['w_key'], BF16, tn=HK_W).reshape(n, t, HK_W)
    if st is None:
        hk, n_keys, b_off = hk_new, t, 0
        s5r = jnp.zeros((n, 1, S5_LANES), F32)
        s5i = jnp.zeros((n, 1, S5_LANES), F32)
        gdn0 = jnp.zeros((n, GDN_HEADS, GDN_DK, GDN_DV), F32)
        conv0 = jnp.zeros((n, SUBLANE, GDN_CONV_CH), F32)
        gla0 = jnp.zeros((n, GLA_DV, GLA_HEADS * GLA_DK), F32)
    else:
        past = st['k'].shape[1]
        n_keys = past + t
        b_off = past // Q_BLOCK
        lp = -(-n_keys // KEY_TILE) * KEY_TILE
        hk = jnp.concatenate([_dup_keys(st['k'], st['v'], st['kidx']), hk_new,
                              jnp.zeros((n, lp - n_keys, HK_W), BF16)], axis=1)
        s5r = st['s5_re'].reshape(n, 1, S5_LANES)
        s5i = st['s5_im'].reshape(n, 1, S5_LANES)
        gdn0 = st['gdn']
        conv0 = jnp.concatenate([jnp.zeros((n, CONV_LEAD, GDN_CONV_CH), F32), st['gdn_conv']], axis=1)
        gla0 = jnp.moveaxis(st['gla'], 3, 1).reshape(n, GLA_DV, GLA_HEADS * GLA_DK)
    topk = min(DSA_TOPK_MAX, n_keys // 4)

    ya, s5r_new, s5i_new = _s5_call(h, n, t, s5r, s5i, lw['s5'])
    yb = _dsa_call(h, hk, n, t, lw['bias'], lw['tril'], b_off=b_off, n_keys=n_keys, topk=topk)
    h3 = h.reshape(n, t, PROJ_W)
    yc, gdn_new, conv_new = _gdn_call(h3, gdn0, conv0, lw['gdn'])
    yd, gla_new = _gla_call(h3, gla0, lw['gla'])
    yc = yc.reshape(n * t, GDN_VW)
    yd = yd.reshape(n * t, GLA_HEADS * GLA_DV)
    y = _merge_call(x2, h, ya, yb, yc, yd, lw['w_branch'], lw['w_out'], lw['ln_g'], lw['ln_b'])

    def cols(name, width):
        return h[:, PROJ_OFF[name]:PROJ_OFF[name] + width].reshape(n, t, width)

    new = {'k': cols('b_k', DSA_KV_HEADS * DSA_HEAD_DIM).reshape(n, t, DSA_KV_HEADS, DSA_HEAD_DIM),
           'v': cols('b_v', DSA_KV_HEADS * DSA_HEAD_DIM).reshape(n, t, DSA_KV_HEADS, DSA_HEAD_DIM),
           'kidx': cols('b_ki', IDX_DIM),
           's5_re': s5r_new.reshape(n, S5_GROUPS, S5_STATE), 's5_im': s5i_new.reshape(n, S5_GROUPS, S5_STATE),
           'gdn': gdn_new, 'gdn_conv': conv_new[:, CONV_LEAD:],
           'gla': jnp.moveaxis(gla_new.reshape(n, GLA_DV, GLA_HEADS, GLA_DK), 1, 3)}
    return y.reshape(n, t, D_MODEL), new


def kernel(x_prompt, x_sample, cache_k, cache_v, cache_kidx, state_s5_re, state_s5_im, state_gdn, state_gdn_conv, state_gla, w_in, w_branch, w_out, ln_g, ln_b, rel_bias, s5_a_re, s5_a_im, s5_log_dt, s5_b_re, s5_b_im, s5_c_re, s5_c_im, s5_d, s5_w_glu, gdn_conv, gdn_a_log, gdn_dt_bias, gdn_norm, gla_w_g2, gla_b_g, gla_norm):
    bias = _dsa_bias_tables(rel_bias)
    tril = (jnp.arange(KEY_BLOCK)[:, None] >= jnp.arange(KEY_BLOCK)[None, :]).astype(BF16)
    yp, ys = x_prompt, x_sample
    new_p, new_s = [], []
    w_proj, w_key = _regroup_weights(w_in)
    for l in range(DEPTH):
        lw = {'w_proj': w_proj[l], 'w_key': w_key[l], 'bias': bias, 'tril': tril,
              'w_branch': w_branch[l].astype(BF16), 'w_out': w_out[l].astype(BF16),
              'ln_g': ln_g[l].reshape(1, D_MODEL), 'ln_b': ln_b[l].reshape(1, D_MODEL),
              's5': _s5_prepare(s5_a_re[l], s5_a_im[l], s5_log_dt[l], s5_b_re[l], s5_b_im[l],
                                s5_c_re[l], s5_c_im[l], s5_d[l], s5_w_glu[l]),
              'gdn': {'convw': gdn_conv[l], 'alog': _lane_place(gdn_a_log[l], MISC_A),
                      'dtb': _lane_place(gdn_dt_bias[l], MISC_A), 'norm': gdn_norm[l].reshape(1, GDN_DV)},
              'gla': _gla_prepare(gla_w_g2[l], gla_b_g[l], gla_norm[l])}
        st = {'k': cache_k[l], 'v': cache_v[l], 'kidx': cache_kidx[l], 's5_re': state_s5_re[l],
              's5_im': state_s5_im[l], 'gdn': state_gdn[l], 'gdn_conv': state_gdn_conv[l], 'gla': state_gla[l]}
        yp, stp = _layer(yp, lw, None)
        ys, sts = _layer(ys, lw, st)
        new_p.append(stp)
        new_s.append(sts)
    names = ('k', 'v', 'kidx', 's5_re', 's5_im', 'gdn', 'gdn_conv', 'gla')
    stack = lambda states, name: jnp.stack([s[name] for s in states], axis=0)
    return ((yp, ys) + tuple(stack(new_p, nm) for nm in names) + tuple(stack(new_s, nm) for nm in names))
```
